```python
import math
import jax
import jax.numpy as jnp
from jax import lax
import numpy as np

D_MODEL = 2048
BATCH = 2
SEQ = 4096
DEPTH = 2
DEC_BATCH = 8
DEC_SEQ = 8
PAST_LEN = 16384
PAGE_SIZE = 128

F32 = jnp.float32
EPS = 1e-6
NEG_INF = -1e30

SSD_INNER = D_MODEL // 2
SSD_HEAD_DIM = 64
SSD_HEADS = SSD_INNER // SSD_HEAD_DIM
SSD_GROUPS = 2
SSD_STATE = 128
SSD_CONV = 4
SSD_CONV_DIM = SSD_INNER + 2 * SSD_GROUPS * SSD_STATE
SSD_CHUNK = 128
NSA_WIDTH = D_MODEL // 4
NSA_HEAD_DIM = 64
NSA_HEADS = NSA_WIDTH // NSA_HEAD_DIM
NSA_KV_HEADS = 2
NSA_REP = NSA_HEADS // NSA_KV_HEADS
NSA_BLOCK = 64
NSA_TOPN = 16
NSA_WINDOW = 512
NSA_CMP_HIDDEN = 128
SEL_Q_BLOCK = 64
WIN_Q_BLOCK = 128
RET_WIDTH = D_MODEL // 4
RET_HEADS = 4
RET_VAL_DIM = RET_WIDTH // RET_HEADS
RET_KEY_DIM = RET_VAL_DIM // 2
RET_CHUNK = 128
D_MIX = SSD_INNER + NSA_WIDTH + RET_WIDTH
MOE_GROUPS = 4
MOE_EXPERTS_PER_GROUP = 8
MOE_EXPERTS = MOE_GROUPS * MOE_EXPERTS_PER_GROUP
MOE_TOPK = 2
MOE_HIDDEN = D_MODEL // 4
IN_SIZES = (SSD_INNER, SSD_CONV_DIM, SSD_HEADS, NSA_WIDTH, 3 * 2 * NSA_KV_HEADS * NSA_HEAD_DIM, 3 * NSA_HEADS, RET_HEADS * RET_KEY_DIM, RET_HEADS * RET_KEY_DIM, RET_WIDTH, RET_WIDTH)
IN_COLS = SSD_INNER + SSD_CONV_DIM + SSD_HEADS + NSA_WIDTH + 3 * 2 * NSA_KV_HEADS * NSA_HEAD_DIM + 3 * NSA_HEADS + 2 * RET_HEADS * RET_KEY_DIM + 2 * RET_WIDTH

kernel_name = 'hymba_ssd_nsa_retention_hmoe_step'


def rms_norm(x, gain=None):
    xf = x.astype(F32)
    y = xf * lax.rsqrt(jnp.mean(xf * xf, axis=-1, keepdims=True) + EPS)
    if gain is not None:
        y = y * gain.astype(F32)
    return y


def masked_softmax(s, mask):
    s = jnp.where(mask, s.astype(F32), NEG_INF)
    return jax.nn.softmax(s, axis=-1) * mask


def alibi_slopes(n):
    return jnp.exp2(-8.0 * jnp.arange(1, n + 1, dtype=F32) / n)


def segsum(a):
    cs = jnp.cumsum(a, axis=-1)
    d = cs[..., :, None] - cs[..., None, :]
    n = a.shape[-1]
    return jnp.where(jnp.tril(jnp.ones((n, n), bool)), d, -jnp.inf)


def split_cols(a, sizes):
    idx = [int(s) for s in np.cumsum(sizes)[:-1]]
    return jnp.split(a, idx, axis=-1)


def ssd_chunked(x, dt, a, bm, cm, h0):
    bsz, s = x.shape[:2]
    lc = math.gcd(s, SSD_CHUNK)
    nc = s // lc
    g, r, p, n = SSD_GROUPS, SSD_HEADS // SSD_GROUPS, SSD_HEAD_DIM, SSD_STATE
    xdt = (x.astype(F32) * dt[..., None]).reshape(bsz, nc, lc, g, r, p)
    da = (dt * a).reshape(bsz, nc, lc, g, r).transpose(0, 3, 4, 1, 2)
    bm = bm.astype(F32).reshape(bsz, nc, lc, g, n)
    cm = cm.astype(F32).reshape(bsz, nc, lc, g, n)
    a_cs = jnp.cumsum(da, axis=-1)
    lmat = jnp.exp(segsum(da))
    cb = jnp.einsum('bclgn,bcsgn->bgcls', cm, bm)
    y_diag = jnp.einsum('bgrcls,bcsgrp->bclgrp', cb[:, :, None] * lmat, xdt)
    decay_states = jnp.exp(a_cs[..., -1:] - a_cs).transpose(0, 3, 4, 1, 2)
    states = jnp.einsum('bclgn,bclgrp->bcgrpn', bm, xdt * decay_states[..., None])
    states = jnp.concatenate([h0.astype(F32).reshape(bsz, 1, g, r, p, n), states], axis=1)
    a_last = jnp.pad(a_cs[..., -1], ((0, 0), (0, 0), (0, 0), (1, 0)))
    decay_chunk = jnp.exp(segsum(a_last))
    new_states = jnp.einsum('bgrzc,bcgrpn->bzgrpn', decay_chunk, states)
    y_off = jnp.einsum('bclgn,bcgrpn->bclgrp', cm, new_states[:, :-1]) * jnp.exp(a_cs).transpose(0, 3, 4, 1, 2)[..., None]
    y = (y_diag + y_off).reshape(bsz, s, SSD_HEADS, p)
    return y, new_states[:, -1].reshape(bsz, SSD_HEADS, p, n)


def rotary(x, pos):
    half = x.shape[-1] // 2
    freqs = 1.0 / (10000.0 ** jnp.linspace(0.0, 1.0, half, dtype=F32))
    ang = pos.astype(F32)[:, None] * freqs[None, :]
    cos = jnp.cos(ang)[None, :, None, :]
    sin = jnp.sin(ang)[None, :, None, :]
    x1, x2 = x[..., :half], x[..., half:]
    return jnp.concatenate([x1 * cos - x2 * sin, x1 * sin + x2 * cos], axis=-1)


def retention(q, k, v, r0, offset):
    bsz, s, h, _ = q.shape
    pos = offset + jnp.arange(s)
    q = rotary(q.astype(F32), pos)
    k = rotary(k.astype(F32), pos) * (RET_KEY_DIM ** -0.5)
    v = v.astype(F32)
    lc = math.gcd(s, RET_CHUNK)
    nc = s // lc
    log_g = jnp.log1p(-jnp.exp2(-5.0 - jnp.arange(h, dtype=F32)))
    i = jnp.arange(lc, dtype=F32)
    diff = i[:, None] - i[None, :]
    dmat = jnp.where(diff[None] >= 0, jnp.exp(diff[None] * log_g[:, None, None]), 0.0)
    q_dec = jnp.exp((i[:, None] + 1.0) * log_g[None, :])
    k_dec = jnp.exp((lc - 1.0 - i)[:, None] * log_g[None, :])
    c_dec = jnp.exp(lc * log_g)

    def chunks(t):
        return t.reshape(bsz, nc, lc, h, -1).swapaxes(0, 1)

    def step(rs, qkv):
        qc, kc, vc = qkv
        sc = jnp.einsum('blhd,bshd->bhls', qc, kc) * dmat
        intra = jnp.einsum('bhls,bshe->blhe', sc, vc)
        cross = jnp.einsum('blhd,bhde->blhe', qc, rs) * q_dec[None, :, :, None]
        r_new = rs * c_dec[None, :, None, None] + jnp.einsum('bshd,bshe->bhde', kc * k_dec[None, :, :, None], vc)
        return r_new, intra + cross

    r_fin, out = lax.scan(step, r0.astype(F32), (chunks(q), chunks(k), chunks(v)))
    return out.swapaxes(0, 1).reshape(bsz, s, h, RET_VAL_DIM), r_fin


def nsa_compress(kv_cmp, lp):
    bsz, l = kv_cmp.shape[:2]
    nbc = l // NSA_BLOCK
    blk = kv_cmp[:, :nbc * NSA_BLOCK].astype(F32).reshape(bsz, nbc, NSA_BLOCK, 2, NSA_KV_HEADS, NSA_HEAD_DIM)
    blk = blk + jnp.transpose(lp['cmp_pe'], (1, 0, 2))[None, None, :, :, None, :]
    hid = jax.nn.silu(jnp.einsum('bnlcgd,cldf->bncgf', blk, lp['cmp_w1']))
    out = jnp.einsum('bncgf,cfd->bncgd', hid, lp['cmp_w2'])
    return rms_norm(out[:, :, 0], lp['qk_gain'][4]), out[:, :, 1]


def nsa(q, kv_cmp, kv_sel, kv_win, gates, offset, lp):
    bsz, s = q.shape[:2]
    g, r, hd = NSA_KV_HEADS, NSA_REP, NSA_HEAD_DIM
    l = kv_cmp.shape[1]
    t = offset + jnp.arange(s)
    qg = q.reshape(bsz, s, g, r, hd) * (hd ** -0.5)
    slopes = alibi_slopes(NSA_HEADS).reshape(g, r)
    kc, vc = nsa_compress(kv_cmp, lp)
    nbc = kc.shape[1]
    ends = (jnp.arange(nbc) + 1) * NSA_BLOCK - 1
    dist_c = t[:, None] - ends[None, :]
    s_c = jnp.einsum('bsgrd,bngd->bgrsn', qg, kc) - slopes[None, :, :, None, None] * dist_c.astype(F32)
    p_c = masked_softmax(s_c, dist_c >= 0)
    o_c = jnp.einsum('bgrsn,bngd->bsgrd', p_c, vc)
    cur = t // NSA_BLOCK
    imp = jnp.where(jnp.arange(nbc)[None, :] < cur[:, None], p_c.sum(axis=2), -1.0)
    n_top = NSA_TOPN - 1
    if nbc < n_top:
        imp = jnp.pad(imp, ((0, 0), (0, 0), (0, 0), (0, n_top - nbc)), constant_values=-1.0)
    top_v, top_i = lax.top_k(imp, n_top)
    nb_all = -(-l // NSA_BLOCK)
    sel_idx = jnp.concatenate([jnp.broadcast_to(cur[None, None, :, None], (bsz, g, s, 1)), jnp.minimum(top_i, nb_all - 1)], axis=-1)
    sel_ok = jnp.concatenate([jnp.ones((bsz, g, s, 1), bool), top_v >= 0.0], axis=-1)
    sel_idx = sel_idx.transpose(0, 2, 1, 3)
    sel_ok = sel_ok.transpose(0, 2, 1, 3)
    kv_sel_p = jnp.pad(kv_sel, ((0, 0), (0, nb_all * NSA_BLOCK - l), (0, 0), (0, 0), (0, 0)))
    kb = kv_sel_p.reshape(bsz, nb_all, NSA_BLOCK, 2, g, hd).transpose(0, 4, 1, 2, 3, 5)
    b_ix = jnp.arange(bsz)[:, None, None, None]
    g_ix = jnp.arange(g)[None, None, :, None]
    qb = math.gcd(s, SEL_Q_BLOCK)
    nqb = s // qb

    def sel_block(args):
        q_b, idx_b, ok_b, t_b = args
        kv_b = kb[b_ix, g_ix, idx_b]
        pos = idx_b[..., None] * NSA_BLOCK + jnp.arange(NSA_BLOCK)
        dist = t_b[None, :, None, None, None] - pos
        mask = ok_b[..., None] & (dist >= 0)
        sc = jnp.einsum('bqgrd,bqgnkd->bqgrnk', q_b, kv_b[..., 0, :]) - slopes[None, None, :, :, None, None] * dist[:, :, :, None].astype(F32)
        shp = sc.shape
        pr = masked_softmax(sc.reshape(shp[0], shp[1], shp[2], shp[3], -1), mask.reshape(shp[0], shp[1], shp[2], 1, -1))
        return jnp.einsum('bqgrnk,bqgnkd->bqgrd', pr.reshape(shp), kv_b[..., 1, :])

    def to_blocks(a):
        return a.reshape((bsz, nqb, qb) + a.shape[2:]).swapaxes(0, 1)

    o_s = lax.map(sel_block, (to_blocks(qg), to_blocks(sel_idx), to_blocks(sel_ok), t.reshape(nqb, qb)))
    o_s = o_s.swapaxes(0, 1).reshape(bsz, s, g, r, hd)
    wq = math.gcd(s, WIN_Q_BLOCK)
    nqw = s // wq
    kidx = jnp.arange(nqw)[:, None] * wq + jnp.arange(NSA_WINDOW + wq)[None, :]
    kw = kv_win[:, kidx]
    kabs = offset - NSA_WINDOW + kidx
    dist_w = t.reshape(nqw, wq)[:, :, None] - kabs[:, None, :]
    mask_w = (kabs[:, None, :] >= 0) & (dist_w >= 0) & (dist_w <= NSA_WINDOW)
    qw = qg.reshape(bsz, nqw, wq, g, r, hd)
    s_w = jnp.einsum('bcqgrd,bckgd->bcgrqk', qw, kw[:, :, :, 0]) - slopes[None, None, :, :, None, None] * dist_w[None, :, None, None].astype(F32)
    p_w = masked_softmax(s_w, mask_w[None, :, None, None])
    o_w = jnp.einsum('bcgrqk,bckgd->bcqgrd', p_w, kw[:, :, :, 1]).reshape(bsz, s, g, r, hd)
    gt = jax.nn.sigmoid(gates.astype(F32)).reshape(bsz, s, 3, g, r, 1)
    o = gt[:, :, 0] * o_c + gt[:, :, 1] * o_s + gt[:, :, 2] * o_w
    return o.reshape(bsz, s, NSA_WIDTH)


def mixer(h, offset, lp, conv_prev, ssm_prev, ret_prev, cmp_past, sel_past, win_prev, win_keep):
    bsz, s, _ = h.shape
    proj = jnp.einsum('bsd,de->bse', h, lp['w_in'])
    z, xbc, dt_raw, nq, nkv, ngate, rq, rk, rv, rg = split_cols(proj, IN_SIZES)
    conv_in = jnp.concatenate([conv_prev.astype(F32), xbc.astype(F32)], axis=1)
    new_conv = conv_in[:, conv_in.shape[1] - (SSD_CONV - 1):]
    xbc = jax.nn.silu(lax.conv_general_dilated(conv_in, lp['conv_w'].astype(F32)[:, None, :], (1,), 'VALID', dimension_numbers=('NWC', 'WIO', 'NWC'), feature_group_count=SSD_CONV_DIM) + lp['conv_b'])
    xs, bm, cm = jnp.split(xbc, [SSD_INNER, SSD_INNER + SSD_GROUPS * SSD_STATE], axis=-1)
    xs = xs.reshape(bsz, s, SSD_HEADS, SSD_HEAD_DIM)
    dt = jax.nn.softplus(dt_raw.astype(F32) + lp['dt_bias'])
    a = -jnp.exp(lp['a_log'].astype(F32))
    y, new_ssm = ssd_chunked(xs, dt, a, bm.reshape(bsz, s, SSD_GROUPS, SSD_STATE), cm.reshape(bsz, s, SSD_GROUPS, SSD_STATE), ssm_prev)
    y = (y + lp['ssd_d'][:, None] * xs).reshape(bsz, s, SSD_INNER) * jax.nn.silu(z.astype(F32))
    y = rms_norm(y.reshape(bsz, s, SSD_GROUPS, -1)).reshape(bsz, s, SSD_INNER) * lp['ssd_norm']
    q = rms_norm(nq.reshape(bsz, s, NSA_HEADS, NSA_HEAD_DIM), lp['qk_gain'][0])
    nkv = nkv.reshape(bsz, s, 3, 2, NSA_KV_HEADS, NSA_HEAD_DIM)

    def branch_kv(i):
        return jnp.stack([rms_norm(nkv[:, :, i, 0], lp['qk_gain'][1 + i]), nkv[:, :, i, 1].astype(F32)], axis=2)

    new_cmp, new_sel, new_win = branch_kv(0), branch_kv(1), branch_kv(2)
    kv_cmp = jnp.concatenate([cmp_past, new_cmp], axis=1)
    kv_sel = jnp.concatenate([sel_past, new_sel], axis=1)
    kv_win = jnp.concatenate([win_prev, new_win], axis=1)
    o_nsa = nsa(q, kv_cmp, kv_sel, kv_win, ngate.reshape(bsz, s, 3, NSA_HEADS), offset, lp)
    new_win_buf = kv_win[:, kv_win.shape[1] - win_keep:]
    o_ret, new_ret = retention(rq.reshape(bsz, s, RET_HEADS, RET_KEY_DIM), rk.reshape(bsz, s, RET_HEADS, RET_KEY_DIM), rv.reshape(bsz, s, RET_HEADS, RET_VAL_DIM), ret_prev, offset)
    o_ret = rms_norm(o_ret).reshape(bsz, s, RET_WIDTH) * jax.nn.silu(rg.astype(F32))
    mix = jnp.concatenate([y, o_nsa, o_ret], axis=-1)
    out = jnp.einsum('bse,ed->bsd', mix, lp['w_out'])
    return out, (new_cmp, new_sel, new_win_buf, new_ssm, new_conv, new_ret)


def hier_moe(h, lp):
    bsz, s, d = h.shape
    hf = h.reshape(bsz * s, d)
    n_tok = hf.shape[0]
    epg = MOE_EXPERTS_PER_GROUP
    g_logit = (hf @ lp['router_group_w'] + lp['router_group_b']).astype(F32)
    g_prob = jax.nn.softmax(g_logit, axis=-1)
    g_idx = jnp.argmax(g_logit, axis=-1)
    g_p = jnp.max(g_prob, axis=-1)
    e_logit = (hf @ lp['router_expert_w'] + lp['router_expert_b']).astype(F32).reshape(n_tok, MOE_GROUPS, epg)
    e_in = e_logit[jnp.arange(n_tok), g_idx]
    top_v, top_i = lax.top_k(e_in, MOE_TOPK)
    w = jax.nn.softmax(top_v, axis=-1) * g_p[:, None]
    ew = jnp.einsum('tk,tke->te', w, jax.nn.one_hot(top_i, epg, dtype=F32))
    comb = jax.nn.one_hot(g_idx, MOE_GROUPS, dtype=F32)[:, :, None] * ew[:, None, :]
    out = jnp.zeros((n_tok, d), F32)
    for grp in range(MOE_GROUPS):
        sl = slice(grp * epg, (grp + 1) * epg)
        a = jnp.einsum('td,edf->tef', hf, lp['expert_w_gate'][sl])
        u = jnp.einsum('td,edf->tef', hf, lp['expert_w_up'][sl])
        hid = jax.nn.silu(a) * u * comb[:, grp, :, None]
        out = out + jnp.einsum('tef,efd->td', hid, lp['expert_w_down'][sl])
    return out.reshape(bsz, s, d)


def layer(x, c, lp, offset, init, win_keep):
    mod = jnp.einsum('bd,de->be', jax.nn.silu(c.astype(F32)), lp['w_ada']) + lp['b_ada']
    sh_a, sc_a, g_a, sh_f, sc_f, g_f = jnp.split(mod[:, None, :], 6, axis=-1)
    h = rms_norm(x, lp['norm_mix']) * (1.0 + sc_a) + sh_a
    mix, new = mixer(h, offset, lp, init[0], init[1], init[2], init[3], init[4], init[5], win_keep)
    x1 = x.astype(F32) + g_a * mix
    h2 = rms_norm(x1, lp['norm_ffn']) * (1.0 + sc_f) + sh_f
    x2 = x1 + g_f * hier_moe(h2, lp)
    return x2.astype(x.dtype), new


def setup_inputs(seed: int = 0) -> dict:
    keys = iter(jax.random.split(jax.random.key(seed), 48))

    def nrm(shape, scale):
        return jax.random.normal(next(keys), shape, F32) * scale

    n_pages = PAST_LEN // PAGE_SIZE
    n_used = DEC_BATCH * n_pages
    n_pool = n_used + (n_used + 3) // 4
    w_buf = min(NSA_WINDOW, PAST_LEN)
    kv_row = (2, NSA_KV_HEADS, NSA_HEAD_DIM)
    page_table = jax.random.permutation(next(keys), n_pool)[:n_used].reshape(DEC_BATCH, n_pages).astype(jnp.int32)
    dt = jnp.exp(jax.random.uniform(next(keys), (DEPTH, SSD_HEADS), F32, math.log(1e-3), math.log(1e-1)))
    dt_bias = dt + jnp.log(-jnp.expm1(-dt))
    a_log = jnp.log(jax.random.uniform(next(keys), (DEPTH, SSD_HEADS), F32, 1.0, 16.0))
    return {
        'x_prompt': nrm((BATCH, SEQ, D_MODEL), 1.0),
        'x_sample': nrm((DEC_BATCH, DEC_SEQ, D_MODEL), 1.0),
        'cache_cmp_kv': nrm((DEPTH, n_pool, PAGE_SIZE) + kv_row, 1.0),
        'cache_sel_kv': nrm((DEPTH, n_pool, PAGE_SIZE) + kv_row, 1.0),
        'cache_win_kv': nrm((DEPTH, DEC_BATCH, w_buf) + kv_row, 1.0),
        'state_ssm': nrm((DEPTH, DEC_BATCH, SSD_HEADS, SSD_HEAD_DIM, SSD_STATE), 0.1),
        'state_conv': nrm((DEPTH, DEC_BATCH, SSD_CONV - 1, SSD_CONV_DIM), 1.0),
        'state_ret': nrm((DEPTH, DEC_BATCH, RET_HEADS, RET_KEY_DIM, RET_VAL_DIM), 0.5),
        'page_table': page_table,
        'c_prompt': nrm((BATCH, D_MODEL), 1.0),
        'c_sample': nrm((DEC_BATCH, D_MODEL), 1.0),
        'w_ada': nrm((DEPTH, D_MODEL, 6 * D_MODEL), 0.2 * D_MODEL ** -0.5),
        'b_ada': nrm((DEPTH, 6 * D_MODEL), 0.02),
        'norm_mix': 1.0 + nrm((DEPTH, D_MODEL), 0.02),
        'norm_ffn': 1.0 + nrm((DEPTH, D_MODEL), 0.02),
        'w_in': nrm((DEPTH, D_MODEL, IN_COLS), D_MODEL ** -0.5),
        'w_out': nrm((DEPTH, D_MIX, D_MODEL), D_MIX ** -0.5),
        'conv_w': nrm((DEPTH, SSD_CONV, SSD_CONV_DIM), SSD_CONV ** -0.5),
        'conv_b': nrm((DEPTH, SSD_CONV_DIM), 0.02),
        'dt_bias': dt_bias,
        'a_log': a_log,
        'ssd_d': 1.0 + nrm((DEPTH, SSD_HEADS), 0.02),
        'ssd_norm': 1.0 + nrm((DEPTH, SSD_INNER), 0.02),
        'qk_gain': 1.0 + nrm((DEPTH, 5, NSA_HEAD_DIM), 0.02),
        'cmp_pe': nrm((DEPTH, 2, NSA_BLOCK, NSA_HEAD_DIM), 0.02),
        'cmp_w1': nrm((DEPTH, 2, NSA_BLOCK, NSA_HEAD_DIM, NSA_CMP_HIDDEN), (NSA_BLOCK * NSA_HEAD_DIM) ** -0.5),
        'cmp_w2': nrm((DEPTH, 2, NSA_CMP_HIDDEN, NSA_HEAD_DIM), NSA_CMP_HIDDEN ** -0.5),
        'router_group_w': nrm((DEPTH, D_MODEL, MOE_GROUPS), D_MODEL ** -0.5),
        'router_group_b': nrm((DEPTH, MOE_GROUPS), 0.01),
        'router_expert_w': nrm((DEPTH, D_MODEL, MOE_EXPERTS), D_MODEL ** -0.5),
        'router_expert_b': nrm((DEPTH, MOE_EXPERTS), 0.01),
        'expert_w_gate': nrm((DEPTH, MOE_EXPERTS, D_MODEL, MOE_HIDDEN), D_MODEL ** -0.5),
        'expert_w_up': nrm((DEPTH, MOE_EXPERTS, D_MODEL, MOE_HIDDEN), D_MODEL ** -0.5),
        'expert_w_down': nrm((DEPTH, MOE_EXPERTS, MOE_HIDDEN, D_MODEL), MOE_HIDDEN ** -0.5),
    }


def reference(x_prompt, x_sample, cache_cmp_kv, cache_sel_kv, cache_win_kv, state_ssm, state_conv, state_ret, page_table, c_prompt, c_sample, w_ada, b_ada, norm_mix, norm_ffn, w_in, w_out, conv_w, conv_b, dt_bias, a_log, ssd_d, ssd_norm, qk_gain, cmp_pe, cmp_w1, cmp_w2, router_group_w, router_group_b, router_expert_w, router_expert_b, expert_w_gate, expert_w_up, expert_w_down):
    bp, sp = x_prompt.shape[:2]
    bd = x_sample.shape[0]
    n_pages = page_table.shape[1]
    past = n_pages * PAGE_SIZE
    w_buf = cache_win_kv.shape[2]
    kv_row = (2, NSA_KV_HEADS, NSA_HEAD_DIM)
    xp, xs = x_prompt, x_sample
    news_p, news_s = [], []
    for li in range(DEPTH):
        lp = {
            'w_ada': w_ada[li], 'b_ada': b_ada[li], 'norm_mix': norm_mix[li], 'norm_ffn': norm_ffn[li],
            'w_in': w_in[li], 'w_out': w_out[li], 'conv_w': conv_w[li], 'conv_b': conv_b[li],
            'dt_bias': dt_bias[li], 'a_log': a_log[li], 'ssd_d': ssd_d[li], 'ssd_norm': ssd_norm[li],
            'qk_gain': qk_gain[li], 'cmp_pe': cmp_pe[li], 'cmp_w1': cmp_w1[li], 'cmp_w2': cmp_w2[li],
            'router_group_w': router_group_w[li], 'router_group_b': router_group_b[li],
            'router_expert_w': router_expert_w[li], 'router_expert_b': router_expert_b[li],
            'expert_w_gate': expert_w_gate[li], 'expert_w_up': expert_w_up[li], 'expert_w_down': expert_w_down[li],
        }
        init_p = (jnp.zeros((bp, SSD_CONV - 1, SSD_CONV_DIM), F32),
                  jnp.zeros((bp, SSD_HEADS, SSD_HEAD_DIM, SSD_STATE), F32),
                  jnp.zeros((bp, RET_HEADS, RET_KEY_DIM, RET_VAL_DIM), F32),
                  jnp.zeros((bp, 0) + kv_row, F32),
                  jnp.zeros((bp, 0) + kv_row, F32),
                  jnp.zeros((bp, NSA_WINDOW) + kv_row, F32))
        xp, new_p = layer(xp, c_prompt, lp, 0, init_p, min(NSA_WINDOW, sp))
        news_p.append(new_p)
        cmp_past = cache_cmp_kv[li][page_table].reshape((bd, past) + kv_row)
        sel_past = cache_sel_kv[li][page_table].reshape((bd, past) + kv_row)
        win_prev = jnp.pad(cache_win_kv[li], ((0, 0), (NSA_WINDOW - w_buf, 0), (0, 0), (0, 0), (0, 0)))
        init_s = (state_conv[li], state_ssm[li], state_ret[li], cmp_past, sel_past, win_prev)
        xs, new_s = layer(xs, c_sample, lp, past, init_s, w_buf)
        news_s.append(new_s)
    cmp_p = jnp.stack([n[0] for n in news_p])
    cmp_s = jnp.stack([n[0] for n in news_s])
    sel_p = jnp.stack([n[1] for n in news_p])
    sel_s = jnp.stack([n[1] for n in news_s])
    win_p = jnp.stack([n[2] for n in news_p])
    win_s = jnp.stack([n[2] for n in news_s])
    ssm_p = jnp.stack([n[3] for n in news_p])
    ssm_s = jnp.stack([n[3] for n in news_s])
    conv_p = jnp.stack([n[4] for n in news_p])
    conv_s = jnp.stack([n[4] for n in news_s])
    ret_p = jnp.stack([n[5] for n in news_p])
    ret_s = jnp.stack([n[5] for n in news_s])
    return (xp, xs, cmp_p, cmp_s, sel_p, sel_s, win_p, win_s, ssm_p, ssm_s, conv_p, conv_s, ret_p, ret_s)
```

```python
import functools
import math

import numpy as np
import jax
import jax.numpy as jnp
from jax import lax
from jax.experimental import pallas as pl
from jax.experimental.pallas import tpu as pltpu

F32 = jnp.float32
BF16 = jnp.bfloat16
HI = lax.Precision.HIGHEST
EPS = 1e-6
NEG_INF = -1e30

D_MODEL = 2048
DEPTH = 2
PAGE_SIZE = 128
SSD_INNER = D_MODEL // 2
SSD_HEAD_DIM = 64
SSD_HEADS = SSD_INNER // SSD_HEAD_DIM
SSD_GROUPS = 2
SSD_STATE = 128
SSD_CONV = 4
SSD_CONV_DIM = SSD_INNER + 2 * SSD_GROUPS * SSD_STATE
NSA_WIDTH = D_MODEL // 4
NSA_HEAD_DIM = 64
NSA_HEADS = NSA_WIDTH // NSA_HEAD_DIM
NSA_KV_HEADS = 2
NSA_REP = NSA_HEADS // NSA_KV_HEADS
NSA_BLOCK = 64
NSA_TOPN = 16
NSA_WINDOW = 512
NSA_CMP_HIDDEN = 128
RET_WIDTH = D_MODEL // 4
RET_HEADS = 4
RET_VAL_DIM = RET_WIDTH // RET_HEADS
RET_KEY_DIM = RET_VAL_DIM // 2
D_MIX = SSD_INNER + NSA_WIDTH + RET_WIDTH
MOE_GROUPS = 4
MOE_EXPERTS_PER_GROUP = 8
MOE_EXPERTS = MOE_GROUPS * MOE_EXPERTS_PER_GROUP
MOE_TOPK = 2
MOE_HIDDEN = D_MODEL // 4
KV_ROW = 2 * NSA_KV_HEADS * NSA_HEAD_DIM

LANE = 128
SUBLANE = 8
C_Z, C_NQ, C_RV, C_RG, C_RQ, C_RK, C_XBC, C_NKV, C_DT, C_NG = 0, 1024, 1536, 2048, 2560, 2816, 3072, 4608, 5376, 5504
N_PROJ = 5632
LC = 128
TQ = 128
TM_MOE = 256
VMEM_LIMIT = 56 * 2 ** 20


def _cparams(*sem):
    return pltpu.CompilerParams(dimension_semantics=sem, vmem_limit_bytes=VMEM_LIMIT)


def _silu(x):
    return x * jax.nn.sigmoid(x)


def _nt(a, b, precision=None):
    return lax.dot_general(a, b, (((1,), (1,)), ((), ())), precision=precision, preferred_element_type=F32)


def _tn(a, b, precision=None):
    return lax.dot_general(a, b, (((0,), (0,)), ((), ())), precision=precision, preferred_element_type=F32)


def _iota(shape, dim):
    return lax.broadcasted_iota(jnp.int32, shape, dim)


def _seg_sum(x2, seg):
    n = x2.shape[-1]
    bd = (_iota((n, n), 0) // seg == _iota((n, n), 1) // seg).astype(BF16)
    hi = x2.astype(BF16)
    lo = (x2 - hi.astype(F32)).astype(BF16)
    return jnp.dot(hi, bd, preferred_element_type=F32) + jnp.dot(lo, bd, preferred_element_type=F32)


def _ada_kernel(c_ref, w_ref, b_ref, o_ref):
    o_ref[0] = jnp.dot(_silu(c_ref[...]), w_ref[0], precision=HI, preferred_element_type=F32) + b_ref[0]


def _ada_call(c_all, w_ada, b_ada):
    rows = c_all.shape[0]
    tn = 1024
    return pl.pallas_call(
        _ada_kernel,
        grid=(DEPTH, 6 * D_MODEL // tn),
        in_specs=[pl.BlockSpec((rows, D_MODEL), lambda l, j: (0, 0)),
                  pl.BlockSpec((1, D_MODEL, tn), lambda l, j: (l, 0, j)),
                  pl.BlockSpec((1, 1, tn), lambda l, j: (l, 0, j))],
        out_specs=pl.BlockSpec((1, rows, tn), lambda l, j: (l, 0, j)),
        out_shape=jax.ShapeDtypeStruct((DEPTH, rows, 6 * D_MODEL), F32),
        compiler_params=_cparams("parallel", "parallel"),
        name="ada",
    )(c_all, w_ada, b_ada.reshape(DEPTH, 1, 6 * D_MODEL))


def _mod_norm(x, gain, scale, shift):
    ms = jnp.mean(x * x, axis=-1, keepdims=True)
    return x * lax.rsqrt(ms + EPS) * gain * (1.0 + scale) + shift


def _inproj_kernel(x_ref, sc_ref, sh_ref, g_ref, w_ref, o_ref, h_ref):
    @pl.when(pl.program_id(2) == 0)
    def _():
        h_ref[...] = _mod_norm(x_ref[0], g_ref[...], sc_ref[0], sh_ref[0]).astype(BF16)

    o_ref[0] = jnp.dot(h_ref[...], w_ref[...], preferred_element_type=F32)


def _inproj_call(x, scale, shift, gain, w_bf):
    b, s, d = x.shape
    ts = min(s, 1024)
    tn = 512
    sm = scale.shape[1]
    tsm = 1 if sm == 1 else ts
    mod_map = (lambda bi, i, j: (bi, 0, 0)) if sm == 1 else (lambda bi, i, j: (bi, i, 0))
    return pl.pallas_call(
        _inproj_kernel,
        grid=(b, s // ts, N_PROJ // tn),
        in_specs=[pl.BlockSpec((1, ts, d), lambda bi, i, j: (bi, i, 0)),
                  pl.BlockSpec((1, tsm, d), mod_map),
                  pl.BlockSpec((1, tsm, d), mod_map),
                  pl.BlockSpec((1, d), lambda bi, i, j: (0, 0)),
                  pl.BlockSpec((d, tn), lambda bi, i, j: (0, j))],
        out_specs=pl.BlockSpec((1, ts, tn), lambda bi, i, j: (bi, i, j)),
        out_shape=jax.ShapeDtypeStruct((b, s, N_PROJ), F32),
        scratch_shapes=[pltpu.VMEM((ts, d), BF16)],
        compiler_params=_cparams("parallel", "parallel", "arbitrary"),
        name="in_proj",
    )(x, scale, shift, gain.reshape(1, d), w_bf)


def _ssd_kernel(n_valid, xbc_ref, z_ref, dt_ref, cprev_ref, sprev_ref, cw_ref, cb_ref, dtb_ref, alog_ref, dsk_ref,
                nrm_ref, y_ref, snew_ref, cnew_ref, cbuf, state, ybuf):
    c = pl.program_id(1)
    nc = pl.num_programs(1)
    hd, nst = SSD_HEAD_DIM, SSD_STATE
    hpg = SSD_HEADS // SSD_GROUPS

    @pl.when(c == 0)
    def _():
        cbuf[0:SUBLANE, :] = jnp.zeros((SUBLANE, SSD_CONV_DIM), F32)
        cbuf[5:8, :] = cprev_ref[0]
        state[...] = sprev_ref[0]

    cbuf[8:8 + LC, :] = xbc_ref[0]
    conv = cb_ref[...] + cw_ref[0:1, :] * cbuf[5:5 + LC, :]
    for k in range(1, SSD_CONV):
        conv = conv + cw_ref[k:k + 1, :] * cbuf[5 + k:5 + k + LC, :]
    tail = cbuf[5 + n_valid:8 + n_valid, :]
    cbuf[5:8, :] = tail
    xbc = _silu(conv)
    xs = xbc[:, :SSD_INNER]
    bm = xbc[:, SSD_INNER:SSD_INNER + SSD_GROUPS * nst].astype(BF16)
    cm = xbc[:, SSD_INNER + SSD_GROUPS * nst:].astype(BF16)

    dtr = dt_ref[0] + dtb_ref[...]
    dt = jnp.maximum(dtr, 0.0) + jnp.log1p(jnp.exp(-jnp.abs(dtr)))
    if n_valid < LC:
        dt = jnp.where(_iota((LC, LANE), 0) < n_valid, dt, 0.0)
    a = -jnp.exp(alog_ref[...])
    da = dt * a
    row = _iota((LC, LC), 0)
    col = _iota((LC, LC), 1)
    tril = row >= col
    acs = jnp.dot(tril.astype(F32), da, precision=HI, preferred_element_type=F32)
    eye = (row == col).astype(F32)
    acs_t = _nt(eye, acs, precision=HI)
    expand = (_iota((LANE, SSD_INNER), 0) == _iota((LANE, SSD_INNER), 1) // hd).astype(F32)
    e_acs = jnp.exp(acs)
    decay = jnp.exp(acs[LC - 1:LC, :] - acs)
    dt_x = jnp.dot(dt, expand, precision=HI, preferred_element_type=F32)
    e_acs_x = jnp.dot(e_acs, expand, precision=HI, preferred_element_type=F32)
    decay_x = jnp.dot(decay, expand, precision=HI, preferred_element_type=F32)
    xdt = xs * dt_x
    xdec = (xdt * decay_x).astype(BF16)
    xdt_b = xdt.astype(BF16)
    cbs = [_nt(cm[:, g * nst:(g + 1) * nst], bm[:, g * nst:(g + 1) * nst]) for g in range(SSD_GROUPS)]
    for h in range(SSD_HEADS):
        g = h // hpg
        lmat = jnp.exp(jnp.where(tril, acs[:, h:h + 1] - acs_t[h:h + 1, :], -jnp.inf))
        m = (cbs[g] * lmat).astype(BF16)
        y_diag = jnp.dot(m, xdt_b[:, h * hd:(h + 1) * hd], preferred_element_type=F32)
        st = state[h]
        y_off = _nt(cm[:, g * nst:(g + 1) * nst], st.astype(BF16)) * e_acs_x[:, h * hd:(h + 1) * hd]
        state[h] = st * e_acs[LC - 1:LC, h:h + 1] + _tn(xdec[:, h * hd:(h + 1) * hd], bm[:, g * nst:(g + 1) * nst])
        ybuf[:, h * hd:(h + 1) * hd] = y_diag + y_off
    y = (ybuf[...] + dsk_ref[...] * xs) * _silu(z_ref[0])
    gw = SSD_INNER // SSD_GROUPS
    for g in range(SSD_GROUPS):
        seg = y[:, g * gw:(g + 1) * gw]
        ms = jnp.mean(seg * seg, axis=-1, keepdims=True)
        y_ref[0, :, g * gw:(g + 1) * gw] = seg * lax.rsqrt(ms + EPS) * nrm_ref[:, g * gw:(g + 1) * gw]

    @pl.when(c == nc - 1)
    def _():
        snew_ref[0] = state[...]
        cnew_ref[0] = tail


def _ssd_call(proj, n_valid, conv_prev, ssm_prev, conv_w, conv_b, dt_bias, a_log, ssd_d, ssd_norm):
    b, sp, _ = proj.shape
    nc = sp // LC
    pad = lambda v: jnp.pad(v, (0, LANE - SSD_HEADS)).reshape(1, LANE)
    const2 = lambda bi, c: (0, 0)
    return pl.pallas_call(
        functools.partial(_ssd_kernel, n_valid),
        grid=(b, nc),
        in_specs=[pl.BlockSpec((1, LC, SSD_CONV_DIM), lambda bi, c: (bi, c, C_XBC // SSD_CONV_DIM)),
                  pl.BlockSpec((1, LC, SSD_INNER), lambda bi, c: (bi, c, C_Z // SSD_INNER)),
                  pl.BlockSpec((1, LC, LANE), lambda bi, c: (bi, c, C_DT // LANE)),
                  pl.BlockSpec((1, SSD_CONV - 1, SSD_CONV_DIM), lambda bi, c: (bi, 0, 0)),
                  pl.BlockSpec((1, SSD_HEADS, SSD_HEAD_DIM, SSD_STATE), lambda bi, c: (bi, 0, 0, 0)),
                  pl.BlockSpec((SSD_CONV, SSD_CONV_DIM), const2),
                  pl.BlockSpec((1, SSD_CONV_DIM), const2),
                  pl.BlockSpec((1, LANE), const2),
                  pl.BlockSpec((1, LANE), const2),
                  pl.BlockSpec((1, SSD_INNER), const2),
                  pl.BlockSpec((1, SSD_INNER), const2)],
        out_specs=[pl.BlockSpec((1, LC, SSD_INNER), lambda bi, c: (bi, c, 0)),
                   pl.BlockSpec((1, SSD_HEADS, SSD_HEAD_DIM, SSD_STATE), lambda bi, c: (bi, 0, 0, 0)),
                   pl.BlockSpec((1, SSD_CONV - 1, SSD_CONV_DIM), lambda bi, c: (bi, 0, 0))],
        out_shape=[jax.ShapeDtypeStruct((b, sp, SSD_INNER), F32),
                   jax.ShapeDtypeStruct((b, SSD_HEADS, SSD_HEAD_DIM, SSD_STATE), F32),
                   jax.ShapeDtypeStruct((b, SSD_CONV - 1, SSD_CONV_DIM), F32)],
        scratch_shapes=[pltpu.VMEM((8 + LC, SSD_CONV_DIM), F32),
                        pltpu.VMEM((SSD_HEADS, SSD_HEAD_DIM, SSD_STATE), F32),
                        pltpu.VMEM((LC, SSD_INNER), F32)],
        compiler_params=_cparams("parallel", "arbitrary"),
        name="ssd",
    )(proj, proj, proj, conv_prev, ssm_prev, conv_w, conv_b.reshape(1, -1), pad(dt_bias), pad(a_log),
      jnp.repeat(ssd_d, SSD_HEAD_DIM).reshape(1, -1), ssd_norm.reshape(1, -1))


def _ret_log_g(h):
    return float(np.log1p(-np.exp2(np.float32(-5.0 - h)), dtype=np.float32))


def _ret_kernel(n_valid, q_ref, k_ref, v_ref, g_ref, cos_ref, sin_ref, rprev_ref, o_ref, rnew_ref, state):
    c = pl.program_id(1)
    nc = pl.num_programs(1)
    kd, vd = RET_KEY_DIM, RET_VAL_DIM
    half = kd // 2

    @pl.when(c == 0)
    def _():
        state[...] = rprev_ref[0]

    cos = cos_ref[...]
    sin = sin_ref[...]
    first_half = (_iota((LC, RET_HEADS * kd), 1) % kd) < half

    def rot(x):
        n = x.shape[-1]
        swapped = jnp.where(first_half, pltpu.roll(x, n - half, 1), pltpu.roll(x, half, 1))
        return x * cos + swapped * sin

    q = rot(q_ref[0])
    k = rot(k_ref[0]) * (kd ** -0.5)
    v = v_ref[0]
    gate = g_ref[0]
    ti = _iota((LC, LC), 0)
    tj = _iota((LC, LC), 1)
    diff = (ti - tj).astype(F32)
    ipos = _iota((LC, 1), 0)
    i1 = ipos.astype(F32)
    for h in range(RET_HEADS):
        lg = _ret_log_g(h)
        dmat = jnp.where(diff >= 0, jnp.exp(diff * lg), 0.0)
        q_dec = jnp.exp((i1 + 1.0) * lg)
        k_dec = jnp.where(ipos < n_valid, jnp.exp((n_valid - 1.0 - i1) * lg), 0.0)
        c_dec = math.exp(n_valid * lg)
        qh = q[:, h * kd:(h + 1) * kd].astype(BF16)
        kh = k[:, h * kd:(h + 1) * kd]
        vh = v[:, h * vd:(h + 1) * vd].astype(BF16)
        sc = _nt(qh, kh.astype(BF16)) * dmat
        intra = jnp.dot(sc.astype(BF16), vh, preferred_element_type=F32)
        rs = state[h]
        cross = jnp.dot(qh, rs.astype(BF16), preferred_element_type=F32) * q_dec
        state[h] = rs * c_dec + _tn((kh * k_dec).astype(BF16), vh)
        o = intra + cross
        ms = jnp.mean(o * o, axis=-1, keepdims=True)
        o_ref[0, :, h * vd:(h + 1) * vd] = o * lax.rsqrt(ms + EPS) * _silu(gate[:, h * vd:(h + 1) * vd])

    @pl.when(c == nc - 1)
    def _():
        rnew_ref[0] = state[...]


def _rotary_tables(offset, s_real, s_pad):
    half = RET_KEY_DIM // 2
    freqs = 1.0 / (10000.0 ** jnp.linspace(0.0, 1.0, half, dtype=F32))
    pos = (offset + jnp.arange(s_real)).astype(F32)
    ang = pos[:, None] * freqs[None, :]
    cos, sin = jnp.cos(ang), jnp.sin(ang)
    cos = jnp.tile(jnp.concatenate([cos, cos], axis=-1), (1, RET_HEADS))
    sin = jnp.tile(jnp.concatenate([-sin, sin], axis=-1), (1, RET_HEADS))
    padr = ((0, s_pad - s_real), (0, 0))
    return jnp.pad(cos, padr), jnp.pad(sin, padr)


def _ret_call(proj, n_valid, ret_prev, cos, sin):
    b, sp, _ = proj.shape
    nc = sp // LC
    qw = RET_HEADS * RET_KEY_DIM
    return pl.pallas_call(
        functools.partial(_ret_kernel, n_valid),
        grid=(b, nc),
        in_specs=[pl.BlockSpec((1, LC, qw), lambda bi, c: (bi, c, C_RQ // qw)),
                  pl.BlockSpec((1, LC, qw), lambda bi, c: (bi, c, C_RK // qw)),
                  pl.BlockSpec((1, LC, RET_WIDTH), lambda bi, c: (bi, c, C_RV // RET_WIDTH)),
                  pl.BlockSpec((1, LC, RET_WIDTH), lambda bi, c: (bi, c, C_RG // RET_WIDTH)),
                  pl.BlockSpec((LC, qw), lambda bi, c: (c, 0)),
                  pl.BlockSpec((LC, qw), lambda bi, c: (c, 0)),
                  pl.BlockSpec((1, RET_HEADS, RET_KEY_DIM, RET_VAL_DIM), lambda bi, c: (bi, 0, 0, 0))],
        out_specs=[pl.BlockSpec((1, LC, RET_WIDTH), lambda bi, c: (bi, c, 0)),
                   pl.BlockSpec((1, RET_HEADS, RET_KEY_DIM, RET_VAL_DIM), lambda bi, c: (bi, 0, 0, 0))],
        out_shape=[jax.ShapeDtypeStruct((b, sp, RET_WIDTH), F32),
                   jax.ShapeDtypeStruct((b, RET_HEADS, RET_KEY_DIM, RET_VAL_DIM), F32)],
        scratch_shapes=[pltpu.VMEM((RET_HEADS, RET_KEY_DIM, RET_VAL_DIM), F32)],
        compiler_params=_cparams("parallel", "arbitrary"),
        name="retention",
    )(proj, proj, proj, proj, cos, sin, ret_prev)


def _slope(h):
    return float(2.0 ** -(h + 1))


def _prep_kernel(nq_ref, nkv_ref, gq_ref, gk_ref, isk_ref, qn_ref, kvn_ref, kvb_ref):
    inv = 1.0 / NSA_HEAD_DIM
    nq = nq_ref[0]
    qn_ref[0] = nq * lax.rsqrt(_seg_sum(nq * nq, NSA_HEAD_DIM) * inv + EPS) * gq_ref[...]
    kv = nkv_ref[0]
    normed = kv * lax.rsqrt(_seg_sum(kv * kv, NSA_HEAD_DIM) * inv + EPS) * gk_ref[...]
    kvn = jnp.where(isk_ref[...] > 0.5, normed, kv)
    kvn_ref[0] = kvn
    kvb_ref[0] = kvn[:, KV_ROW:].astype(BF16)


def _prep_call(proj, s_real, qk_gain):
    b = proj.shape[0]
    ts = min(s_real, 512)
    hd = NSA_HEAD_DIM
    gq = (jnp.tile(qk_gain[0], NSA_HEADS) * (hd ** -0.5)).reshape(1, -1)
    ones = jnp.ones((2 * hd,), F32)
    gk = jnp.concatenate([jnp.concatenate([jnp.tile(qk_gain[1 + i], 2), ones]) for i in range(3)]).reshape(1, -1)
    isk = jnp.tile(jnp.concatenate([ones, 0.0 * ones]), 3).reshape(1, -1)
    const2 = lambda bi, i: (0, 0)
    return pl.pallas_call(
        _prep_kernel,
        grid=(b, s_real // ts),
        in_specs=[pl.BlockSpec((1, ts, NSA_WIDTH), lambda bi, i: (bi, i, C_NQ // NSA_WIDTH)),
                  pl.BlockSpec((1, ts, 3 * KV_ROW), lambda bi, i: (bi, i, C_NKV // (3 * KV_ROW))),
                  pl.BlockSpec((1, NSA_WIDTH), const2),
                  pl.BlockSpec((1, 3 * KV_ROW), const2),
                  pl.BlockSpec((1, 3 * KV_ROW), const2)],
        out_specs=[pl.BlockSpec((1, ts, NSA_WIDTH), lambda bi, i: (bi, i, 0)),
                   pl.BlockSpec((1, ts, 3 * KV_ROW), lambda bi, i: (bi, i, 0)),
                   pl.BlockSpec((1, ts, 2 * KV_ROW), lambda bi, i: (bi, i, 0))],
        out_shape=[jax.ShapeDtypeStruct((b, s_real, NSA_WIDTH), F32),
                   jax.ShapeDtypeStruct((b, s_real, 3 * KV_ROW), F32),
                   jax.ShapeDtypeStruct((b, s_real, 2 * KV_ROW), BF16)],
        compiler_params=_cparams("parallel", "parallel"),
        name="nsa_prep",
    )(proj, proj, gq, gk, isk)


def _compress_weights(cmp_pe, cmp_w1, cmp_w2, gain4):
    hd, hid = NSA_HEAD_DIM, NSA_CMP_HIDDEN
    wbig = jnp.zeros((NSA_BLOCK, KV_ROW, 4 * hid), BF16)
    w2big = jnp.zeros((4 * hid, KV_ROW), BF16)
    for cg in range(4):
        c = cg // 2
        wbig = wbig.at[:, cg * hd:(cg + 1) * hd, cg * hid:(cg + 1) * hid].set(cmp_w1[c].astype(BF16))
        w2big = w2big.at[cg * hid:(cg + 1) * hid, cg * hd:(cg + 1) * hd].set(cmp_w2[c].astype(BF16))
    pe_row = jnp.concatenate([cmp_pe[0], cmp_pe[0], cmp_pe[1], cmp_pe[1]], axis=-1)
    g4 = jnp.concatenate([jnp.tile(gain4, 2), jnp.ones((2 * hd,), F32)]).reshape(1, -1)
    return pe_row, wbig, w2big, g4


def _compress_core(get_rows, nblk, pe_ref, wbig_ref, w2_ref, g4_ref):
    acc = jnp.zeros((nblk, 4 * NSA_CMP_HIDDEN), F32)
    for l in range(NSA_BLOCK):
        x = get_rows(l) + pe_ref[l:l + 1, :]
        acc = acc + jnp.dot(x.astype(BF16), wbig_ref[l], preferred_element_type=F32)
    out = jnp.dot(_silu(acc).astype(BF16), w2_ref[...], preferred_element_type=F32)
    ss = _seg_sum(out * out, NSA_HEAD_DIM)
    normed = out * lax.rsqrt(ss * (1.0 / NSA_HEAD_DIM) + EPS) * g4_ref[...]
    return jnp.where(_iota(out.shape, 1) < 2 * NSA_HEAD_DIM, normed, out)


def _cmp_prompt_kernel(nblk, kv_ref, pe_ref, wbig_ref, w2_ref, g4_ref, o_ref):
    o_ref[0] = _compress_core(lambda l: kv_ref[0, :, l, :], nblk, pe_ref, wbig_ref, w2_ref, g4_ref)


def _cmp_prompt_call(kvn, cw):
    b, s, _ = kvn.shape
    nblk = s // NSA_BLOCK
    pe_row, wbig, w2big, g4 = cw
    kv4 = kvn.reshape(b, nblk, NSA_BLOCK, 3 * KV_ROW)
    return pl.pallas_call(
        functools.partial(_cmp_prompt_kernel, nblk),
        grid=(b,),
        in_specs=[pl.BlockSpec((1, nblk, NSA_BLOCK, KV_ROW), lambda bi: (bi, 0, 0, 0)),
                  pl.BlockSpec(pe_row.shape, lambda bi: (0, 0)),
                  pl.BlockSpec(wbig.shape, lambda bi: (0, 0, 0)),
                  pl.BlockSpec(w2big.shape, lambda bi: (0, 0)),
                  pl.BlockSpec(g4.shape, lambda bi: (0, 0))],
        out_specs=pl.BlockSpec((1, nblk, KV_ROW), lambda bi: (bi, 0, 0)),
        out_shape=jax.ShapeDtypeStruct((b, nblk, KV_ROW), F32),
        compiler_params=_cparams("arbitrary"),
        name="nsa_cmp_prompt",
    )(kv4, pe_row, wbig, w2big, g4)


def _masked_softmax(s, mask):
    s = jnp.where(mask, s, NEG_INF)
    p = jnp.exp(s - jnp.max(s, axis=-1, keepdims=True))
    return p / jnp.sum(p, axis=-1, keepdims=True) * mask.astype(F32)


def _select_blocks(imp, blk, n_top):
    sel = jnp.zeros(imp.shape, F32)
    big = imp.shape[-1]
    for _ in range(n_top):
        m = jnp.max(imp, axis=-1, keepdims=True)
        idx = jnp.min(jnp.where(imp == m, blk, big), axis=-1, keepdims=True)
        hit = blk == idx
        sel = jnp.where(hit & (m >= 0.0), 1.0, sel)
        imp = jnp.where(hit, -2.0, imp)
    return sel


def _nsa_prompt_kernel(nbc, s_len, tk, wk, q_ref, gate_ref, cmp_ref, ksel_ref, kwin_ref, o_ref, selx_ref):
    qi = pl.program_id(1)
    hd, rep, nh = NSA_HEAD_DIM, NSA_REP, NSA_HEADS
    q = q_ref[0]
    qb = q.astype(BF16)
    gates = jax.nn.sigmoid(gate_ref[0])
    t = qi * TQ + _iota((TQ, 1), 0)
    kcvc = cmp_ref[0]
    blk = _iota((TQ, nbc), 1)
    dist_c = t - ((blk + 1) * NSA_BLOCK - 1)
    mask_c = dist_c >= 0
    dist_cf = dist_c.astype(F32)
    cur = t // NSA_BLOCK
    expand = (_iota((nbc, s_len), 0) == _iota((nbc, s_len), 1) // NSA_BLOCK).astype(BF16)
    o_cmp = []
    for g in range(NSA_KV_HEADS):
        kc = kcvc[:, g * hd:(g + 1) * hd]
        vc = kcvc[:, 2 * hd + g * hd:2 * hd + (g + 1) * hd]
        imp = jnp.zeros((TQ, nbc), F32)
        for r in range(rep):
            h = g * rep + r
            p = _masked_softmax(_nt(q[:, h * hd:(h + 1) * hd], kc, precision=HI) - _slope(h) * dist_cf, mask_c)
            imp = imp + p
            o_cmp.append(jnp.dot(p, vc, precision=HI, preferred_element_type=F32))
        imp = jnp.where(blk < cur, imp, -1.0)
        sel = jnp.maximum(_select_blocks(imp, blk, NSA_TOPN - 1), (blk == cur).astype(F32))
        selx_ref[g] = jnp.dot(sel.astype(BF16), expand, preferred_element_type=F32).astype(BF16)

    n_kt = (qi * TQ + TQ + tk - 1) // tk
    start_w = pl.multiple_of(jnp.maximum(qi * TQ + TQ - wk, 0), TQ)
    kw_all = kwin_ref[0, pl.ds(start_w, wk), :]
    dist_w = t - (start_w + _iota((TQ, wk), 1))
    mask_w = (dist_w >= 0) & (dist_w <= NSA_WINDOW)
    dist_wf = dist_w.astype(F32)
    for h in range(nh):
        g = h // rep
        qh = qb[:, h * hd:(h + 1) * hd]
        slope = _slope(h)

        def body(kt, carry, g=g, qh=qh, slope=slope):
            m, l, acc = carry
            k0 = pl.multiple_of(kt * tk, tk)
            kk = ksel_ref[0, pl.ds(k0, tk), g * hd:(g + 1) * hd]
            vv = ksel_ref[0, pl.ds(k0, tk), 2 * hd + g * hd:2 * hd + (g + 1) * hd]
            dist = t - (k0 + _iota((TQ, tk), 1))
            mask = (selx_ref[g, :, pl.ds(k0, tk)] > 0.5) & (dist >= 0)
            s = jnp.where(mask, _nt(qh, kk) - slope * dist.astype(F32), NEG_INF)
            m_new = jnp.maximum(m, jnp.max(s, axis=-1, keepdims=True))
            alpha = jnp.exp(m - m_new)
            p = jnp.where(mask, jnp.exp(s - m_new), 0.0)
            l = alpha * l + jnp.sum(p, axis=-1, keepdims=True)
            acc = alpha * acc + jnp.dot(p.astype(BF16), vv, preferred_element_type=F32)
            return m_new, l, acc

        init = (jnp.full((TQ, 1), NEG_INF, F32), jnp.zeros((TQ, 1), F32), jnp.zeros((TQ, hd), F32))
        _, l_s, acc_s = lax.fori_loop(0, n_kt, body, init)
        o_sel = acc_s / l_s
        p_w = _masked_softmax(_nt(qh, kw_all[:, g * hd:(g + 1) * hd]) - slope * dist_wf, mask_w)
        o_win = jnp.dot(p_w.astype(BF16), kw_all[:, 2 * hd + g * hd:2 * hd + (g + 1) * hd], preferred_element_type=F32)
        o_ref[0, :, h * hd:(h + 1) * hd] = (gates[:, h:h + 1] * o_cmp[h] + gates[:, nh + h:nh + h + 1] * o_sel
                                            + gates[:, 2 * nh + h:2 * nh + h + 1] * o_win)


def _nsa_prompt_call(proj, qn, kvb, kcvc):
    b, s, _ = qn.shape
    nbc = kcvc.shape[1]
    tk = min(512, s)
    wk = min(NSA_WINDOW + TQ, s)
    return pl.pallas_call(
        functools.partial(_nsa_prompt_kernel, nbc, s, tk, wk),
        grid=(b, s // TQ),
        in_specs=[pl.BlockSpec((1, TQ, NSA_WIDTH), lambda bi, i: (bi, i, 0)),
                  pl.BlockSpec((1, TQ, LANE), lambda bi, i: (bi, i, C_NG // LANE)),
                  pl.BlockSpec((1, nbc, KV_ROW), lambda bi, i: (bi, 0, 0)),
                  pl.BlockSpec((1, s, KV_ROW), lambda bi, i: (bi, 0, 0)),
                  pl.BlockSpec((1, s, KV_ROW), lambda bi, i: (bi, 0, 1))],
        out_specs=pl.BlockSpec((1, TQ, NSA_WIDTH), lambda bi, i: (bi, i, 0)),
        out_shape=jax.ShapeDtypeStruct((b, s, NSA_WIDTH), F32),
        scratch_shapes=[pltpu.VMEM((NSA_KV_HEADS, TQ, s), BF16)],
        compiler_params=_cparams("parallel", "arbitrary"),
        name="nsa_prompt",
    )(qn, proj, kcvc, kvb, kvb)


def _page_copies(cache_hbm, li, pt_ref, b, step, buf, sem, slot, pps, rows_per_page):
    return [pltpu.make_async_copy(cache_hbm.at[li, pt_ref[b, step * pps + k]],
                                  buf.at[slot, pl.ds(k * rows_per_page, rows_per_page)], sem.at[slot])
            for k in range(pps)]


def _pipelined_pages(cache_hbm, li, pt_ref, buf, sem, pps, rows_per_page):
    b, s, ns = pl.program_id(0), pl.program_id(1), pl.num_programs(1)
    slot = s % 2
    copies = functools.partial(_page_copies, cache_hbm, li, pt_ref, b, buf=buf, sem=sem, pps=pps,
                               rows_per_page=rows_per_page)

    @pl.when(s == 0)
    def _():
        for cp in copies(step=s, slot=slot):
            cp.start()

    @pl.when(s + 1 < ns)
    def _():
        for cp in copies(step=s + 1, slot=1 - slot):
            cp.start()

    for cp in copies(step=s, slot=slot):
        cp.wait()
    return slot


def _cmp_sample_kernel(li, pps, pt_ref, cache_hbm, pe_ref, wbig_ref, w2_ref, g4_ref, o_ref, buf, sem):
    slot = _pipelined_pages(cache_hbm, li, pt_ref, buf, sem, pps, 2)
    page_blocks = buf.at[slot]
    o_ref[0] = _compress_core(lambda l: page_blocks[:, l, :], 2 * pps, pe_ref, wbig_ref, w2_ref, g4_ref)


def _cmp_sample_call(cache, li, page_table, cw):
    bd, n_pages = page_table.shape
    pps = math.gcd(n_pages, 32)
    blocks_per_page = PAGE_SIZE // NSA_BLOCK
    pe_row, wbig, w2big, g4 = cw
    cache5 = cache.reshape(cache.shape[0], cache.shape[1], blocks_per_page, NSA_BLOCK, KV_ROW)
    grid_spec = pltpu.PrefetchScalarGridSpec(
        num_scalar_prefetch=1,
        grid=(bd, n_pages // pps),
        in_specs=[pl.BlockSpec(memory_space=pl.ANY),
                  pl.BlockSpec(pe_row.shape, lambda b, s, pt: (0, 0)),
                  pl.BlockSpec(wbig.shape, lambda b, s, pt: (0, 0, 0)),
                  pl.BlockSpec(w2big.shape, lambda b, s, pt: (0, 0)),
                  pl.BlockSpec(g4.shape, lambda b, s, pt: (0, 0))],
        out_specs=pl.BlockSpec((1, blocks_per_page * pps, KV_ROW), lambda b, s, pt: (b, s, 0)),
        scratch_shapes=[pltpu.VMEM((2, blocks_per_page * pps, NSA_BLOCK, KV_ROW), F32),
                        pltpu.SemaphoreType.DMA((2,))])
    return pl.pallas_call(
        functools.partial(_cmp_sample_kernel, li, pps),
        grid_spec=grid_spec,
        out_shape=jax.ShapeDtypeStruct((bd, blocks_per_page * n_pages, KV_ROW), F32),
        compiler_params=_cparams("arbitrary", "arbitrary"),
        name="nsa_cmp_sample",
    )(page_table, cache5, pe_row, wbig, w2big, g4)


def _pick_group(x, rows_g0):
    w = x.shape[-1] // 2
    return jnp.where(rows_g0, x[:, :w], x[:, w:])


def _nsa_sample_kernel(li, pps, past, n_tok, pt_ref, cache_hbm, q_ref, gate_ref, cmp_ref, knew_ref, wold_ref, wnew_ref,
                       o_ref, buf, sem, qbd_ref, sel_ref, ocw_ref, m_ref, l_ref, acc_ref):
    s, ns = pl.program_id(1), pl.num_programs(1)
    hd, nh, rep = NSA_HEAD_DIM, NSA_HEADS, NSA_REP
    rows = nh * n_tok
    nbc = past // NSA_BLOCK
    tk = pps * PAGE_SIZE
    slot = _pipelined_pages(cache_hbm, li, pt_ref, buf, sem, pps, PAGE_SIZE)
    ridx = _iota((rows, 1), 0)
    rows_g0 = ridx < rep * n_tok
    tok = ridx % n_tok
    t = past + tok
    slope = jnp.exp2(-(ridx // n_tok + 1).astype(F32))

    @pl.when(s == 0)
    def _():
        qtok = q_ref[0]
        qrows = jnp.concatenate([qtok[:, h * hd:(h + 1) * hd] for h in range(nh)], axis=0)
        zero = jnp.zeros_like(qrows)
        qbd = jnp.where(rows_g0, jnp.concatenate([qrows, zero], axis=1), jnp.concatenate([zero, qrows], axis=1))
        qbd_ref[...] = qbd
        qbb = qbd.astype(BF16)
        kcvc = cmp_ref[0]
        blk = _iota((rows, nbc), 1)
        dist_c = t - ((blk + 1) * NSA_BLOCK - 1)
        p_c = _masked_softmax(_nt(qbd, kcvc[:, :2 * hd], precision=HI) - slope * dist_c.astype(F32), dist_c >= 0)
        o_c = _pick_group(jnp.dot(p_c, kcvc[:, 2 * hd:], precision=HI, preferred_element_type=F32), rows_g0)
        blk_t = _iota((n_tok, nbc), 1)
        cur_t = (past + _iota((n_tok, 1), 0)) // NSA_BLOCK
        sels = []
        for g in range(NSA_KV_HEADS):
            imp = p_c[g * rep * n_tok:(g * rep + 1) * n_tok]
            for r in range(1, rep):
                imp = imp + p_c[(g * rep + r) * n_tok:(g * rep + r + 1) * n_tok]
            imp = jnp.where(blk_t < cur_t, imp, -1.0)
            sels += [_select_blocks(imp, blk_t, NSA_TOPN - 1)] * rep
        sel_ref[...] = jnp.concatenate(sels, axis=0).astype(BF16)
        knew = knew_ref[0]
        dist_n = tok - _iota((rows, n_tok), 1)
        mask_n = dist_n >= 0
        s_n = jnp.where(mask_n, _nt(qbb, knew[:, :2 * hd].astype(BF16)) - slope * dist_n.astype(F32), NEG_INF)
        m0 = jnp.max(s_n, axis=-1, keepdims=True)
        p_n = jnp.where(mask_n, jnp.exp(s_n - m0), 0.0)
        m_ref[...] = m0
        l_ref[...] = jnp.sum(p_n, axis=-1, keepdims=True)
        acc_ref[...] = jnp.dot(p_n.astype(BF16), knew[:, 2 * hd:].astype(BF16), preferred_element_type=F32)
        kw = jnp.concatenate([wold_ref[0], wnew_ref[0]], axis=0)
        dist_w = tok + NSA_WINDOW - _iota((rows, NSA_WINDOW + n_tok), 1)
        mask_w = (dist_w >= 0) & (dist_w <= NSA_WINDOW)
        p_w = _masked_softmax(_nt(qbb, kw[:, :2 * hd].astype(BF16)) - slope * dist_w.astype(F32), mask_w)
        o_w = _pick_group(jnp.dot(p_w.astype(BF16), kw[:, 2 * hd:].astype(BF16), preferred_element_type=F32), rows_g0)
        ocw_ref[0] = o_c
        ocw_ref[1] = o_w

    page_rows = buf[slot]
    k0 = s * tk
    expand = (_iota((nbc, tk), 0) == (k0 + _iota((nbc, tk), 1)) // NSA_BLOCK).astype(BF16)
    mask = jnp.dot(sel_ref[...], expand, preferred_element_type=F32) > 0.5
    dist = t - (k0 + _iota((rows, tk), 1))
    sc = jnp.where(mask, _nt(qbd_ref[...].astype(BF16), page_rows[:, :2 * hd].astype(BF16)) - slope * dist.astype(F32),
                   NEG_INF)
    m_old = m_ref[...]
    m_new = jnp.maximum(m_old, jnp.max(sc, axis=-1, keepdims=True))
    alpha = jnp.exp(m_old - m_new)
    p = jnp.where(mask, jnp.exp(sc - m_new), 0.0)
    m_ref[...] = m_new
    l_ref[...] = alpha * l_ref[...] + jnp.sum(p, axis=-1, keepdims=True)
    acc_ref[...] = alpha * acc_ref[...] + jnp.dot(p.astype(BF16), page_rows[:, 2 * hd:].astype(BF16),
                                                  preferred_element_type=F32)

    @pl.when(s == ns - 1)
    def _():
        o_s = _pick_group(acc_ref[...] / l_ref[...], rows_g0)
        gates = jax.nn.sigmoid(gate_ref[0])

        def gate_rows(br):
            return jnp.concatenate([gates[:, br * nh + h:br * nh + h + 1] for h in range(nh)], axis=0)

        o_ref[0] = gate_rows(0) * ocw_ref[0] + gate_rows(1) * o_s + gate_rows(2) * ocw_ref[1]


def _nsa_sample_call(cache_sel, li, page_table, proj, qn, kvn, kcvc, win_old, past):
    bd, n_pages = page_table.shape
    n_tok = qn.shape[1]
    pps = math.gcd(n_pages, 32)
    rows = NSA_HEADS * n_tok
    nbc = kcvc.shape[1]
    cache4 = cache_sel.reshape(cache_sel.shape[0], cache_sel.shape[1], PAGE_SIZE, KV_ROW)
    grid_spec = pltpu.PrefetchScalarGridSpec(
        num_scalar_prefetch=1,
        grid=(bd, n_pages // pps),
        in_specs=[pl.BlockSpec(memory_space=pl.ANY),
                  pl.BlockSpec((1, n_tok, NSA_WIDTH), lambda b, s, pt: (b, 0, 0)),
                  pl.BlockSpec((1, n_tok, LANE), lambda b, s, pt: (b, 0, C_NG // LANE)),
                  pl.BlockSpec((1, nbc, KV_ROW), lambda b, s, pt: (b, 0, 0)),
                  pl.BlockSpec((1, n_tok, KV_ROW), lambda b, s, pt: (b, 0, 1)),
                  pl.BlockSpec((1, NSA_WINDOW, KV_ROW), lambda b, s, pt: (b, 0, 0)),
                  pl.BlockSpec((1, n_tok, KV_ROW), lambda b, s, pt: (b, 0, 2))],
        out_specs=pl.BlockSpec((1, rows, NSA_HEAD_DIM), lambda b, s, pt: (b, 0, 0)),
        scratch_shapes=[pltpu.VMEM((2, pps * PAGE_SIZE, KV_ROW), F32),
                        pltpu.SemaphoreType.DMA((2,)),
                        pltpu.VMEM((rows, 2 * NSA_HEAD_DIM), F32),
                        pltpu.VMEM((rows, nbc), BF16),
                        pltpu.VMEM((2, rows, NSA_HEAD_DIM), F32),
                        pltpu.VMEM((rows, 1), F32),
                        pltpu.VMEM((rows, 1), F32),
                        pltpu.VMEM((rows, 2 * NSA_HEAD_DIM), F32)])
    return pl.pallas_call(
        functools.partial(_nsa_sample_kernel, li, pps, past, n_tok),
        grid_spec=grid_spec,
        out_shape=jax.ShapeDtypeStruct((bd, rows, NSA_HEAD_DIM), F32),
        compiler_params=_cparams("arbitrary", "arbitrary"),
        name="nsa_sample",
    )(page_table, cache4, qn, proj, kcvc, kvn, win_old, kvn)


def _outproj_kernel(y_ref, n_ref, r_ref, w_ref, x_ref, g_ref, o_ref, mix_ref):
    @pl.when(pl.program_id(2) == 0)
    def _():
        mix_ref[:, :SSD_INNER] = y_ref[0].astype(BF16)
        mix_ref[:, SSD_INNER:SSD_INNER + NSA_WIDTH] = n_ref[0].astype(BF16)
        mix_ref[:, SSD_INNER + NSA_WIDTH:] = r_ref[0].astype(BF16)

    o_ref[0] = x_ref[0] + g_ref[0] * jnp.dot(mix_ref[...], w_ref[...], preferred_element_type=F32)


def _outproj_call(y, o_nsa, o_ret, w_bf, x, gate):
    b, s, d = x.shape
    ts = min(s, 1024)
    tn = 512
    sm = gate.shape[1]
    tsm = 1 if sm == 1 else ts
    mod_map = (lambda bi, i, j: (bi, 0, j)) if sm == 1 else (lambda bi, i, j: (bi, i, j))
    row_map = lambda bi, i, j: (bi, i, 0)
    return pl.pallas_call(
        _outproj_kernel,
        grid=(b, s // ts, d // tn),
        in_specs=[pl.BlockSpec((1, ts, SSD_INNER), row_map),
                  pl.BlockSpec((1, ts, NSA_WIDTH), row_map),
                  pl.BlockSpec((1, ts, RET_WIDTH), row_map),
                  pl.BlockSpec((D_MIX, tn), lambda bi, i, j: (0, j)),
                  pl.BlockSpec((1, ts, tn), lambda bi, i, j: (bi, i, j)),
                  pl.BlockSpec((1, tsm, tn), mod_map)],
        out_specs=pl.BlockSpec((1, ts, tn), lambda bi, i, j: (bi, i, j)),
        out_shape=jax.ShapeDtypeStruct((b, s, d), F32),
        scratch_shapes=[pltpu.VMEM((ts, D_MIX), BF16)],
        compiler_params=_cparams("parallel", "parallel", "arbitrary"),
        name="out_proj",
    )(y, o_nsa, o_ret, w_bf, x, gate)


def _route_kernel(x_ref, sc_ref, sh_ref, g_ref, wr_ref, br_ref, h_ref, meta_ref):
    h = _mod_norm(x_ref[0], g_ref[...], sc_ref[0], sh_ref[0])
    h_ref[0] = h
    logit = jnp.dot(h, wr_ref[...], precision=HI, preferred_element_type=F32) + br_ref[...]
    lane = _iota(logit.shape, 1)
    first = lambda hit: jnp.min(jnp.where(hit, lane, LANE), axis=-1, keepdims=True)
    is_g = lane < MOE_GROUPS
    gl = jnp.where(is_g, logit, -jnp.inf)
    gmax = jnp.max(gl, axis=-1, keepdims=True)
    g_idx = first(gl == gmax)
    g_p = 1.0 / jnp.sum(jnp.where(is_g, jnp.exp(logit - gmax), 0.0), axis=-1, keepdims=True)
    e_lane = lane - MOE_GROUPS
    in_grp = (e_lane >= 0) & (e_lane < MOE_EXPERTS) & (e_lane // MOE_EXPERTS_PER_GROUP == g_idx)
    el = jnp.where(in_grp, logit, -jnp.inf)
    v1 = jnp.max(el, axis=-1, keepdims=True)
    i1 = first(el == v1)
    el = jnp.where(lane == i1, -jnp.inf, el)
    v2 = jnp.max(el, axis=-1, keepdims=True)
    i2 = first(el == v2)
    e = jnp.exp(v2 - v1)
    w1 = g_p / (1.0 + e)
    w2 = g_p * e / (1.0 + e)
    meta = jnp.where(lane == 0, (i1 - MOE_GROUPS).astype(F32), 0.0)
    meta = jnp.where(lane == 1, (i2 - MOE_GROUPS).astype(F32), meta)
    meta = jnp.where(lane == 2, w1, meta)
    meta_ref[0] = jnp.where(lane == 3, w2, meta)


def _route_call(x1, scale, shift, gain, wr, br):
    b, s, d = x1.shape
    ts = min(s, 512)
    sm = scale.shape[1]
    tsm = 1 if sm == 1 else ts
    mod_map = (lambda bi, i: (bi, 0, 0)) if sm == 1 else (lambda bi, i: (bi, i, 0))
    return pl.pallas_call(
        _route_kernel,
        grid=(b, s // ts),
        in_specs=[pl.BlockSpec((1, ts, d), lambda bi, i: (bi, i, 0)),
                  pl.BlockSpec((1, tsm, d), mod_map),
                  pl.BlockSpec((1, tsm, d), mod_map),
                  pl.BlockSpec((1, d), lambda bi, i: (0, 0)),
                  pl.BlockSpec((d, LANE), lambda bi, i: (0, 0)),
                  pl.BlockSpec((1, LANE), lambda bi, i: (0, 0))],
        out_specs=[pl.BlockSpec((1, ts, d), lambda bi, i: (bi, i, 0)),
                   pl.BlockSpec((1, ts, LANE), lambda bi, i: (bi, i, 0))],
        out_shape=[jax.ShapeDtypeStruct((b, s, d), F32), jax.ShapeDtypeStruct((b, s, LANE), F32)],
        compiler_params=_cparams("parallel", "parallel"),
        name="moe_route",
    )(x1, scale, shift, gain.reshape(1, d), wr, br)


def _moe_plan(e_ids, w):
    n_pairs = e_ids.size
    tm = TM_MOE
    n_tiles = -(-n_pairs // tm) + MOE_EXPERTS
    n_slots = n_tiles * tm
    e_flat = e_ids.reshape(-1)
    order = jnp.argsort(e_flat, stable=True).astype(jnp.int32)
    sorted_e = e_flat[order]
    counts = jnp.zeros((MOE_EXPERTS,), jnp.int32).at[e_flat].add(1)
    tiles_per = (counts + tm - 1) // tm
    tile_end = jnp.cumsum(tiles_per)
    tile_start = tile_end - tiles_per
    grp_start = jnp.cumsum(counts) - counts
    rank = jnp.arange(n_pairs, dtype=jnp.int32) - grp_start[sorted_e]
    slot = tile_start[sorted_e] * tm + rank
    src = jnp.zeros((n_slots,), jnp.int32).at[slot].set(order // MOE_TOPK)
    dst = jnp.full((n_slots,), -1, jnp.int32).at[slot].set(order)
    w_slot = jnp.zeros((n_slots,), F32).at[slot].set(w.reshape(-1)[order])
    n_used = tile_end[-1]
    tile_ids = jnp.minimum(jnp.arange(n_tiles, dtype=jnp.int32), n_used - 1)
    tile_e = jnp.minimum(jnp.searchsorted(tile_end, tile_ids, side="right"), MOE_EXPERTS - 1).astype(jnp.int32)
    active = jnp.arange(n_tiles) < n_used
    first = active & (tile_ids == tile_start[tile_e])
    flag = (active.astype(jnp.int32) + first.astype(jnp.int32))
    return tile_e, flag, src, dst, w_slot.reshape(n_slots, 1)


def _moe_kernel(te_ref, flag_ref, src_ref, dst_ref, h_hbm, w_ref, wg_ref, wu_ref, wd_ref, y_hbm, xbuf, obuf, wgb,
                wub, wdb, sem_in, sem_out):
    i = pl.program_id(0)
    flag = flag_ref[i]
    base = i * TM_MOE

    def gather(r):
        return pltpu.make_async_copy(h_hbm.at[pl.ds(src_ref[base + r], 1)], xbuf.at[pl.ds(r, 1)], sem_in)

    def scatter(r, code):
        return pltpu.make_async_copy(obuf.at[pl.ds(r, 1)], y_hbm.at[code % MOE_TOPK, pl.ds(code // MOE_TOPK, 1)],
                                     sem_out)

    @pl.when(flag > 0)
    def _():
        def start_in(r, carry):
            gather(r).start()
            return carry

        lax.fori_loop(0, TM_MOE, start_in, 0)

        @pl.when(flag > 1)
        def _():
            wgb[...] = wg_ref[0].astype(BF16)
            wub[...] = wu_ref[0].astype(BF16)
            wdb[...] = wd_ref[0].astype(BF16)

        def wait_in(r, carry):
            gather(r).wait()
            return carry

        lax.fori_loop(0, TM_MOE, wait_in, 0)
        x = xbuf[...].astype(BF16)
        a = jnp.dot(x, wgb[...], preferred_element_type=F32)
        u = jnp.dot(x, wub[...], preferred_element_type=F32)
        hid = (_silu(a) * u * w_ref[...]).astype(BF16)
        obuf[...] = jnp.dot(hid, wdb[...], preferred_element_type=F32)

        def start_out(r, carry):
            code = dst_ref[base + r]

            @pl.when(code >= 0)
            def _():
                scatter(r, code).start()

            return carry

        lax.fori_loop(0, TM_MOE, start_out, 0)

        def wait_out(r, carry):
            code = dst_ref[base + r]

            @pl.when(code >= 0)
            def _():
                scatter(r, code).wait()

            return carry

        lax.fori_loop(0, TM_MOE, wait_out, 0)


def _moe_call(h_all, plan, w_gate, w_up, w_down):
    t, d = h_all.shape
    tile_e, flag, src, dst, w_slot = plan
    n_tiles = tile_e.shape[0]
    tm = TM_MOE
    grid_spec = pltpu.PrefetchScalarGridSpec(
        num_scalar_prefetch=4,
        grid=(n_tiles,),
        in_specs=[pl.BlockSpec(memory_space=pl.ANY),
                  pl.BlockSpec((tm, 1), lambda i, te, fl, sr, ds: (i, 0)),
                  pl.BlockSpec((1, d, MOE_HIDDEN), lambda i, te, fl, sr, ds: (te[i], 0, 0)),
                  pl.BlockSpec((1, d, MOE_HIDDEN), lambda i, te, fl, sr, ds: (te[i], 0, 0)),
                  pl.BlockSpec((1, MOE_HIDDEN, d), lambda i, te, fl, sr, ds: (te[i], 0, 0))],
        out_specs=pl.BlockSpec(memory_space=pl.ANY),
        scratch_shapes=[pltpu.VMEM((tm, d), F32), pltpu.VMEM((tm, d), F32),
                        pltpu.VMEM((d, MOE_HIDDEN), BF16), pltpu.VMEM((d, MOE_HIDDEN), BF16),
                        pltpu.VMEM((MOE_HIDDEN, d), BF16),
                        pltpu.SemaphoreType.DMA(()), pltpu.SemaphoreType.DMA(())])
    return pl.pallas_call(
        _moe_kernel,
        grid_spec=grid_spec,
        out_shape=jax.ShapeDtypeStruct((MOE_TOPK, t, d), F32),
        compiler_params=_cparams("arbitrary"),
        name="moe_experts",
    )(tile_e, flag, src, dst, h_all, w_slot, w_gate, w_up, w_down)


def _combine_kernel(x_ref, g_ref, y_ref, o_ref):
    o_ref[0] = x_ref[0] + g_ref[0] * (y_ref[0] + y_ref[1])


def _combine_call(x1, gate, y_all, row0):
    b, s, d = x1.shape
    ts = min(s, 512)
    sm = gate.shape[1]
    tsm = 1 if sm == 1 else ts
    mod_map = (lambda bi, i: (bi, 0, 0)) if sm == 1 else (lambda bi, i: (bi, i, 0))
    blk0 = row0 // ts
    per_b = s // ts
    return pl.pallas_call(
        _combine_kernel,
        grid=(b, per_b),
        in_specs=[pl.BlockSpec((1, ts, d), lambda bi, i: (bi, i, 0)),
                  pl.BlockSpec((1, tsm, d), mod_map),
                  pl.BlockSpec((MOE_TOPK, ts, d), lambda bi, i: (0, blk0 + bi * per_b + i, 0))],
        out_specs=pl.BlockSpec((1, ts, d), lambda bi, i: (bi, i, 0)),
        out_shape=jax.ShapeDtypeStruct((b, s, d), F32),
        compiler_params=_cparams("parallel", "parallel"),
        name="moe_combine",
    )(x1, gate, y_all)


def _reorder_w_in(w):
    sizes = (SSD_INNER, SSD_CONV_DIM, SSD_HEADS, NSA_WIDTH, 3 * KV_ROW, 3 * NSA_HEADS, RET_HEADS * RET_KEY_DIM,
             RET_HEADS * RET_KEY_DIM, RET_WIDTH, RET_WIDTH)
    z, xbc, dt, nq, nkv, ngate, rq, rk, rv, rg = jnp.split(w, [int(v) for v in np.cumsum(sizes)[:-1]], axis=1)
    padc = lambda a: jnp.pad(a, ((0, 0), (0, LANE - a.shape[1])))
    return jnp.concatenate([z, nq, rv, rg, rq, rk, xbc, nkv, padc(dt), padc(ngate)], axis=1).astype(BF16)


def _kv_rows(a):
    return a.reshape(a.shape[:2] + (2, NSA_KV_HEADS, NSA_HEAD_DIM))


def kernel(x_prompt, x_sample, cache_cmp_kv, cache_sel_kv, cache_win_kv, state_ssm, state_conv, state_ret, page_table, c_prompt, c_sample, w_ada, b_ada, norm_mix, norm_ffn, w_in, w_out, conv_w, conv_b, dt_bias, a_log, ssd_d, ssd_norm, qk_gain, cmp_pe, cmp_w1, cmp_w2, router_group_w, router_group_b, router_expert_w, router_expert_b, expert_w_gate, expert_w_up, expert_w_down):
    bp, sp, d = x_prompt.shape
    bd, sd, _ = x_sample.shape
    n_pages = page_table.shape[1]
    past = n_pages * PAGE_SIZE
    assert d == D_MODEL and sp % LC == 0 and sp % NSA_BLOCK == 0 and sp >= NSA_WINDOW
    assert sd % SUBLANE == 0 and sd <= NSA_BLOCK and sd <= LC and cache_win_kv.shape[2] == NSA_WINDOW
    n_seq = bp + bd
    c_all = jnp.pad(jnp.concatenate([c_prompt, c_sample]), ((0, -n_seq % SUBLANE), (0, 0)))
    mod = _ada_call(c_all, w_ada, b_ada)
    cos_p, sin_p = _rotary_tables(0, sp, sp)
    cos_s, sin_s = _rotary_tables(past, sd, LC)
    xp = x_prompt
    xs = x_sample.reshape(1, bd * sd, d)
    outs_p, outs_s = [], []
    for li in range(DEPTH):
        mods_p = [mod[li, :bp, k * d:(k + 1) * d].reshape(bp, 1, d) for k in range(6)]
        mods_s = [jnp.repeat(mod[li, bp:n_seq, k * d:(k + 1) * d], sd, axis=0).reshape(1, bd * sd, d) for k in range(6)]
        w_in_b = _reorder_w_in(w_in[li])
        w_out_b = w_out[li].astype(BF16)
        cw = _compress_weights(cmp_pe[li], cmp_w1[li], cmp_w2[li], qk_gain[li, 4])
        ssd_w = (conv_w[li], conv_b[li], dt_bias[li], a_log[li], ssd_d[li], ssd_norm[li])
        wr = jnp.pad(jnp.concatenate([router_group_w[li], router_expert_w[li]], axis=1),
                     ((0, 0), (0, LANE - MOE_GROUPS - MOE_EXPERTS)))
        br = jnp.pad(jnp.concatenate([router_group_b[li], router_expert_b[li]]),
                     (0, LANE - MOE_GROUPS - MOE_EXPERTS)).reshape(1, LANE)

        sh_a, sc_a, g_a, sh_f, sc_f, g_f = mods_p
        proj = _inproj_call(xp, sc_a, sh_a, norm_mix[li], w_in_b)
        y_ssd, ssm_p, conv_p = _ssd_call(proj, LC, jnp.zeros((bp, SSD_CONV - 1, SSD_CONV_DIM), F32),
                                         jnp.zeros((bp, SSD_HEADS, SSD_HEAD_DIM, SSD_STATE), F32), *ssd_w)
        o_ret, ret_p = _ret_call(proj, LC, jnp.zeros((bp, RET_HEADS, RET_KEY_DIM, RET_VAL_DIM), F32), cos_p, sin_p)
        qn, kvn_p, kvb = _prep_call(proj, sp, qk_gain[li])
        kcvc = _cmp_prompt_call(kvn_p, cw)
        o_nsa = _nsa_prompt_call(proj, qn, kvb, kcvc)
        x1_p = _outproj_call(y_ssd, o_nsa, o_ret, w_out_b, xp, g_a)
        h2_p, meta_p = _route_call(x1_p, sc_f, sh_f, norm_ffn[li], wr, br)
        gf_p = g_f

        sh_a, sc_a, g_a, sh_f, sc_f, g_f = mods_s
        proj = _inproj_call(xs, sc_a, sh_a, norm_mix[li], w_in_b).reshape(bd, sd, N_PROJ)
        proj = jnp.pad(proj, ((0, 0), (0, LC - sd), (0, 0)))
        y_ssd, ssm_s, conv_s = _ssd_call(proj, sd, state_conv[li], state_ssm[li], *ssd_w)
        o_ret, ret_s = _ret_call(proj, sd, state_ret[li], cos_s, sin_s)
        qn, kvn_s, _ = _prep_call(proj, sd, qk_gain[li])
        kcvc = _cmp_sample_call(cache_cmp_kv, li, page_table, cw)
        win_old = cache_win_kv[li].reshape(bd, NSA_WINDOW, KV_ROW)
        o_nsa = _nsa_sample_call(cache_sel_kv, li, page_table, proj, qn, kvn_s, kcvc, win_old, past)
        o_nsa = o_nsa.reshape(bd, NSA_HEADS, sd, NSA_HEAD_DIM).transpose(0, 2, 1, 3).reshape(1, bd * sd, NSA_WIDTH)
        x1_s = _outproj_call(y_ssd[:, :sd].reshape(1, bd * sd, SSD_INNER), o_nsa,
                             o_ret[:, :sd].reshape(1, bd * sd, RET_WIDTH), w_out_b, xs, g_a)
        h2_s, meta_s = _route_call(x1_s, sc_f, sh_f, norm_ffn[li], wr, br)

        h_all = jnp.concatenate([h2_p.reshape(bp * sp, d), h2_s.reshape(bd * sd, d)])
        meta = jnp.concatenate([meta_p.reshape(bp * sp, LANE), meta_s.reshape(bd * sd, LANE)])
        plan = _moe_plan(meta[:, :MOE_TOPK].astype(jnp.int32), meta[:, MOE_TOPK:2 * MOE_TOPK])
        y_all = _moe_call(h_all, plan, expert_w_gate[li], expert_w_up[li], expert_w_down[li])
        xp = _combine_call(x1_p, gf_p, y_all, 0)
        xs = _combine_call(x1_s, g_f, y_all, bp * sp)

        win_s = jnp.concatenate([win_old[:, sd:], kvn_s[:, :, 2 * KV_ROW:]], axis=1)
        outs_p.append((_kv_rows(kvn_p[:, :, :KV_ROW]), _kv_rows(kvn_p[:, :, KV_ROW:2 * KV_ROW]),
                       _kv_rows(kvn_p[:, sp - NSA_WINDOW:, 2 * KV_ROW:]), ssm_p, conv_p, ret_p))
        outs_s.append((_kv_rows(kvn_s[:, :, :KV_ROW]), _kv_rows(kvn_s[:, :, KV_ROW:2 * KV_ROW]), _kv_rows(win_s),
                       ssm_s, conv_s, ret_s))
    res = [xp, xs.reshape(bd, sd, d)]
    for k in range(6):
        res.append(jnp.stack([o[k] for o in outs_p]))
        res.append(jnp.stack([o[k] for o in outs_s]))
    return tuple(res)
```

```python
import functools
import math

import numpy as np
import jax
import jax.numpy as jnp
from jax import lax
from jax.experimental import pallas as pl
from jax.experimental.pallas import tpu as pltpu

F32 = jnp.float32
BF16 = jnp.bfloat16
HI = lax.Precision.HIGHEST
EPS = 1e-6
NEG_INF = -1e30

D_MODEL = 2048
DEPTH = 2
PAGE_SIZE = 128
SSD_INNER = D_MODEL // 2
SSD_HEAD_DIM = 64
SSD_HEADS = SSD_INNER // SSD_HEAD_DIM
SSD_GROUPS = 2
SSD_STATE = 128
SSD_CONV = 4
SSD_CONV_DIM = SSD_INNER + 2 * SSD_GROUPS * SSD_STATE
NSA_WIDTH = D_MODEL // 4
NSA_HEAD_DIM = 64
NSA_HEADS = NSA_WIDTH // NSA_HEAD_DIM
NSA_KV_HEADS = 2
NSA_REP = NSA_HEADS // NSA_KV_HEADS
NSA_BLOCK = 64
NSA_TOPN = 16
NSA_WINDOW = 512
NSA_CMP_HIDDEN = 128
RET_WIDTH = D_MODEL // 4
RET_HEADS = 4
RET_VAL_DIM = RET_WIDTH // RET_HEADS
RET_KEY_DIM = RET_VAL_DIM // 2
D_MIX = SSD_INNER + NSA_WIDTH + RET_WIDTH
MOE_GROUPS = 4
MOE_EXPERTS_PER_GROUP = 8
MOE_EXPERTS = MOE_GROUPS * MOE_EXPERTS_PER_GROUP
MOE_TOPK = 2
MOE_HIDDEN = D_MODEL // 4
KV_ROW = 2 * NSA_KV_HEADS * NSA_HEAD_DIM

LANE = 128
SUBLANE = 8
C_Z, C_NQ, C_RV, C_RG, C_RQ, C_RK, C_XBC, C_NKV, C_DT, C_NG = 0, 1024, 1536, 2048, 2560, 2816, 3072, 4608, 5376, 5504
N_PROJ = 5632
LC = 128
TQ = 128
TM_MOE = 256
ROW_CHUNKS = D_MODEL // LANE
VMEM_LIMIT = 56 * 2 ** 20


def _cparams(*sem):
    return pltpu.CompilerParams(dimension_semantics=sem, vmem_limit_bytes=VMEM_LIMIT)


def _silu(x):
    return x * jax.nn.sigmoid(x)


def _nt(a, b, precision=None):
    return lax.dot_general(a, b, (((1,), (1,)), ((), ())), precision=precision, preferred_element_type=F32)


def _tn(a, b, precision=None):
    return lax.dot_general(a, b, (((0,), (0,)), ((), ())), precision=precision, preferred_element_type=F32)


def _iota(shape, dim):
    return lax.broadcasted_iota(jnp.int32, shape, dim)


def _seg_sum(x2, seg):
    n = x2.shape[-1]
    bd = (_iota((n, n), 0) // seg == _iota((n, n), 1) // seg).astype(BF16)
    hi = x2.astype(BF16)
    lo = (x2 - hi.astype(F32)).astype(BF16)
    return jnp.dot(hi, bd, preferred_element_type=F32) + jnp.dot(lo, bd, preferred_element_type=F32)


def _ada_kernel(c_ref, w_ref, b_ref, o_ref):
    o_ref[0] = jnp.dot(_silu(c_ref[...]), w_ref[0], precision=HI, preferred_element_type=F32) + b_ref[0]


def _ada_call(c_all, w_ada, b_ada):
    rows = c_all.shape[0]
    tn = 1024
    return pl.pallas_call(
        _ada_kernel,
        grid=(DEPTH, 6 * D_MODEL // tn),
        in_specs=[pl.BlockSpec((rows, D_MODEL), lambda l, j: (0, 0)),
                  pl.BlockSpec((1, D_MODEL, tn), lambda l, j: (l, 0, j)),
                  pl.BlockSpec((1, 1, tn), lambda l, j: (l, 0, j))],
        out_specs=pl.BlockSpec((1, rows, tn), lambda l, j: (l, 0, j)),
        out_shape=jax.ShapeDtypeStruct((DEPTH, rows, 6 * D_MODEL), F32),
        compiler_params=_cparams("parallel", "parallel"),
        name="ada",
    )(c_all, w_ada, b_ada.reshape(DEPTH, 1, 6 * D_MODEL))


def _mod_norm(x, gain, scale, shift):
    ms = jnp.mean(x * x, axis=-1, keepdims=True)
    return x * lax.rsqrt(ms + EPS) * gain * (1.0 + scale) + shift


def _inproj_kernel(x_ref, sc_ref, sh_ref, g_ref, w_ref, o_ref, h_ref):
    @pl.when(pl.program_id(2) == 0)
    def _():
        h_ref[...] = _mod_norm(x_ref[0], g_ref[...], sc_ref[0], sh_ref[0]).astype(BF16)

    o_ref[0] = jnp.dot(h_ref[...], w_ref[...], preferred_element_type=F32)


def _inproj_call(x, scale, shift, gain, w_bf):
    b, s, d = x.shape
    ts = min(s, 1024)
    tn = 512
    sm = scale.shape[1]
    tsm = 1 if sm == 1 else ts
    mod_map = (lambda bi, i, j: (bi, 0, 0)) if sm == 1 else (lambda bi, i, j: (bi, i, 0))
    return pl.pallas_call(
        _inproj_kernel,
        grid=(b, s // ts, N_PROJ // tn),
        in_specs=[pl.BlockSpec((1, ts, d), lambda bi, i, j: (bi, i, 0)),
                  pl.BlockSpec((1, tsm, d), mod_map),
                  pl.BlockSpec((1, tsm, d), mod_map),
                  pl.BlockSpec((1, d), lambda bi, i, j: (0, 0)),
                  pl.BlockSpec((d, tn), lambda bi, i, j: (0, j))],
        out_specs=pl.BlockSpec((1, ts, tn), lambda bi, i, j: (bi, i, j)),
        out_shape=jax.ShapeDtypeStruct((b, s, N_PROJ), F32),
        scratch_shapes=[pltpu.VMEM((ts, d), BF16)],
        compiler_params=_cparams("parallel", "parallel", "arbitrary"),
        name="in_proj",
    )(x, scale, shift, gain.reshape(1, d), w_bf)


def _ssd_kernel(n_valid, xbc_ref, z_ref, dt_ref, cprev_ref, sprev_ref, cw_ref, cb_ref, dtb_ref, alog_ref, dsk_ref,
                nrm_ref, y_ref, snew_ref, cnew_ref, cbuf, state, ybuf):
    c = pl.program_id(1)
    nc = pl.num_programs(1)
    hd, nst = SSD_HEAD_DIM, SSD_STATE
    hpg = SSD_HEADS // SSD_GROUPS

    @pl.when(c == 0)
    def _():
        cbuf[0:SUBLANE, :] = jnp.zeros((SUBLANE, SSD_CONV_DIM), F32)
        cbuf[5:8, :] = cprev_ref[0]
        state[...] = sprev_ref[0]

    cbuf[8:8 + LC, :] = xbc_ref[0]
    conv = cb_ref[...] + cw_ref[0:1, :] * cbuf[5:5 + LC, :]
    for k in range(1, SSD_CONV):
        conv = conv + cw_ref[k:k + 1, :] * cbuf[5 + k:5 + k + LC, :]
    tail = cbuf[5 + n_valid:8 + n_valid, :]
    cbuf[5:8, :] = tail
    xbc = _silu(conv)
    xs = xbc[:, :SSD_INNER]
    bm = xbc[:, SSD_INNER:SSD_INNER + SSD_GROUPS * nst].astype(BF16)
    cm = xbc[:, SSD_INNER + SSD_GROUPS * nst:].astype(BF16)

    dtr = dt_ref[0] + dtb_ref[...]
    dt = jnp.maximum(dtr, 0.0) + jnp.log1p(jnp.exp(-jnp.abs(dtr)))
    if n_valid < LC:
        dt = jnp.where(_iota((LC, LANE), 0) < n_valid, dt, 0.0)
    a = -jnp.exp(alog_ref[...])
    da = dt * a
    row = _iota((LC, LC), 0)
    col = _iota((LC, LC), 1)
    tril = row >= col
    acs = jnp.dot(tril.astype(F32), da, precision=HI, preferred_element_type=F32)
    eye = (row == col).astype(F32)
    acs_t = _nt(eye, acs, precision=HI)
    expand = (_iota((LANE, SSD_INNER), 0) == _iota((LANE, SSD_INNER), 1) // hd).astype(F32)
    e_acs = jnp.exp(acs)
    decay = jnp.exp(acs[LC - 1:LC, :] - acs)
    dt_x = jnp.dot(dt, expand, precision=HI, preferred_element_type=F32)
    e_acs_x = jnp.dot(e_acs, expand, precision=HI, preferred_element_type=F32)
    decay_x = jnp.dot(decay, expand, precision=HI, preferred_element_type=F32)
    xdt = xs * dt_x
    xdec = (xdt * decay_x).astype(BF16)
    xdt_b = xdt.astype(BF16)
    cbs = [_nt(cm[:, g * nst:(g + 1) * nst], bm[:, g * nst:(g + 1) * nst]) for g in range(SSD_GROUPS)]
    for h in range(SSD_HEADS):
        g = h // hpg
        lmat = jnp.exp(jnp.where(tril, acs[:, h:h + 1] - acs_t[h:h + 1, :], -jnp.inf))
        m = (cbs[g] * lmat).astype(BF16)
        y_diag = jnp.dot(m, xdt_b[:, h * hd:(h + 1) * hd], preferred_element_type=F32)
        st = state[h]
        y_off = _nt(cm[:, g * nst:(g + 1) * nst], st.astype(BF16)) * e_acs_x[:, h * hd:(h + 1) * hd]
        state[h] = st * e_acs[LC - 1:LC, h:h + 1] + _tn(xdec[:, h * hd:(h + 1) * hd], bm[:, g * nst:(g + 1) * nst])
        ybuf[:, h * hd:(h + 1) * hd] = y_diag + y_off
    y = (ybuf[...] + dsk_ref[...] * xs) * _silu(z_ref[0])
    gw = SSD_INNER // SSD_GROUPS
    for g in range(SSD_GROUPS):
        seg = y[:, g * gw:(g + 1) * gw]
        ms = jnp.mean(seg * seg, axis=-1, keepdims=True)
        y_ref[0, :, g * gw:(g + 1) * gw] = seg * lax.rsqrt(ms + EPS) * nrm_ref[:, g * gw:(g + 1) * gw]

    @pl.when(c == nc - 1)
    def _():
        snew_ref[0] = state[...]
        cnew_ref[0] = tail


def _ssd_call(proj, n_valid, conv_prev, ssm_prev, conv_w, conv_b, dt_bias, a_log, ssd_d, ssd_norm):
    b, sp, _ = proj.shape
    nc = sp // LC
    pad = lambda v: jnp.pad(v, (0, LANE - SSD_HEADS)).reshape(1, LANE)
    const2 = lambda bi, c: (0, 0)
    return pl.pallas_call(
        functools.partial(_ssd_kernel, n_valid),
        grid=(b, nc),
        in_specs=[pl.BlockSpec((1, LC, SSD_CONV_DIM), lambda bi, c: (bi, c, C_XBC // SSD_CONV_DIM)),
                  pl.BlockSpec((1, LC, SSD_INNER), lambda bi, c: (bi, c, C_Z // SSD_INNER)),
                  pl.BlockSpec((1, LC, LANE), lambda bi, c: (bi, c, C_DT // LANE)),
                  pl.BlockSpec((1, SSD_CONV - 1, SSD_CONV_DIM), lambda bi, c: (bi, 0, 0)),
                  pl.BlockSpec((1, SSD_HEADS, SSD_HEAD_DIM, SSD_STATE), lambda bi, c: (bi, 0, 0, 0)),
                  pl.BlockSpec((SSD_CONV, SSD_CONV_DIM), const2),
                  pl.BlockSpec((1, SSD_CONV_DIM), const2),
                  pl.BlockSpec((1, LANE), const2),
                  pl.BlockSpec((1, LANE), const2),
                  pl.BlockSpec((1, SSD_INNER), const2),
                  pl.BlockSpec((1, SSD_INNER), const2)],
        out_specs=[pl.BlockSpec((1, LC, SSD_INNER), lambda bi, c: (bi, c, 0)),
                   pl.BlockSpec((1, SSD_HEADS, SSD_HEAD_DIM, SSD_STATE), lambda bi, c: (bi, 0, 0, 0)),
                   pl.BlockSpec((1, SSD_CONV - 1, SSD_CONV_DIM), lambda bi, c: (bi, 0, 0))],
        out_shape=[jax.ShapeDtypeStruct((b, sp, SSD_INNER), F32),
                   jax.ShapeDtypeStruct((b, SSD_HEADS, SSD_HEAD_DIM, SSD_STATE), F32),
                   jax.ShapeDtypeStruct((b, SSD_CONV - 1, SSD_CONV_DIM), F32)],
        scratch_shapes=[pltpu.VMEM((8 + LC, SSD_CONV_DIM), F32),
                        pltpu.VMEM((SSD_HEADS, SSD_HEAD_DIM, SSD_STATE), F32),
                        pltpu.VMEM((LC, SSD_INNER), F32)],
        compiler_params=_cparams("parallel", "arbitrary"),
        name="ssd",
    )(proj, proj, proj, conv_prev, ssm_prev, conv_w, conv_b.reshape(1, -1), pad(dt_bias), pad(a_log),
      jnp.repeat(ssd_d, SSD_HEAD_DIM).reshape(1, -1), ssd_norm.reshape(1, -1))


def _ret_log_g(h):
    return float(np.log1p(-np.exp2(np.float32(-5.0 - h)), dtype=np.float32))


def _ret_kernel(n_valid, q_ref, k_ref, v_ref, g_ref, cos_ref, sin_ref, rprev_ref, o_ref, rnew_ref, state):
    c = pl.program_id(1)
    nc = pl.num_programs(1)
    kd, vd = RET_KEY_DIM, RET_VAL_DIM
    half = kd // 2

    @pl.when(c == 0)
    def _():
        state[...] = rprev_ref[0]

    cos = cos_ref[...]
    sin = sin_ref[...]
    first_half = (_iota((LC, RET_HEADS * kd), 1) % kd) < half

    def rot(x):
        n = x.shape[-1]
        swapped = jnp.where(first_half, pltpu.roll(x, n - half, 1), pltpu.roll(x, half, 1))
        return x * cos + swapped * sin

    q = rot(q_ref[0])
    k = rot(k_ref[0]) * (kd ** -0.5)
    v = v_ref[0]
    gate = g_ref[0]
    ti = _iota((LC, LC), 0)
    tj = _iota((LC, LC), 1)
    diff = (ti - tj).astype(F32)
    ipos = _iota((LC, 1), 0)
    i1 = ipos.astype(F32)
    for h in range(RET_HEADS):
        lg = _ret_log_g(h)
        dmat = jnp.where(diff >= 0, jnp.exp(diff * lg), 0.0)
        q_dec = jnp.exp((i1 + 1.0) * lg)
        k_dec = jnp.where(ipos < n_valid, jnp.exp((n_valid - 1.0 - i1) * lg), 0.0)
        c_dec = math.exp(n_valid * lg)
        qh = q[:, h * kd:(h + 1) * kd].astype(BF16)
        kh = k[:, h * kd:(h + 1) * kd]
        vh = v[:, h * vd:(h + 1) * vd].astype(BF16)
        sc = _nt(qh, kh.astype(BF16)) * dmat
        intra = jnp.dot(sc.astype(BF16), vh, preferred_element_type=F32)
        rs = state[h]
        cross = jnp.dot(qh, rs.astype(BF16), preferred_element_type=F32) * q_dec
        state[h] = rs * c_dec + _tn((kh * k_dec).astype(BF16), vh)
        o = intra + cross
        ms = jnp.mean(o * o, axis=-1, keepdims=True)
        o_ref[0, :, h * vd:(h + 1) * vd] = o * lax.rsqrt(ms + EPS) * _silu(gate[:, h * vd:(h + 1) * vd])

    @pl.when(c == nc - 1)
    def _():
        rnew_ref[0] = state[...]


def _rotary_tables(offset, s_real, s_pad):
    half = RET_KEY_DIM // 2
    freqs = 1.0 / (10000.0 ** jnp.linspace(0.0, 1.0, half, dtype=F32))
    pos = (offset + jnp.arange(s_real)).astype(F32)
    ang = pos[:, None] * freqs[None, :]
    cos, sin = jnp.cos(ang), jnp.sin(ang)
    cos = jnp.tile(jnp.concatenate([cos, cos], axis=-1), (1, RET_HEADS))
    sin = jnp.tile(jnp.concatenate([-sin, sin], axis=-1), (1, RET_HEADS))
    padr = ((0, s_pad - s_real), (0, 0))
    return jnp.pad(cos, padr), jnp.pad(sin, padr)


def _ret_call(proj, n_valid, ret_prev, cos, sin):
    b, sp, _ = proj.shape
    nc = sp // LC
    qw = RET_HEADS * RET_KEY_DIM
    return pl.pallas_call(
        functools.partial(_ret_kernel, n_valid),
        grid=(b, nc),
        in_specs=[pl.BlockSpec((1, LC, qw), lambda bi, c: (bi, c, C_RQ // qw)),
                  pl.BlockSpec((1, LC, qw), lambda bi, c: (bi, c, C_RK // qw)),
                  pl.BlockSpec((1, LC, RET_WIDTH), lambda bi, c: (bi, c, C_RV // RET_WIDTH)),
                  pl.BlockSpec((1, LC, RET_WIDTH), lambda bi, c: (bi, c, C_RG // RET_WIDTH)),
                  pl.BlockSpec((LC, qw), lambda bi, c: (c, 0)),
                  pl.BlockSpec((LC, qw), lambda bi, c: (c, 0)),
                  pl.BlockSpec((1, RET_HEADS, RET_KEY_DIM, RET_VAL_DIM), lambda bi, c: (bi, 0, 0, 0))],
        out_specs=[pl.BlockSpec((1, LC, RET_WIDTH), lambda bi, c: (bi, c, 0)),
                   pl.BlockSpec((1, RET_HEADS, RET_KEY_DIM, RET_VAL_DIM), lambda bi, c: (bi, 0, 0, 0))],
        out_shape=[jax.ShapeDtypeStruct((b, sp, RET_WIDTH), F32),
                   jax.ShapeDtypeStruct((b, RET_HEADS, RET_KEY_DIM, RET_VAL_DIM), F32)],
        scratch_shapes=[pltpu.VMEM((RET_HEADS, RET_KEY_DIM, RET_VAL_DIM), F32)],
        compiler_params=_cparams("parallel", "arbitrary"),
        name="retention",
    )(proj, proj, proj, proj, cos, sin, ret_prev)


def _slope(h):
    return float(2.0 ** -(h + 1))


def _prep_kernel(nq_ref, nkv_ref, gq_ref, gk_ref, isk_ref, qn_ref, kvn_ref, kvb_ref):
    inv = 1.0 / NSA_HEAD_DIM
    nq = nq_ref[0]
    qn_ref[0] = nq * lax.rsqrt(_seg_sum(nq * nq, NSA_HEAD_DIM) * inv + EPS) * gq_ref[...]
    kv = nkv_ref[0]
    normed = kv * lax.rsqrt(_seg_sum(kv * kv, NSA_HEAD_DIM) * inv + EPS) * gk_ref[...]
    kvn = jnp.where(isk_ref[...] > 0.5, normed, kv)
    kvn_ref[0] = kvn
    kvb_ref[0] = kvn[:, KV_ROW:].astype(BF16)


def _prep_call(proj, s_real, qk_gain):
    b = proj.shape[0]
    ts = min(s_real, 512)
    hd = NSA_HEAD_DIM
    gq = (jnp.tile(qk_gain[0], NSA_HEADS) * (hd ** -0.5)).reshape(1, -1)
    ones = jnp.ones((2 * hd,), F32)
    gk = jnp.concatenate([jnp.concatenate([jnp.tile(qk_gain[1 + i], 2), ones]) for i in range(3)]).reshape(1, -1)
    isk = jnp.tile(jnp.concatenate([ones, 0.0 * ones]), 3).reshape(1, -1)
    const2 = lambda bi, i: (0, 0)
    return pl.pallas_call(
        _prep_kernel,
        grid=(b, s_real // ts),
        in_specs=[pl.BlockSpec((1, ts, NSA_WIDTH), lambda bi, i: (bi, i, C_NQ // NSA_WIDTH)),
                  pl.BlockSpec((1, ts, 3 * KV_ROW), lambda bi, i: (bi, i, C_NKV // (3 * KV_ROW))),
                  pl.BlockSpec((1, NSA_WIDTH), const2),
                  pl.BlockSpec((1, 3 * KV_ROW), const2),
                  pl.BlockSpec((1, 3 * KV_ROW), const2)],
        out_specs=[pl.BlockSpec((1, ts, NSA_WIDTH), lambda bi, i: (bi, i, 0)),
                   pl.BlockSpec((1, ts, 3 * KV_ROW), lambda bi, i: (bi, i, 0)),
                   pl.BlockSpec((1, ts, 2 * KV_ROW), lambda bi, i: (bi, i, 0))],
        out_shape=[jax.ShapeDtypeStruct((b, s_real, NSA_WIDTH), F32),
                   jax.ShapeDtypeStruct((b, s_real, 3 * KV_ROW), F32),
                   jax.ShapeDtypeStruct((b, s_real, 2 * KV_ROW), BF16)],
        compiler_params=_cparams("parallel", "parallel"),
        name="nsa_prep",
    )(proj, proj, gq, gk, isk)


def _compress_weights(cmp_pe, cmp_w1, cmp_w2, gain4):
    hd, hid = NSA_HEAD_DIM, NSA_CMP_HIDDEN
    wbig = jnp.zeros((NSA_BLOCK, KV_ROW, 4 * hid), BF16)
    w2big = jnp.zeros((4 * hid, KV_ROW), BF16)
    for cg in range(4):
        c = cg // 2
        wbig = wbig.at[:, cg * hd:(cg + 1) * hd, cg * hid:(cg + 1) * hid].set(cmp_w1[c].astype(BF16))
        w2big = w2big.at[cg * hid:(cg + 1) * hid, cg * hd:(cg + 1) * hd].set(cmp_w2[c].astype(BF16))
    pe_row = jnp.concatenate([cmp_pe[0], cmp_pe[0], cmp_pe[1], cmp_pe[1]], axis=-1)
    g4 = jnp.concatenate([jnp.tile(gain4, 2), jnp.ones((2 * hd,), F32)]).reshape(1, -1)
    return pe_row, wbig, w2big, g4


def _compress_core(get_rows, nblk, pe_ref, wbig_ref, w2_ref, g4_ref):
    acc = jnp.zeros((nblk, 4 * NSA_CMP_HIDDEN), F32)
    for l in range(NSA_BLOCK):
        x = get_rows(l) + pe_ref[l:l + 1, :]
        acc = acc + jnp.dot(x.astype(BF16), wbig_ref[l], preferred_element_type=F32)
    out = jnp.dot(_silu(acc).astype(BF16), w2_ref[...], preferred_element_type=F32)
    ss = _seg_sum(out * out, NSA_HEAD_DIM)
    normed = out * lax.rsqrt(ss * (1.0 / NSA_HEAD_DIM) + EPS) * g4_ref[...]
    return jnp.where(_iota(out.shape, 1) < 2 * NSA_HEAD_DIM, normed, out)


def _cmp_prompt_kernel(nblk, kv_ref, pe_ref, wbig_ref, w2_ref, g4_ref, o_ref):
    o_ref[0] = _compress_core(lambda l: kv_ref[0, :, l, :], nblk, pe_ref, wbig_ref, w2_ref, g4_ref)


def _cmp_prompt_call(kvn, cw):
    b, s, _ = kvn.shape
    nblk = s // NSA_BLOCK
    pe_row, wbig, w2big, g4 = cw
    kv4 = kvn.reshape(b, nblk, NSA_BLOCK, 3 * KV_ROW)
    return pl.pallas_call(
        functools.partial(_cmp_prompt_kernel, nblk),
        grid=(b,),
        in_specs=[pl.BlockSpec((1, nblk, NSA_BLOCK, KV_ROW), lambda bi: (bi, 0, 0, 0)),
                  pl.BlockSpec(pe_row.shape, lambda bi: (0, 0)),
                  pl.BlockSpec(wbig.shape, lambda bi: (0, 0, 0)),
                  pl.BlockSpec(w2big.shape, lambda bi: (0, 0)),
                  pl.BlockSpec(g4.shape, lambda bi: (0, 0))],
        out_specs=pl.BlockSpec((1, nblk, KV_ROW), lambda bi: (bi, 0, 0)),
        out_shape=jax.ShapeDtypeStruct((b, nblk, KV_ROW), F32),
        compiler_params=_cparams("arbitrary"),
        name="nsa_cmp_prompt",
    )(kv4, pe_row, wbig, w2big, g4)


def _masked_softmax(s, mask):
    s = jnp.where(mask, s, NEG_INF)
    p = jnp.exp(s - jnp.max(s, axis=-1, keepdims=True))
    return p / jnp.sum(p, axis=-1, keepdims=True) * mask.astype(F32)


def _select_blocks(imp, blk, n_top, axis=-1):
    sel = jnp.zeros(imp.shape, F32)
    big = imp.shape[axis]
    for _ in range(n_top):
        m = jnp.max(imp, axis=axis, keepdims=True)
        idx = jnp.min(jnp.where(imp == m, blk, big), axis=axis, keepdims=True)
        hit = blk == idx
        sel = jnp.where(hit & (m >= 0.0), 1.0, sel)
        imp = jnp.where(hit, -2.0, imp)
    return sel


M_FLOOR = -1e29


def _nsa_prompt_kernel(nbc, s_len, tk, wk, q_ref, gate_ref, cmp_ref, ksel_ref, kwin_ref, o_ref, bias_ref, dist_ref):
    qi = pl.program_id(1)
    hd, rep, nh = NSA_HEAD_DIM, NSA_REP, NSA_HEADS
    q = q_ref[0]
    qb = q.astype(BF16)
    gates = jax.nn.sigmoid(gate_ref[0])
    t = qi * TQ + _iota((TQ, 1), 0)
    t_row = qi * TQ + _iota((1, TQ), 1)
    kcvc = cmp_ref[0]
    blk = _iota((nbc, TQ), 0)
    dist_c = t_row - ((blk + 1) * NSA_BLOCK - 1)
    mask_c = dist_c >= 0
    mask_cf = mask_c.astype(F32)
    dist_cf = dist_c.astype(F32)
    cur = t_row // NSA_BLOCK
    o_cmp = []
    sels = []
    for g in range(NSA_KV_HEADS):
        kc = kcvc[:, g * hd:(g + 1) * hd]
        vc = kcvc[:, 2 * hd + g * hd:2 * hd + (g + 1) * hd]
        imp = jnp.zeros((nbc, TQ), F32)
        for r in range(rep):
            h = g * rep + r
            s = jnp.where(mask_c, _nt(kc, q[:, h * hd:(h + 1) * hd], precision=HI) - _slope(h) * dist_cf, NEG_INF)
            p = jnp.exp(s - jnp.max(s, axis=0, keepdims=True))
            p = p / jnp.sum(p, axis=0, keepdims=True) * mask_cf
            imp = imp + p
            o_cmp.append(_tn(p, vc, precision=HI))
        imp = jnp.where(blk < cur, imp, -1.0)
        sel = jnp.maximum(_select_blocks(imp, blk, NSA_TOPN - 1, axis=0), (blk == cur).astype(F32))
        sels.append(sel.astype(BF16))

    n_kt = (qi * TQ + TQ + tk - 1) // tk

    def fill(kt, carry):
        k0 = pl.multiple_of(kt * tk, tk)
        dist = t - (k0 + _iota((TQ, tk), 1))
        dist_ref[:, pl.ds(k0, tk)] = dist.astype(F32)
        expand = (_iota((nbc, tk), 0) == (k0 + _iota((nbc, tk), 1)) // NSA_BLOCK).astype(BF16)
        for g in range(NSA_KV_HEADS):
            picked = _tn(sels[g], expand) > 0.5
            bias_ref[g, :, pl.ds(k0, tk)] = jnp.where(picked & (dist >= 0), 0.0, NEG_INF)
        return carry

    lax.fori_loop(0, n_kt, fill, 0)

    start_w = pl.multiple_of(jnp.maximum(qi * TQ + TQ - wk, 0), TQ)
    kw_all = kwin_ref[0, pl.ds(start_w, wk), :]
    dist_w = t - (start_w + _iota((TQ, wk), 1))
    bias_w = jnp.where((dist_w >= 0) & (dist_w <= NSA_WINDOW), 0.0, NEG_INF)
    dist_wf = dist_w.astype(F32)
    for h in range(nh):
        g = h // rep
        qh = qb[:, h * hd:(h + 1) * hd]
        slope = _slope(h)

        def body(kt, carry, g=g, qh=qh, slope=slope):
            m, l, acc = carry
            k0 = pl.multiple_of(kt * tk, tk)
            kk = ksel_ref[0, pl.ds(k0, tk), g * hd:(g + 1) * hd]
            vv = ksel_ref[0, pl.ds(k0, tk), 2 * hd + g * hd:2 * hd + (g + 1) * hd]
            s = _nt(qh, kk) + (bias_ref[g, :, pl.ds(k0, tk)] - slope * dist_ref[:, pl.ds(k0, tk)])
            m_new = jnp.maximum(m, jnp.max(s, axis=-1, keepdims=True))
            alpha = jnp.exp(m - m_new)
            p = jnp.exp(s - m_new)
            l = alpha * l + jnp.sum(p, axis=-1, keepdims=True)
            acc = alpha * acc + jnp.dot(p.astype(BF16), vv, preferred_element_type=F32)
            return m_new, l, acc

        init = (jnp.full((TQ, 1), M_FLOOR, F32), jnp.zeros((TQ, 1), F32), jnp.zeros((TQ, hd), F32))
        _, l_s, acc_s = lax.fori_loop(0, n_kt, body, init)
        o_sel = acc_s / l_s
        s_w = _nt(qh, kw_all[:, g * hd:(g + 1) * hd]) + (bias_w - slope * dist_wf)
        p_w = jnp.exp(s_w - jnp.max(s_w, axis=-1, keepdims=True))
        o_win = jnp.dot(p_w.astype(BF16), kw_all[:, 2 * hd + g * hd:2 * hd + (g + 1) * hd],
                        preferred_element_type=F32) / jnp.sum(p_w, axis=-1, keepdims=True)
        o_ref[0, :, h * hd:(h + 1) * hd] = (gates[:, h:h + 1] * o_cmp[h] + gates[:, nh + h:nh + h + 1] * o_sel
                                            + gates[:, 2 * nh + h:2 * nh + h + 1] * o_win)


def _nsa_prompt_call(proj, qn, kvb, kcvc):
    b, s, _ = qn.shape
    nbc = kcvc.shape[1]
    tk = min(512, s)
    wk = min(NSA_WINDOW + TQ, s)
    return pl.pallas_call(
        functools.partial(_nsa_prompt_kernel, nbc, s, tk, wk),
        grid=(b, s // TQ),
        in_specs=[pl.BlockSpec((1, TQ, NSA_WIDTH), lambda bi, i: (bi, i, 0)),
                  pl.BlockSpec((1, TQ, LANE), lambda bi, i: (bi, i, C_NG // LANE)),
                  pl.BlockSpec((1, nbc, KV_ROW), lambda bi, i: (bi, 0, 0)),
                  pl.BlockSpec((1, s, KV_ROW), lambda bi, i: (bi, 0, 0)),
                  pl.BlockSpec((1, s, KV_ROW), lambda bi, i: (bi, 0, 1))],
        out_specs=pl.BlockSpec((1, TQ, NSA_WIDTH), lambda bi, i: (bi, i, 0)),
        out_shape=jax.ShapeDtypeStruct((b, s, NSA_WIDTH), F32),
        scratch_shapes=[pltpu.VMEM((NSA_KV_HEADS, TQ, s), F32), pltpu.VMEM((TQ, s), F32)],
        compiler_params=_cparams("parallel", "arbitrary"),
        name="nsa_prompt",
    )(qn, proj, kcvc, kvb, kvb)


def _feature_major_pages(cache):
    return cache.transpose(0, 1, 3, 4, 5, 2).reshape(cache.shape[0], cache.shape[1], KV_ROW, PAGE_SIZE)


def _page_copies(cache_hbm, li, pt_ref, b, step, dst_of, sem, slot, pps):
    return [pltpu.make_async_copy(cache_hbm.at[li, pt_ref[b, step * pps + k]], dst_of(slot, k), sem.at[slot])
            for k in range(pps)]


def _pipelined_pages(cache_hbm, li, pt_ref, dst_of, sem, pps):
    b, s, ns = pl.program_id(0), pl.program_id(1), pl.num_programs(1)
    slot = s % 2
    copies = functools.partial(_page_copies, cache_hbm, li, pt_ref, b, dst_of=dst_of, sem=sem, pps=pps)

    @pl.when(s == 0)
    def _():
        for cp in copies(step=s, slot=slot):
            cp.start()

    @pl.when(s + 1 < ns)
    def _():
        for cp in copies(step=s + 1, slot=1 - slot):
            cp.start()

    for cp in copies(step=s, slot=slot):
        cp.wait()
    return slot


def _cmp_sample_weights(cmp_pe, cmp_w1, cmp_w2, gain4):
    wt = cmp_w1.transpose(0, 2, 1, 3).astype(BF16)
    z = jnp.zeros_like(wt)
    w1 = jnp.concatenate([jnp.concatenate([wt, z], axis=-1), jnp.concatenate([z, wt], axis=-1)], axis=2)
    w2 = cmp_w2.astype(BF16)
    z2 = jnp.zeros_like(w2)
    w2bd = jnp.concatenate([jnp.concatenate([w2, z2], axis=-1), jnp.concatenate([z2, w2], axis=-1)], axis=1)
    pe_t = jnp.tile(cmp_pe.transpose(0, 2, 1), (1, 1, 2)).reshape(2 * NSA_HEAD_DIM, 2 * NSA_BLOCK)
    return pe_t, w1, w2bd, jnp.tile(gain4, 2).reshape(1, -1)


def _cmp_sample_kernel(li, pps, pt_ref, cache_hbm, pe_ref, w1_ref, w2_ref, g_ref, o_ref, buf, sem):
    slot = _pipelined_pages(cache_hbm, li, pt_ref, lambda sl, k: buf.at[sl, k], sem, pps)
    pages = buf.at[slot]
    hd = NSA_HEAD_DIM
    for c in range(2):
        acc = jnp.zeros((NSA_KV_HEADS * pps, 2 * NSA_CMP_HIDDEN), F32)
        for d in range(hd):
            x = jnp.concatenate([pages[:, (2 * c + g) * hd + d, :] for g in range(NSA_KV_HEADS)], axis=0)
            x = x + pe_ref[c * hd + d:c * hd + d + 1, :]
            acc = acc + jnp.dot(x.astype(BF16), w1_ref[c, d], preferred_element_type=F32)
        out = jnp.dot(_silu(acc).astype(BF16), w2_ref[c], preferred_element_type=F32)
        if c == 0:
            out = out * lax.rsqrt(_seg_sum(out * out, hd) * (1.0 / hd) + EPS) * g_ref[...]
        for g in range(NSA_KV_HEADS):
            o_ref[0, :, (2 * c + g) * 2 * hd:(2 * c + g + 1) * 2 * hd] = out[g * pps:(g + 1) * pps]


def _cmp_sample_call(cache_t, li, page_table, cw):
    bd, n_pages = page_table.shape
    pps = math.gcd(n_pages, 64)
    pe_t, w1, w2bd, gain = cw
    hd = NSA_HEAD_DIM
    grid_spec = pltpu.PrefetchScalarGridSpec(
        num_scalar_prefetch=1,
        grid=(bd, n_pages // pps),
        in_specs=[pl.BlockSpec(memory_space=pl.ANY),
                  pl.BlockSpec(pe_t.shape, lambda b, s, pt: (0, 0)),
                  pl.BlockSpec(w1.shape, lambda b, s, pt: (0, 0, 0, 0)),
                  pl.BlockSpec(w2bd.shape, lambda b, s, pt: (0, 0, 0)),
                  pl.BlockSpec(gain.shape, lambda b, s, pt: (0, 0))],
        out_specs=pl.BlockSpec((1, pps, 2 * KV_ROW), lambda b, s, pt: (b, s, 0)),
        scratch_shapes=[pltpu.VMEM((2, pps, KV_ROW, PAGE_SIZE), F32), pltpu.SemaphoreType.DMA((2,))])
    packed = pl.pallas_call(
        functools.partial(_cmp_sample_kernel, li, pps),
        grid_spec=grid_spec,
        out_shape=jax.ShapeDtypeStruct((bd, n_pages, 2 * KV_ROW), F32),
        compiler_params=_cparams("arbitrary", "arbitrary"),
        name="nsa_cmp_sample",
    )(page_table, cache_t, pe_t, w1, w2bd, gain)
    return packed.reshape(bd, n_pages, 4, 2, hd).transpose(0, 1, 3, 2, 4).reshape(bd, 2 * n_pages, KV_ROW)


def _pick_group(x, rows_g0):
    w = x.shape[-1] // 2
    return jnp.where(rows_g0, x[:, :w], x[:, w:])


def _nsa_sample_kernel(li, pps, past, n_tok, pt_ref, cache_hbm, q_ref, gate_ref, cmp_ref, knew_ref, wold_ref, wnew_ref,
                       o_ref, buf, sem, qbd_ref, sel_ref, ocw_ref, m_ref, l_ref, acc_ref):
    s, ns = pl.program_id(1), pl.num_programs(1)
    hd, nh, rep = NSA_HEAD_DIM, NSA_HEADS, NSA_REP
    rows = nh * n_tok
    nbc = past // NSA_BLOCK
    tk = pps * PAGE_SIZE
    slot = _pipelined_pages(cache_hbm, li, pt_ref,
                            lambda sl, k: buf.at[sl, :, pl.ds(k * PAGE_SIZE, PAGE_SIZE)], sem, pps)
    ridx = _iota((rows, 1), 0)
    rows_g0 = ridx < rep * n_tok
    tok = ridx % n_tok
    t = past + tok
    slope = jnp.exp2(-(ridx // n_tok + 1).astype(F32))

    @pl.when(s == 0)
    def _():
        qtok = q_ref[0]
        qrows = jnp.concatenate([qtok[:, h * hd:(h + 1) * hd] for h in range(nh)], axis=0)
        zero = jnp.zeros_like(qrows)
        qbd = jnp.where(rows_g0, jnp.concatenate([qrows, zero], axis=1), jnp.concatenate([zero, qrows], axis=1))
        qbd_ref[...] = qbd
        qbb = qbd.astype(BF16)
        kcvc = cmp_ref[0]
        blk = _iota((rows, nbc), 1)
        dist_c = t - ((blk + 1) * NSA_BLOCK - 1)
        p_c = _masked_softmax(_nt(qbd, kcvc[:, :2 * hd], precision=HI) - slope * dist_c.astype(F32), dist_c >= 0)
        o_c = _pick_group(jnp.dot(p_c, kcvc[:, 2 * hd:], precision=HI, preferred_element_type=F32), rows_g0)
        blk_t = _iota((n_tok, nbc), 1)
        cur_t = (past + _iota((n_tok, 1), 0)) // NSA_BLOCK
        sels = []
        for g in range(NSA_KV_HEADS):
            imp = p_c[g * rep * n_tok:(g * rep + 1) * n_tok]
            for r in range(1, rep):
                imp = imp + p_c[(g * rep + r) * n_tok:(g * rep + r + 1) * n_tok]
            imp = jnp.where(blk_t < cur_t, imp, -1.0)
            sels += [_select_blocks(imp, blk_t, NSA_TOPN - 1)] * rep
        sel_ref[...] = jnp.concatenate(sels, axis=0).astype(BF16)
        knew = knew_ref[0]
        dist_n = tok - _iota((rows, n_tok), 1)
        mask_n = dist_n >= 0
        s_n = jnp.where(mask_n, _nt(qbb, knew[:, :2 * hd].astype(BF16)) - slope * dist_n.astype(F32), NEG_INF)
        m0 = jnp.max(s_n, axis=-1, keepdims=True)
        p_n = jnp.where(mask_n, jnp.exp(s_n - m0), 0.0)
        m_ref[...] = m0
        l_ref[...] = jnp.sum(p_n, axis=-1, keepdims=True)
        acc_ref[...] = jnp.dot(p_n.astype(BF16), knew[:, 2 * hd:].astype(BF16), preferred_element_type=F32)
        kw = jnp.concatenate([wold_ref[0], wnew_ref[0]], axis=0)
        dist_w = tok + NSA_WINDOW - _iota((rows, NSA_WINDOW + n_tok), 1)
        mask_w = (dist_w >= 0) & (dist_w <= NSA_WINDOW)
        p_w = _masked_softmax(_nt(qbb, kw[:, :2 * hd].astype(BF16)) - slope * dist_w.astype(F32), mask_w)
        o_w = _pick_group(jnp.dot(p_w.astype(BF16), kw[:, 2 * hd:].astype(BF16), preferred_element_type=F32), rows_g0)
        ocw_ref[0] = o_c
        ocw_ref[1] = o_w

    pages = buf.at[slot]
    k_t = pages[:2 * hd, :].astype(BF16)
    v_t = pages[2 * hd:, :].astype(BF16)
    k0 = s * tk
    expand = (_iota((nbc, tk), 0) == (k0 + _iota((nbc, tk), 1)) // NSA_BLOCK).astype(BF16)
    mask = jnp.dot(sel_ref[...], expand, preferred_element_type=F32) > 0.5
    dist = t - (k0 + _iota((rows, tk), 1))
    qk = jnp.dot(qbd_ref[...].astype(BF16), k_t, preferred_element_type=F32)
    sc = jnp.where(mask, qk - slope * dist.astype(F32), NEG_INF)
    m_old = m_ref[...]
    m_new = jnp.maximum(m_old, jnp.max(sc, axis=-1, keepdims=True))
    alpha = jnp.exp(m_old - m_new)
    p = jnp.where(mask, jnp.exp(sc - m_new), 0.0)
    m_ref[...] = m_new
    l_ref[...] = alpha * l_ref[...] + jnp.sum(p, axis=-1, keepdims=True)
    acc_ref[...] = alpha * acc_ref[...] + _nt(p.astype(BF16), v_t)

    @pl.when(s == ns - 1)
    def _():
        o_s = _pick_group(acc_ref[...] / l_ref[...], rows_g0)
        gates = jax.nn.sigmoid(gate_ref[0])

        def gate_rows(br):
            return jnp.concatenate([gates[:, br * nh + h:br * nh + h + 1] for h in range(nh)], axis=0)

        o_ref[0] = gate_rows(0) * ocw_ref[0] + gate_rows(1) * o_s + gate_rows(2) * ocw_ref[1]


def _nsa_sample_call(cache_t, li, page_table, proj, qn, kvn, kcvc, win_old, past):
    bd, n_pages = page_table.shape
    n_tok = qn.shape[1]
    pps = math.gcd(n_pages, 32)
    rows = NSA_HEADS * n_tok
    nbc = kcvc.shape[1]
    grid_spec = pltpu.PrefetchScalarGridSpec(
        num_scalar_prefetch=1,
        grid=(bd, n_pages // pps),
        in_specs=[pl.BlockSpec(memory_space=pl.ANY),
                  pl.BlockSpec((1, n_tok, NSA_WIDTH), lambda b, s, pt: (b, 0, 0)),
                  pl.BlockSpec((1, n_tok, LANE), lambda b, s, pt: (b, 0, C_NG // LANE)),
                  pl.BlockSpec((1, nbc, KV_ROW), lambda b, s, pt: (b, 0, 0)),
                  pl.BlockSpec((1, n_tok, KV_ROW), lambda b, s, pt: (b, 0, 1)),
                  pl.BlockSpec((1, NSA_WINDOW, KV_ROW), lambda b, s, pt: (b, 0, 0)),
                  pl.BlockSpec((1, n_tok, KV_ROW), lambda b, s, pt: (b, 0, 2))],
        out_specs=pl.BlockSpec((1, rows, NSA_HEAD_DIM), lambda b, s, pt: (b, 0, 0)),
        scratch_shapes=[pltpu.VMEM((2, KV_ROW, pps * PAGE_SIZE), F32),
                        pltpu.SemaphoreType.DMA((2,)),
                        pltpu.VMEM((rows, 2 * NSA_HEAD_DIM), F32),
                        pltpu.VMEM((rows, nbc), BF16),
                        pltpu.VMEM((2, rows, NSA_HEAD_DIM), F32),
                        pltpu.VMEM((rows, 1), F32),
                        pltpu.VMEM((rows, 1), F32),
                        pltpu.VMEM((rows, 2 * NSA_HEAD_DIM), F32)])
    return pl.pallas_call(
        functools.partial(_nsa_sample_kernel, li, pps, past, n_tok),
        grid_spec=grid_spec,
        out_shape=jax.ShapeDtypeStruct((bd, rows, NSA_HEAD_DIM), F32),
        compiler_params=_cparams("arbitrary", "arbitrary"),
        name="nsa_sample",
    )(page_table, cache_t, qn, proj, kcvc, kvn, win_old, kvn)


def _outproj_kernel(y_ref, n_ref, r_ref, w_ref, x_ref, g_ref, o_ref, mix_ref):
    @pl.when(pl.program_id(2) == 0)
    def _():
        mix_ref[:, :SSD_INNER] = y_ref[0].astype(BF16)
        mix_ref[:, SSD_INNER:SSD_INNER + NSA_WIDTH] = n_ref[0].astype(BF16)
        mix_ref[:, SSD_INNER + NSA_WIDTH:] = r_ref[0].astype(BF16)

    o_ref[0] = x_ref[0] + g_ref[0] * jnp.dot(mix_ref[...], w_ref[...], preferred_element_type=F32)


def _outproj_call(y, o_nsa, o_ret, w_bf, x, gate):
    b, s, d = x.shape
    ts = min(s, 1024)
    tn = 512
    sm = gate.shape[1]
    tsm = 1 if sm == 1 else ts
    mod_map = (lambda bi, i, j: (bi, 0, j)) if sm == 1 else (lambda bi, i, j: (bi, i, j))
    row_map = lambda bi, i, j: (bi, i, 0)
    return pl.pallas_call(
        _outproj_kernel,
        grid=(b, s // ts, d // tn),
        in_specs=[pl.BlockSpec((1, ts, SSD_INNER), row_map),
                  pl.BlockSpec((1, ts, NSA_WIDTH), row_map),
                  pl.BlockSpec((1, ts, RET_WIDTH), row_map),
                  pl.BlockSpec((D_MIX, tn), lambda bi, i, j: (0, j)),
                  pl.BlockSpec((1, ts, tn), lambda bi, i, j: (bi, i, j)),
                  pl.BlockSpec((1, tsm, tn), mod_map)],
        out_specs=pl.BlockSpec((1, ts, tn), lambda bi, i, j: (bi, i, j)),
        out_shape=jax.ShapeDtypeStruct((b, s, d), F32),
        scratch_shapes=[pltpu.VMEM((ts, D_MIX), BF16)],
        compiler_params=_cparams("parallel", "parallel", "arbitrary"),
        name="out_proj",
    )(y, o_nsa, o_ret, w_bf, x, gate)


def _route_kernel(x_ref, sc_ref, sh_ref, g_ref, wr_ref, br_ref, h_ref, meta_ref):
    h = _mod_norm(x_ref[0], g_ref[...], sc_ref[0], sh_ref[0])
    for c in range(ROW_CHUNKS):
        h_ref[0, :, c, :] = h[:, c * LANE:(c + 1) * LANE]
    logit = jnp.dot(h, wr_ref[...], precision=HI, preferred_element_type=F32) + br_ref[...]
    lane = _iota(logit.shape, 1)
    first = lambda hit: jnp.min(jnp.where(hit, lane, LANE), axis=-1, keepdims=True)
    is_g = lane < MOE_GROUPS
    gl = jnp.where(is_g, logit, -jnp.inf)
    gmax = jnp.max(gl, axis=-1, keepdims=True)
    g_idx = first(gl == gmax)
    g_p = 1.0 / jnp.sum(jnp.where(is_g, jnp.exp(logit - gmax), 0.0), axis=-1, keepdims=True)
    e_lane = lane - MOE_GROUPS
    in_grp = (e_lane >= 0) & (e_lane < MOE_EXPERTS) & (e_lane // MOE_EXPERTS_PER_GROUP == g_idx)
    el = jnp.where(in_grp, logit, -jnp.inf)
    v1 = jnp.max(el, axis=-1, keepdims=True)
    i1 = first(el == v1)
    el = jnp.where(lane == i1, -jnp.inf, el)
    v2 = jnp.max(el, axis=-1, keepdims=True)
    i2 = first(el == v2)
    e = jnp.exp(v2 - v1)
    w1 = g_p / (1.0 + e)
    w2 = g_p * e / (1.0 + e)
    meta = jnp.where(lane == 0, (i1 - MOE_GROUPS).astype(F32), 0.0)
    meta = jnp.where(lane == 1, (i2 - MOE_GROUPS).astype(F32), meta)
    meta = jnp.where(lane == 2, w1, meta)
    meta_ref[0] = jnp.where(lane == 3, w2, meta)


def _route_call(x1, scale, shift, gain, wr, br):
    b, s, d = x1.shape
    ts = min(s, 512)
    sm = scale.shape[1]
    tsm = 1 if sm == 1 else ts
    mod_map = (lambda bi, i: (bi, 0, 0)) if sm == 1 else (lambda bi, i: (bi, i, 0))
    return pl.pallas_call(
        _route_kernel,
        grid=(b, s // ts),
        in_specs=[pl.BlockSpec((1, ts, d), lambda bi, i: (bi, i, 0)),
                  pl.BlockSpec((1, tsm, d), mod_map),
                  pl.BlockSpec((1, tsm, d), mod_map),
                  pl.BlockSpec((1, d), lambda bi, i: (0, 0)),
                  pl.BlockSpec((d, LANE), lambda bi, i: (0, 0)),
                  pl.BlockSpec((1, LANE), lambda bi, i: (0, 0))],
        out_specs=[pl.BlockSpec((1, ts, ROW_CHUNKS, LANE), lambda bi, i: (bi, i, 0, 0)),
                   pl.BlockSpec((1, ts, LANE), lambda bi, i: (bi, i, 0))],
        out_shape=[jax.ShapeDtypeStruct((b, s, ROW_CHUNKS, LANE), F32), jax.ShapeDtypeStruct((b, s, LANE), F32)],
        compiler_params=_cparams("parallel", "parallel"),
        name="moe_route",
    )(x1, scale, shift, gain.reshape(1, d), wr, br)


def _moe_plan(e_ids, w):
    n_pairs = e_ids.size
    tm = TM_MOE
    n_tiles = -(-n_pairs // tm) + MOE_EXPERTS
    n_slots = n_tiles * tm
    n_tok = n_pairs // MOE_TOPK
    e_flat = e_ids.reshape(-1)
    order = jnp.argsort(e_flat, stable=True).astype(jnp.int32)
    experts = jnp.arange(MOE_EXPERTS, dtype=jnp.int32)
    counts = jnp.sum((e_flat[:, None] == experts[None, :]).astype(jnp.int32), axis=0)
    tiles_per = (counts + tm - 1) // tm
    tile_end = jnp.cumsum(tiles_per)
    tile_start = tile_end - tiles_per
    grp_start = jnp.cumsum(counts) - counts
    n_used = tile_end[-1]
    tiles = jnp.arange(n_tiles, dtype=jnp.int32)
    tile_ids = jnp.minimum(tiles, n_used - 1)
    tile_e = jnp.sum((tile_ids[:, None] >= tile_end[None, :]).astype(jnp.int32), axis=1)
    active = tiles < n_used
    first = active & (tile_ids == tile_start[tile_e])
    flag = active.astype(jnp.int32) + first.astype(jnp.int32)
    slots = jnp.arange(n_slots, dtype=jnp.int32)
    slot_e = jnp.repeat(tile_e, tm)
    rank = slots - jnp.repeat(tile_start[tile_e], tm) * tm
    valid = jnp.repeat(active, tm) & (rank < counts[slot_e])
    pair = order[jnp.clip(grp_start[slot_e] + rank, 0, n_pairs - 1)]
    src = jnp.where(valid, pair // MOE_TOPK, 0)
    dst = jnp.where(valid, pair, (n_tok + slots % tm) * MOE_TOPK)
    w_slot = jnp.where(valid, w.reshape(-1)[pair], 0.0)
    return tile_e, flag, src, dst, w_slot.reshape(n_slots, 1)


def _moe_kernel(te_ref, flag_ref, src_ref, dst_ref, h_hbm, w_ref, wg_ref, wu_ref, wd_ref, y_hbm, xbuf, obuf, wgb,
                wub, wdb, sem_in, sem_out):
    i = pl.program_id(0)
    n_tiles = pl.num_programs(0)
    flag = flag_ref[i]
    nxt = jnp.minimum(i + 1, n_tiles - 1)
    next_active = (i + 1 < n_tiles) & (flag_ref[nxt] > 0)
    slot = i % 2

    def start_gather(tile, sl):
        def body(r, carry):
            pltpu.make_async_copy(h_hbm.at[src_ref[tile * TM_MOE + r]], xbuf.at[sl, r], sem_in.at[sl]).start()
            return carry

        lax.fori_loop(0, TM_MOE, body, 0, unroll=8)

    def wait_gather(sl):
        pltpu.make_async_copy(h_hbm.at[pl.ds(0, TM_MOE)], xbuf.at[sl], sem_in.at[sl]).wait()

    def wait_scatter(sl):
        pltpu.make_async_copy(obuf.at[sl], y_hbm.at[0, pl.ds(0, TM_MOE)], sem_out.at[sl]).wait()

    @pl.when(i == 0)
    def _():
        obuf[0] = jnp.zeros(obuf.shape[1:], F32)
        pad_rows = pl.ds(y_hbm.shape[1] - TM_MOE, TM_MOE)
        for plane in range(MOE_TOPK):
            pltpu.make_async_copy(obuf.at[0], y_hbm.at[plane, pad_rows], sem_out.at[0]).start()
        for plane in range(MOE_TOPK):
            pltpu.make_async_copy(obuf.at[0], y_hbm.at[plane, pad_rows], sem_out.at[0]).wait()

    @pl.when(flag > 0)
    def _():
        @pl.when(i == 0)
        def _():
            start_gather(i, slot)

        @pl.when(next_active)
        def _():
            start_gather(i + 1, 1 - slot)

        @pl.when(flag > 1)
        def _():
            wgb[...] = wg_ref[0, 0].astype(BF16)
            wub[...] = wu_ref[0, 0].astype(BF16)
            wdb[...] = wd_ref[0, 0].astype(BF16)

        wait_gather(slot)
        rows = xbuf.at[slot]
        a = jnp.zeros((TM_MOE, MOE_HIDDEN), F32)
        u = jnp.zeros((TM_MOE, MOE_HIDDEN), F32)
        kw = 2 * LANE
        for c in range(ROW_CHUNKS // 2):
            xk = jnp.concatenate([rows[:, 2 * c, :], rows[:, 2 * c + 1, :]], axis=1).astype(BF16)
            a = a + jnp.dot(xk, wgb[c * kw:(c + 1) * kw, :], preferred_element_type=F32)
            u = u + jnp.dot(xk, wub[c * kw:(c + 1) * kw, :], preferred_element_type=F32)
        hid = (_silu(a) * u * w_ref[...]).astype(BF16)
        out = jnp.dot(hid, wdb[...], preferred_element_type=F32)

        @pl.when(i >= 2)
        def _():
            wait_scatter(slot)

        for c in range(ROW_CHUNKS):
            obuf[slot, :, c, :] = out[:, c * LANE:(c + 1) * LANE]

        def start_out(r, carry):
            code = dst_ref[i * TM_MOE + r]
            pltpu.make_async_copy(obuf.at[slot, r], y_hbm.at[code % MOE_TOPK, code // MOE_TOPK],
                                  sem_out.at[slot]).start()
            return carry

        lax.fori_loop(0, TM_MOE, start_out, 0, unroll=8)

        @pl.when(jnp.logical_not(next_active))
        def _():
            wait_scatter(slot)

            @pl.when(i >= 1)
            def _():
                wait_scatter(1 - slot)


def _moe_call(h_all, plan, li, w_gate, w_up, w_down):
    t = h_all.shape[0]
    d = D_MODEL
    tile_e, flag, src, dst, w_slot = plan
    n_tiles = tile_e.shape[0]
    tm = TM_MOE
    w_map = lambda i, te, fl, sr, ds: (li, te[i], 0, 0)
    grid_spec = pltpu.PrefetchScalarGridSpec(
        num_scalar_prefetch=4,
        grid=(n_tiles,),
        in_specs=[pl.BlockSpec(memory_space=pl.ANY),
                  pl.BlockSpec((tm, 1), lambda i, te, fl, sr, ds: (i, 0)),
                  pl.BlockSpec((1, 1, d, MOE_HIDDEN), w_map),
                  pl.BlockSpec((1, 1, d, MOE_HIDDEN), w_map),
                  pl.BlockSpec((1, 1, MOE_HIDDEN, d), w_map)],
        out_specs=pl.BlockSpec(memory_space=pl.ANY),
        scratch_shapes=[pltpu.VMEM((2, tm, ROW_CHUNKS, LANE), F32), pltpu.VMEM((2, tm, ROW_CHUNKS, LANE), F32),
                        pltpu.VMEM((d, MOE_HIDDEN), BF16), pltpu.VMEM((d, MOE_HIDDEN), BF16),
                        pltpu.VMEM((MOE_HIDDEN, d), BF16),
                        pltpu.SemaphoreType.DMA((2,)), pltpu.SemaphoreType.DMA((2,))])
    return pl.pallas_call(
        _moe_kernel,
        grid_spec=grid_spec,
        out_shape=jax.ShapeDtypeStruct((MOE_TOPK, t + tm, ROW_CHUNKS, LANE), F32),
        compiler_params=_cparams("arbitrary"),
        name="moe_experts",
    )(tile_e, flag, src, dst, h_all, w_slot, w_gate, w_up, w_down)


def _combine_kernel(x_ref, g_ref, y_ref, o_ref):
    for c in range(ROW_CHUNKS):
        cols = slice(c * LANE, (c + 1) * LANE)
        o_ref[0, :, cols] = x_ref[0, :, cols] + g_ref[0, :, cols] * (y_ref[0, :, c, :] + y_ref[1, :, c, :])


def _combine_call(x1, gate, y_all, row0):
    b, s, d = x1.shape
    ts = min(s, 512)
    sm = gate.shape[1]
    tsm = 1 if sm == 1 else ts
    mod_map = (lambda bi, i: (bi, 0, 0)) if sm == 1 else (lambda bi, i: (bi, i, 0))
    blk0 = row0 // ts
    per_b = s // ts
    return pl.pallas_call(
        _combine_kernel,
        grid=(b, per_b),
        in_specs=[pl.BlockSpec((1, ts, d), lambda bi, i: (bi, i, 0)),
                  pl.BlockSpec((1, tsm, d), mod_map),
                  pl.BlockSpec((MOE_TOPK, ts, ROW_CHUNKS, LANE), lambda bi, i: (0, blk0 + bi * per_b + i, 0, 0))],
        out_specs=pl.BlockSpec((1, ts, d), lambda bi, i: (bi, i, 0)),
        out_shape=jax.ShapeDtypeStruct((b, s, d), F32),
        compiler_params=_cparams("parallel", "parallel"),
        name="moe_combine",
    )(x1, gate, y_all)


def _reorder_w_in(w):
    sizes = (SSD_INNER, SSD_CONV_DIM, SSD_HEADS, NSA_WIDTH, 3 * KV_ROW, 3 * NSA_HEADS, RET_HEADS * RET_KEY_DIM,
             RET_HEADS * RET_KEY_DIM, RET_WIDTH, RET_WIDTH)
    z, xbc, dt, nq, nkv, ngate, rq, rk, rv, rg = jnp.split(w, [int(v) for v in np.cumsum(sizes)[:-1]], axis=1)
    padc = lambda a: jnp.pad(a, ((0, 0), (0, LANE - a.shape[1])))
    return jnp.concatenate([z, nq, rv, rg, rq, rk, xbc, nkv, padc(dt), padc(ngate)], axis=1).astype(BF16)


def _kv_rows(a):
    return a.reshape(a.shape[:2] + (2, NSA_KV_HEADS, NSA_HEAD_DIM))


def kernel(x_prompt, x_sample, cache_cmp_kv, cache_sel_kv, cache_win_kv, state_ssm, state_conv, state_ret, page_table, c_prompt, c_sample, w_ada, b_ada, norm_mix, norm_ffn, w_in, w_out, conv_w, conv_b, dt_bias, a_log, ssd_d, ssd_norm, qk_gain, cmp_pe, cmp_w1, cmp_w2, router_group_w, router_group_b, router_expert_w, router_expert_b, expert_w_gate, expert_w_up, expert_w_down):
    bp, sp, d = x_prompt.shape
    bd, sd, _ = x_sample.shape
    n_pages = page_table.shape[1]
    past = n_pages * PAGE_SIZE
    assert d == D_MODEL and sp % LC == 0 and sp % NSA_BLOCK == 0 and sp >= NSA_WINDOW
    assert sd % SUBLANE == 0 and sd <= NSA_BLOCK and sd <= LC and cache_win_kv.shape[2] == NSA_WINDOW
    n_seq = bp + bd
    c_all = jnp.pad(jnp.concatenate([c_prompt, c_sample]), ((0, -n_seq % SUBLANE), (0, 0)))
    mod = _ada_call(c_all, w_ada, b_ada)
    cos_p, sin_p = _rotary_tables(0, sp, sp)
    cos_s, sin_s = _rotary_tables(past, sd, LC)
    cmp_pages = _feature_major_pages(cache_cmp_kv)
    sel_pages = _feature_major_pages(cache_sel_kv)
    xp = x_prompt
    xs = x_sample.reshape(1, bd * sd, d)
    outs_p, outs_s = [], []
    for li in range(DEPTH):
        mods_p = [mod[li, :bp, k * d:(k + 1) * d].reshape(bp, 1, d) for k in range(6)]
        mods_s = [jnp.repeat(mod[li, bp:n_seq, k * d:(k + 1) * d], sd, axis=0).reshape(1, bd * sd, d) for k in range(6)]
        w_in_b = _reorder_w_in(w_in[li])
        w_out_b = w_out[li].astype(BF16)
        cw = _compress_weights(cmp_pe[li], cmp_w1[li], cmp_w2[li], qk_gain[li, 4])
        cw_pages = _cmp_sample_weights(cmp_pe[li], cmp_w1[li], cmp_w2[li], qk_gain[li, 4])
        ssd_w = (conv_w[li], conv_b[li], dt_bias[li], a_log[li], ssd_d[li], ssd_norm[li])
        wr = jnp.pad(jnp.concatenate([router_group_w[li], router_expert_w[li]], axis=1),
                     ((0, 0), (0, LANE - MOE_GROUPS - MOE_EXPERTS)))
        br = jnp.pad(jnp.concatenate([router_group_b[li], router_expert_b[li]]),
                     (0, LANE - MOE_GROUPS - MOE_EXPERTS)).reshape(1, LANE)

        sh_a, sc_a, g_a, sh_f, sc_f, g_f = mods_p
        proj = _inproj_call(xp, sc_a, sh_a, norm_mix[li], w_in_b)
        y_ssd, ssm_p, conv_p = _ssd_call(proj, LC, jnp.zeros((bp, SSD_CONV - 1, SSD_CONV_DIM), F32),
                                         jnp.zeros((bp, SSD_HEADS, SSD_HEAD_DIM, SSD_STATE), F32), *ssd_w)
        o_ret, ret_p = _ret_call(proj, LC, jnp.zeros((bp, RET_HEADS, RET_KEY_DIM, RET_VAL_DIM), F32), cos_p, sin_p)
        qn, kvn_p, kvb = _prep_call(proj, sp, qk_gain[li])
        kcvc = _cmp_prompt_call(kvn_p, cw)
        o_nsa = _nsa_prompt_call(proj, qn, kvb, kcvc)
        x1_p = _outproj_call(y_ssd, o_nsa, o_ret, w_out_b, xp, g_a)
        h2_p, meta_p = _route_call(x1_p, sc_f, sh_f, norm_ffn[li], wr, br)
        gf_p = g_f

        sh_a, sc_a, g_a, sh_f, sc_f, g_f = mods_s
        proj = _inproj_call(xs, sc_a, sh_a, norm_mix[li], w_in_b).reshape(bd, sd, N_PROJ)
        proj = jnp.pad(proj, ((0, 0), (0, LC - sd), (0, 0)))
        y_ssd, ssm_s, conv_s = _ssd_call(proj, sd, state_conv[li], state_ssm[li], *ssd_w)
        o_ret, ret_s = _ret_call(proj, sd, state_ret[li], cos_s, sin_s)
        qn, kvn_s, _ = _prep_call(proj, sd, qk_gain[li])
        kcvc = _cmp_sample_call(cmp_pages, li, page_table, cw_pages)
        win_old = cache_win_kv[li].reshape(bd, NSA_WINDOW, KV_ROW)
        o_nsa = _nsa_sample_call(sel_pages, li, page_table, proj, qn, kvn_s, kcvc, win_old, past)
        o_nsa = o_nsa.reshape(bd, NSA_HEADS, sd, NSA_HEAD_DIM).transpose(0, 2, 1, 3).reshape(1, bd * sd, NSA_WIDTH)
        x1_s = _outproj_call(y_ssd[:, :sd].reshape(1, bd * sd, SSD_INNER), o_nsa,
                             o_ret[:, :sd].reshape(1, bd * sd, RET_WIDTH), w_out_b, xs, g_a)
        h2_s, meta_s = _route_call(x1_s, sc_f, sh_f, norm_ffn[li], wr, br)

        h_all = jnp.concatenate([h2_p.reshape(bp * sp, ROW_CHUNKS, LANE), h2_s.reshape(bd * sd, ROW_CHUNKS, LANE)])
        meta = jnp.concatenate([meta_p.reshape(bp * sp, LANE), meta_s.reshape(bd * sd, LANE)])
        plan = _moe_plan(meta[:, :MOE_TOPK].astype(jnp.int32), meta[:, MOE_TOPK:2 * MOE_TOPK])
        y_all = _moe_call(h_all, plan, li, expert_w_gate, expert_w_up, expert_w_down)
        xp = _combine_call(x1_p, gf_p, y_all, 0)
        xs = _combine_call(x1_s, g_f, y_all, bp * sp)

        win_s = jnp.concatenate([win_old[:, sd:], kvn_s[:, :, 2 * KV_ROW:]], axis=1)
        outs_p.append((_kv_rows(kvn_p[:, :, :KV_ROW]), _kv_rows(kvn_p[:, :, KV_ROW:2 * KV_ROW]),
                       _kv_rows(kvn_p[:, sp - NSA_WINDOW:, 2 * KV_ROW:]), ssm_p, conv_p, ret_p))
        outs_s.append((_kv_rows(kvn_s[:, :, :KV_ROW]), _kv_rows(kvn_s[:, :, KV_ROW:2 * KV_ROW]), _kv_rows(win_s),
                       ssm_s, conv_s, ret_s))
    res = [xp, xs.reshape(bd, sd, d)]
    for k in range(6):
        res.append(jnp.stack([o[k] for o in outs_p]))
        res.append(jnp.stack([o[k] for o in outs_s]))
    return tuple(res)
```

```python
import functools
import math

import numpy as np
import jax
import jax.numpy as jnp
from jax import lax
from jax.experimental import pallas as pl
from jax.experimental.pallas import tpu as pltpu

F32 = jnp.float32
BF16 = jnp.bfloat16
HI = lax.Precision.HIGHEST
EPS = 1e-6
NEG_INF = -1e30

D_MODEL = 2048
DEPTH = 2
PAGE_SIZE = 128
SSD_INNER = D_MODEL // 2
SSD_HEAD_DIM = 64
SSD_HEADS = SSD_INNER // SSD_HEAD_DIM
SSD_GROUPS = 2
SSD_STATE = 128
SSD_CONV = 4
SSD_CONV_DIM = SSD_INNER + 2 * SSD_GROUPS * SSD_STATE
NSA_WIDTH = D_MODEL // 4
NSA_HEAD_DIM = 64
NSA_HEADS = NSA_WIDTH // NSA_HEAD_DIM
NSA_KV_HEADS = 2
NSA_REP = NSA_HEADS // NSA_KV_HEADS
NSA_BLOCK = 64
NSA_TOPN = 16
NSA_WINDOW = 512
NSA_CMP_HIDDEN = 128
RET_WIDTH = D_MODEL // 4
RET_HEADS = 4
RET_VAL_DIM = RET_WIDTH // RET_HEADS
RET_KEY_DIM = RET_VAL_DIM // 2
D_MIX = SSD_INNER + NSA_WIDTH + RET_WIDTH
MOE_GROUPS = 4
MOE_EXPERTS_PER_GROUP = 8
MOE_EXPERTS = MOE_GROUPS * MOE_EXPERTS_PER_GROUP
MOE_TOPK = 2
MOE_HIDDEN = D_MODEL // 4
KV_ROW = 2 * NSA_KV_HEADS * NSA_HEAD_DIM

LANE = 128
SUBLANE = 8
C_Z, C_NQ, C_RV, C_RG, C_RQ, C_RK, C_XBC, C_NKV, C_DT, C_NG = 0, 1024, 1536, 2048, 2560, 2816, 3072, 4608, 5376, 5504
N_PROJ = 5632
LC = 128
TQ = 128
TM_MOE = 256
ROW_CHUNKS = D_MODEL // LANE
VMEM_LIMIT = 56 * 2 ** 20


def _cparams(*sem):
    return pltpu.CompilerParams(dimension_semantics=sem, vmem_limit_bytes=VMEM_LIMIT)


def _silu(x):
    return x * jax.nn.sigmoid(x)


def _nt(a, b, precision=None):
    return lax.dot_general(a, b, (((1,), (1,)), ((), ())), precision=precision, preferred_element_type=F32)


def _tn(a, b, precision=None):
    return lax.dot_general(a, b, (((0,), (0,)), ((), ())), precision=precision, preferred_element_type=F32)


def _iota(shape, dim):
    return lax.broadcasted_iota(jnp.int32, shape, dim)


def _seg_sum(x2, seg):
    n = x2.shape[-1]
    bd = (_iota((n, n), 0) // seg == _iota((n, n), 1) // seg).astype(BF16)
    hi = x2.astype(BF16)
    lo = (x2 - hi.astype(F32)).astype(BF16)
    return jnp.dot(hi, bd, preferred_element_type=F32) + jnp.dot(lo, bd, preferred_element_type=F32)


def _ada_kernel(c_ref, w_ref, b_ref, o_ref):
    o_ref[0] = jnp.dot(_silu(c_ref[...]), w_ref[0], precision=HI, preferred_element_type=F32) + b_ref[0]


def _ada_call(c_all, w_ada, b_ada):
    rows = c_all.shape[0]
    tn = 1024
    return pl.pallas_call(
        _ada_kernel,
        grid=(DEPTH, 6 * D_MODEL // tn),
        in_specs=[pl.BlockSpec((rows, D_MODEL), lambda l, j: (0, 0)),
                  pl.BlockSpec((1, D_MODEL, tn), lambda l, j: (l, 0, j)),
                  pl.BlockSpec((1, 1, tn), lambda l, j: (l, 0, j))],
        out_specs=pl.BlockSpec((1, rows, tn), lambda l, j: (l, 0, j)),
        out_shape=jax.ShapeDtypeStruct((DEPTH, rows, 6 * D_MODEL), F32),
        compiler_params=_cparams("parallel", "parallel"),
        name="ada",
    )(c_all, w_ada, b_ada.reshape(DEPTH, 1, 6 * D_MODEL))


def _mod_norm(x, gain, scale, shift):
    ms = jnp.mean(x * x, axis=-1, keepdims=True)
    return x * lax.rsqrt(ms + EPS) * gain * (1.0 + scale) + shift


def _inproj_kernel(x_ref, sc_ref, sh_ref, g_ref, w_ref, o_ref, h_ref):
    @pl.when(pl.program_id(2) == 0)
    def _():
        h_ref[...] = _mod_norm(x_ref[0], g_ref[...], sc_ref[0], sh_ref[0]).astype(BF16)

    o_ref[0] = jnp.dot(h_ref[...], w_ref[...], preferred_element_type=F32)


def _inproj_call(x, scale, shift, gain, w_bf):
    b, s, d = x.shape
    ts = min(s, 1024)
    tn = 512
    sm = scale.shape[1]
    tsm = 1 if sm == 1 else ts
    mod_map = (lambda bi, i, j: (bi, 0, 0)) if sm == 1 else (lambda bi, i, j: (bi, i, 0))
    return pl.pallas_call(
        _inproj_kernel,
        grid=(b, s // ts, N_PROJ // tn),
        in_specs=[pl.BlockSpec((1, ts, d), lambda bi, i, j: (bi, i, 0)),
                  pl.BlockSpec((1, tsm, d), mod_map),
                  pl.BlockSpec((1, tsm, d), mod_map),
                  pl.BlockSpec((1, d), lambda bi, i, j: (0, 0)),
                  pl.BlockSpec((d, tn), lambda bi, i, j: (0, j))],
        out_specs=pl.BlockSpec((1, ts, tn), lambda bi, i, j: (bi, i, j)),
        out_shape=jax.ShapeDtypeStruct((b, s, N_PROJ), F32),
        scratch_shapes=[pltpu.VMEM((ts, d), BF16)],
        compiler_params=_cparams("parallel", "parallel", "arbitrary"),
        name="in_proj",
    )(x, scale, shift, gain.reshape(1, d), w_bf)


def _ssd_kernel(n_valid, xbc_ref, z_ref, dt_ref, cprev_ref, sprev_ref, cw_ref, cb_ref, dtb_ref, alog_ref, dsk_ref,
                nrm_ref, y_ref, snew_ref, cnew_ref, cbuf, state, ybuf):
    c = pl.program_id(1)
    nc = pl.num_programs(1)
    hd, nst = SSD_HEAD_DIM, SSD_STATE
    hpg = SSD_HEADS // SSD_GROUPS

    @pl.when(c == 0)
    def _():
        cbuf[0:SUBLANE, :] = jnp.zeros((SUBLANE, SSD_CONV_DIM), F32)
        cbuf[5:8, :] = cprev_ref[0]
        state[...] = sprev_ref[0]

    cbuf[8:8 + LC, :] = xbc_ref[0]
    conv = cb_ref[...] + cw_ref[0:1, :] * cbuf[5:5 + LC, :]
    for k in range(1, SSD_CONV):
        conv = conv + cw_ref[k:k + 1, :] * cbuf[5 + k:5 + k + LC, :]
    tail = cbuf[5 + n_valid:8 + n_valid, :]
    cbuf[5:8, :] = tail
    xbc = _silu(conv)
    xs = xbc[:, :SSD_INNER]
    bm = xbc[:, SSD_INNER:SSD_INNER + SSD_GROUPS * nst].astype(BF16)
    cm = xbc[:, SSD_INNER + SSD_GROUPS * nst:].astype(BF16)

    dtr = dt_ref[0] + dtb_ref[...]
    dt = jnp.maximum(dtr, 0.0) + jnp.log1p(jnp.exp(-jnp.abs(dtr)))
    if n_valid < LC:
        dt = jnp.where(_iota((LC, LANE), 0) < n_valid, dt, 0.0)
    a = -jnp.exp(alog_ref[...])
    da = dt * a
    row = _iota((LC, LC), 0)
    col = _iota((LC, LC), 1)
    tril = row >= col
    acs = jnp.dot(tril.astype(F32), da, precision=HI, preferred_element_type=F32)
    eye = (row == col).astype(F32)
    acs_t = _nt(eye, acs, precision=HI)
    expand = (_iota((LANE, SSD_INNER), 0) == _iota((LANE, SSD_INNER), 1) // hd).astype(BF16)

    def per_head_to_lanes(v):
        hi = v.astype(BF16)
        lo = (v - hi.astype(F32)).astype(BF16)
        return (jnp.dot(hi, expand, preferred_element_type=F32) + jnp.dot(lo, expand, preferred_element_type=F32))

    e_acs = jnp.exp(acs)
    decay = jnp.exp(acs[LC - 1:LC, :] - acs)
    dt_x = per_head_to_lanes(dt)
    e_acs_x = per_head_to_lanes(e_acs)
    decay_x = per_head_to_lanes(decay)
    xdt = xs * dt_x
    xdec = (xdt * decay_x).astype(BF16)
    xdt_b = xdt.astype(BF16)
    cbs = [_nt(cm[:, g * nst:(g + 1) * nst], bm[:, g * nst:(g + 1) * nst]) for g in range(SSD_GROUPS)]
    for h in range(SSD_HEADS):
        g = h // hpg
        lmat = jnp.exp(jnp.where(tril, acs[:, h:h + 1] - acs_t[h:h + 1, :], -jnp.inf))
        m = (cbs[g] * lmat).astype(BF16)
        y_diag = jnp.dot(m, xdt_b[:, h * hd:(h + 1) * hd], preferred_element_type=F32)
        st = state[h]
        y_off = _nt(cm[:, g * nst:(g + 1) * nst], st.astype(BF16)) * e_acs_x[:, h * hd:(h + 1) * hd]
        state[h] = st * e_acs[LC - 1:LC, h:h + 1] + _tn(xdec[:, h * hd:(h + 1) * hd], bm[:, g * nst:(g + 1) * nst])
        ybuf[:, h * hd:(h + 1) * hd] = y_diag + y_off
    y = (ybuf[...] + dsk_ref[...] * xs) * _silu(z_ref[0])
    gw = SSD_INNER // SSD_GROUPS
    for g in range(SSD_GROUPS):
        seg = y[:, g * gw:(g + 1) * gw]
        ms = jnp.mean(seg * seg, axis=-1, keepdims=True)
        y_ref[0, :, g * gw:(g + 1) * gw] = seg * lax.rsqrt(ms + EPS) * nrm_ref[:, g * gw:(g + 1) * gw]

    @pl.when(c == nc - 1)
    def _():
        snew_ref[0] = state[...]
        cnew_ref[0] = tail


def _ssd_call(proj, n_valid, conv_prev, ssm_prev, conv_w, conv_b, dt_bias, a_log, ssd_d, ssd_norm):
    b, sp, _ = proj.shape
    nc = sp // LC
    pad = lambda v: jnp.pad(v, (0, LANE - SSD_HEADS)).reshape(1, LANE)
    const2 = lambda bi, c: (0, 0)
    return pl.pallas_call(
        functools.partial(_ssd_kernel, n_valid),
        grid=(b, nc),
        in_specs=[pl.BlockSpec((1, LC, SSD_CONV_DIM), lambda bi, c: (bi, c, C_XBC // SSD_CONV_DIM)),
                  pl.BlockSpec((1, LC, SSD_INNER), lambda bi, c: (bi, c, C_Z // SSD_INNER)),
                  pl.BlockSpec((1, LC, LANE), lambda bi, c: (bi, c, C_DT // LANE)),
                  pl.BlockSpec((1, SSD_CONV - 1, SSD_CONV_DIM), lambda bi, c: (bi, 0, 0)),
                  pl.BlockSpec((1, SSD_HEADS, SSD_HEAD_DIM, SSD_STATE), lambda bi, c: (bi, 0, 0, 0)),
                  pl.BlockSpec((SSD_CONV, SSD_CONV_DIM), const2),
                  pl.BlockSpec((1, SSD_CONV_DIM), const2),
                  pl.BlockSpec((1, LANE), const2),
                  pl.BlockSpec((1, LANE), const2),
                  pl.BlockSpec((1, SSD_INNER), const2),
                  pl.BlockSpec((1, SSD_INNER), const2)],
        out_specs=[pl.BlockSpec((1, LC, SSD_INNER), lambda bi, c: (bi, c, 0)),
                   pl.BlockSpec((1, SSD_HEADS, SSD_HEAD_DIM, SSD_STATE), lambda bi, c: (bi, 0, 0, 0)),
                   pl.BlockSpec((1, SSD_CONV - 1, SSD_CONV_DIM), lambda bi, c: (bi, 0, 0))],
        out_shape=[jax.ShapeDtypeStruct((b, sp, SSD_INNER), F32),
                   jax.ShapeDtypeStruct((b, SSD_HEADS, SSD_HEAD_DIM, SSD_STATE), F32),
                   jax.ShapeDtypeStruct((b, SSD_CONV - 1, SSD_CONV_DIM), F32)],
        scratch_shapes=[pltpu.VMEM((8 + LC, SSD_CONV_DIM), F32),
                        pltpu.VMEM((SSD_HEADS, SSD_HEAD_DIM, SSD_STATE), F32),
                        pltpu.VMEM((LC, SSD_INNER), F32)],
        compiler_params=_cparams("parallel", "arbitrary"),
        name="ssd",
    )(proj, proj, proj, conv_prev, ssm_prev, conv_w, conv_b.reshape(1, -1), pad(dt_bias), pad(a_log),
      jnp.repeat(ssd_d, SSD_HEAD_DIM).reshape(1, -1), ssd_norm.reshape(1, -1))


def _ret_log_g(h):
    return float(np.log1p(-np.exp2(np.float32(-5.0 - h)), dtype=np.float32))


def _ret_kernel(n_valid, q_ref, k_ref, v_ref, g_ref, cos_ref, sin_ref, rprev_ref, o_ref, rnew_ref, state):
    c = pl.program_id(1)
    nc = pl.num_programs(1)
    kd, vd = RET_KEY_DIM, RET_VAL_DIM
    half = kd // 2

    @pl.when(c == 0)
    def _():
        state[...] = rprev_ref[0]

    cos = cos_ref[...]
    sin = sin_ref[...]
    first_half = (_iota((LC, RET_HEADS * kd), 1) % kd) < half

    def rot(x):
        n = x.shape[-1]
        swapped = jnp.where(first_half, pltpu.roll(x, n - half, 1), pltpu.roll(x, half, 1))
        return x * cos + swapped * sin

    q = rot(q_ref[0])
    k = rot(k_ref[0]) * (kd ** -0.5)
    v = v_ref[0]
    gate = g_ref[0]
    ti = _iota((LC, LC), 0)
    tj = _iota((LC, LC), 1)
    diff = (ti - tj).astype(F32)
    ipos = _iota((LC, 1), 0)
    i1 = ipos.astype(F32)
    for h in range(RET_HEADS):
        lg = _ret_log_g(h)
        dmat = jnp.where(diff >= 0, jnp.exp(diff * lg), 0.0)
        q_dec = jnp.exp((i1 + 1.0) * lg)
        k_dec = jnp.where(ipos < n_valid, jnp.exp((n_valid - 1.0 - i1) * lg), 0.0)
        c_dec = math.exp(n_valid * lg)
        qh = q[:, h * kd:(h + 1) * kd].astype(BF16)
        kh = k[:, h * kd:(h + 1) * kd]
        vh = v[:, h * vd:(h + 1) * vd].astype(BF16)
        sc = _nt(qh, kh.astype(BF16)) * dmat
        intra = jnp.dot(sc.astype(BF16), vh, preferred_element_type=F32)
        rs = state[h]
        cross = jnp.dot(qh, rs.astype(BF16), preferred_element_type=F32) * q_dec
        state[h] = rs * c_dec + _tn((kh * k_dec).astype(BF16), vh)
        o = intra + cross
        ms = jnp.mean(o * o, axis=-1, keepdims=True)
        o_ref[0, :, h * vd:(h + 1) * vd] = o * lax.rsqrt(ms + EPS) * _silu(gate[:, h * vd:(h + 1) * vd])

    @pl.when(c == nc - 1)
    def _():
        rnew_ref[0] = state[...]


def _rotary_tables(offset, s_real, s_pad):
    half = RET_KEY_DIM // 2
    freqs = 1.0 / (10000.0 ** jnp.linspace(0.0, 1.0, half, dtype=F32))
    pos = (offset + jnp.arange(s_real)).astype(F32)
    ang = pos[:, None] * freqs[None, :]
    cos, sin = jnp.cos(ang), jnp.sin(ang)
    cos = jnp.tile(jnp.concatenate([cos, cos], axis=-1), (1, RET_HEADS))
    sin = jnp.tile(jnp.concatenate([-sin, sin], axis=-1), (1, RET_HEADS))
    padr = ((0, s_pad - s_real), (0, 0))
    return jnp.pad(cos, padr), jnp.pad(sin, padr)


def _ret_call(proj, n_valid, ret_prev, cos, sin):
    b, sp, _ = proj.shape
    nc = sp // LC
    qw = RET_HEADS * RET_KEY_DIM
    return pl.pallas_call(
        functools.partial(_ret_kernel, n_valid),
        grid=(b, nc),
        in_specs=[pl.BlockSpec((1, LC, qw), lambda bi, c: (bi, c, C_RQ // qw)),
                  pl.BlockSpec((1, LC, qw), lambda bi, c: (bi, c, C_RK // qw)),
                  pl.BlockSpec((1, LC, RET_WIDTH), lambda bi, c: (bi, c, C_RV // RET_WIDTH)),
                  pl.BlockSpec((1, LC, RET_WIDTH), lambda bi, c: (bi, c, C_RG // RET_WIDTH)),
                  pl.BlockSpec((LC, qw), lambda bi, c: (c, 0)),
                  pl.BlockSpec((LC, qw), lambda bi, c: (c, 0)),
                  pl.BlockSpec((1, RET_HEADS, RET_KEY_DIM, RET_VAL_DIM), lambda bi, c: (bi, 0, 0, 0))],
        out_specs=[pl.BlockSpec((1, LC, RET_WIDTH), lambda bi, c: (bi, c, 0)),
                   pl.BlockSpec((1, RET_HEADS, RET_KEY_DIM, RET_VAL_DIM), lambda bi, c: (bi, 0, 0, 0))],
        out_shape=[jax.ShapeDtypeStruct((b, sp, RET_WIDTH), F32),
                   jax.ShapeDtypeStruct((b, RET_HEADS, RET_KEY_DIM, RET_VAL_DIM), F32)],
        scratch_shapes=[pltpu.VMEM((RET_HEADS, RET_KEY_DIM, RET_VAL_DIM), F32)],
        compiler_params=_cparams("parallel", "arbitrary"),
        name="retention",
    )(proj, proj, proj, proj, cos, sin, ret_prev)


def _slope(h):
    return float(2.0 ** -(h + 1))


def _prep_kernel(nq_ref, nkv_ref, gq_ref, gk_ref, isk_ref, qn_ref, kvn_ref, kvb_ref):
    inv = 1.0 / NSA_HEAD_DIM
    nq = nq_ref[0]
    qn_ref[0] = nq * lax.rsqrt(_seg_sum(nq * nq, NSA_HEAD_DIM) * inv + EPS) * gq_ref[...]
    kv = nkv_ref[0]
    normed = kv * lax.rsqrt(_seg_sum(kv * kv, NSA_HEAD_DIM) * inv + EPS) * gk_ref[...]
    kvn = jnp.where(isk_ref[...] > 0.5, normed, kv)
    kvn_ref[0] = kvn
    kvb_ref[0] = kvn[:, KV_ROW:].astype(BF16)


def _prep_call(proj, s_real, qk_gain):
    b = proj.shape[0]
    ts = min(s_real, 512)
    hd = NSA_HEAD_DIM
    gq = (jnp.tile(qk_gain[0], NSA_HEADS) * (hd ** -0.5)).reshape(1, -1)
    ones = jnp.ones((2 * hd,), F32)
    gk = jnp.concatenate([jnp.concatenate([jnp.tile(qk_gain[1 + i], 2), ones]) for i in range(3)]).reshape(1, -1)
    isk = jnp.tile(jnp.concatenate([ones, 0.0 * ones]), 3).reshape(1, -1)
    const2 = lambda bi, i: (0, 0)
    return pl.pallas_call(
        _prep_kernel,
        grid=(b, s_real // ts),
        in_specs=[pl.BlockSpec((1, ts, NSA_WIDTH), lambda bi, i: (bi, i, C_NQ // NSA_WIDTH)),
                  pl.BlockSpec((1, ts, 3 * KV_ROW), lambda bi, i: (bi, i, C_NKV // (3 * KV_ROW))),
                  pl.BlockSpec((1, NSA_WIDTH), const2),
                  pl.BlockSpec((1, 3 * KV_ROW), const2),
                  pl.BlockSpec((1, 3 * KV_ROW), const2)],
        out_specs=[pl.BlockSpec((1, ts, NSA_WIDTH), lambda bi, i: (bi, i, 0)),
                   pl.BlockSpec((1, ts, 3 * KV_ROW), lambda bi, i: (bi, i, 0)),
                   pl.BlockSpec((1, ts, 2 * KV_ROW), lambda bi, i: (bi, i, 0))],
        out_shape=[jax.ShapeDtypeStruct((b, s_real, NSA_WIDTH), F32),
                   jax.ShapeDtypeStruct((b, s_real, 3 * KV_ROW), F32),
                   jax.ShapeDtypeStruct((b, s_real, 2 * KV_ROW), BF16)],
        compiler_params=_cparams("parallel", "parallel"),
        name="nsa_prep",
    )(proj, proj, gq, gk, isk)


def _compress_weights(cmp_pe, cmp_w1, cmp_w2, gain4):
    hd, hid = NSA_HEAD_DIM, NSA_CMP_HIDDEN
    wbig = jnp.zeros((NSA_BLOCK, KV_ROW, 4 * hid), BF16)
    w2big = jnp.zeros((4 * hid, KV_ROW), BF16)
    for cg in range(4):
        c = cg // 2
        wbig = wbig.at[:, cg * hd:(cg + 1) * hd, cg * hid:(cg + 1) * hid].set(cmp_w1[c].astype(BF16))
        w2big = w2big.at[cg * hid:(cg + 1) * hid, cg * hd:(cg + 1) * hd].set(cmp_w2[c].astype(BF16))
    pe_row = jnp.concatenate([cmp_pe[0], cmp_pe[0], cmp_pe[1], cmp_pe[1]], axis=-1)
    g4 = jnp.concatenate([jnp.tile(gain4, 2), jnp.ones((2 * hd,), F32)]).reshape(1, -1)
    return pe_row, wbig, w2big, g4


def _compress_core(get_rows, nblk, pe_ref, wbig_ref, w2_ref, g4_ref):
    acc = jnp.zeros((nblk, 4 * NSA_CMP_HIDDEN), F32)
    for l in range(NSA_BLOCK):
        x = get_rows(l) + pe_ref[l:l + 1, :]
        acc = acc + jnp.dot(x.astype(BF16), wbig_ref[l], preferred_element_type=F32)
    out = jnp.dot(_silu(acc).astype(BF16), w2_ref[...], preferred_element_type=F32)
    ss = _seg_sum(out * out, NSA_HEAD_DIM)
    normed = out * lax.rsqrt(ss * (1.0 / NSA_HEAD_DIM) + EPS) * g4_ref[...]
    return jnp.where(_iota(out.shape, 1) < 2 * NSA_HEAD_DIM, normed, out)


def _cmp_prompt_kernel(nblk, kv_ref, pe_ref, wbig_ref, w2_ref, g4_ref, o_ref):
    o_ref[0] = _compress_core(lambda l: kv_ref[0, :, l, :], nblk, pe_ref, wbig_ref, w2_ref, g4_ref)


def _cmp_prompt_call(kvn, cw):
    b, s, _ = kvn.shape
    nblk = s // NSA_BLOCK
    pe_row, wbig, w2big, g4 = cw
    kv4 = kvn.reshape(b, nblk, NSA_BLOCK, 3 * KV_ROW)
    return pl.pallas_call(
        functools.partial(_cmp_prompt_kernel, nblk),
        grid=(b,),
        in_specs=[pl.BlockSpec((1, nblk, NSA_BLOCK, KV_ROW), lambda bi: (bi, 0, 0, 0)),
                  pl.BlockSpec(pe_row.shape, lambda bi: (0, 0)),
                  pl.BlockSpec(wbig.shape, lambda bi: (0, 0, 0)),
                  pl.BlockSpec(w2big.shape, lambda bi: (0, 0)),
                  pl.BlockSpec(g4.shape, lambda bi: (0, 0))],
        out_specs=pl.BlockSpec((1, nblk, KV_ROW), lambda bi: (bi, 0, 0)),
        out_shape=jax.ShapeDtypeStruct((b, nblk, KV_ROW), F32),
        compiler_params=_cparams("arbitrary"),
        name="nsa_cmp_prompt",
    )(kv4, pe_row, wbig, w2big, g4)


def _masked_softmax(s, mask):
    s = jnp.where(mask, s, NEG_INF)
    p = jnp.exp(s - jnp.max(s, axis=-1, keepdims=True))
    return p / jnp.sum(p, axis=-1, keepdims=True) * mask.astype(F32)


def _select_blocks(imp, blk, n_top, axis=-1):
    sel = jnp.zeros(imp.shape, F32)
    big = imp.shape[axis]
    for _ in range(n_top):
        m = jnp.max(imp, axis=axis, keepdims=True)
        idx = jnp.min(jnp.where(imp == m, blk, big), axis=axis, keepdims=True)
        hit = blk == idx
        sel = jnp.where(hit & (m >= 0.0), 1.0, sel)
        imp = jnp.where(hit, -2.0, imp)
    return sel


M_FLOOR = -1e29


def _nsa_prompt_kernel(nbc, s_len, tk, wk, q_ref, gate_ref, cmp_ref, ksel_ref, kwin_ref, o_ref, bias_ref, dist_ref):
    qi = pl.program_id(1)
    hd, rep, nh = NSA_HEAD_DIM, NSA_REP, NSA_HEADS
    q = q_ref[0]
    qb = q.astype(BF16)
    gates = jax.nn.sigmoid(gate_ref[0])
    t = qi * TQ + _iota((TQ, 1), 0)
    t_row = qi * TQ + _iota((1, TQ), 1)
    kcvc = cmp_ref[0]
    blk = _iota((nbc, TQ), 0)
    dist_c = t_row - ((blk + 1) * NSA_BLOCK - 1)
    mask_c = dist_c >= 0
    mask_cf = mask_c.astype(F32)
    dist_cf = dist_c.astype(F32)
    cur = t_row // NSA_BLOCK
    o_cmp = []
    sels = []
    for g in range(NSA_KV_HEADS):
        kc = kcvc[:, g * hd:(g + 1) * hd]
        vc = kcvc[:, 2 * hd + g * hd:2 * hd + (g + 1) * hd]
        imp = jnp.zeros((nbc, TQ), F32)
        for r in range(rep):
            h = g * rep + r
            s = jnp.where(mask_c, _nt(kc, q[:, h * hd:(h + 1) * hd], precision=HI) - _slope(h) * dist_cf, NEG_INF)
            p = jnp.exp(s - jnp.max(s, axis=0, keepdims=True))
            p = p / jnp.sum(p, axis=0, keepdims=True) * mask_cf
            imp = imp + p
            o_cmp.append(_tn(p, vc, precision=HI))
        imp = jnp.where(blk < cur, imp, -1.0)
        sel = jnp.maximum(_select_blocks(imp, blk, NSA_TOPN - 1, axis=0), (blk == cur).astype(F32))
        sels.append(sel.astype(BF16))

    n_kt = (qi * TQ + TQ + tk - 1) // tk

    def fill(kt, carry):
        k0 = pl.multiple_of(kt * tk, tk)
        dist = t - (k0 + _iota((TQ, tk), 1))
        dist_ref[:, pl.ds(k0, tk)] = dist.astype(F32)
        expand = (_iota((nbc, tk), 0) == (k0 + _iota((nbc, tk), 1)) // NSA_BLOCK).astype(BF16)
        for g in range(NSA_KV_HEADS):
            picked = _tn(sels[g], expand) > 0.5
            bias_ref[g, :, pl.ds(k0, tk)] = jnp.where(picked & (dist >= 0), 0.0, NEG_INF)
        return carry

    lax.fori_loop(0, n_kt, fill, 0)

    start_w = pl.multiple_of(jnp.maximum(qi * TQ + TQ - wk, 0), TQ)
    kw_all = kwin_ref[0, pl.ds(start_w, wk), :]
    dist_w = t - (start_w + _iota((TQ, wk), 1))
    bias_w = jnp.where((dist_w >= 0) & (dist_w <= NSA_WINDOW), 0.0, NEG_INF)
    dist_wf = dist_w.astype(F32)
    rows_g = rep * TQ
    for g in range(NSA_KV_HEADS):
        heads = range(g * rep, (g + 1) * rep)
        qg = jnp.concatenate([qb[:, h * hd:(h + 1) * hd] for h in heads], axis=0)

        def stacked(bias, dist, heads=heads):
            return jnp.concatenate([bias - _slope(h) * dist for h in heads], axis=0)

        def body(kt, carry, g=g, qg=qg, stacked=stacked):
            m, l, acc = carry
            k0 = pl.multiple_of(kt * tk, tk)
            kk = ksel_ref[0, pl.ds(k0, tk), g * hd:(g + 1) * hd]
            vv = ksel_ref[0, pl.ds(k0, tk), 2 * hd + g * hd:2 * hd + (g + 1) * hd]
            s = _nt(qg, kk) + stacked(bias_ref[g, :, pl.ds(k0, tk)], dist_ref[:, pl.ds(k0, tk)])
            m_new = jnp.maximum(m, jnp.max(s, axis=-1, keepdims=True))
            alpha = jnp.exp(m - m_new)
            p = jnp.exp(s - m_new)
            l = alpha * l + jnp.sum(p, axis=-1, keepdims=True)
            acc = alpha * acc + jnp.dot(p.astype(BF16), vv, preferred_element_type=F32)
            return m_new, l, acc

        init = (jnp.full((rows_g, 1), M_FLOOR, F32), jnp.zeros((rows_g, 1), F32), jnp.zeros((rows_g, hd), F32))
        _, l_s, acc_s = lax.fori_loop(0, n_kt, body, init)
        o_sel = acc_s / l_s
        s_w = _nt(qg, kw_all[:, g * hd:(g + 1) * hd]) + stacked(bias_w, dist_wf)
        p_w = jnp.exp(s_w - jnp.max(s_w, axis=-1, keepdims=True))
        o_win = jnp.dot(p_w.astype(BF16), kw_all[:, 2 * hd + g * hd:2 * hd + (g + 1) * hd],
                        preferred_element_type=F32) / jnp.sum(p_w, axis=-1, keepdims=True)
        for j, h in enumerate(heads):
            rs = slice(j * TQ, (j + 1) * TQ)
            o_ref[0, :, h * hd:(h + 1) * hd] = (gates[:, h:h + 1] * o_cmp[h] + gates[:, nh + h:nh + h + 1] * o_sel[rs]
                                                + gates[:, 2 * nh + h:2 * nh + h + 1] * o_win[rs])


def _nsa_prompt_call(proj, qn, kvb, kcvc):
    b, s, _ = qn.shape
    nbc = kcvc.shape[1]
    tk = min(512, s)
    wk = min(NSA_WINDOW + TQ, s)
    return pl.pallas_call(
        functools.partial(_nsa_prompt_kernel, nbc, s, tk, wk),
        grid=(b, s // TQ),
        in_specs=[pl.BlockSpec((1, TQ, NSA_WIDTH), lambda bi, i: (bi, i, 0)),
                  pl.BlockSpec((1, TQ, LANE), lambda bi, i: (bi, i, C_NG // LANE)),
                  pl.BlockSpec((1, nbc, KV_ROW), lambda bi, i: (bi, 0, 0)),
                  pl.BlockSpec((1, s, KV_ROW), lambda bi, i: (bi, 0, 0)),
                  pl.BlockSpec((1, s, KV_ROW), lambda bi, i: (bi, 0, 1))],
        out_specs=pl.BlockSpec((1, TQ, NSA_WIDTH), lambda bi, i: (bi, i, 0)),
        out_shape=jax.ShapeDtypeStruct((b, s, NSA_WIDTH), F32),
        scratch_shapes=[pltpu.VMEM((NSA_KV_HEADS, TQ, s), F32), pltpu.VMEM((TQ, s), F32)],
        compiler_params=_cparams("parallel", "arbitrary"),
        name="nsa_prompt",
    )(qn, proj, kcvc, kvb, kvb)


def _feature_major_pages(cache):
    return cache.transpose(0, 1, 3, 4, 5, 2).reshape(cache.shape[0], cache.shape[1], KV_ROW, PAGE_SIZE)


def _page_copies(cache_hbm, li, pt_ref, b, step, dst_of, sem, slot, pps):
    return [pltpu.make_async_copy(cache_hbm.at[li, pt_ref[b, step * pps + k]], dst_of(slot, k), sem.at[slot])
            for k in range(pps)]


def _pipelined_pages(cache_hbm, li, pt_ref, dst_of, sem, pps):
    b, s, ns = pl.program_id(0), pl.program_id(1), pl.num_programs(1)
    slot = s % 2
    copies = functools.partial(_page_copies, cache_hbm, li, pt_ref, b, dst_of=dst_of, sem=sem, pps=pps)

    @pl.when(s == 0)
    def _():
        for cp in copies(step=s, slot=slot):
            cp.start()

    @pl.when(s + 1 < ns)
    def _():
        for cp in copies(step=s + 1, slot=1 - slot):
            cp.start()

    for cp in copies(step=s, slot=slot):
        cp.wait()
    return slot


def _cmp_sample_weights(cmp_pe, cmp_w1, cmp_w2, gain4):
    wt = cmp_w1.transpose(0, 2, 1, 3).astype(BF16)
    z = jnp.zeros_like(wt)
    w1 = jnp.concatenate([jnp.concatenate([wt, z], axis=-1), jnp.concatenate([z, wt], axis=-1)], axis=2)
    w2 = cmp_w2.astype(BF16)
    z2 = jnp.zeros_like(w2)
    w2bd = jnp.concatenate([jnp.concatenate([w2, z2], axis=-1), jnp.concatenate([z2, w2], axis=-1)], axis=1)
    pe_t = jnp.tile(cmp_pe.transpose(0, 2, 1), (1, 1, 2)).reshape(2 * NSA_HEAD_DIM, 2 * NSA_BLOCK)
    return pe_t, w1, w2bd, jnp.tile(gain4, 2).reshape(1, -1)


def _cmp_sample_kernel(li, pps, pt_ref, cache_hbm, pe_ref, w1_ref, w2_ref, g_ref, o_ref, buf, sem):
    slot = _pipelined_pages(cache_hbm, li, pt_ref, lambda sl, k: buf.at[sl, k], sem, pps)
    pages = buf.at[slot]
    hd = NSA_HEAD_DIM
    for c in range(2):
        acc = jnp.zeros((NSA_KV_HEADS * pps, 2 * NSA_CMP_HIDDEN), F32)
        for d in range(hd):
            x = jnp.concatenate([pages[:, (2 * c + g) * hd + d, :] for g in range(NSA_KV_HEADS)], axis=0)
            x = x + pe_ref[c * hd + d:c * hd + d + 1, :]
            acc = acc + jnp.dot(x.astype(BF16), w1_ref[c, d], preferred_element_type=F32)
        out = jnp.dot(_silu(acc).astype(BF16), w2_ref[c], preferred_element_type=F32)
        if c == 0:
            out = out * lax.rsqrt(_seg_sum(out * out, hd) * (1.0 / hd) + EPS) * g_ref[...]
        for g in range(NSA_KV_HEADS):
            o_ref[0, :, (2 * c + g) * 2 * hd:(2 * c + g + 1) * 2 * hd] = out[g * pps:(g + 1) * pps]


def _cmp_sample_call(cache_t, li, page_table, cw):
    bd, n_pages = page_table.shape
    pps = math.gcd(n_pages, 64)
    pe_t, w1, w2bd, gain = cw
    hd = NSA_HEAD_DIM
    grid_spec = pltpu.PrefetchScalarGridSpec(
        num_scalar_prefetch=1,
        grid=(bd, n_pages // pps),
        in_specs=[pl.BlockSpec(memory_space=pl.ANY),
                  pl.BlockSpec(pe_t.shape, lambda b, s, pt: (0, 0)),
                  pl.BlockSpec(w1.shape, lambda b, s, pt: (0, 0, 0, 0)),
                  pl.BlockSpec(w2bd.shape, lambda b, s, pt: (0, 0, 0)),
                  pl.BlockSpec(gain.shape, lambda b, s, pt: (0, 0))],
        out_specs=pl.BlockSpec((1, pps, 2 * KV_ROW), lambda b, s, pt: (b, s, 0)),
        scratch_shapes=[pltpu.VMEM((2, pps, KV_ROW, PAGE_SIZE), F32), pltpu.SemaphoreType.DMA((2,))])
    packed = pl.pallas_call(
        functools.partial(_cmp_sample_kernel, li, pps),
        grid_spec=grid_spec,
        out_shape=jax.ShapeDtypeStruct((bd, n_pages, 2 * KV_ROW), F32),
        compiler_params=_cparams("arbitrary", "arbitrary"),
        name="nsa_cmp_sample",
    )(page_table, cache_t, pe_t, w1, w2bd, gain)
    return packed.reshape(bd, n_pages, 4, 2, hd).transpose(0, 1, 3, 2, 4).reshape(bd, 2 * n_pages, KV_ROW)


def _pick_group(x, rows_g0):
    w = x.shape[-1] // 2
    return jnp.where(rows_g0, x[:, :w], x[:, w:])


def _nsa_sample_kernel(li, pps, past, n_tok, pt_ref, cache_hbm, q_ref, gate_ref, cmp_ref, knew_ref, wold_ref, wnew_ref,
                       o_ref, buf, sem, qbd_ref, sel_ref, ocw_ref, m_ref, l_ref, acc_ref):
    s, ns = pl.program_id(1), pl.num_programs(1)
    hd, nh, rep = NSA_HEAD_DIM, NSA_HEADS, NSA_REP
    rows = nh * n_tok
    nbc = past // NSA_BLOCK
    tk = pps * PAGE_SIZE
    slot = _pipelined_pages(cache_hbm, li, pt_ref,
                            lambda sl, k: buf.at[sl, :, pl.ds(k * PAGE_SIZE, PAGE_SIZE)], sem, pps)
    ridx = _iota((rows, 1), 0)
    rows_g0 = ridx < rep * n_tok
    tok = ridx % n_tok
    t = past + tok
    slope = jnp.exp2(-(ridx // n_tok + 1).astype(F32))

    @pl.when(s == 0)
    def _():
        qtok = q_ref[0]
        qrows = jnp.concatenate([qtok[:, h * hd:(h + 1) * hd] for h in range(nh)], axis=0)
        zero = jnp.zeros_like(qrows)
        qbd = jnp.where(rows_g0, jnp.concatenate([qrows, zero], axis=1), jnp.concatenate([zero, qrows], axis=1))
        qbd_ref[...] = qbd
        qbb = qbd.astype(BF16)
        kcvc = cmp_ref[0]
        blk = _iota((rows, nbc), 1)
        dist_c = t - ((blk + 1) * NSA_BLOCK - 1)
        p_c = _masked_softmax(_nt(qbd, kcvc[:, :2 * hd], precision=HI) - slope * dist_c.astype(F32), dist_c >= 0)
        o_c = _pick_group(jnp.dot(p_c, kcvc[:, 2 * hd:], precision=HI, preferred_element_type=F32), rows_g0)
        blk_t = _iota((n_tok, nbc), 1)
        cur_t = (past + _iota((n_tok, 1), 0)) // NSA_BLOCK
        sels = []
        for g in range(NSA_KV_HEADS):
            imp = p_c[g * rep * n_tok:(g * rep + 1) * n_tok]
            for r in range(1, rep):
                imp = imp + p_c[(g * rep + r) * n_tok:(g * rep + r + 1) * n_tok]
            imp = jnp.where(blk_t < cur_t, imp, -1.0)
            sels += [_select_blocks(imp, blk_t, NSA_TOPN - 1)] * rep
        sel_ref[...] = jnp.concatenate(sels, axis=0).astype(BF16)
        knew = knew_ref[0]
        dist_n = tok - _iota((rows, n_tok), 1)
        mask_n = dist_n >= 0
        s_n = jnp.where(mask_n, _nt(qbb, knew[:, :2 * hd].astype(BF16)) - slope * dist_n.astype(F32), NEG_INF)
        m0 = jnp.max(s_n, axis=-1, keepdims=True)
        p_n = jnp.where(mask_n, jnp.exp(s_n - m0), 0.0)
        m_ref[...] = m0
        l_ref[...] = jnp.sum(p_n, axis=-1, keepdims=True)
        acc_ref[...] = jnp.dot(p_n.astype(BF16), knew[:, 2 * hd:].astype(BF16), preferred_element_type=F32)
        kw = jnp.concatenate([wold_ref[0], wnew_ref[0]], axis=0)
        dist_w = tok + NSA_WINDOW - _iota((rows, NSA_WINDOW + n_tok), 1)
        mask_w = (dist_w >= 0) & (dist_w <= NSA_WINDOW)
        p_w = _masked_softmax(_nt(qbb, kw[:, :2 * hd].astype(BF16)) - slope * dist_w.astype(F32), mask_w)
        o_w = _pick_group(jnp.dot(p_w.astype(BF16), kw[:, 2 * hd:].astype(BF16), preferred_element_type=F32), rows_g0)
        ocw_ref[0] = o_c
        ocw_ref[1] = o_w

    pages = buf.at[slot]
    k_t = pages[:2 * hd, :].astype(BF16)
    v_t = pages[2 * hd:, :].astype(BF16)
    k0 = s * tk
    expand = (_iota((nbc, tk), 0) == (k0 + _iota((nbc, tk), 1)) // NSA_BLOCK).astype(BF16)
    mask = jnp.dot(sel_ref[...], expand, preferred_element_type=F32) > 0.5
    dist = t - (k0 + _iota((rows, tk), 1))
    qk = jnp.dot(qbd_ref[...].astype(BF16), k_t, preferred_element_type=F32)
    sc = jnp.where(mask, qk - slope * dist.astype(F32), NEG_INF)
    m_old = m_ref[...]
    m_new = jnp.maximum(m_old, jnp.max(sc, axis=-1, keepdims=True))
    alpha = jnp.exp(m_old - m_new)
    p = jnp.where(mask, jnp.exp(sc - m_new), 0.0)
    m_ref[...] = m_new
    l_ref[...] = alpha * l_ref[...] + jnp.sum(p, axis=-1, keepdims=True)
    acc_ref[...] = alpha * acc_ref[...] + _nt(p.astype(BF16), v_t)

    @pl.when(s == ns - 1)
    def _():
        o_s = _pick_group(acc_ref[...] / l_ref[...], rows_g0)
        gates = jax.nn.sigmoid(gate_ref[0])

        def gate_rows(br):
            return jnp.concatenate([gates[:, br * nh + h:br * nh + h + 1] for h in range(nh)], axis=0)

        o_ref[0] = gate_rows(0) * ocw_ref[0] + gate_rows(1) * o_s + gate_rows(2) * ocw_ref[1]


def _nsa_sample_call(cache_t, li, page_table, proj, qn, kvn, kcvc, win_old, past):
    bd, n_pages = page_table.shape
    n_tok = qn.shape[1]
    pps = math.gcd(n_pages, 32)
    rows = NSA_HEADS * n_tok
    nbc = kcvc.shape[1]
    grid_spec = pltpu.PrefetchScalarGridSpec(
        num_scalar_prefetch=1,
        grid=(bd, n_pages // pps),
        in_specs=[pl.BlockSpec(memory_space=pl.ANY),
                  pl.BlockSpec((1, n_tok, NSA_WIDTH), lambda b, s, pt: (b, 0, 0)),
                  pl.BlockSpec((1, n_tok, LANE), lambda b, s, pt: (b, 0, C_NG // LANE)),
                  pl.BlockSpec((1, nbc, KV_ROW), lambda b, s, pt: (b, 0, 0)),
                  pl.BlockSpec((1, n_tok, KV_ROW), lambda b, s, pt: (b, 0, 1)),
                  pl.BlockSpec((1, NSA_WINDOW, KV_ROW), lambda b, s, pt: (b, 0, 0)),
                  pl.BlockSpec((1, n_tok, KV_ROW), lambda b, s, pt: (b, 0, 2))],
        out_specs=pl.BlockSpec((1, rows, NSA_HEAD_DIM), lambda b, s, pt: (b, 0, 0)),
        scratch_shapes=[pltpu.VMEM((2, KV_ROW, pps * PAGE_SIZE), F32),
                        pltpu.SemaphoreType.DMA((2,)),
                        pltpu.VMEM((rows, 2 * NSA_HEAD_DIM), F32),
                        pltpu.VMEM((rows, nbc), BF16),
                        pltpu.VMEM((2, rows, NSA_HEAD_DIM), F32),
                        pltpu.VMEM((rows, 1), F32),
                        pltpu.VMEM((rows, 1), F32),
                        pltpu.VMEM((rows, 2 * NSA_HEAD_DIM), F32)])
    return pl.pallas_call(
        functools.partial(_nsa_sample_kernel, li, pps, past, n_tok),
        grid_spec=grid_spec,
        out_shape=jax.ShapeDtypeStruct((bd, rows, NSA_HEAD_DIM), F32),
        compiler_params=_cparams("arbitrary", "arbitrary"),
        name="nsa_sample",
    )(page_table, cache_t, qn, proj, kcvc, kvn, win_old, kvn)


def _outproj_kernel(y_ref, n_ref, r_ref, w_ref, x_ref, g_ref, o_ref, mix_ref):
    @pl.when(pl.program_id(2) == 0)
    def _():
        mix_ref[:, :SSD_INNER] = y_ref[0].astype(BF16)
        mix_ref[:, SSD_INNER:SSD_INNER + NSA_WIDTH] = n_ref[0].astype(BF16)
        mix_ref[:, SSD_INNER + NSA_WIDTH:] = r_ref[0].astype(BF16)

    o_ref[0] = x_ref[0] + g_ref[0] * jnp.dot(mix_ref[...], w_ref[...], preferred_element_type=F32)


def _outproj_call(y, o_nsa, o_ret, w_bf, x, gate):
    b, s, d = x.shape
    ts = min(s, 1024)
    tn = 512
    sm = gate.shape[1]
    tsm = 1 if sm == 1 else ts
    mod_map = (lambda bi, i, j: (bi, 0, j)) if sm == 1 else (lambda bi, i, j: (bi, i, j))
    row_map = lambda bi, i, j: (bi, i, 0)
    return pl.pallas_call(
        _outproj_kernel,
        grid=(b, s // ts, d // tn),
        in_specs=[pl.BlockSpec((1, ts, SSD_INNER), row_map),
                  pl.BlockSpec((1, ts, NSA_WIDTH), row_map),
                  pl.BlockSpec((1, ts, RET_WIDTH), row_map),
                  pl.BlockSpec((D_MIX, tn), lambda bi, i, j: (0, j)),
                  pl.BlockSpec((1, ts, tn), lambda bi, i, j: (bi, i, j)),
                  pl.BlockSpec((1, tsm, tn), mod_map)],
        out_specs=pl.BlockSpec((1, ts, tn), lambda bi, i, j: (bi, i, j)),
        out_shape=jax.ShapeDtypeStruct((b, s, d), F32),
        scratch_shapes=[pltpu.VMEM((ts, D_MIX), BF16)],
        compiler_params=_cparams("parallel", "parallel", "arbitrary"),
        name="out_proj",
    )(y, o_nsa, o_ret, w_bf, x, gate)


def _route_kernel(x_ref, sc_ref, sh_ref, g_ref, wr_ref, br_ref, h_ref, meta_ref, cnt_ref):
    h = _mod_norm(x_ref[0], g_ref[...], sc_ref[0], sh_ref[0])
    for c in range(ROW_CHUNKS):
        h_ref[0, :, c, :] = h[:, c * LANE:(c + 1) * LANE]
    logit = jnp.dot(h, wr_ref[...], precision=HI, preferred_element_type=F32) + br_ref[...]
    lane = _iota(logit.shape, 1)
    first = lambda hit: jnp.min(jnp.where(hit, lane, LANE), axis=-1, keepdims=True)
    is_g = lane < MOE_GROUPS
    gl = jnp.where(is_g, logit, -jnp.inf)
    gmax = jnp.max(gl, axis=-1, keepdims=True)
    g_idx = first(gl == gmax)
    g_p = 1.0 / jnp.sum(jnp.where(is_g, jnp.exp(logit - gmax), 0.0), axis=-1, keepdims=True)
    e_lane = lane - MOE_GROUPS
    in_grp = (e_lane >= 0) & (e_lane < MOE_EXPERTS) & (e_lane // MOE_EXPERTS_PER_GROUP == g_idx)
    el = jnp.where(in_grp, logit, -jnp.inf)
    v1 = jnp.max(el, axis=-1, keepdims=True)
    i1 = first(el == v1)
    el = jnp.where(lane == i1, -jnp.inf, el)
    v2 = jnp.max(el, axis=-1, keepdims=True)
    i2 = first(el == v2)
    e = jnp.exp(v2 - v1)
    w1 = g_p / (1.0 + e)
    w2 = g_p * e / (1.0 + e)
    meta = jnp.where(lane == 0, (i1 - MOE_GROUPS).astype(F32), 0.0)
    meta = jnp.where(lane == 1, (i2 - MOE_GROUPS).astype(F32), meta)
    meta = jnp.where(lane == 2, w1, meta)
    meta_ref[0] = jnp.where(lane == 3, w2, meta)
    cnt_ref[0, 0] = jnp.sum(((lane == i1) | (lane == i2)).astype(F32), axis=0, keepdims=True)


def _route_call(x1, scale, shift, gain, wr, br):
    b, s, d = x1.shape
    ts = min(s, 512)
    sm = scale.shape[1]
    tsm = 1 if sm == 1 else ts
    mod_map = (lambda bi, i: (bi, 0, 0)) if sm == 1 else (lambda bi, i: (bi, i, 0))
    return pl.pallas_call(
        _route_kernel,
        grid=(b, s // ts),
        in_specs=[pl.BlockSpec((1, ts, d), lambda bi, i: (bi, i, 0)),
                  pl.BlockSpec((1, tsm, d), mod_map),
                  pl.BlockSpec((1, tsm, d), mod_map),
                  pl.BlockSpec((1, d), lambda bi, i: (0, 0)),
                  pl.BlockSpec((d, LANE), lambda bi, i: (0, 0)),
                  pl.BlockSpec((1, LANE), lambda bi, i: (0, 0))],
        out_specs=[pl.BlockSpec((1, ts, ROW_CHUNKS, LANE), lambda bi, i: (bi, i, 0, 0)),
                   pl.BlockSpec((1, ts, LANE), lambda bi, i: (bi, i, 0)),
                   pl.BlockSpec((1, 1, 1, LANE), lambda bi, i: (bi, i, 0, 0))],
        out_shape=[jax.ShapeDtypeStruct((b, s, ROW_CHUNKS, LANE), F32), jax.ShapeDtypeStruct((b, s, LANE), F32),
                   jax.ShapeDtypeStruct((b, s // ts, 1, LANE), F32)],
        compiler_params=_cparams("parallel", "parallel"),
        name="moe_route",
    )(x1, scale, shift, gain.reshape(1, d), wr, br)


def _moe_plan(e_ids, w, counts):
    n_pairs = e_ids.size
    tm = TM_MOE
    n_tiles = -(-n_pairs // tm) + MOE_EXPERTS
    n_slots = n_tiles * tm
    n_tok = n_pairs // MOE_TOPK
    e_flat = e_ids.reshape(-1)
    order = jnp.argsort(e_flat, stable=True).astype(jnp.int32)
    tiles_per = (counts + tm - 1) // tm
    tile_end = jnp.cumsum(tiles_per)
    tile_start = tile_end - tiles_per
    grp_start = jnp.cumsum(counts) - counts
    n_used = tile_end[-1]
    tiles = jnp.arange(n_tiles, dtype=jnp.int32)
    tile_ids = jnp.minimum(tiles, n_used - 1)
    tile_e = jnp.sum((tile_ids[:, None] >= tile_end[None, :]).astype(jnp.int32), axis=1)
    active = tiles < n_used
    first = active & (tile_ids == tile_start[tile_e])
    flag = active.astype(jnp.int32) + first.astype(jnp.int32)
    slots = jnp.arange(n_slots, dtype=jnp.int32)
    slot_e = jnp.repeat(tile_e, tm)
    rank = slots - jnp.repeat(tile_start[tile_e], tm) * tm
    valid = jnp.repeat(active, tm) & (rank < counts[slot_e])
    pair = order[jnp.clip(grp_start[slot_e] + rank, 0, n_pairs - 1)]
    src = jnp.where(valid, pair // MOE_TOPK, 0)
    dst_row = jnp.where(valid, pair // MOE_TOPK, n_tok + slots % tm)
    dst_plane = jnp.where(valid, pair % MOE_TOPK, 0)
    w_slot = jnp.where(valid, w.reshape(-1)[pair], 0.0)
    return tile_e, flag, src, dst_row, dst_plane, w_slot.reshape(n_slots, 1)


def _moe_kernel(te_ref, flag_ref, src_ref, row_ref, plane_ref, h_hbm, w_ref, wg_ref, wu_ref, wd_ref, y_hbm, xbuf,
                obuf, xs, wgb, wub, wdb, sem_in, sem_out):
    i = pl.program_id(0)
    n_tiles = pl.num_programs(0)
    flag = flag_ref[i]
    nxt = jnp.minimum(i + 1, n_tiles - 1)
    next_active = (i + 1 < n_tiles) & (flag_ref[nxt] > 0)
    slot = i % 2

    def start_gather(tile, sl):
        def body(r, carry):
            pltpu.make_async_copy(h_hbm.at[src_ref[tile * TM_MOE + r]], xbuf.at[sl, r], sem_in.at[sl]).start()
            return carry

        lax.fori_loop(0, TM_MOE, body, 0, unroll=8)

    def wait_gather(sl):
        pltpu.make_async_copy(h_hbm.at[pl.ds(0, TM_MOE)], xbuf.at[sl], sem_in.at[sl]).wait()

    def wait_scatter(sl):
        pltpu.make_async_copy(obuf.at[sl], y_hbm.at[0, pl.ds(0, TM_MOE)], sem_out.at[sl]).wait()

    @pl.when(i == 0)
    def _():
        obuf[0] = jnp.zeros(obuf.shape[1:], F32)
        pad_rows = pl.ds(y_hbm.shape[1] - TM_MOE, TM_MOE)
        for plane in range(MOE_TOPK):
            pltpu.make_async_copy(obuf.at[0], y_hbm.at[plane, pad_rows], sem_out.at[0]).start()
        for plane in range(MOE_TOPK):
            pltpu.make_async_copy(obuf.at[0], y_hbm.at[plane, pad_rows], sem_out.at[0]).wait()

    @pl.when(flag > 0)
    def _():
        @pl.when(i == 0)
        def _():
            start_gather(i, slot)

        @pl.when(next_active)
        def _():
            start_gather(i + 1, 1 - slot)

        @pl.when(flag > 1)
        def _():
            wgb[...] = wg_ref[0, 0].astype(BF16)
            wub[...] = wu_ref[0, 0].astype(BF16)
            wdb[...] = wd_ref[0, 0].astype(BF16)

        wait_gather(slot)
        rows = xbuf.at[slot]
        for c in range(ROW_CHUNKS):
            xs[:, c * LANE:(c + 1) * LANE] = rows[:, c, :].astype(BF16)
        x = xs[...]
        a = jnp.dot(x, wgb[...], preferred_element_type=F32)
        u = jnp.dot(x, wub[...], preferred_element_type=F32)
        hid = (_silu(a) * u * w_ref[...]).astype(BF16)
        out = jnp.dot(hid, wdb[...], preferred_element_type=F32)

        @pl.when(i >= 2)
        def _():
            wait_scatter(slot)

        for c in range(ROW_CHUNKS):
            obuf[slot, :, c, :] = out[:, c * LANE:(c + 1) * LANE]

        def start_out(r, carry):
            k = i * TM_MOE + r
            pltpu.make_async_copy(obuf.at[slot, r], y_hbm.at[plane_ref[k], row_ref[k]], sem_out.at[slot]).start()
            return carry

        lax.fori_loop(0, TM_MOE, start_out, 0, unroll=8)

        @pl.when(jnp.logical_not(next_active))
        def _():
            wait_scatter(slot)

            @pl.when(i >= 1)
            def _():
                wait_scatter(1 - slot)


def _moe_call(h_all, plan, li, w_gate, w_up, w_down):
    t = h_all.shape[0]
    d = D_MODEL
    tile_e, flag, src, dst_row, dst_plane, w_slot = plan
    n_tiles = tile_e.shape[0]
    tm = TM_MOE
    w_map = lambda i, te, *_: (li, te[i], 0, 0)
    grid_spec = pltpu.PrefetchScalarGridSpec(
        num_scalar_prefetch=5,
        grid=(n_tiles,),
        in_specs=[pl.BlockSpec(memory_space=pl.ANY),
                  pl.BlockSpec((tm, 1), lambda i, *_: (i, 0)),
                  pl.BlockSpec((1, 1, d, MOE_HIDDEN), w_map),
                  pl.BlockSpec((1, 1, d, MOE_HIDDEN), w_map),
                  pl.BlockSpec((1, 1, MOE_HIDDEN, d), w_map)],
        out_specs=pl.BlockSpec(memory_space=pl.ANY),
        scratch_shapes=[pltpu.VMEM((2, tm, ROW_CHUNKS, LANE), F32), pltpu.VMEM((2, tm, ROW_CHUNKS, LANE), F32),
                        pltpu.VMEM((tm, d), BF16),
                        pltpu.VMEM((d, MOE_HIDDEN), BF16), pltpu.VMEM((d, MOE_HIDDEN), BF16),
                        pltpu.VMEM((MOE_HIDDEN, d), BF16),
                        pltpu.SemaphoreType.DMA((2,)), pltpu.SemaphoreType.DMA((2,))])
    return pl.pallas_call(
        _moe_kernel,
        grid_spec=grid_spec,
        out_shape=jax.ShapeDtypeStruct((MOE_TOPK, t + tm, ROW_CHUNKS, LANE), F32),
        compiler_params=_cparams("arbitrary"),
        name="moe_experts",
    )(tile_e, flag, src, dst_row, dst_plane, h_all, w_slot, w_gate, w_up, w_down)


def _combine_kernel(x_ref, g_ref, y_ref, o_ref):
    for c in range(ROW_CHUNKS):
        cols = slice(c * LANE, (c + 1) * LANE)
        o_ref[0, :, cols] = x_ref[0, :, cols] + g_ref[0, :, cols] * (y_ref[0, :, c, :] + y_ref[1, :, c, :])


def _combine_call(x1, gate, y_all, row0):
    b, s, d = x1.shape
    ts = min(s, 512)
    sm = gate.shape[1]
    tsm = 1 if sm == 1 else ts
    mod_map = (lambda bi, i: (bi, 0, 0)) if sm == 1 else (lambda bi, i: (bi, i, 0))
    blk0 = row0 // ts
    per_b = s // ts
    return pl.pallas_call(
        _combine_kernel,
        grid=(b, per_b),
        in_specs=[pl.BlockSpec((1, ts, d), lambda bi, i: (bi, i, 0)),
                  pl.BlockSpec((1, tsm, d), mod_map),
                  pl.BlockSpec((MOE_TOPK, ts, ROW_CHUNKS, LANE), lambda bi, i: (0, blk0 + bi * per_b + i, 0, 0))],
        out_specs=pl.BlockSpec((1, ts, d), lambda bi, i: (bi, i, 0)),
        out_shape=jax.ShapeDtypeStruct((b, s, d), F32),
        compiler_params=_cparams("parallel", "parallel"),
        name="moe_combine",
    )(x1, gate, y_all)


def _reorder_w_in(w):
    sizes = (SSD_INNER, SSD_CONV_DIM, SSD_HEADS, NSA_WIDTH, 3 * KV_ROW, 3 * NSA_HEADS, RET_HEADS * RET_KEY_DIM,
             RET_HEADS * RET_KEY_DIM, RET_WIDTH, RET_WIDTH)
    z, xbc, dt, nq, nkv, ngate, rq, rk, rv, rg = jnp.split(w, [int(v) for v in np.cumsum(sizes)[:-1]], axis=1)
    padc = lambda a: jnp.pad(a, ((0, 0), (0, LANE - a.shape[1])))
    return jnp.concatenate([z, nq, rv, rg, rq, rk, xbc, nkv, padc(dt), padc(ngate)], axis=1).astype(BF16)


def _kv_rows(a):
    return a.reshape(a.shape[:2] + (2, NSA_KV_HEADS, NSA_HEAD_DIM))


def kernel(x_prompt, x_sample, cache_cmp_kv, cache_sel_kv, cache_win_kv, state_ssm, state_conv, state_ret, page_table, c_prompt, c_sample, w_ada, b_ada, norm_mix, norm_ffn, w_in, w_out, conv_w, conv_b, dt_bias, a_log, ssd_d, ssd_norm, qk_gain, cmp_pe, cmp_w1, cmp_w2, router_group_w, router_group_b, router_expert_w, router_expert_b, expert_w_gate, expert_w_up, expert_w_down):
    bp, sp, d = x_prompt.shape
    bd, sd, _ = x_sample.shape
    n_pages = page_table.shape[1]
    past = n_pages * PAGE_SIZE
    assert d == D_MODEL and sp % LC == 0 and sp % NSA_BLOCK == 0 and sp >= NSA_WINDOW
    assert sd % SUBLANE == 0 and sd <= NSA_BLOCK and sd <= LC and cache_win_kv.shape[2] == NSA_WINDOW
    n_seq = bp + bd
    c_all = jnp.pad(jnp.concatenate([c_prompt, c_sample]), ((0, -n_seq % SUBLANE), (0, 0)))
    mod = _ada_call(c_all, w_ada, b_ada)
    cos_p, sin_p = _rotary_tables(0, sp, sp)
    cos_s, sin_s = _rotary_tables(past, sd, LC)
    cmp_pages = _feature_major_pages(cache_cmp_kv)
    sel_pages = _feature_major_pages(cache_sel_kv)
    xp = x_prompt
    xs = x_sample.reshape(1, bd * sd, d)
    outs_p, outs_s = [], []
    for li in range(DEPTH):
        mods_p = [mod[li, :bp, k * d:(k + 1) * d].reshape(bp, 1, d) for k in range(6)]
        mods_s = [jnp.repeat(mod[li, bp:n_seq, k * d:(k + 1) * d], sd, axis=0).reshape(1, bd * sd, d) for k in range(6)]
        w_in_b = _reorder_w_in(w_in[li])
        w_out_b = w_out[li].astype(BF16)
        cw = _compress_weights(cmp_pe[li], cmp_w1[li], cmp_w2[li], qk_gain[li, 4])
        cw_pages = _cmp_sample_weights(cmp_pe[li], cmp_w1[li], cmp_w2[li], qk_gain[li, 4])
        ssd_w = (conv_w[li], conv_b[li], dt_bias[li], a_log[li], ssd_d[li], ssd_norm[li])
        wr = jnp.pad(jnp.concatenate([router_group_w[li], router_expert_w[li]], axis=1),
                     ((0, 0), (0, LANE - MOE_GROUPS - MOE_EXPERTS)))
        br = jnp.pad(jnp.concatenate([router_group_b[li], router_expert_b[li]]),
                     (0, LANE - MOE_GROUPS - MOE_EXPERTS)).reshape(1, LANE)

        sh_a, sc_a, g_a, sh_f, sc_f, g_f = mods_p
        proj = _inproj_call(xp, sc_a, sh_a, norm_mix[li], w_in_b)
        y_ssd, ssm_p, conv_p = _ssd_call(proj, LC, jnp.zeros((bp, SSD_CONV - 1, SSD_CONV_DIM), F32),
                                         jnp.zeros((bp, SSD_HEADS, SSD_HEAD_DIM, SSD_STATE), F32), *ssd_w)
        o_ret, ret_p = _ret_call(proj, LC, jnp.zeros((bp, RET_HEADS, RET_KEY_DIM, RET_VAL_DIM), F32), cos_p, sin_p)
        qn, kvn_p, kvb = _prep_call(proj, sp, qk_gain[li])
        kcvc = _cmp_prompt_call(kvn_p, cw)
        o_nsa = _nsa_prompt_call(proj, qn, kvb, kcvc)
        x1_p = _outproj_call(y_ssd, o_nsa, o_ret, w_out_b, xp, g_a)
        h2_p, meta_p, cnt_p = _route_call(x1_p, sc_f, sh_f, norm_ffn[li], wr, br)
        gf_p = g_f

        sh_a, sc_a, g_a, sh_f, sc_f, g_f = mods_s
        proj = _inproj_call(xs, sc_a, sh_a, norm_mix[li], w_in_b).reshape(bd, sd, N_PROJ)
        proj = jnp.pad(proj, ((0, 0), (0, LC - sd), (0, 0)))
        y_ssd, ssm_s, conv_s = _ssd_call(proj, sd, state_conv[li], state_ssm[li], *ssd_w)
        o_ret, ret_s = _ret_call(proj, sd, state_ret[li], cos_s, sin_s)
        qn, kvn_s, _ = _prep_call(proj, sd, qk_gain[li])
        kcvc = _cmp_sample_call(cmp_pages, li, page_table, cw_pages)
        win_old = cache_win_kv[li].reshape(bd, NSA_WINDOW, KV_ROW)
        o_nsa = _nsa_sample_call(sel_pages, li, page_table, proj, qn, kvn_s, kcvc, win_old, past)
        o_nsa = o_nsa.reshape(bd, NSA_HEADS, sd, NSA_HEAD_DIM).transpose(0, 2, 1, 3).reshape(1, bd * sd, NSA_WIDTH)
        x1_s = _outproj_call(y_ssd[:, :sd].reshape(1, bd * sd, SSD_INNER), o_nsa,
                             o_ret[:, :sd].reshape(1, bd * sd, RET_WIDTH), w_out_b, xs, g_a)
        h2_s, meta_s, cnt_s = _route_call(x1_s, sc_f, sh_f, norm_ffn[li], wr, br)

        h_all = jnp.concatenate([h2_p.reshape(bp * sp, ROW_CHUNKS, LANE), h2_s.reshape(bd * sd, ROW_CHUNKS, LANE)])
        meta = jnp.concatenate([meta_p.reshape(bp * sp, LANE), meta_s.reshape(bd * sd, LANE)])
        counts = (cnt_p.sum(axis=(0, 1, 2)) + cnt_s.sum(axis=(0, 1, 2)))[MOE_GROUPS:MOE_GROUPS + MOE_EXPERTS]
        plan = _moe_plan(meta[:, :MOE_TOPK].astype(jnp.int32), meta[:, MOE_TOPK:2 * MOE_TOPK],
                         counts.astype(jnp.int32))
        y_all = _moe_call(h_all, plan, li, expert_w_gate, expert_w_up, expert_w_down)
        xp = _combine_call(x1_p, gf_p, y_all, 0)
        xs = _combine_call(x1_s, g_f, y_all, bp * sp)

        win_s = jnp.concatenate([win_old[:, sd:], kvn_s[:, :, 2 * KV_ROW:]], axis=1)
        outs_p.append((_kv_rows(kvn_p[:, :, :KV_ROW]), _kv_rows(kvn_p[:, :, KV_ROW:2 * KV_ROW]),
                       _kv_rows(kvn_p[:, sp - NSA_WINDOW:, 2 * KV_ROW:]), ssm_p, conv_p, ret_p))
        outs_s.append((_kv_rows(kvn_s[:, :, :KV_ROW]), _kv_rows(kvn_s[:, :, KV_ROW:2 * KV_ROW]), _kv_rows(win_s),
                       ssm_s, conv_s, ret_s))
    res = [xp, xs.reshape(bd, sd, d)]
    for k in range(6):
        res.append(jnp.stack([o[k] for o in outs_p]))
        res.append(jnp.stack([o[k] for o in outs_s]))
    return tuple(res)
```

```python
import functools
import math

import numpy as np
import jax
import jax.numpy as jnp
from jax import lax
from jax.experimental import pallas as pl
from jax.experimental.pallas import tpu as pltpu

F32 = jnp.float32
BF16 = jnp.bfloat16
HI = lax.Precision.HIGHEST
EPS = 1e-6
NEG_INF = -1e30

D_MODEL = 2048
DEPTH = 2
PAGE_SIZE = 128
SSD_INNER = D_MODEL // 2
SSD_HEAD_DIM = 64
SSD_HEADS = SSD_INNER // SSD_HEAD_DIM
SSD_GROUPS = 2
SSD_STATE = 128
SSD_CONV = 4
SSD_CONV_DIM = SSD_INNER + 2 * SSD_GROUPS * SSD_STATE
NSA_WIDTH = D_MODEL // 4
NSA_HEAD_DIM = 64
NSA_HEADS = NSA_WIDTH // NSA_HEAD_DIM
NSA_KV_HEADS = 2
NSA_REP = NSA_HEADS // NSA_KV_HEADS
NSA_BLOCK = 64
NSA_TOPN = 16
NSA_WINDOW = 512
NSA_CMP_HIDDEN = 128
RET_WIDTH = D_MODEL // 4
RET_HEADS = 4
RET_VAL_DIM = RET_WIDTH // RET_HEADS
RET_KEY_DIM = RET_VAL_DIM // 2
D_MIX = SSD_INNER + NSA_WIDTH + RET_WIDTH
MOE_GROUPS = 4
MOE_EXPERTS_PER_GROUP = 8
MOE_EXPERTS = MOE_GROUPS * MOE_EXPERTS_PER_GROUP
MOE_TOPK = 2
MOE_HIDDEN = D_MODEL // 4
KV_ROW = 2 * NSA_KV_HEADS * NSA_HEAD_DIM

LANE = 128
SUBLANE = 8
C_Z, C_NQ, C_RV, C_RG, C_RQ, C_RK, C_XBC, C_NKV, C_DT, C_NG = 0, 1024, 1536, 2048, 2560, 2816, 3072, 4608, 5376, 5504
N_PROJ = 5632
LC = 128
TQ = 128
TM_MOE = 256
ROW_CHUNKS = D_MODEL // LANE
VMEM_LIMIT = 56 * 2 ** 20


def _cparams(*sem):
    return pltpu.CompilerParams(dimension_semantics=sem, vmem_limit_bytes=VMEM_LIMIT)


def _silu(x):
    return x * jax.nn.sigmoid(x)


def _nt(a, b, precision=None):
    return lax.dot_general(a, b, (((1,), (1,)), ((), ())), precision=precision, preferred_element_type=F32)


def _tn(a, b, precision=None):
    return lax.dot_general(a, b, (((0,), (0,)), ((), ())), precision=precision, preferred_element_type=F32)


def _iota(shape, dim):
    return lax.broadcasted_iota(jnp.int32, shape, dim)


def _seg_sum(x2, seg):
    n = x2.shape[-1]
    bd = (_iota((n, n), 0) // seg == _iota((n, n), 1) // seg).astype(BF16)
    hi = x2.astype(BF16)
    lo = (x2 - hi.astype(F32)).astype(BF16)
    return jnp.dot(hi, bd, preferred_element_type=F32) + jnp.dot(lo, bd, preferred_element_type=F32)


def _ada_kernel(c_ref, w_ref, b_ref, o_ref):
    o_ref[0] = jnp.dot(_silu(c_ref[...]), w_ref[0], precision=HI, preferred_element_type=F32) + b_ref[0]


def _ada_call(c_all, w_ada, b_ada):
    rows = c_all.shape[0]
    tn = 1024
    return pl.pallas_call(
        _ada_kernel,
        grid=(DEPTH, 6 * D_MODEL // tn),
        in_specs=[pl.BlockSpec((rows, D_MODEL), lambda l, j: (0, 0)),
                  pl.BlockSpec((1, D_MODEL, tn), lambda l, j: (l, 0, j)),
                  pl.BlockSpec((1, 1, tn), lambda l, j: (l, 0, j))],
        out_specs=pl.BlockSpec((1, rows, tn), lambda l, j: (l, 0, j)),
        out_shape=jax.ShapeDtypeStruct((DEPTH, rows, 6 * D_MODEL), F32),
        compiler_params=_cparams("parallel", "parallel"),
        name="ada",
    )(c_all, w_ada, b_ada.reshape(DEPTH, 1, 6 * D_MODEL))


def _mod_norm(x, gain, scale, shift):
    ms = jnp.mean(x * x, axis=-1, keepdims=True)
    return x * lax.rsqrt(ms + EPS) * gain * (1.0 + scale) + shift


def _inproj_kernel(x_ref, sc_ref, sh_ref, g_ref, w_ref, o_ref, h_ref):
    @pl.when(pl.program_id(2) == 0)
    def _():
        h_ref[...] = _mod_norm(x_ref[0], g_ref[...], sc_ref[0], sh_ref[0]).astype(BF16)

    o_ref[0] = jnp.dot(h_ref[...], w_ref[...], preferred_element_type=F32)


def _inproj_call(x, scale, shift, gain, w_bf):
    b, s, d = x.shape
    ts = min(s, 1024)
    tn = 512
    sm = scale.shape[1]
    tsm = 1 if sm == 1 else ts
    mod_map = (lambda bi, i, j: (bi, 0, 0)) if sm == 1 else (lambda bi, i, j: (bi, i, 0))
    return pl.pallas_call(
        _inproj_kernel,
        grid=(b, s // ts, N_PROJ // tn),
        in_specs=[pl.BlockSpec((1, ts, d), lambda bi, i, j: (bi, i, 0)),
                  pl.BlockSpec((1, tsm, d), mod_map),
                  pl.BlockSpec((1, tsm, d), mod_map),
                  pl.BlockSpec((1, d), lambda bi, i, j: (0, 0)),
                  pl.BlockSpec((d, tn), lambda bi, i, j: (0, j))],
        out_specs=pl.BlockSpec((1, ts, tn), lambda bi, i, j: (bi, i, j)),
        out_shape=jax.ShapeDtypeStruct((b, s, N_PROJ), F32),
        scratch_shapes=[pltpu.VMEM((ts, d), BF16)],
        compiler_params=_cparams("parallel", "parallel", "arbitrary"),
        name="in_proj",
    )(x, scale, shift, gain.reshape(1, d), w_bf)


def _ssd_kernel(n_valid, xbc_ref, z_ref, dt_ref, cprev_ref, sprev_ref, cw_ref, cb_ref, dtb_ref, alog_ref, dsk_ref,
                nrm_ref, y_ref, snew_ref, cnew_ref, cbuf, state, ybuf):
    c = pl.program_id(1)
    nc = pl.num_programs(1)
    hd, nst = SSD_HEAD_DIM, SSD_STATE
    hpg = SSD_HEADS // SSD_GROUPS

    @pl.when(c == 0)
    def _():
        cbuf[0:SUBLANE, :] = jnp.zeros((SUBLANE, SSD_CONV_DIM), F32)
        cbuf[5:8, :] = cprev_ref[0]
        state[...] = sprev_ref[0]

    cbuf[8:8 + LC, :] = xbc_ref[0]
    conv = cb_ref[...] + cw_ref[0:1, :] * cbuf[5:5 + LC, :]
    for k in range(1, SSD_CONV):
        conv = conv + cw_ref[k:k + 1, :] * cbuf[5 + k:5 + k + LC, :]
    tail = cbuf[5 + n_valid:8 + n_valid, :]
    cbuf[5:8, :] = tail
    xbc = _silu(conv)
    xs = xbc[:, :SSD_INNER]
    bm = xbc[:, SSD_INNER:SSD_INNER + SSD_GROUPS * nst].astype(BF16)
    cm = xbc[:, SSD_INNER + SSD_GROUPS * nst:].astype(BF16)

    dtr = dt_ref[0] + dtb_ref[...]
    dt = jnp.maximum(dtr, 0.0) + jnp.log1p(jnp.exp(-jnp.abs(dtr)))
    if n_valid < LC:
        dt = jnp.where(_iota((LC, LANE), 0) < n_valid, dt, 0.0)
    a = -jnp.exp(alog_ref[...])
    da = dt * a
    row = _iota((LC, LC), 0)
    col = _iota((LC, LC), 1)
    tril = row >= col
    acs = jnp.dot(tril.astype(F32), da, precision=HI, preferred_element_type=F32)
    eye = (row == col).astype(F32)
    acs_t = _nt(eye, acs, precision=HI)
    expand = (_iota((LANE, SSD_INNER), 0) == _iota((LANE, SSD_INNER), 1) // hd).astype(BF16)

    def per_head_to_lanes(v):
        hi = v.astype(BF16)
        lo = (v - hi.astype(F32)).astype(BF16)
        return (jnp.dot(hi, expand, preferred_element_type=F32) + jnp.dot(lo, expand, preferred_element_type=F32))

    e_acs = jnp.exp(acs)
    decay = jnp.exp(acs[LC - 1:LC, :] - acs)
    dt_x = per_head_to_lanes(dt)
    e_acs_x = per_head_to_lanes(e_acs)
    decay_x = per_head_to_lanes(decay)
    xdt = xs * dt_x
    xdec = (xdt * decay_x).astype(BF16)
    xdt_b = xdt.astype(BF16)
    cbs = [_nt(cm[:, g * nst:(g + 1) * nst], bm[:, g * nst:(g + 1) * nst]) for g in range(SSD_GROUPS)]
    for h in range(SSD_HEADS):
        g = h // hpg
        lmat = jnp.exp(jnp.where(tril, acs[:, h:h + 1] - acs_t[h:h + 1, :], -jnp.inf))
        m = (cbs[g] * lmat).astype(BF16)
        y_diag = jnp.dot(m, xdt_b[:, h * hd:(h + 1) * hd], preferred_element_type=F32)
        st = state[h]
        y_off = _nt(cm[:, g * nst:(g + 1) * nst], st.astype(BF16)) * e_acs_x[:, h * hd:(h + 1) * hd]
        state[h] = st * e_acs[LC - 1:LC, h:h + 1] + _tn(xdec[:, h * hd:(h + 1) * hd], bm[:, g * nst:(g + 1) * nst])
        ybuf[:, h * hd:(h + 1) * hd] = y_diag + y_off
    y = (ybuf[...] + dsk_ref[...] * xs) * _silu(z_ref[0])
    gw = SSD_INNER // SSD_GROUPS
    for g in range(SSD_GROUPS):
        seg = y[:, g * gw:(g + 1) * gw]
        ms = jnp.mean(seg * seg, axis=-1, keepdims=True)
        y_ref[0, :, g * gw:(g + 1) * gw] = seg * lax.rsqrt(ms + EPS) * nrm_ref[:, g * gw:(g + 1) * gw]

    @pl.when(c == nc - 1)
    def _():
        snew_ref[0] = state[...]
        cnew_ref[0] = tail


def _ssd_call(proj, n_valid, conv_prev, ssm_prev, conv_w, conv_b, dt_bias, a_log, ssd_d, ssd_norm):
    b, sp, _ = proj.shape
    nc = sp // LC
    pad = lambda v: jnp.pad(v, (0, LANE - SSD_HEADS)).reshape(1, LANE)
    const2 = lambda bi, c: (0, 0)
    return pl.pallas_call(
        functools.partial(_ssd_kernel, n_valid),
        grid=(b, nc),
        in_specs=[pl.BlockSpec((1, LC, SSD_CONV_DIM), lambda bi, c: (bi, c, C_XBC // SSD_CONV_DIM)),
                  pl.BlockSpec((1, LC, SSD_INNER), lambda bi, c: (bi, c, C_Z // SSD_INNER)),
                  pl.BlockSpec((1, LC, LANE), lambda bi, c: (bi, c, C_DT // LANE)),
                  pl.BlockSpec((1, SSD_CONV - 1, SSD_CONV_DIM), lambda bi, c: (bi, 0, 0)),
                  pl.BlockSpec((1, SSD_HEADS, SSD_HEAD_DIM, SSD_STATE), lambda bi, c: (bi, 0, 0, 0)),
                  pl.BlockSpec((SSD_CONV, SSD_CONV_DIM), const2),
                  pl.BlockSpec((1, SSD_CONV_DIM), const2),
                  pl.BlockSpec((1, LANE), const2),
                  pl.BlockSpec((1, LANE), const2),
                  pl.BlockSpec((1, SSD_INNER), const2),
                  pl.BlockSpec((1, SSD_INNER), const2)],
        out_specs=[pl.BlockSpec((1, LC, SSD_INNER), lambda bi, c: (bi, c, 0)),
                   pl.BlockSpec((1, SSD_HEADS, SSD_HEAD_DIM, SSD_STATE), lambda bi, c: (bi, 0, 0, 0)),
                   pl.BlockSpec((1, SSD_CONV - 1, SSD_CONV_DIM), lambda bi, c: (bi, 0, 0))],
        out_shape=[jax.ShapeDtypeStruct((b, sp, SSD_INNER), F32),
                   jax.ShapeDtypeStruct((b, SSD_HEADS, SSD_HEAD_DIM, SSD_STATE), F32),
                   jax.ShapeDtypeStruct((b, SSD_CONV - 1, SSD_CONV_DIM), F32)],
        scratch_shapes=[pltpu.VMEM((8 + LC, SSD_CONV_DIM), F32),
                        pltpu.VMEM((SSD_HEADS, SSD_HEAD_DIM, SSD_STATE), F32),
                        pltpu.VMEM((LC, SSD_INNER), F32)],
        compiler_params=_cparams("parallel", "arbitrary"),
        name="ssd",
    )(proj, proj, proj, conv_prev, ssm_prev, conv_w, conv_b.reshape(1, -1), pad(dt_bias), pad(a_log),
      jnp.repeat(ssd_d, SSD_HEAD_DIM).reshape(1, -1), ssd_norm.reshape(1, -1))


def _ret_log_g(h):
    return float(np.log1p(-np.exp2(np.float32(-5.0 - h)), dtype=np.float32))


def _ret_kernel(n_valid, q_ref, k_ref, v_ref, g_ref, cos_ref, sin_ref, rprev_ref, o_ref, rnew_ref, state):
    c = pl.program_id(1)
    nc = pl.num_programs(1)
    kd, vd = RET_KEY_DIM, RET_VAL_DIM
    half = kd // 2

    @pl.when(c == 0)
    def _():
        state[...] = rprev_ref[0]

    cos = cos_ref[...]
    sin = sin_ref[...]
    first_half = (_iota((LC, RET_HEADS * kd), 1) % kd) < half

    def rot(x):
        n = x.shape[-1]
        swapped = jnp.where(first_half, pltpu.roll(x, n - half, 1), pltpu.roll(x, half, 1))
        return x * cos + swapped * sin

    q = rot(q_ref[0])
    k = rot(k_ref[0]) * (kd ** -0.5)
    v = v_ref[0]
    gate = g_ref[0]
    ti = _iota((LC, LC), 0)
    tj = _iota((LC, LC), 1)
    diff = (ti - tj).astype(F32)
    ipos = _iota((LC, 1), 0)
    i1 = ipos.astype(F32)
    for h in range(RET_HEADS):
        lg = _ret_log_g(h)
        dmat = jnp.where(diff >= 0, jnp.exp(diff * lg), 0.0)
        q_dec = jnp.exp((i1 + 1.0) * lg)
        k_dec = jnp.where(ipos < n_valid, jnp.exp((n_valid - 1.0 - i1) * lg), 0.0)
        c_dec = math.exp(n_valid * lg)
        qh = q[:, h * kd:(h + 1) * kd].astype(BF16)
        kh = k[:, h * kd:(h + 1) * kd]
        vh = v[:, h * vd:(h + 1) * vd].astype(BF16)
        sc = _nt(qh, kh.astype(BF16)) * dmat
        intra = jnp.dot(sc.astype(BF16), vh, preferred_element_type=F32)
        rs = state[h]
        cross = jnp.dot(qh, rs.astype(BF16), preferred_element_type=F32) * q_dec
        state[h] = rs * c_dec + _tn((kh * k_dec).astype(BF16), vh)
        o = intra + cross
        ms = jnp.mean(o * o, axis=-1, keepdims=True)
        o_ref[0, :, h * vd:(h + 1) * vd] = o * lax.rsqrt(ms + EPS) * _silu(gate[:, h * vd:(h + 1) * vd])

    @pl.when(c == nc - 1)
    def _():
        rnew_ref[0] = state[...]


def _rotary_tables(offset, s_real, s_pad):
    half = RET_KEY_DIM // 2
    freqs = 1.0 / (10000.0 ** jnp.linspace(0.0, 1.0, half, dtype=F32))
    pos = (offset + jnp.arange(s_real)).astype(F32)
    ang = pos[:, None] * freqs[None, :]
    cos, sin = jnp.cos(ang), jnp.sin(ang)
    cos = jnp.tile(jnp.concatenate([cos, cos], axis=-1), (1, RET_HEADS))
    sin = jnp.tile(jnp.concatenate([-sin, sin], axis=-1), (1, RET_HEADS))
    padr = ((0, s_pad - s_real), (0, 0))
    return jnp.pad(cos, padr), jnp.pad(sin, padr)


def _ret_call(proj, n_valid, ret_prev, cos, sin):
    b, sp, _ = proj.shape
    nc = sp // LC
    qw = RET_HEADS * RET_KEY_DIM
    return pl.pallas_call(
        functools.partial(_ret_kernel, n_valid),
        grid=(b, nc),
        in_specs=[pl.BlockSpec((1, LC, qw), lambda bi, c: (bi, c, C_RQ // qw)),
                  pl.BlockSpec((1, LC, qw), lambda bi, c: (bi, c, C_RK // qw)),
                  pl.BlockSpec((1, LC, RET_WIDTH), lambda bi, c: (bi, c, C_RV // RET_WIDTH)),
                  pl.BlockSpec((1, LC, RET_WIDTH), lambda bi, c: (bi, c, C_RG // RET_WIDTH)),
                  pl.BlockSpec((LC, qw), lambda bi, c: (c, 0)),
                  pl.BlockSpec((LC, qw), lambda bi, c: (c, 0)),
                  pl.BlockSpec((1, RET_HEADS, RET_KEY_DIM, RET_VAL_DIM), lambda bi, c: (bi, 0, 0, 0))],
        out_specs=[pl.BlockSpec((1, LC, RET_WIDTH), lambda bi, c: (bi, c, 0)),
                   pl.BlockSpec((1, RET_HEADS, RET_KEY_DIM, RET_VAL_DIM), lambda bi, c: (bi, 0, 0, 0))],
        out_shape=[jax.ShapeDtypeStruct((b, sp, RET_WIDTH), F32),
                   jax.ShapeDtypeStruct((b, RET_HEADS, RET_KEY_DIM, RET_VAL_DIM), F32)],
        scratch_shapes=[pltpu.VMEM((RET_HEADS, RET_KEY_DIM, RET_VAL_DIM), F32)],
        compiler_params=_cparams("parallel", "arbitrary"),
        name="retention",
    )(proj, proj, proj, proj, cos, sin, ret_prev)


def _slope(h):
    return float(2.0 ** -(h + 1))


def _prep_kernel(nq_ref, nkv_ref, gq_ref, gk_ref, isk_ref, qn_ref, kvn_ref, kvb_ref):
    inv = 1.0 / NSA_HEAD_DIM
    nq = nq_ref[0]
    qn_ref[0] = nq * lax.rsqrt(_seg_sum(nq * nq, NSA_HEAD_DIM) * inv + EPS) * gq_ref[...]
    kv = nkv_ref[0]
    normed = kv * lax.rsqrt(_seg_sum(kv * kv, NSA_HEAD_DIM) * inv + EPS) * gk_ref[...]
    kvn = jnp.where(isk_ref[...] > 0.5, normed, kv)
    kvn_ref[0] = kvn
    kvb_ref[0] = kvn[:, KV_ROW:].astype(BF16)


def _prep_call(proj, s_real, qk_gain):
    b = proj.shape[0]
    ts = min(s_real, 512)
    hd = NSA_HEAD_DIM
    gq = (jnp.tile(qk_gain[0], NSA_HEADS) * (hd ** -0.5)).reshape(1, -1)
    ones = jnp.ones((2 * hd,), F32)
    gk = jnp.concatenate([jnp.concatenate([jnp.tile(qk_gain[1 + i], 2), ones]) for i in range(3)]).reshape(1, -1)
    isk = jnp.tile(jnp.concatenate([ones, 0.0 * ones]), 3).reshape(1, -1)
    const2 = lambda bi, i: (0, 0)
    return pl.pallas_call(
        _prep_kernel,
        grid=(b, s_real // ts),
        in_specs=[pl.BlockSpec((1, ts, NSA_WIDTH), lambda bi, i: (bi, i, C_NQ // NSA_WIDTH)),
                  pl.BlockSpec((1, ts, 3 * KV_ROW), lambda bi, i: (bi, i, C_NKV // (3 * KV_ROW))),
                  pl.BlockSpec((1, NSA_WIDTH), const2),
                  pl.BlockSpec((1, 3 * KV_ROW), const2),
                  pl.BlockSpec((1, 3 * KV_ROW), const2)],
        out_specs=[pl.BlockSpec((1, ts, NSA_WIDTH), lambda bi, i: (bi, i, 0)),
                   pl.BlockSpec((1, ts, 3 * KV_ROW), lambda bi, i: (bi, i, 0)),
                   pl.BlockSpec((1, ts, 2 * KV_ROW), lambda bi, i: (bi, i, 0))],
        out_shape=[jax.ShapeDtypeStruct((b, s_real, NSA_WIDTH), F32),
                   jax.ShapeDtypeStruct((b, s_real, 3 * KV_ROW), F32),
                   jax.ShapeDtypeStruct((b, s_real, 2 * KV_ROW), BF16)],
        compiler_params=_cparams("parallel", "parallel"),
        name="nsa_prep",
    )(proj, proj, gq, gk, isk)


def _compress_weights(cmp_pe, cmp_w1, cmp_w2, gain4):
    hd, hid = NSA_HEAD_DIM, NSA_CMP_HIDDEN
    wbig = jnp.zeros((NSA_BLOCK, KV_ROW, 4 * hid), BF16)
    w2big = jnp.zeros((4 * hid, KV_ROW), BF16)
    for cg in range(4):
        c = cg // 2
        wbig = wbig.at[:, cg * hd:(cg + 1) * hd, cg * hid:(cg + 1) * hid].set(cmp_w1[c].astype(BF16))
        w2big = w2big.at[cg * hid:(cg + 1) * hid, cg * hd:(cg + 1) * hd].set(cmp_w2[c].astype(BF16))
    pe_row = jnp.concatenate([cmp_pe[0], cmp_pe[0], cmp_pe[1], cmp_pe[1]], axis=-1)
    g4 = jnp.concatenate([jnp.tile(gain4, 2), jnp.ones((2 * hd,), F32)]).reshape(1, -1)
    return pe_row, wbig, w2big, g4


def _compress_core(get_rows, nblk, pe_ref, wbig_ref, w2_ref, g4_ref):
    acc = jnp.zeros((nblk, 4 * NSA_CMP_HIDDEN), F32)
    for l in range(NSA_BLOCK):
        x = get_rows(l) + pe_ref[l:l + 1, :]
        acc = acc + jnp.dot(x.astype(BF16), wbig_ref[l], preferred_element_type=F32)
    out = jnp.dot(_silu(acc).astype(BF16), w2_ref[...], preferred_element_type=F32)
    ss = _seg_sum(out * out, NSA_HEAD_DIM)
    normed = out * lax.rsqrt(ss * (1.0 / NSA_HEAD_DIM) + EPS) * g4_ref[...]
    return jnp.where(_iota(out.shape, 1) < 2 * NSA_HEAD_DIM, normed, out)


def _cmp_prompt_kernel(nblk, kv_ref, pe_ref, wbig_ref, w2_ref, g4_ref, o_ref):
    o_ref[0] = _compress_core(lambda l: kv_ref[0, :, l, :], nblk, pe_ref, wbig_ref, w2_ref, g4_ref)


def _cmp_prompt_call(kvn, cw):
    b, s, _ = kvn.shape
    nblk = s // NSA_BLOCK
    pe_row, wbig, w2big, g4 = cw
    kv4 = kvn.reshape(b, nblk, NSA_BLOCK, 3 * KV_ROW)
    return pl.pallas_call(
        functools.partial(_cmp_prompt_kernel, nblk),
        grid=(b,),
        in_specs=[pl.BlockSpec((1, nblk, NSA_BLOCK, KV_ROW), lambda bi: (bi, 0, 0, 0)),
                  pl.BlockSpec(pe_row.shape, lambda bi: (0, 0)),
                  pl.BlockSpec(wbig.shape, lambda bi: (0, 0, 0)),
                  pl.BlockSpec(w2big.shape, lambda bi: (0, 0)),
                  pl.BlockSpec(g4.shape, lambda bi: (0, 0))],
        out_specs=pl.BlockSpec((1, nblk, KV_ROW), lambda bi: (bi, 0, 0)),
        out_shape=jax.ShapeDtypeStruct((b, nblk, KV_ROW), F32),
        compiler_params=_cparams("arbitrary"),
        name="nsa_cmp_prompt",
    )(kv4, pe_row, wbig, w2big, g4)


def _masked_softmax(s, mask):
    s = jnp.where(mask, s, NEG_INF)
    p = jnp.exp(s - jnp.max(s, axis=-1, keepdims=True))
    return p / jnp.sum(p, axis=-1, keepdims=True) * mask.astype(F32)


def _select_blocks(imp, blk, n_top, axis=-1):
    sel = jnp.zeros(imp.shape, F32)
    big = imp.shape[axis]
    for _ in range(n_top):
        m = jnp.max(imp, axis=axis, keepdims=True)
        idx = jnp.min(jnp.where(imp == m, blk, big), axis=axis, keepdims=True)
        hit = blk == idx
        sel = jnp.where(hit & (m >= 0.0), 1.0, sel)
        imp = jnp.where(hit, -2.0, imp)
    return sel


M_FLOOR = -1e29


def _nsa_prompt_kernel(nbc, s_len, tk, wk, q_ref, gate_ref, cmp_ref, ksel_ref, kwin_ref, o_ref, bias_ref, dist_ref):
    qi = pl.program_id(1)
    hd, rep, nh = NSA_HEAD_DIM, NSA_REP, NSA_HEADS
    q = q_ref[0]
    qb = q.astype(BF16)
    gates = jax.nn.sigmoid(gate_ref[0])
    t = qi * TQ + _iota((TQ, 1), 0)
    t_row = qi * TQ + _iota((1, TQ), 1)
    kcvc = cmp_ref[0]
    blk = _iota((nbc, TQ), 0)
    dist_c = t_row - ((blk + 1) * NSA_BLOCK - 1)
    mask_c = dist_c >= 0
    mask_cf = mask_c.astype(F32)
    dist_cf = dist_c.astype(F32)
    cur = t_row // NSA_BLOCK
    o_cmp = []
    sels = []
    first_kt = []
    n_kt = (qi * TQ + TQ + tk - 1) // tk
    for g in range(NSA_KV_HEADS):
        kc = kcvc[:, g * hd:(g + 1) * hd]
        vc = kcvc[:, 2 * hd + g * hd:2 * hd + (g + 1) * hd]
        imp = jnp.zeros((nbc, TQ), F32)
        for r in range(rep):
            h = g * rep + r
            s = jnp.where(mask_c, _nt(kc, q[:, h * hd:(h + 1) * hd], precision=HI) - _slope(h) * dist_cf, NEG_INF)
            p = jnp.exp(s - jnp.max(s, axis=0, keepdims=True))
            p = p / jnp.sum(p, axis=0, keepdims=True) * mask_cf
            imp = imp + p
            o_cmp.append(_tn(p, vc, precision=HI))
        imp = jnp.where(blk < cur, imp, -1.0)
        sel = jnp.maximum(_select_blocks(imp, blk, NSA_TOPN - 1, axis=0), (blk == cur).astype(F32))
        sels.append(sel.astype(BF16))
        bpt = tk // NSA_BLOCK
        first = n_kt
        for kt in reversed(range(nbc // bpt)):
            first = jnp.where(jnp.max(sel[kt * bpt:(kt + 1) * bpt, :]) > 0.0, kt, first)
        first_kt.append(first)

    def fill(kt, carry):
        k0 = pl.multiple_of(kt * tk, tk)
        dist = t - (k0 + _iota((TQ, tk), 1))
        dist_ref[:, pl.ds(k0, tk)] = dist.astype(F32)
        expand = (_iota((nbc, tk), 0) == (k0 + _iota((nbc, tk), 1)) // NSA_BLOCK).astype(BF16)
        for g in range(NSA_KV_HEADS):
            picked = _tn(sels[g], expand) > 0.5
            bias_ref[g, :, pl.ds(k0, tk)] = jnp.where(picked & (dist >= 0), 0.0, NEG_INF)
        return carry

    lax.fori_loop(jnp.minimum(first_kt[0], first_kt[1]), n_kt, fill, 0)

    start_w = pl.multiple_of(jnp.maximum(qi * TQ + TQ - wk, 0), TQ)
    kw_all = kwin_ref[0, pl.ds(start_w, wk), :]
    dist_w = t - (start_w + _iota((TQ, wk), 1))
    bias_w = jnp.where((dist_w >= 0) & (dist_w <= NSA_WINDOW), 0.0, NEG_INF)
    dist_wf = dist_w.astype(F32)
    rows_g = rep * TQ
    for g in range(NSA_KV_HEADS):
        heads = range(g * rep, (g + 1) * rep)
        qg = jnp.concatenate([qb[:, h * hd:(h + 1) * hd] for h in heads], axis=0)

        def stacked(bias, dist, heads=heads):
            return jnp.concatenate([bias - _slope(h) * dist for h in heads], axis=0)

        def body(kt, carry, g=g, qg=qg, stacked=stacked):
            m, l, acc = carry
            k0 = pl.multiple_of(kt * tk, tk)
            kk = ksel_ref[0, pl.ds(k0, tk), g * hd:(g + 1) * hd]
            vv = ksel_ref[0, pl.ds(k0, tk), 2 * hd + g * hd:2 * hd + (g + 1) * hd]
            s = _nt(qg, kk) + stacked(bias_ref[g, :, pl.ds(k0, tk)], dist_ref[:, pl.ds(k0, tk)])
            m_new = jnp.maximum(m, jnp.max(s, axis=-1, keepdims=True))
            alpha = jnp.exp(m - m_new)
            p = jnp.exp(s - m_new)
            l = alpha * l + jnp.sum(p, axis=-1, keepdims=True)
            acc = alpha * acc + jnp.dot(p.astype(BF16), vv, preferred_element_type=F32)
            return m_new, l, acc

        init = (jnp.full((rows_g, 1), M_FLOOR, F32), jnp.zeros((rows_g, 1), F32), jnp.zeros((rows_g, hd), F32))
        _, l_s, acc_s = lax.fori_loop(first_kt[g], n_kt, body, init)
        o_sel = acc_s / l_s
        s_w = _nt(qg, kw_all[:, g * hd:(g + 1) * hd]) + stacked(bias_w, dist_wf)
        p_w = jnp.exp(s_w - jnp.max(s_w, axis=-1, keepdims=True))
        o_win = jnp.dot(p_w.astype(BF16), kw_all[:, 2 * hd + g * hd:2 * hd + (g + 1) * hd],
                        preferred_element_type=F32) / jnp.sum(p_w, axis=-1, keepdims=True)
        for j, h in enumerate(heads):
            rs = slice(j * TQ, (j + 1) * TQ)
            o_ref[0, :, h * hd:(h + 1) * hd] = (gates[:, h:h + 1] * o_cmp[h] + gates[:, nh + h:nh + h + 1] * o_sel[rs]
                                                + gates[:, 2 * nh + h:2 * nh + h + 1] * o_win[rs])


def _nsa_prompt_call(proj, qn, kvb, kcvc):
    b, s, _ = qn.shape
    nbc = kcvc.shape[1]
    tk = min(512, s)
    wk = min(NSA_WINDOW + TQ, s)
    return pl.pallas_call(
        functools.partial(_nsa_prompt_kernel, nbc, s, tk, wk),
        grid=(b, s // TQ),
        in_specs=[pl.BlockSpec((1, TQ, NSA_WIDTH), lambda bi, i: (bi, i, 0)),
                  pl.BlockSpec((1, TQ, LANE), lambda bi, i: (bi, i, C_NG // LANE)),
                  pl.BlockSpec((1, nbc, KV_ROW), lambda bi, i: (bi, 0, 0)),
                  pl.BlockSpec((1, s, KV_ROW), lambda bi, i: (bi, 0, 0)),
                  pl.BlockSpec((1, s, KV_ROW), lambda bi, i: (bi, 0, 1))],
        out_specs=pl.BlockSpec((1, TQ, NSA_WIDTH), lambda bi, i: (bi, i, 0)),
        out_shape=jax.ShapeDtypeStruct((b, s, NSA_WIDTH), F32),
        scratch_shapes=[pltpu.VMEM((NSA_KV_HEADS, TQ, s), F32), pltpu.VMEM((TQ, s), F32)],
        compiler_params=_cparams("parallel", "arbitrary"),
        name="nsa_prompt",
    )(qn, proj, kcvc, kvb, kvb)


def _feature_major_pages(cache):
    return cache.transpose(0, 1, 3, 4, 5, 2).reshape(cache.shape[0], cache.shape[1], KV_ROW, PAGE_SIZE)


def _page_copies(cache_hbm, li, pt_ref, b, step, dst_of, sem, slot, pps):
    return [pltpu.make_async_copy(cache_hbm.at[li, pt_ref[b, step * pps + k]], dst_of(slot, k), sem.at[slot])
            for k in range(pps)]


def _pipelined_pages(cache_hbm, li, pt_ref, dst_of, sem, pps):
    b, s, ns = pl.program_id(0), pl.program_id(1), pl.num_programs(1)
    slot = s % 2
    copies = functools.partial(_page_copies, cache_hbm, li, pt_ref, b, dst_of=dst_of, sem=sem, pps=pps)

    @pl.when(s == 0)
    def _():
        for cp in copies(step=s, slot=slot):
            cp.start()

    @pl.when(s + 1 < ns)
    def _():
        for cp in copies(step=s + 1, slot=1 - slot):
            cp.start()

    for cp in copies(step=s, slot=slot):
        cp.wait()
    return slot


def _cmp_sample_weights(cmp_pe, cmp_w1, cmp_w2, gain4):
    wt = cmp_w1.transpose(0, 2, 1, 3).astype(BF16)
    z = jnp.zeros_like(wt)
    w1 = jnp.concatenate([jnp.concatenate([wt, z], axis=-1), jnp.concatenate([z, wt], axis=-1)], axis=2)
    w2 = cmp_w2.astype(BF16)
    z2 = jnp.zeros_like(w2)
    w2bd = jnp.concatenate([jnp.concatenate([w2, z2], axis=-1), jnp.concatenate([z2, w2], axis=-1)], axis=1)
    pe_t = jnp.tile(cmp_pe.transpose(0, 2, 1), (1, 1, 2)).reshape(2 * NSA_HEAD_DIM, 2 * NSA_BLOCK)
    return pe_t, w1, w2bd, jnp.tile(gain4, 2).reshape(1, -1)


def _cmp_sample_kernel(li, pps, pt_ref, cache_hbm, pe_ref, w1_ref, w2_ref, g_ref, o_ref, buf, sem):
    slot = _pipelined_pages(cache_hbm, li, pt_ref, lambda sl, k: buf.at[sl, k], sem, pps)
    pages = buf.at[slot]
    hd = NSA_HEAD_DIM
    for c in range(2):
        acc = jnp.zeros((NSA_KV_HEADS * pps, 2 * NSA_CMP_HIDDEN), F32)
        for d in range(hd):
            x = jnp.concatenate([pages[:, (2 * c + g) * hd + d, :] for g in range(NSA_KV_HEADS)], axis=0)
            x = x + pe_ref[c * hd + d:c * hd + d + 1, :]
            acc = acc + jnp.dot(x.astype(BF16), w1_ref[c, d], preferred_element_type=F32)
        out = jnp.dot(_silu(acc).astype(BF16), w2_ref[c], preferred_element_type=F32)
        if c == 0:
            out = out * lax.rsqrt(_seg_sum(out * out, hd) * (1.0 / hd) + EPS) * g_ref[...]
        for g in range(NSA_KV_HEADS):
            o_ref[0, :, (2 * c + g) * 2 * hd:(2 * c + g + 1) * 2 * hd] = out[g * pps:(g + 1) * pps]


def _cmp_sample_call(cache_t, li, page_table, cw):
    bd, n_pages = page_table.shape
    pps = math.gcd(n_pages, 64)
    pe_t, w1, w2bd, gain = cw
    hd = NSA_HEAD_DIM
    grid_spec = pltpu.PrefetchScalarGridSpec(
        num_scalar_prefetch=1,
        grid=(bd, n_pages // pps),
        in_specs=[pl.BlockSpec(memory_space=pl.ANY),
                  pl.BlockSpec(pe_t.shape, lambda b, s, pt: (0, 0)),
                  pl.BlockSpec(w1.shape, lambda b, s, pt: (0, 0, 0, 0)),
                  pl.BlockSpec(w2bd.shape, lambda b, s, pt: (0, 0, 0)),
                  pl.BlockSpec(gain.shape, lambda b, s, pt: (0, 0))],
        out_specs=pl.BlockSpec((1, pps, 2 * KV_ROW), lambda b, s, pt: (b, s, 0)),
        scratch_shapes=[pltpu.VMEM((2, pps, KV_ROW, PAGE_SIZE), F32), pltpu.SemaphoreType.DMA((2,))])
    packed = pl.pallas_call(
        functools.partial(_cmp_sample_kernel, li, pps),
        grid_spec=grid_spec,
        out_shape=jax.ShapeDtypeStruct((bd, n_pages, 2 * KV_ROW), F32),
        compiler_params=_cparams("arbitrary", "arbitrary"),
        name="nsa_cmp_sample",
    )(page_table, cache_t, pe_t, w1, w2bd, gain)
    return packed.reshape(bd, n_pages, 4, 2, hd).transpose(0, 1, 3, 2, 4).reshape(bd, 2 * n_pages, KV_ROW)


def _pick_group(x, rows_g0):
    w = x.shape[-1] // 2
    return jnp.where(rows_g0, x[:, :w], x[:, w:])


def _nsa_sample_kernel(li, pps, past, n_tok, pt_ref, cache_hbm, q_ref, gate_ref, cmp_ref, knew_ref, wold_ref, wnew_ref,
                       o_ref, buf, sem, qbd_ref, sel_ref, ocw_ref, m_ref, l_ref, acc_ref):
    s, ns = pl.program_id(1), pl.num_programs(1)
    hd, nh, rep = NSA_HEAD_DIM, NSA_HEADS, NSA_REP
    rows = nh * n_tok
    nbc = past // NSA_BLOCK
    tk = pps * PAGE_SIZE
    slot = _pipelined_pages(cache_hbm, li, pt_ref,
                            lambda sl, k: buf.at[sl, :, pl.ds(k * PAGE_SIZE, PAGE_SIZE)], sem, pps)
    ridx = _iota((rows, 1), 0)
    rows_g0 = ridx < rep * n_tok
    tok = ridx % n_tok
    t = past + tok
    slope = jnp.exp2(-(ridx // n_tok + 1).astype(F32))

    @pl.when(s == 0)
    def _():
        qtok = q_ref[0]
        qrows = jnp.concatenate([qtok[:, h * hd:(h + 1) * hd] for h in range(nh)], axis=0)
        zero = jnp.zeros_like(qrows)
        qbd = jnp.where(rows_g0, jnp.concatenate([qrows, zero], axis=1), jnp.concatenate([zero, qrows], axis=1))
        qbd_ref[...] = qbd
        qbb = qbd.astype(BF16)
        kcvc = cmp_ref[0]
        blk = _iota((rows, nbc), 1)
        dist_c = t - ((blk + 1) * NSA_BLOCK - 1)
        p_c = _masked_softmax(_nt(qbd, kcvc[:, :2 * hd], precision=HI) - slope * dist_c.astype(F32), dist_c >= 0)
        o_c = _pick_group(jnp.dot(p_c, kcvc[:, 2 * hd:], precision=HI, preferred_element_type=F32), rows_g0)
        blk_t = _iota((n_tok, nbc), 1)
        cur_t = (past + _iota((n_tok, 1), 0)) // NSA_BLOCK
        sels = []
        for g in range(NSA_KV_HEADS):
            imp = p_c[g * rep * n_tok:(g * rep + 1) * n_tok]
            for r in range(1, rep):
                imp = imp + p_c[(g * rep + r) * n_tok:(g * rep + r + 1) * n_tok]
            imp = jnp.where(blk_t < cur_t, imp, -1.0)
            sels += [_select_blocks(imp, blk_t, NSA_TOPN - 1)] * rep
        sel_ref[...] = jnp.concatenate(sels, axis=0).astype(BF16)
        knew = knew_ref[0]
        dist_n = tok - _iota((rows, n_tok), 1)
        mask_n = dist_n >= 0
        s_n = jnp.where(mask_n, _nt(qbb, knew[:, :2 * hd].astype(BF16)) - slope * dist_n.astype(F32), NEG_INF)
        m0 = jnp.max(s_n, axis=-1, keepdims=True)
        p_n = jnp.where(mask_n, jnp.exp(s_n - m0), 0.0)
        m_ref[...] = m0
        l_ref[...] = jnp.sum(p_n, axis=-1, keepdims=True)
        acc_ref[...] = jnp.dot(p_n.astype(BF16), knew[:, 2 * hd:].astype(BF16), preferred_element_type=F32)
        kw = jnp.concatenate([wold_ref[0], wnew_ref[0]], axis=0)
        dist_w = tok + NSA_WINDOW - _iota((rows, NSA_WINDOW + n_tok), 1)
        mask_w = (dist_w >= 0) & (dist_w <= NSA_WINDOW)
        p_w = _masked_softmax(_nt(qbb, kw[:, :2 * hd].astype(BF16)) - slope * dist_w.astype(F32), mask_w)
        o_w = _pick_group(jnp.dot(p_w.astype(BF16), kw[:, 2 * hd:].astype(BF16), preferred_element_type=F32), rows_g0)
        ocw_ref[0] = o_c
        ocw_ref[1] = o_w

    pages = buf.at[slot]
    k_t = pages[:2 * hd, :].astype(BF16)
    v_t = pages[2 * hd:, :].astype(BF16)
    k0 = s * tk
    expand = (_iota((nbc, tk), 0) == (k0 + _iota((nbc, tk), 1)) // NSA_BLOCK).astype(BF16)
    mask = jnp.dot(sel_ref[...], expand, preferred_element_type=F32) > 0.5
    dist = t - (k0 + _iota((rows, tk), 1))
    qk = jnp.dot(qbd_ref[...].astype(BF16), k_t, preferred_element_type=F32)
    sc = jnp.where(mask, qk - slope * dist.astype(F32), NEG_INF)
    m_old = m_ref[...]
    m_new = jnp.maximum(m_old, jnp.max(sc, axis=-1, keepdims=True))
    alpha = jnp.exp(m_old - m_new)
    p = jnp.where(mask, jnp.exp(sc - m_new), 0.0)
    m_ref[...] = m_new
    l_ref[...] = alpha * l_ref[...] + jnp.sum(p, axis=-1, keepdims=True)
    acc_ref[...] = alpha * acc_ref[...] + _nt(p.astype(BF16), v_t)

    @pl.when(s == ns - 1)
    def _():
        o_s = _pick_group(acc_ref[...] / l_ref[...], rows_g0)
        gates = jax.nn.sigmoid(gate_ref[0])

        def gate_rows(br):
            return jnp.concatenate([gates[:, br * nh + h:br * nh + h + 1] for h in range(nh)], axis=0)

        o_ref[0] = gate_rows(0) * ocw_ref[0] + gate_rows(1) * o_s + gate_rows(2) * ocw_ref[1]


def _nsa_sample_call(cache_t, li, page_table, proj, qn, kvn, kcvc, win_old, past):
    bd, n_pages = page_table.shape
    n_tok = qn.shape[1]
    pps = math.gcd(n_pages, 32)
    rows = NSA_HEADS * n_tok
    nbc = kcvc.shape[1]
    grid_spec = pltpu.PrefetchScalarGridSpec(
        num_scalar_prefetch=1,
        grid=(bd, n_pages // pps),
        in_specs=[pl.BlockSpec(memory_space=pl.ANY),
                  pl.BlockSpec((1, n_tok, NSA_WIDTH), lambda b, s, pt: (b, 0, 0)),
                  pl.BlockSpec((1, n_tok, LANE), lambda b, s, pt: (b, 0, C_NG // LANE)),
                  pl.BlockSpec((1, nbc, KV_ROW), lambda b, s, pt: (b, 0, 0)),
                  pl.BlockSpec((1, n_tok, KV_ROW), lambda b, s, pt: (b, 0, 1)),
                  pl.BlockSpec((1, NSA_WINDOW, KV_ROW), lambda b, s, pt: (b, 0, 0)),
                  pl.BlockSpec((1, n_tok, KV_ROW), lambda b, s, pt: (b, 0, 2))],
        out_specs=pl.BlockSpec((1, rows, NSA_HEAD_DIM), lambda b, s, pt: (b, 0, 0)),
        scratch_shapes=[pltpu.VMEM((2, KV_ROW, pps * PAGE_SIZE), F32),
                        pltpu.SemaphoreType.DMA((2,)),
                        pltpu.VMEM((rows, 2 * NSA_HEAD_DIM), F32),
                        pltpu.VMEM((rows, nbc), BF16),
                        pltpu.VMEM((2, rows, NSA_HEAD_DIM), F32),
                        pltpu.VMEM((rows, 1), F32),
                        pltpu.VMEM((rows, 1), F32),
                        pltpu.VMEM((rows, 2 * NSA_HEAD_DIM), F32)])
    return pl.pallas_call(
        functools.partial(_nsa_sample_kernel, li, pps, past, n_tok),
        grid_spec=grid_spec,
        out_shape=jax.ShapeDtypeStruct((bd, rows, NSA_HEAD_DIM), F32),
        compiler_params=_cparams("arbitrary", "arbitrary"),
        name="nsa_sample",
    )(page_table, cache_t, qn, proj, kcvc, kvn, win_old, kvn)


def _outproj_kernel(y_ref, n_ref, r_ref, w_ref, x_ref, g_ref, o_ref, mix_ref):
    @pl.when(pl.program_id(2) == 0)
    def _():
        mix_ref[:, :SSD_INNER] = y_ref[0].astype(BF16)
        mix_ref[:, SSD_INNER:SSD_INNER + NSA_WIDTH] = n_ref[0].astype(BF16)
        mix_ref[:, SSD_INNER + NSA_WIDTH:] = r_ref[0].astype(BF16)

    o_ref[0] = x_ref[0] + g_ref[0] * jnp.dot(mix_ref[...], w_ref[...], preferred_element_type=F32)


def _outproj_call(y, o_nsa, o_ret, w_bf, x, gate):
    b, s, d = x.shape
    ts = min(s, 1024)
    tn = 512
    sm = gate.shape[1]
    tsm = 1 if sm == 1 else ts
    mod_map = (lambda bi, i, j: (bi, 0, j)) if sm == 1 else (lambda bi, i, j: (bi, i, j))
    row_map = lambda bi, i, j: (bi, i, 0)
    return pl.pallas_call(
        _outproj_kernel,
        grid=(b, s // ts, d // tn),
        in_specs=[pl.BlockSpec((1, ts, SSD_INNER), row_map),
                  pl.BlockSpec((1, ts, NSA_WIDTH), row_map),
                  pl.BlockSpec((1, ts, RET_WIDTH), row_map),
                  pl.BlockSpec((D_MIX, tn), lambda bi, i, j: (0, j)),
                  pl.BlockSpec((1, ts, tn), lambda bi, i, j: (bi, i, j)),
                  pl.BlockSpec((1, tsm, tn), mod_map)],
        out_specs=pl.BlockSpec((1, ts, tn), lambda bi, i, j: (bi, i, j)),
        out_shape=jax.ShapeDtypeStruct((b, s, d), F32),
        scratch_shapes=[pltpu.VMEM((ts, D_MIX), BF16)],
        compiler_params=_cparams("parallel", "parallel", "arbitrary"),
        name="out_proj",
    )(y, o_nsa, o_ret, w_bf, x, gate)


def _route_kernel(x_ref, sc_ref, sh_ref, g_ref, wr_ref, br_ref, h_ref, meta_ref, cnt_ref):
    h = _mod_norm(x_ref[0], g_ref[...], sc_ref[0], sh_ref[0])
    n_rows = h.shape[0]
    for c in range(ROW_CHUNKS):
        h_ref[0, pl.ds(c, n_rows, stride=ROW_CHUNKS), :] = h[:, c * LANE:(c + 1) * LANE]
    logit = jnp.dot(h, wr_ref[...], precision=HI, preferred_element_type=F32) + br_ref[...]
    lane = _iota(logit.shape, 1)
    first = lambda hit: jnp.min(jnp.where(hit, lane, LANE), axis=-1, keepdims=True)
    is_g = lane < MOE_GROUPS
    gl = jnp.where(is_g, logit, -jnp.inf)
    gmax = jnp.max(gl, axis=-1, keepdims=True)
    g_idx = first(gl == gmax)
    g_p = 1.0 / jnp.sum(jnp.where(is_g, jnp.exp(logit - gmax), 0.0), axis=-1, keepdims=True)
    e_lane = lane - MOE_GROUPS
    in_grp = (e_lane >= 0) & (e_lane < MOE_EXPERTS) & (e_lane // MOE_EXPERTS_PER_GROUP == g_idx)
    el = jnp.where(in_grp, logit, -jnp.inf)
    v1 = jnp.max(el, axis=-1, keepdims=True)
    i1 = first(el == v1)
    el = jnp.where(lane == i1, -jnp.inf, el)
    v2 = jnp.max(el, axis=-1, keepdims=True)
    i2 = first(el == v2)
    e = jnp.exp(v2 - v1)
    w1 = g_p / (1.0 + e)
    w2 = g_p * e / (1.0 + e)
    meta = jnp.where(lane == 0, (i1 - MOE_GROUPS).astype(F32), 0.0)
    meta = jnp.where(lane == 1, (i2 - MOE_GROUPS).astype(F32), meta)
    meta = jnp.where(lane == 2, w1, meta)
    meta_ref[0] = jnp.where(lane == 3, w2, meta)
    cnt_ref[0, 0] = jnp.sum(((lane == i1) | (lane == i2)).astype(F32), axis=0, keepdims=True)


def _route_call(x1, scale, shift, gain, wr, br):
    b, s, d = x1.shape
    ts = min(s, 512)
    sm = scale.shape[1]
    tsm = 1 if sm == 1 else ts
    mod_map = (lambda bi, i: (bi, 0, 0)) if sm == 1 else (lambda bi, i: (bi, i, 0))
    return pl.pallas_call(
        _route_kernel,
        grid=(b, s // ts),
        in_specs=[pl.BlockSpec((1, ts, d), lambda bi, i: (bi, i, 0)),
                  pl.BlockSpec((1, tsm, d), mod_map),
                  pl.BlockSpec((1, tsm, d), mod_map),
                  pl.BlockSpec((1, d), lambda bi, i: (0, 0)),
                  pl.BlockSpec((d, LANE), lambda bi, i: (0, 0)),
                  pl.BlockSpec((1, LANE), lambda bi, i: (0, 0))],
        out_specs=[pl.BlockSpec((1, ts * ROW_CHUNKS, LANE), lambda bi, i: (bi, i, 0)),
                   pl.BlockSpec((1, ts, LANE), lambda bi, i: (bi, i, 0)),
                   pl.BlockSpec((1, 1, 1, LANE), lambda bi, i: (bi, i, 0, 0))],
        out_shape=[jax.ShapeDtypeStruct((b, s * ROW_CHUNKS, LANE), F32), jax.ShapeDtypeStruct((b, s, LANE), F32),
                   jax.ShapeDtypeStruct((b, s // ts, 1, LANE), F32)],
        compiler_params=_cparams("parallel", "parallel"),
        name="moe_route",
    )(x1, scale, shift, gain.reshape(1, d), wr, br)


def _moe_plan(e_ids, w, counts):
    n_pairs = e_ids.size
    tm = TM_MOE
    n_tiles = -(-n_pairs // tm) + MOE_EXPERTS
    n_slots = n_tiles * tm
    n_tok = n_pairs // MOE_TOPK
    pair_ids = jnp.arange(n_pairs, dtype=jnp.int32)
    _, order, w_sorted = lax.sort((e_ids.reshape(-1), pair_ids, w.reshape(-1)), num_keys=1, is_stable=True)
    tiles_per = (counts + tm - 1) // tm
    tile_end = jnp.cumsum(tiles_per)
    tile_start = tile_end - tiles_per
    grp_start = jnp.cumsum(counts) - counts
    n_used = tile_end[-1]
    tiles = jnp.arange(n_tiles, dtype=jnp.int32)
    tile_ids = jnp.minimum(tiles, n_used - 1)
    tile_e = jnp.sum((tile_ids[:, None] >= tile_end[None, :]).astype(jnp.int32), axis=1)
    active = tiles < n_used
    rank0 = (tile_ids - tile_start[tile_e]) * tm
    flag = active.astype(jnp.int32) + (active & (rank0 == 0)).astype(jnp.int32)
    r = jnp.arange(tm, dtype=jnp.int32)[None, :]
    rank = rank0[:, None] + r
    valid = (active[:, None] & (rank < counts[tile_e][:, None])).reshape(-1)
    idx = jnp.clip(grp_start[tile_e][:, None] + rank, 0, n_pairs - 1).reshape(-1)
    packed = jnp.stack([order, lax.bitcast_convert_type(w_sorted, jnp.int32)], axis=1)[idx]
    pair = packed[:, 0]
    tok = pair // MOE_TOPK
    src = jnp.where(valid, tok, 0) * ROW_CHUNKS
    pad_row = jnp.broadcast_to(n_tok + r, (n_tiles, tm)).reshape(-1)
    dst_row = jnp.where(valid, tok, pad_row) * ROW_CHUNKS
    dst_plane = jnp.where(valid, pair % MOE_TOPK, 0)
    w_slot = jnp.where(valid, lax.bitcast_convert_type(packed[:, 1], F32), 0.0)
    return tile_e, flag, src, dst_row, dst_plane, w_slot.reshape(n_slots, 1)


def _moe_kernel(te_ref, flag_ref, src_ref, row_ref, plane_ref, h_hbm, w_ref, wg_ref, wu_ref, wd_ref, y_hbm, xbuf,
                obuf, xs, wgb, wub, wdb, sem_in, sem_out):
    i = pl.program_id(0)
    n_tiles = pl.num_programs(0)
    flag = flag_ref[i]
    nxt = jnp.minimum(i + 1, n_tiles - 1)
    next_active = (i + 1 < n_tiles) & (flag_ref[nxt] > 0)
    slot = i % 2

    rc = ROW_CHUNKS

    def slab(ref_2d, first_row):
        return ref_2d.at[pl.ds(pl.multiple_of(first_row, rc), rc)]

    def start_gather(tile, sl):
        def body(r, carry):
            pltpu.make_async_copy(slab(h_hbm, src_ref[tile * TM_MOE + r]), slab(xbuf.at[sl], r * rc),
                                  sem_in.at[sl]).start()
            return carry

        lax.fori_loop(0, TM_MOE, body, 0, unroll=8)

    def wait_gather(sl):
        pltpu.make_async_copy(h_hbm.at[pl.ds(0, TM_MOE * rc)], xbuf.at[sl], sem_in.at[sl]).wait()

    def wait_scatter(sl):
        pltpu.make_async_copy(obuf.at[sl], y_hbm.at[0, pl.ds(0, TM_MOE * rc)], sem_out.at[sl]).wait()

    @pl.when(i == 0)
    def _():
        obuf[0] = jnp.zeros(obuf.shape[1:], F32)
        pad_rows = pl.ds(y_hbm.shape[1] - TM_MOE * rc, TM_MOE * rc)
        for plane in range(MOE_TOPK):
            pltpu.make_async_copy(obuf.at[0], y_hbm.at[plane, pad_rows], sem_out.at[0]).start()
        for plane in range(MOE_TOPK):
            pltpu.make_async_copy(obuf.at[0], y_hbm.at[plane, pad_rows], sem_out.at[0]).wait()

    @pl.when(flag > 0)
    def _():
        @pl.when(i == 0)
        def _():
            start_gather(i, slot)

        @pl.when(next_active)
        def _():
            start_gather(i + 1, 1 - slot)

        @pl.when(flag > 1)
        def _():
            wgb[...] = wg_ref[0, 0].astype(BF16)
            wub[...] = wu_ref[0, 0].astype(BF16)
            wdb[...] = wd_ref[0, 0].astype(BF16)

        wait_gather(slot)
        rows = xbuf.at[slot]
        for c in range(rc):
            xs[:, c * LANE:(c + 1) * LANE] = rows[pl.ds(c, TM_MOE, stride=rc), :].astype(BF16)
        x = xs[...]
        a = jnp.dot(x, wgb[...], preferred_element_type=F32)
        u = jnp.dot(x, wub[...], preferred_element_type=F32)
        hid = (_silu(a) * u * w_ref[...]).astype(BF16)
        out = jnp.dot(hid, wdb[...], preferred_element_type=F32)

        @pl.when(i >= 2)
        def _():
            wait_scatter(slot)

        out_rows = obuf.at[slot]
        for c in range(rc):
            out_rows[pl.ds(c, TM_MOE, stride=rc), :] = out[:, c * LANE:(c + 1) * LANE]

        def start_out(r, carry):
            k = i * TM_MOE + r
            pltpu.make_async_copy(slab(out_rows, r * rc), slab(y_hbm.at[plane_ref[k]], row_ref[k]),
                                  sem_out.at[slot]).start()
            return carry

        lax.fori_loop(0, TM_MOE, start_out, 0, unroll=8)

        @pl.when(jnp.logical_not(next_active))
        def _():
            wait_scatter(slot)

            @pl.when(i >= 1)
            def _():
                wait_scatter(1 - slot)


def _moe_call(h_all, plan, li, w_gate, w_up, w_down):
    t = h_all.shape[0] // ROW_CHUNKS
    d = D_MODEL
    tile_e, flag, src, dst_row, dst_plane, w_slot = plan
    n_tiles = tile_e.shape[0]
    tm = TM_MOE
    w_map = lambda i, te, *_: (li, te[i], 0, 0)
    grid_spec = pltpu.PrefetchScalarGridSpec(
        num_scalar_prefetch=5,
        grid=(n_tiles,),
        in_specs=[pl.BlockSpec(memory_space=pl.ANY),
                  pl.BlockSpec((tm, 1), lambda i, *_: (i, 0)),
                  pl.BlockSpec((1, 1, d, MOE_HIDDEN), w_map),
                  pl.BlockSpec((1, 1, d, MOE_HIDDEN), w_map),
                  pl.BlockSpec((1, 1, MOE_HIDDEN, d), w_map)],
        out_specs=pl.BlockSpec(memory_space=pl.ANY),
        scratch_shapes=[pltpu.VMEM((2, tm * ROW_CHUNKS, LANE), F32), pltpu.VMEM((2, tm * ROW_CHUNKS, LANE), F32),
                        pltpu.VMEM((tm, d), BF16),
                        pltpu.VMEM((d, MOE_HIDDEN), BF16), pltpu.VMEM((d, MOE_HIDDEN), BF16),
                        pltpu.VMEM((MOE_HIDDEN, d), BF16),
                        pltpu.SemaphoreType.DMA((2,)), pltpu.SemaphoreType.DMA((2,))])
    return pl.pallas_call(
        _moe_kernel,
        grid_spec=grid_spec,
        out_shape=jax.ShapeDtypeStruct((MOE_TOPK, (t + tm) * ROW_CHUNKS, LANE), F32),
        compiler_params=_cparams("arbitrary"),
        name="moe_experts",
    )(tile_e, flag, src, dst_row, dst_plane, h_all, w_slot, w_gate, w_up, w_down)


def _combine_kernel(x_ref, g_ref, y_ref, o_ref):
    n_rows = x_ref.shape[1]
    for c in range(ROW_CHUNKS):
        cols = slice(c * LANE, (c + 1) * LANE)
        chunk = pl.ds(c, n_rows, stride=ROW_CHUNKS)
        o_ref[0, :, cols] = x_ref[0, :, cols] + g_ref[0, :, cols] * (y_ref[0, chunk, :] + y_ref[1, chunk, :])


def _combine_call(x1, gate, y_all, row0):
    b, s, d = x1.shape
    ts = min(s, 512)
    sm = gate.shape[1]
    tsm = 1 if sm == 1 else ts
    mod_map = (lambda bi, i: (bi, 0, 0)) if sm == 1 else (lambda bi, i: (bi, i, 0))
    blk0 = row0 // ts
    per_b = s // ts
    return pl.pallas_call(
        _combine_kernel,
        grid=(b, per_b),
        in_specs=[pl.BlockSpec((1, ts, d), lambda bi, i: (bi, i, 0)),
                  pl.BlockSpec((1, tsm, d), mod_map),
                  pl.BlockSpec((MOE_TOPK, ts * ROW_CHUNKS, LANE), lambda bi, i: (0, blk0 + bi * per_b + i, 0))],
        out_specs=pl.BlockSpec((1, ts, d), lambda bi, i: (bi, i, 0)),
        out_shape=jax.ShapeDtypeStruct((b, s, d), F32),
        compiler_params=_cparams("parallel", "parallel"),
        name="moe_combine",
    )(x1, gate, y_all)


def _reorder_w_in(w):
    sizes = (SSD_INNER, SSD_CONV_DIM, SSD_HEADS, NSA_WIDTH, 3 * KV_ROW, 3 * NSA_HEADS, RET_HEADS * RET_KEY_DIM,
             RET_HEADS * RET_KEY_DIM, RET_WIDTH, RET_WIDTH)
    z, xbc, dt, nq, nkv, ngate, rq, rk, rv, rg = jnp.split(w, [int(v) for v in np.cumsum(sizes)[:-1]], axis=1)
    padc = lambda a: jnp.pad(a, ((0, 0), (0, LANE - a.shape[1])))
    return jnp.concatenate([z, nq, rv, rg, rq, rk, xbc, nkv, padc(dt), padc(ngate)], axis=1).astype(BF16)


def _kv_rows(a):
    return a.reshape(a.shape[:2] + (2, NSA_KV_HEADS, NSA_HEAD_DIM))


def kernel(x_prompt, x_sample, cache_cmp_kv, cache_sel_kv, cache_win_kv, state_ssm, state_conv, state_ret, page_table, c_prompt, c_sample, w_ada, b_ada, norm_mix, norm_ffn, w_in, w_out, conv_w, conv_b, dt_bias, a_log, ssd_d, ssd_norm, qk_gain, cmp_pe, cmp_w1, cmp_w2, router_group_w, router_group_b, router_expert_w, router_expert_b, expert_w_gate, expert_w_up, expert_w_down):
    bp, sp, d = x_prompt.shape
    bd, sd, _ = x_sample.shape
    n_pages = page_table.shape[1]
    past = n_pages * PAGE_SIZE
    assert d == D_MODEL and sp % LC == 0 and sp % NSA_BLOCK == 0 and sp >= NSA_WINDOW
    assert sd % SUBLANE == 0 and sd <= NSA_BLOCK and sd <= LC and cache_win_kv.shape[2] == NSA_WINDOW
    n_seq = bp + bd
    c_all = jnp.pad(jnp.concatenate([c_prompt, c_sample]), ((0, -n_seq % SUBLANE), (0, 0)))
    mod = _ada_call(c_all, w_ada, b_ada)
    cos_p, sin_p = _rotary_tables(0, sp, sp)
    cos_s, sin_s = _rotary_tables(past, sd, LC)
    cmp_pages = _feature_major_pages(cache_cmp_kv)
    sel_pages = _feature_major_pages(cache_sel_kv)
    xp = x_prompt
    xs = x_sample.reshape(1, bd * sd, d)
    outs_p, outs_s = [], []
    for li in range(DEPTH):
        mods_p = [mod[li, :bp, k * d:(k + 1) * d].reshape(bp, 1, d) for k in range(6)]
        mods_s = [jnp.repeat(mod[li, bp:n_seq, k * d:(k + 1) * d], sd, axis=0).reshape(1, bd * sd, d) for k in range(6)]
        w_in_b = _reorder_w_in(w_in[li])
        w_out_b = w_out[li].astype(BF16)
        cw = _compress_weights(cmp_pe[li], cmp_w1[li], cmp_w2[li], qk_gain[li, 4])
        cw_pages = _cmp_sample_weights(cmp_pe[li], cmp_w1[li], cmp_w2[li], qk_gain[li, 4])
        ssd_w = (conv_w[li], conv_b[li], dt_bias[li], a_log[li], ssd_d[li], ssd_norm[li])
        wr = jnp.pad(jnp.concatenate([router_group_w[li], router_expert_w[li]], axis=1),
                     ((0, 0), (0, LANE - MOE_GROUPS - MOE_EXPERTS)))
        br = jnp.pad(jnp.concatenate([router_group_b[li], router_expert_b[li]]),
                     (0, LANE - MOE_GROUPS - MOE_EXPERTS)).reshape(1, LANE)

        sh_a, sc_a, g_a, sh_f, sc_f, g_f = mods_p
        proj = _inproj_call(xp, sc_a, sh_a, norm_mix[li], w_in_b)
        y_ssd, ssm_p, conv_p = _ssd_call(proj, LC, jnp.zeros((bp, SSD_CONV - 1, SSD_CONV_DIM), F32),
                                         jnp.zeros((bp, SSD_HEADS, SSD_HEAD_DIM, SSD_STATE), F32), *ssd_w)
        o_ret, ret_p = _ret_call(proj, LC, jnp.zeros((bp, RET_HEADS, RET_KEY_DIM, RET_VAL_DIM), F32), cos_p, sin_p)
        qn, kvn_p, kvb = _prep_call(proj, sp, qk_gain[li])
        kcvc = _cmp_prompt_call(kvn_p, cw)
        o_nsa = _nsa_prompt_call(proj, qn, kvb, kcvc)
        x1_p = _outproj_call(y_ssd, o_nsa, o_ret, w_out_b, xp, g_a)
        h2_p, meta_p, cnt_p = _route_call(x1_p, sc_f, sh_f, norm_ffn[li], wr, br)
        gf_p = g_f

        sh_a, sc_a, g_a, sh_f, sc_f, g_f = mods_s
        proj = _inproj_call(xs, sc_a, sh_a, norm_mix[li], w_in_b).reshape(bd, sd, N_PROJ)
        proj = jnp.pad(proj, ((0, 0), (0, LC - sd), (0, 0)))
        y_ssd, ssm_s, conv_s = _ssd_call(proj, sd, state_conv[li], state_ssm[li], *ssd_w)
        o_ret, ret_s = _ret_call(proj, sd, state_ret[li], cos_s, sin_s)
        qn, kvn_s, _ = _prep_call(proj, sd, qk_gain[li])
        kcvc = _cmp_sample_call(cmp_pages, li, page_table, cw_pages)
        win_old = cache_win_kv[li].reshape(bd, NSA_WINDOW, KV_ROW)
        o_nsa = _nsa_sample_call(sel_pages, li, page_table, proj, qn, kvn_s, kcvc, win_old, past)
        o_nsa = o_nsa.reshape(bd, NSA_HEADS, sd, NSA_HEAD_DIM).transpose(0, 2, 1, 3).reshape(1, bd * sd, NSA_WIDTH)
        x1_s = _outproj_call(y_ssd[:, :sd].reshape(1, bd * sd, SSD_INNER), o_nsa,
                             o_ret[:, :sd].reshape(1, bd * sd, RET_WIDTH), w_out_b, xs, g_a)
        h2_s, meta_s, cnt_s = _route_call(x1_s, sc_f, sh_f, norm_ffn[li], wr, br)

        h_all = jnp.concatenate([h2_p.reshape(bp * sp * ROW_CHUNKS, LANE), h2_s.reshape(bd * sd * ROW_CHUNKS, LANE)])
        meta = jnp.concatenate([meta_p.reshape(bp * sp, LANE), meta_s.reshape(bd * sd, LANE)])
        counts = (cnt_p.sum(axis=(0, 1, 2)) + cnt_s.sum(axis=(0, 1, 2)))[MOE_GROUPS:MOE_GROUPS + MOE_EXPERTS]
        plan = _moe_plan(meta[:, :MOE_TOPK].astype(jnp.int32), meta[:, MOE_TOPK:2 * MOE_TOPK],
                         counts.astype(jnp.int32))
        y_all = _moe_call(h_all, plan, li, expert_w_gate, expert_w_up, expert_w_down)
        xp = _combine_call(x1_p, gf_p, y_all, 0)
        xs = _combine_call(x1_s, g_f, y_all, bp * sp)

        win_s = jnp.concatenate([win_old[:, sd:], kvn_s[:, :, 2 * KV_ROW:]], axis=1)
        outs_p.append((_kv_rows(kvn_p[:, :, :KV_ROW]), _kv_rows(kvn_p[:, :, KV_ROW:2 * KV_ROW]),
                       _kv_rows(kvn_p[:, sp - NSA_WINDOW:, 2 * KV_ROW:]), ssm_p, conv_p, ret_p))
        outs_s.append((_kv_rows(kvn_s[:, :, :KV_ROW]), _kv_rows(kvn_s[:, :, KV_ROW:2 * KV_ROW]), _kv_rows(win_s),
                       ssm_s, conv_s, ret_s))
    res = [xp, xs.reshape(bd, sd, d)]
    for k in range(6):
        res.append(jnp.stack([o[k] for o in outs_p]))
        res.append(jnp.stack([o[k] for o in outs_s]))
    return tuple(res)
```

```python
import functools
import math

import numpy as np
import jax
import jax.numpy as jnp
from jax import lax
from jax.experimental import pallas as pl
from jax.experimental.pallas import tpu as pltpu

F32 = jnp.float32
BF16 = jnp.bfloat16
HI = lax.Precision.HIGHEST
EPS = 1e-6
NEG_INF = -1e30

D_MODEL = 2048
DEPTH = 2
PAGE_SIZE = 128
SSD_INNER = D_MODEL // 2
SSD_HEAD_DIM = 64
SSD_HEADS = SSD_INNER // SSD_HEAD_DIM
SSD_GROUPS = 2
SSD_STATE = 128
SSD_CONV = 4
SSD_CONV_DIM = SSD_INNER + 2 * SSD_GROUPS * SSD_STATE
NSA_WIDTH = D_MODEL // 4
NSA_HEAD_DIM = 64
NSA_HEADS = NSA_WIDTH // NSA_HEAD_DIM
NSA_KV_HEADS = 2
NSA_REP = NSA_HEADS // NSA_KV_HEADS
NSA_BLOCK = 64
NSA_TOPN = 16
NSA_WINDOW = 512
NSA_CMP_HIDDEN = 128
RET_WIDTH = D_MODEL // 4
RET_HEADS = 4
RET_VAL_DIM = RET_WIDTH // RET_HEADS
RET_KEY_DIM = RET_VAL_DIM // 2
D_MIX = SSD_INNER + NSA_WIDTH + RET_WIDTH
MOE_GROUPS = 4
MOE_EXPERTS_PER_GROUP = 8
MOE_EXPERTS = MOE_GROUPS * MOE_EXPERTS_PER_GROUP
MOE_TOPK = 2
MOE_HIDDEN = D_MODEL // 4
KV_ROW = 2 * NSA_KV_HEADS * NSA_HEAD_DIM

LANE = 128
SUBLANE = 8
C_Z, C_NQ, C_RV, C_RG, C_RQ, C_RK, C_XBC, C_NKV, C_DT, C_NG = 0, 1024, 1536, 2048, 2560, 2816, 3072, 4608, 5376, 5504
N_PROJ = 5632
LC = 128
TQ = 256
TM_MOE = 256
ROW_CHUNKS = D_MODEL // LANE
VMEM_LIMIT = 56 * 2 ** 20


def _cparams(*sem):
    return pltpu.CompilerParams(dimension_semantics=sem, vmem_limit_bytes=VMEM_LIMIT)


def _silu(x):
    return x * jax.nn.sigmoid(x)


def _nt(a, b, precision=None):
    return lax.dot_general(a, b, (((1,), (1,)), ((), ())), precision=precision, preferred_element_type=F32)


def _tn(a, b, precision=None):
    return lax.dot_general(a, b, (((0,), (0,)), ((), ())), precision=precision, preferred_element_type=F32)


def _iota(shape, dim):
    return lax.broadcasted_iota(jnp.int32, shape, dim)


def _seg_sum(x2, seg):
    n = x2.shape[-1]
    bd = (_iota((n, n), 0) // seg == _iota((n, n), 1) // seg).astype(BF16)
    hi = x2.astype(BF16)
    lo = (x2 - hi.astype(F32)).astype(BF16)
    return jnp.dot(hi, bd, preferred_element_type=F32) + jnp.dot(lo, bd, preferred_element_type=F32)


def _ada_kernel(c_ref, w_ref, b_ref, o_ref):
    o_ref[0] = jnp.dot(_silu(c_ref[...]), w_ref[0], precision=HI, preferred_element_type=F32) + b_ref[0]


def _ada_call(c_all, w_ada, b_ada):
    rows = c_all.shape[0]
    tn = 1024
    return pl.pallas_call(
        _ada_kernel,
        grid=(DEPTH, 6 * D_MODEL // tn),
        in_specs=[pl.BlockSpec((rows, D_MODEL), lambda l, j: (0, 0)),
                  pl.BlockSpec((1, D_MODEL, tn), lambda l, j: (l, 0, j)),
                  pl.BlockSpec((1, 1, tn), lambda l, j: (l, 0, j))],
        out_specs=pl.BlockSpec((1, rows, tn), lambda l, j: (l, 0, j)),
        out_shape=jax.ShapeDtypeStruct((DEPTH, rows, 6 * D_MODEL), F32),
        compiler_params=_cparams("parallel", "parallel"),
        name="ada",
    )(c_all, w_ada, b_ada.reshape(DEPTH, 1, 6 * D_MODEL))


def _mod_norm(x, gain, scale, shift):
    ms = jnp.mean(x * x, axis=-1, keepdims=True)
    return x * lax.rsqrt(ms + EPS) * gain * (1.0 + scale) + shift


def _inproj_kernel(x_ref, sc_ref, sh_ref, g_ref, w_ref, o_ref, h_ref):
    @pl.when(pl.program_id(2) == 0)
    def _():
        h_ref[...] = _mod_norm(x_ref[0], g_ref[...], sc_ref[0], sh_ref[0]).astype(BF16)

    o_ref[0] = jnp.dot(h_ref[...], w_ref[...], preferred_element_type=F32)


def _inproj_call(x, scale, shift, gain, w_bf):
    b, s, d = x.shape
    ts = min(s, 1024)
    tn = 512
    sm = scale.shape[1]
    tsm = 1 if sm == 1 else ts
    mod_map = (lambda bi, i, j: (bi, 0, 0)) if sm == 1 else (lambda bi, i, j: (bi, i, 0))
    return pl.pallas_call(
        _inproj_kernel,
        grid=(b, s // ts, N_PROJ // tn),
        in_specs=[pl.BlockSpec((1, ts, d), lambda bi, i, j: (bi, i, 0)),
                  pl.BlockSpec((1, tsm, d), mod_map),
                  pl.BlockSpec((1, tsm, d), mod_map),
                  pl.BlockSpec((1, d), lambda bi, i, j: (0, 0)),
                  pl.BlockSpec((d, tn), lambda bi, i, j: (0, j))],
        out_specs=pl.BlockSpec((1, ts, tn), lambda bi, i, j: (bi, i, j)),
        out_shape=jax.ShapeDtypeStruct((b, s, N_PROJ), F32),
        scratch_shapes=[pltpu.VMEM((ts, d), BF16)],
        compiler_params=_cparams("parallel", "parallel", "arbitrary"),
        name="in_proj",
    )(x, scale, shift, gain.reshape(1, d), w_bf)


def _ssd_kernel(n_valid, xbc_ref, z_ref, dt_ref, cprev_ref, sprev_ref, cw_ref, cb_ref, dtb_ref, alog_ref, dsk_ref,
                nrm_ref, y_ref, snew_ref, cnew_ref, cbuf, state, ybuf):
    c = pl.program_id(1)
    nc = pl.num_programs(1)
    hd, nst = SSD_HEAD_DIM, SSD_STATE
    hpg = SSD_HEADS // SSD_GROUPS

    @pl.when(c == 0)
    def _():
        cbuf[0:SUBLANE, :] = jnp.zeros((SUBLANE, SSD_CONV_DIM), F32)
        cbuf[5:8, :] = cprev_ref[0]
        state[...] = sprev_ref[0]

    cbuf[8:8 + LC, :] = xbc_ref[0]
    conv = cb_ref[...] + cw_ref[0:1, :] * cbuf[5:5 + LC, :]
    for k in range(1, SSD_CONV):
        conv = conv + cw_ref[k:k + 1, :] * cbuf[5 + k:5 + k + LC, :]
    tail = cbuf[5 + n_valid:8 + n_valid, :]
    cbuf[5:8, :] = tail
    xbc = _silu(conv)
    xs = xbc[:, :SSD_INNER]
    bm = xbc[:, SSD_INNER:SSD_INNER + SSD_GROUPS * nst].astype(BF16)
    cm = xbc[:, SSD_INNER + SSD_GROUPS * nst:].astype(BF16)

    dtr = dt_ref[0] + dtb_ref[...]
    dt = jnp.maximum(dtr, 0.0) + jnp.log1p(jnp.exp(-jnp.abs(dtr)))
    if n_valid < LC:
        dt = jnp.where(_iota((LC, LANE), 0) < n_valid, dt, 0.0)
    a = -jnp.exp(alog_ref[...])
    da = dt * a
    row = _iota((LC, LC), 0)
    col = _iota((LC, LC), 1)
    tril = row >= col
    acs = jnp.dot(tril.astype(F32), da, precision=HI, preferred_element_type=F32)
    eye = (row == col).astype(F32)
    acs_t = _nt(eye, acs, precision=HI)
    expand = (_iota((LANE, SSD_INNER), 0) == _iota((LANE, SSD_INNER), 1) // hd).astype(BF16)

    def per_head_to_lanes(v):
        hi = v.astype(BF16)
        lo = (v - hi.astype(F32)).astype(BF16)
        return (jnp.dot(hi, expand, preferred_element_type=F32) + jnp.dot(lo, expand, preferred_element_type=F32))

    e_acs = jnp.exp(acs)
    decay = jnp.exp(acs[LC - 1:LC, :] - acs)
    dt_x = per_head_to_lanes(dt)
    e_acs_x = per_head_to_lanes(e_acs)
    decay_x = per_head_to_lanes(decay)
    xdt = xs * dt_x
    xdec = (xdt * decay_x).astype(BF16)
    xdt_b = xdt.astype(BF16)
    cbs = [_nt(cm[:, g * nst:(g + 1) * nst], bm[:, g * nst:(g + 1) * nst]) for g in range(SSD_GROUPS)]
    for h in range(SSD_HEADS):
        g = h // hpg
        lmat = jnp.exp(jnp.where(tril, acs[:, h:h + 1] - acs_t[h:h + 1, :], -jnp.inf))
        m = (cbs[g] * lmat).astype(BF16)
        y_diag = jnp.dot(m, xdt_b[:, h * hd:(h + 1) * hd], preferred_element_type=F32)
        st = state[h]
        y_off = _nt(cm[:, g * nst:(g + 1) * nst], st.astype(BF16)) * e_acs_x[:, h * hd:(h + 1) * hd]
        state[h] = st * e_acs[LC - 1:LC, h:h + 1] + _tn(xdec[:, h * hd:(h + 1) * hd], bm[:, g * nst:(g + 1) * nst])
        ybuf[:, h * hd:(h + 1) * hd] = y_diag + y_off
    y = (ybuf[...] + dsk_ref[...] * xs) * _silu(z_ref[0])
    gw = SSD_INNER // SSD_GROUPS
    for g in range(SSD_GROUPS):
        seg = y[:, g * gw:(g + 1) * gw]
        ms = jnp.mean(seg * seg, axis=-1, keepdims=True)
        y_ref[0, :, g * gw:(g + 1) * gw] = seg * lax.rsqrt(ms + EPS) * nrm_ref[:, g * gw:(g + 1) * gw]

    @pl.when(c == nc - 1)
    def _():
        snew_ref[0] = state[...]
        cnew_ref[0] = tail


def _ssd_call(proj, n_valid, conv_prev, ssm_prev, conv_w, conv_b, dt_bias, a_log, ssd_d, ssd_norm):
    b, sp, _ = proj.shape
    nc = sp // LC
    pad = lambda v: jnp.pad(v, (0, LANE - SSD_HEADS)).reshape(1, LANE)
    const2 = lambda bi, c: (0, 0)
    return pl.pallas_call(
        functools.partial(_ssd_kernel, n_valid),
        grid=(b, nc),
        in_specs=[pl.BlockSpec((1, LC, SSD_CONV_DIM), lambda bi, c: (bi, c, C_XBC // SSD_CONV_DIM)),
                  pl.BlockSpec((1, LC, SSD_INNER), lambda bi, c: (bi, c, C_Z // SSD_INNER)),
                  pl.BlockSpec((1, LC, LANE), lambda bi, c: (bi, c, C_DT // LANE)),
                  pl.BlockSpec((1, SSD_CONV - 1, SSD_CONV_DIM), lambda bi, c: (bi, 0, 0)),
                  pl.BlockSpec((1, SSD_HEADS, SSD_HEAD_DIM, SSD_STATE), lambda bi, c: (bi, 0, 0, 0)),
                  pl.BlockSpec((SSD_CONV, SSD_CONV_DIM), const2),
                  pl.BlockSpec((1, SSD_CONV_DIM), const2),
                  pl.BlockSpec((1, LANE), const2),
                  pl.BlockSpec((1, LANE), const2),
                  pl.BlockSpec((1, SSD_INNER), const2),
                  pl.BlockSpec((1, SSD_INNER), const2)],
        out_specs=[pl.BlockSpec((1, LC, SSD_INNER), lambda bi, c: (bi, c, 0)),
                   pl.BlockSpec((1, SSD_HEADS, SSD_HEAD_DIM, SSD_STATE), lambda bi, c: (bi, 0, 0, 0)),
                   pl.BlockSpec((1, SSD_CONV - 1, SSD_CONV_DIM), lambda bi, c: (bi, 0, 0))],
        out_shape=[jax.ShapeDtypeStruct((b, sp, SSD_INNER), F32),
                   jax.ShapeDtypeStruct((b, SSD_HEADS, SSD_HEAD_DIM, SSD_STATE), F32),
                   jax.ShapeDtypeStruct((b, SSD_CONV - 1, SSD_CONV_DIM), F32)],
        scratch_shapes=[pltpu.VMEM((8 + LC, SSD_CONV_DIM), F32),
                        pltpu.VMEM((SSD_HEADS, SSD_HEAD_DIM, SSD_STATE), F32),
                        pltpu.VMEM((LC, SSD_INNER), F32)],
        compiler_params=_cparams("parallel", "arbitrary"),
        name="ssd",
    )(proj, proj, proj, conv_prev, ssm_prev, conv_w, conv_b.reshape(1, -1), pad(dt_bias), pad(a_log),
      jnp.repeat(ssd_d, SSD_HEAD_DIM).reshape(1, -1), ssd_norm.reshape(1, -1))


def _ret_log_g(h):
    return float(np.log1p(-np.exp2(np.float32(-5.0 - h)), dtype=np.float32))


def _ret_kernel(n_valid, q_ref, k_ref, v_ref, g_ref, cos_ref, sin_ref, rprev_ref, o_ref, rnew_ref, state):
    c = pl.program_id(1)
    nc = pl.num_programs(1)
    kd, vd = RET_KEY_DIM, RET_VAL_DIM
    half = kd // 2

    @pl.when(c == 0)
    def _():
        state[...] = rprev_ref[0]

    cos = cos_ref[...]
    sin = sin_ref[...]
    first_half = (_iota((LC, RET_HEADS * kd), 1) % kd) < half

    def rot(x):
        n = x.shape[-1]
        swapped = jnp.where(first_half, pltpu.roll(x, n - half, 1), pltpu.roll(x, half, 1))
        return x * cos + swapped * sin

    q = rot(q_ref[0])
    k = rot(k_ref[0]) * (kd ** -0.5)
    v = v_ref[0]
    gate = g_ref[0]
    ti = _iota((LC, LC), 0)
    tj = _iota((LC, LC), 1)
    diff = (ti - tj).astype(F32)
    ipos = _iota((LC, 1), 0)
    i1 = ipos.astype(F32)
    for h in range(RET_HEADS):
        lg = _ret_log_g(h)
        dmat = jnp.where(diff >= 0, jnp.exp(diff * lg), 0.0)
        q_dec = jnp.exp((i1 + 1.0) * lg)
        k_dec = jnp.where(ipos < n_valid, jnp.exp((n_valid - 1.0 - i1) * lg), 0.0)
        c_dec = math.exp(n_valid * lg)
        qh = q[:, h * kd:(h + 1) * kd].astype(BF16)
        kh = k[:, h * kd:(h + 1) * kd]
        vh = v[:, h * vd:(h + 1) * vd].astype(BF16)
        sc = _nt(qh, kh.astype(BF16)) * dmat
        intra = jnp.dot(sc.astype(BF16), vh, preferred_element_type=F32)
        rs = state[h]
        cross = jnp.dot(qh, rs.astype(BF16), preferred_element_type=F32) * q_dec
        state[h] = rs * c_dec + _tn((kh * k_dec).astype(BF16), vh)
        o = intra + cross
        ms = jnp.mean(o * o, axis=-1, keepdims=True)
        o_ref[0, :, h * vd:(h + 1) * vd] = o * lax.rsqrt(ms + EPS) * _silu(gate[:, h * vd:(h + 1) * vd])

    @pl.when(c == nc - 1)
    def _():
        rnew_ref[0] = state[...]


def _rotary_tables(offset, s_real, s_pad):
    half = RET_KEY_DIM // 2
    freqs = 1.0 / (10000.0 ** jnp.linspace(0.0, 1.0, half, dtype=F32))
    pos = (offset + jnp.arange(s_real)).astype(F32)
    ang = pos[:, None] * freqs[None, :]
    cos, sin = jnp.cos(ang), jnp.sin(ang)
    cos = jnp.tile(jnp.concatenate([cos, cos], axis=-1), (1, RET_HEADS))
    sin = jnp.tile(jnp.concatenate([-sin, sin], axis=-1), (1, RET_HEADS))
    padr = ((0, s_pad - s_real), (0, 0))
    return jnp.pad(cos, padr), jnp.pad(sin, padr)


def _ret_call(proj, n_valid, ret_prev, cos, sin):
    b, sp, _ = proj.shape
    nc = sp // LC
    qw = RET_HEADS * RET_KEY_DIM
    return pl.pallas_call(
        functools.partial(_ret_kernel, n_valid),
        grid=(b, nc),
        in_specs=[pl.BlockSpec((1, LC, qw), lambda bi, c: (bi, c, C_RQ // qw)),
                  pl.BlockSpec((1, LC, qw), lambda bi, c: (bi, c, C_RK // qw)),
                  pl.BlockSpec((1, LC, RET_WIDTH), lambda bi, c: (bi, c, C_RV // RET_WIDTH)),
                  pl.BlockSpec((1, LC, RET_WIDTH), lambda bi, c: (bi, c, C_RG // RET_WIDTH)),
                  pl.BlockSpec((LC, qw), lambda bi, c: (c, 0)),
                  pl.BlockSpec((LC, qw), lambda bi, c: (c, 0)),
                  pl.BlockSpec((1, RET_HEADS, RET_KEY_DIM, RET_VAL_DIM), lambda bi, c: (bi, 0, 0, 0))],
        out_specs=[pl.BlockSpec((1, LC, RET_WIDTH), lambda bi, c: (bi, c, 0)),
                   pl.BlockSpec((1, RET_HEADS, RET_KEY_DIM, RET_VAL_DIM), lambda bi, c: (bi, 0, 0, 0))],
        out_shape=[jax.ShapeDtypeStruct((b, sp, RET_WIDTH), F32),
                   jax.ShapeDtypeStruct((b, RET_HEADS, RET_KEY_DIM, RET_VAL_DIM), F32)],
        scratch_shapes=[pltpu.VMEM((RET_HEADS, RET_KEY_DIM, RET_VAL_DIM), F32)],
        compiler_params=_cparams("parallel", "arbitrary"),
        name="retention",
    )(proj, proj, proj, proj, cos, sin, ret_prev)


def _slope(h):
    return float(2.0 ** -(h + 1))


def _prep_kernel(nq_ref, nkv_ref, gq_ref, gk_ref, isk_ref, qn_ref, kvn_ref, kvb_ref):
    inv = 1.0 / NSA_HEAD_DIM
    nq = nq_ref[0]
    qn_ref[0] = nq * lax.rsqrt(_seg_sum(nq * nq, NSA_HEAD_DIM) * inv + EPS) * gq_ref[...]
    kv = nkv_ref[0]
    normed = kv * lax.rsqrt(_seg_sum(kv * kv, NSA_HEAD_DIM) * inv + EPS) * gk_ref[...]
    kvn = jnp.where(isk_ref[...] > 0.5, normed, kv)
    kvn_ref[0] = kvn
    kvb_ref[0] = kvn[:, KV_ROW:].astype(BF16)


def _prep_call(proj, s_real, qk_gain):
    b = proj.shape[0]
    ts = min(s_real, 512)
    hd = NSA_HEAD_DIM
    gq = (jnp.tile(qk_gain[0], NSA_HEADS) * (hd ** -0.5)).reshape(1, -1)
    ones = jnp.ones((2 * hd,), F32)
    gk = jnp.concatenate([jnp.concatenate([jnp.tile(qk_gain[1 + i], 2), ones]) for i in range(3)]).reshape(1, -1)
    isk = jnp.tile(jnp.concatenate([ones, 0.0 * ones]), 3).reshape(1, -1)
    const2 = lambda bi, i: (0, 0)
    return pl.pallas_call(
        _prep_kernel,
        grid=(b, s_real // ts),
        in_specs=[pl.BlockSpec((1, ts, NSA_WIDTH), lambda bi, i: (bi, i, C_NQ // NSA_WIDTH)),
                  pl.BlockSpec((1, ts, 3 * KV_ROW), lambda bi, i: (bi, i, C_NKV // (3 * KV_ROW))),
                  pl.BlockSpec((1, NSA_WIDTH), const2),
                  pl.BlockSpec((1, 3 * KV_ROW), const2),
                  pl.BlockSpec((1, 3 * KV_ROW), const2)],
        out_specs=[pl.BlockSpec((1, ts, NSA_WIDTH), lambda bi, i: (bi, i, 0)),
                   pl.BlockSpec((1, ts, 3 * KV_ROW), lambda bi, i: (bi, i, 0)),
                   pl.BlockSpec((1, ts, 2 * KV_ROW), lambda bi, i: (bi, i, 0))],
        out_shape=[jax.ShapeDtypeStruct((b, s_real, NSA_WIDTH), F32),
                   jax.ShapeDtypeStruct((b, s_real, 3 * KV_ROW), F32),
                   jax.ShapeDtypeStruct((b, s_real, 2 * KV_ROW), BF16)],
        compiler_params=_cparams("parallel", "parallel"),
        name="nsa_prep",
    )(proj, proj, gq, gk, isk)


def _compress_weights(cmp_pe, cmp_w1, cmp_w2, gain4):
    hd, hid = NSA_HEAD_DIM, NSA_CMP_HIDDEN
    wbig = jnp.zeros((NSA_BLOCK, KV_ROW, 4 * hid), BF16)
    w2big = jnp.zeros((4 * hid, KV_ROW), BF16)
    for cg in range(4):
        c = cg // 2
        wbig = wbig.at[:, cg * hd:(cg + 1) * hd, cg * hid:(cg + 1) * hid].set(cmp_w1[c].astype(BF16))
        w2big = w2big.at[cg * hid:(cg + 1) * hid, cg * hd:(cg + 1) * hd].set(cmp_w2[c].astype(BF16))
    pe_row = jnp.concatenate([cmp_pe[0], cmp_pe[0], cmp_pe[1], cmp_pe[1]], axis=-1)
    g4 = jnp.concatenate([jnp.tile(gain4, 2), jnp.ones((2 * hd,), F32)]).reshape(1, -1)
    return pe_row, wbig, w2big, g4


def _compress_core(get_rows, nblk, pe_ref, wbig_ref, w2_ref, g4_ref):
    acc = jnp.zeros((nblk, 4 * NSA_CMP_HIDDEN), F32)
    for l in range(NSA_BLOCK):
        x = get_rows(l) + pe_ref[l:l + 1, :]
        acc = acc + jnp.dot(x.astype(BF16), wbig_ref[l], preferred_element_type=F32)
    out = jnp.dot(_silu(acc).astype(BF16), w2_ref[...], preferred_element_type=F32)
    ss = _seg_sum(out * out, NSA_HEAD_DIM)
    normed = out * lax.rsqrt(ss * (1.0 / NSA_HEAD_DIM) + EPS) * g4_ref[...]
    return jnp.where(_iota(out.shape, 1) < 2 * NSA_HEAD_DIM, normed, out)


def _cmp_prompt_kernel(nblk, kv_ref, pe_ref, wbig_ref, w2_ref, g4_ref, o_ref):
    o_ref[0] = _compress_core(lambda l: kv_ref[0, :, l, :], nblk, pe_ref, wbig_ref, w2_ref, g4_ref)


def _cmp_prompt_call(kvn, cw):
    b, s, _ = kvn.shape
    nblk = s // NSA_BLOCK
    pe_row, wbig, w2big, g4 = cw
    kv4 = kvn.reshape(b, nblk, NSA_BLOCK, 3 * KV_ROW)
    return pl.pallas_call(
        functools.partial(_cmp_prompt_kernel, nblk),
        grid=(b,),
        in_specs=[pl.BlockSpec((1, nblk, NSA_BLOCK, KV_ROW), lambda bi: (bi, 0, 0, 0)),
                  pl.BlockSpec(pe_row.shape, lambda bi: (0, 0)),
                  pl.BlockSpec(wbig.shape, lambda bi: (0, 0, 0)),
                  pl.BlockSpec(w2big.shape, lambda bi: (0, 0)),
                  pl.BlockSpec(g4.shape, lambda bi: (0, 0))],
        out_specs=pl.BlockSpec((1, nblk, KV_ROW), lambda bi: (bi, 0, 0)),
        out_shape=jax.ShapeDtypeStruct((b, nblk, KV_ROW), F32),
        compiler_params=_cparams("arbitrary"),
        name="nsa_cmp_prompt",
    )(kv4, pe_row, wbig, w2big, g4)


def _masked_softmax(s, mask):
    s = jnp.where(mask, s, NEG_INF)
    p = jnp.exp(s - jnp.max(s, axis=-1, keepdims=True))
    return p / jnp.sum(p, axis=-1, keepdims=True) * mask.astype(F32)


def _select_blocks(imp, blk, n_top, axis=-1):
    sel = jnp.zeros(imp.shape, F32)
    big = imp.shape[axis]
    for _ in range(n_top):
        m = jnp.max(imp, axis=axis, keepdims=True)
        idx = jnp.min(jnp.where(imp == m, blk, big), axis=axis, keepdims=True)
        hit = blk == idx
        sel = jnp.where(hit & (m >= 0.0), 1.0, sel)
        imp = jnp.where(hit, -2.0, imp)
    return sel


M_FLOOR = -1e29


def _nsa_prompt_kernel(nbc, s_len, tk, wk, q_ref, gate_ref, cmp_ref, ksel_ref, kwin_ref, o_ref, bias_ref, dist_ref):
    qi = pl.program_id(1)
    hd, rep, nh = NSA_HEAD_DIM, NSA_REP, NSA_HEADS
    q = q_ref[0]
    qb = q.astype(BF16)
    gates = jax.nn.sigmoid(gate_ref[0])
    t = qi * TQ + _iota((TQ, 1), 0)
    t_row = qi * TQ + _iota((1, TQ), 1)
    kcvc = cmp_ref[0]
    blk = _iota((nbc, TQ), 0)
    dist_c = t_row - ((blk + 1) * NSA_BLOCK - 1)
    mask_c = dist_c >= 0
    mask_cf = mask_c.astype(F32)
    dist_cf = dist_c.astype(F32)
    cur = t_row // NSA_BLOCK
    o_cmp = []
    sels = []
    first_kt = []
    n_kt = (qi * TQ + TQ + tk - 1) // tk
    for g in range(NSA_KV_HEADS):
        kc = kcvc[:, g * hd:(g + 1) * hd]
        vc = kcvc[:, 2 * hd + g * hd:2 * hd + (g + 1) * hd]
        imp = jnp.zeros((nbc, TQ), F32)
        for r in range(rep):
            h = g * rep + r
            s = jnp.where(mask_c, _nt(kc, q[:, h * hd:(h + 1) * hd], precision=HI) - _slope(h) * dist_cf, NEG_INF)
            p = jnp.exp(s - jnp.max(s, axis=0, keepdims=True))
            p = p / jnp.sum(p, axis=0, keepdims=True) * mask_cf
            imp = imp + p
            o_cmp.append(_tn(p, vc, precision=HI))
        imp = jnp.where(blk < cur, imp, -1.0)
        sel = jnp.maximum(_select_blocks(imp, blk, NSA_TOPN - 1, axis=0), (blk == cur).astype(F32))
        sels.append(sel.astype(BF16))
        bpt = tk // NSA_BLOCK
        first = n_kt
        for kt in reversed(range(nbc // bpt)):
            first = jnp.where(jnp.max(sel[kt * bpt:(kt + 1) * bpt, :]) > 0.0, kt, first)
        first_kt.append(first)

    def fill(kt, carry):
        k0 = pl.multiple_of(kt * tk, tk)
        dist = t - (k0 + _iota((TQ, tk), 1))
        dist_ref[:, pl.ds(k0, tk)] = dist.astype(F32)
        expand = (_iota((nbc, tk), 0) == (k0 + _iota((nbc, tk), 1)) // NSA_BLOCK).astype(BF16)
        for g in range(NSA_KV_HEADS):
            picked = _tn(sels[g], expand) > 0.5
            bias_ref[g, :, pl.ds(k0, tk)] = jnp.where(picked & (dist >= 0), 0.0, NEG_INF)
        return carry

    lax.fori_loop(jnp.minimum(first_kt[0], first_kt[1]), n_kt, fill, 0)

    start_w = pl.multiple_of(jnp.maximum(qi * TQ + TQ - wk, 0), TQ)
    kw_all = kwin_ref[0, pl.ds(start_w, wk), :]
    dist_w = t - (start_w + _iota((TQ, wk), 1))
    bias_w = jnp.where((dist_w >= 0) & (dist_w <= NSA_WINDOW), 0.0, NEG_INF)
    dist_wf = dist_w.astype(F32)
    rows_g = rep * TQ
    for g in range(NSA_KV_HEADS):
        heads = range(g * rep, (g + 1) * rep)
        qg = jnp.concatenate([qb[:, h * hd:(h + 1) * hd] for h in heads], axis=0)

        def stacked(bias, dist, heads=heads):
            return jnp.concatenate([bias - _slope(h) * dist for h in heads], axis=0)

        def body(kt, carry, g=g, qg=qg, stacked=stacked):
            m, l, acc = carry
            k0 = pl.multiple_of(kt * tk, tk)
            kk = ksel_ref[0, pl.ds(k0, tk), g * hd:(g + 1) * hd]
            vv = ksel_ref[0, pl.ds(k0, tk), 2 * hd + g * hd:2 * hd + (g + 1) * hd]
            s = _nt(qg, kk) + stacked(bias_ref[g, :, pl.ds(k0, tk)], dist_ref[:, pl.ds(k0, tk)])
            m_new = jnp.maximum(m, jnp.max(s, axis=-1, keepdims=True))
            alpha = jnp.exp(m - m_new)
            p = jnp.exp(s - m_new)
            l = alpha * l + jnp.sum(p, axis=-1, keepdims=True)
            acc = alpha * acc + jnp.dot(p.astype(BF16), vv, preferred_element_type=F32)
            return m_new, l, acc

        init = (jnp.full((rows_g, 1), M_FLOOR, F32), jnp.zeros((rows_g, 1), F32), jnp.zeros((rows_g, hd), F32))
        _, l_s, acc_s = lax.fori_loop(first_kt[g], n_kt, body, init)
        o_sel = acc_s / l_s
        s_w = _nt(qg, kw_all[:, g * hd:(g + 1) * hd]) + stacked(bias_w, dist_wf)
        p_w = jnp.exp(s_w - jnp.max(s_w, axis=-1, keepdims=True))
        o_win = jnp.dot(p_w.astype(BF16), kw_all[:, 2 * hd + g * hd:2 * hd + (g + 1) * hd],
                        preferred_element_type=F32) / jnp.sum(p_w, axis=-1, keepdims=True)
        for j, h in enumerate(heads):
            rs = slice(j * TQ, (j + 1) * TQ)
            o_ref[0, :, h * hd:(h + 1) * hd] = (gates[:, h:h + 1] * o_cmp[h] + gates[:, nh + h:nh + h + 1] * o_sel[rs]
                                                + gates[:, 2 * nh + h:2 * nh + h + 1] * o_win[rs])


def _nsa_prompt_call(proj, qn, kvb, kcvc):
    b, s, _ = qn.shape
    nbc = kcvc.shape[1]
    tk = min(512, s)
    wk = min(NSA_WINDOW + TQ, s)
    return pl.pallas_call(
        functools.partial(_nsa_prompt_kernel, nbc, s, tk, wk),
        grid=(b, s // TQ),
        in_specs=[pl.BlockSpec((1, TQ, NSA_WIDTH), lambda bi, i: (bi, i, 0)),
                  pl.BlockSpec((1, TQ, LANE), lambda bi, i: (bi, i, C_NG // LANE)),
                  pl.BlockSpec((1, nbc, KV_ROW), lambda bi, i: (bi, 0, 0)),
                  pl.BlockSpec((1, s, KV_ROW), lambda bi, i: (bi, 0, 0)),
                  pl.BlockSpec((1, s, KV_ROW), lambda bi, i: (bi, 0, 1))],
        out_specs=pl.BlockSpec((1, TQ, NSA_WIDTH), lambda bi, i: (bi, i, 0)),
        out_shape=jax.ShapeDtypeStruct((b, s, NSA_WIDTH), F32),
        scratch_shapes=[pltpu.VMEM((NSA_KV_HEADS, TQ, s), F32), pltpu.VMEM((TQ, s), F32)],
        compiler_params=_cparams("parallel", "arbitrary"),
        name="nsa_prompt",
    )(qn, proj, kcvc, kvb, kvb)


def _feature_major_pages(cache):
    return cache.transpose(0, 1, 3, 4, 5, 2).reshape(cache.shape[0], cache.shape[1], KV_ROW, PAGE_SIZE)


def _page_copies(cache_hbm, li, pt_ref, b, step, dst_of, sem, slot, pps):
    return [pltpu.make_async_copy(cache_hbm.at[li, pt_ref[b, step * pps + k]], dst_of(slot, k), sem.at[slot])
            for k in range(pps)]


def _pipelined_pages(cache_hbm, li, pt_ref, dst_of, sem, pps):
    b, s, ns = pl.program_id(0), pl.program_id(1), pl.num_programs(1)
    slot = s % 2
    copies = functools.partial(_page_copies, cache_hbm, li, pt_ref, b, dst_of=dst_of, sem=sem, pps=pps)

    @pl.when(s == 0)
    def _():
        for cp in copies(step=s, slot=slot):
            cp.start()

    @pl.when(s + 1 < ns)
    def _():
        for cp in copies(step=s + 1, slot=1 - slot):
            cp.start()

    for cp in copies(step=s, slot=slot):
        cp.wait()
    return slot


def _cmp_sample_weights(cmp_pe, cmp_w1, cmp_w2, gain4):
    wt = cmp_w1.transpose(0, 2, 1, 3).astype(BF16)
    z = jnp.zeros_like(wt)
    w1 = jnp.concatenate([jnp.concatenate([wt, z], axis=-1), jnp.concatenate([z, wt], axis=-1)], axis=2)
    w1 = w1.reshape(2, NSA_HEAD_DIM // 2, 4 * NSA_BLOCK, 2 * NSA_CMP_HIDDEN)
    w2 = cmp_w2.astype(BF16)
    z2 = jnp.zeros_like(w2)
    w2bd = jnp.concatenate([jnp.concatenate([w2, z2], axis=-1), jnp.concatenate([z2, w2], axis=-1)], axis=1)
    pe_t = jnp.tile(cmp_pe.transpose(0, 2, 1), (1, 1, 2)).reshape(2 * NSA_HEAD_DIM, 2 * NSA_BLOCK)
    return pe_t, w1, w2bd, jnp.tile(gain4, 2).reshape(1, -1)


def _cmp_sample_kernel(li, pps, pt_ref, cache_hbm, pe_ref, w1_ref, w2_ref, g_ref, o_ref, buf, sem):
    slot = _pipelined_pages(cache_hbm, li, pt_ref, lambda sl, k: buf.at[sl, k], sem, pps)
    pages = buf.at[slot]
    hd = NSA_HEAD_DIM
    for c in range(2):
        acc = jnp.zeros((NSA_KV_HEADS * pps, 2 * NSA_CMP_HIDDEN), F32)
        def feature_rows(d):
            x = jnp.concatenate([pages[:, (2 * c + g) * hd + d, :] for g in range(NSA_KV_HEADS)], axis=0)
            return (x + pe_ref[c * hd + d:c * hd + d + 1, :]).astype(BF16)

        for d2 in range(hd // 2):
            x = jnp.concatenate([feature_rows(2 * d2), feature_rows(2 * d2 + 1)], axis=1)
            acc = acc + jnp.dot(x, w1_ref[c, d2], preferred_element_type=F32)
        out = jnp.dot(_silu(acc).astype(BF16), w2_ref[c], preferred_element_type=F32)
        if c == 0:
            out = out * lax.rsqrt(_seg_sum(out * out, hd) * (1.0 / hd) + EPS) * g_ref[...]
        for g in range(NSA_KV_HEADS):
            o_ref[0, :, (2 * c + g) * 2 * hd:(2 * c + g + 1) * 2 * hd] = out[g * pps:(g + 1) * pps]


def _cmp_sample_call(cache_t, li, page_table, cw):
    bd, n_pages = page_table.shape
    pps = math.gcd(n_pages, 64)
    pe_t, w1, w2bd, gain = cw
    hd = NSA_HEAD_DIM
    grid_spec = pltpu.PrefetchScalarGridSpec(
        num_scalar_prefetch=1,
        grid=(bd, n_pages // pps),
        in_specs=[pl.BlockSpec(memory_space=pl.ANY),
                  pl.BlockSpec(pe_t.shape, lambda b, s, pt: (0, 0)),
                  pl.BlockSpec(w1.shape, lambda b, s, pt: (0, 0, 0, 0)),
                  pl.BlockSpec(w2bd.shape, lambda b, s, pt: (0, 0, 0)),
                  pl.BlockSpec(gain.shape, lambda b, s, pt: (0, 0))],
        out_specs=pl.BlockSpec((1, pps, 2 * KV_ROW), lambda b, s, pt: (b, s, 0)),
        scratch_shapes=[pltpu.VMEM((2, pps, KV_ROW, PAGE_SIZE), F32), pltpu.SemaphoreType.DMA((2,))])
    packed = pl.pallas_call(
        functools.partial(_cmp_sample_kernel, li, pps),
        grid_spec=grid_spec,
        out_shape=jax.ShapeDtypeStruct((bd, n_pages, 2 * KV_ROW), F32),
        compiler_params=_cparams("arbitrary", "arbitrary"),
        name="nsa_cmp_sample",
    )(page_table, cache_t, pe_t, w1, w2bd, gain)
    return packed.reshape(bd, n_pages, 4, 2, hd).transpose(0, 1, 3, 2, 4).reshape(bd, 2 * n_pages, KV_ROW)


def _pick_group(x, rows_g0):
    w = x.shape[-1] // 2
    return jnp.where(rows_g0, x[:, :w], x[:, w:])


def _nsa_sample_kernel(li, pps, past, n_tok, pt_ref, cache_hbm, q_ref, gate_ref, cmp_ref, knew_ref, wold_ref, wnew_ref,
                       o_ref, buf, sem, qbd_ref, sel_ref, ocw_ref, m_ref, l_ref, acc_ref):
    s, ns = pl.program_id(1), pl.num_programs(1)
    hd, nh, rep = NSA_HEAD_DIM, NSA_HEADS, NSA_REP
    rows = nh * n_tok
    nbc = past // NSA_BLOCK
    tk = pps * PAGE_SIZE
    slot = _pipelined_pages(cache_hbm, li, pt_ref,
                            lambda sl, k: buf.at[sl, :, pl.ds(k * PAGE_SIZE, PAGE_SIZE)], sem, pps)
    ridx = _iota((rows, 1), 0)
    rows_g0 = ridx < rep * n_tok
    tok = ridx % n_tok
    t = past + tok
    slope = jnp.exp2(-(ridx // n_tok + 1).astype(F32))

    @pl.when(s == 0)
    def _():
        qtok = q_ref[0]
        qrows = jnp.concatenate([qtok[:, h * hd:(h + 1) * hd] for h in range(nh)], axis=0)
        zero = jnp.zeros_like(qrows)
        qbd = jnp.where(rows_g0, jnp.concatenate([qrows, zero], axis=1), jnp.concatenate([zero, qrows], axis=1))
        qbd_ref[...] = qbd
        qbb = qbd.astype(BF16)
        kcvc = cmp_ref[0]
        blk = _iota((rows, nbc), 1)
        dist_c = t - ((blk + 1) * NSA_BLOCK - 1)
        p_c = _masked_softmax(_nt(qbd, kcvc[:, :2 * hd], precision=HI) - slope * dist_c.astype(F32), dist_c >= 0)
        o_c = _pick_group(jnp.dot(p_c, kcvc[:, 2 * hd:], precision=HI, preferred_element_type=F32), rows_g0)
        blk_t = _iota((n_tok, nbc), 1)
        cur_t = (past + _iota((n_tok, 1), 0)) // NSA_BLOCK
        sels = []
        for g in range(NSA_KV_HEADS):
            imp = p_c[g * rep * n_tok:(g * rep + 1) * n_tok]
            for r in range(1, rep):
                imp = imp + p_c[(g * rep + r) * n_tok:(g * rep + r + 1) * n_tok]
            imp = jnp.where(blk_t < cur_t, imp, -1.0)
            sels += [_select_blocks(imp, blk_t, NSA_TOPN - 1)] * rep
        sel_ref[...] = jnp.concatenate(sels, axis=0).astype(BF16)
        knew = knew_ref[0]
        dist_n = tok - _iota((rows, n_tok), 1)
        mask_n = dist_n >= 0
        s_n = jnp.where(mask_n, _nt(qbb, knew[:, :2 * hd].astype(BF16)) - slope * dist_n.astype(F32), NEG_INF)
        m0 = jnp.max(s_n, axis=-1, keepdims=True)
        p_n = jnp.where(mask_n, jnp.exp(s_n - m0), 0.0)
        m_ref[...] = m0
        l_ref[...] = jnp.sum(p_n, axis=-1, keepdims=True)
        acc_ref[...] = jnp.dot(p_n.astype(BF16), knew[:, 2 * hd:].astype(BF16), preferred_element_type=F32)
        kw = jnp.concatenate([wold_ref[0], wnew_ref[0]], axis=0)
        dist_w = tok + NSA_WINDOW - _iota((rows, NSA_WINDOW + n_tok), 1)
        mask_w = (dist_w >= 0) & (dist_w <= NSA_WINDOW)
        p_w = _masked_softmax(_nt(qbb, kw[:, :2 * hd].astype(BF16)) - slope * dist_w.astype(F32), mask_w)
        o_w = _pick_group(jnp.dot(p_w.astype(BF16), kw[:, 2 * hd:].astype(BF16), preferred_element_type=F32), rows_g0)
        ocw_ref[0] = o_c
        ocw_ref[1] = o_w

    pages = buf.at[slot]
    k_t = pages[:2 * hd, :].astype(BF16)
    v_t = pages[2 * hd:, :].astype(BF16)
    k0 = s * tk
    expand = (_iota((nbc, tk), 0) == (k0 + _iota((nbc, tk), 1)) // NSA_BLOCK).astype(BF16)
    mask = jnp.dot(sel_ref[...], expand, preferred_element_type=F32) > 0.5
    dist = t - (k0 + _iota((rows, tk), 1))
    qk = jnp.dot(qbd_ref[...].astype(BF16), k_t, preferred_element_type=F32)
    sc = jnp.where(mask, qk - slope * dist.astype(F32), NEG_INF)
    m_old = m_ref[...]
    m_new = jnp.maximum(m_old, jnp.max(sc, axis=-1, keepdims=True))
    alpha = jnp.exp(m_old - m_new)
    p = jnp.where(mask, jnp.exp(sc - m_new), 0.0)
    m_ref[...] = m_new
    l_ref[...] = alpha * l_ref[...] + jnp.sum(p, axis=-1, keepdims=True)
    acc_ref[...] = alpha * acc_ref[...] + _nt(p.astype(BF16), v_t)

    @pl.when(s == ns - 1)
    def _():
        o_s = _pick_group(acc_ref[...] / l_ref[...], rows_g0)
        gates = jax.nn.sigmoid(gate_ref[0])

        def gate_rows(br):
            return jnp.concatenate([gates[:, br * nh + h:br * nh + h + 1] for h in range(nh)], axis=0)

        o_ref[0] = gate_rows(0) * ocw_ref[0] + gate_rows(1) * o_s + gate_rows(2) * ocw_ref[1]


def _nsa_sample_call(cache_t, li, page_table, proj, qn, kvn, kcvc, win_old, past):
    bd, n_pages = page_table.shape
    n_tok = qn.shape[1]
    pps = math.gcd(n_pages, 32)
    rows = NSA_HEADS * n_tok
    nbc = kcvc.shape[1]
    grid_spec = pltpu.PrefetchScalarGridSpec(
        num_scalar_prefetch=1,
        grid=(bd, n_pages // pps),
        in_specs=[pl.BlockSpec(memory_space=pl.ANY),
                  pl.BlockSpec((1, n_tok, NSA_WIDTH), lambda b, s, pt: (b, 0, 0)),
                  pl.BlockSpec((1, n_tok, LANE), lambda b, s, pt: (b, 0, C_NG // LANE)),
                  pl.BlockSpec((1, nbc, KV_ROW), lambda b, s, pt: (b, 0, 0)),
                  pl.BlockSpec((1, n_tok, KV_ROW), lambda b, s, pt: (b, 0, 1)),
                  pl.BlockSpec((1, NSA_WINDOW, KV_ROW), lambda b, s, pt: (b, 0, 0)),
                  pl.BlockSpec((1, n_tok, KV_ROW), lambda b, s, pt: (b, 0, 2))],
        out_specs=pl.BlockSpec((1, rows, NSA_HEAD_DIM), lambda b, s, pt: (b, 0, 0)),
        scratch_shapes=[pltpu.VMEM((2, KV_ROW, pps * PAGE_SIZE), F32),
                        pltpu.SemaphoreType.DMA((2,)),
                        pltpu.VMEM((rows, 2 * NSA_HEAD_DIM), F32),
                        pltpu.VMEM((rows, nbc), BF16),
                        pltpu.VMEM((2, rows, NSA_HEAD_DIM), F32),
                        pltpu.VMEM((rows, 1), F32),
                        pltpu.VMEM((rows, 1), F32),
                        pltpu.VMEM((rows, 2 * NSA_HEAD_DIM), F32)])
    return pl.pallas_call(
        functools.partial(_nsa_sample_kernel, li, pps, past, n_tok),
        grid_spec=grid_spec,
        out_shape=jax.ShapeDtypeStruct((bd, rows, NSA_HEAD_DIM), F32),
        compiler_params=_cparams("arbitrary", "arbitrary"),
        name="nsa_sample",
    )(page_table, cache_t, qn, proj, kcvc, kvn, win_old, kvn)


def _outproj_kernel(y_ref, n_ref, r_ref, w_ref, x_ref, g_ref, o_ref, mix_ref):
    @pl.when(pl.program_id(2) == 0)
    def _():
        mix_ref[:, :SSD_INNER] = y_ref[0].astype(BF16)
        mix_ref[:, SSD_INNER:SSD_INNER + NSA_WIDTH] = n_ref[0].astype(BF16)
        mix_ref[:, SSD_INNER + NSA_WIDTH:] = r_ref[0].astype(BF16)

    o_ref[0] = x_ref[0] + g_ref[0] * jnp.dot(mix_ref[...], w_ref[...], preferred_element_type=F32)


def _outproj_call(y, o_nsa, o_ret, w_bf, x, gate):
    b, s, d = x.shape
    ts = min(s, 1024)
    tn = 512
    sm = gate.shape[1]
    tsm = 1 if sm == 1 else ts
    mod_map = (lambda bi, i, j: (bi, 0, j)) if sm == 1 else (lambda bi, i, j: (bi, i, j))
    row_map = lambda bi, i, j: (bi, i, 0)
    return pl.pallas_call(
        _outproj_kernel,
        grid=(b, s // ts, d // tn),
        in_specs=[pl.BlockSpec((1, ts, SSD_INNER), row_map),
                  pl.BlockSpec((1, ts, NSA_WIDTH), row_map),
                  pl.BlockSpec((1, ts, RET_WIDTH), row_map),
                  pl.BlockSpec((D_MIX, tn), lambda bi, i, j: (0, j)),
                  pl.BlockSpec((1, ts, tn), lambda bi, i, j: (bi, i, j)),
                  pl.BlockSpec((1, tsm, tn), mod_map)],
        out_specs=pl.BlockSpec((1, ts, tn), lambda bi, i, j: (bi, i, j)),
        out_shape=jax.ShapeDtypeStruct((b, s, d), F32),
        scratch_shapes=[pltpu.VMEM((ts, D_MIX), BF16)],
        compiler_params=_cparams("parallel", "parallel", "arbitrary"),
        name="out_proj",
    )(y, o_nsa, o_ret, w_bf, x, gate)


def _route_kernel(x_ref, sc_ref, sh_ref, g_ref, wr_ref, br_ref, h_ref, meta_ref, cnt_ref):
    h = _mod_norm(x_ref[0], g_ref[...], sc_ref[0], sh_ref[0])
    n_rows = h.shape[0]
    for c in range(ROW_CHUNKS):
        h_ref[0, pl.ds(c, n_rows, stride=ROW_CHUNKS), :] = h[:, c * LANE:(c + 1) * LANE]
    logit = jnp.dot(h, wr_ref[...], precision=HI, preferred_element_type=F32) + br_ref[...]
    lane = _iota(logit.shape, 1)
    first = lambda hit: jnp.min(jnp.where(hit, lane, LANE), axis=-1, keepdims=True)
    is_g = lane < MOE_GROUPS
    gl = jnp.where(is_g, logit, -jnp.inf)
    gmax = jnp.max(gl, axis=-1, keepdims=True)
    g_idx = first(gl == gmax)
    g_p = 1.0 / jnp.sum(jnp.where(is_g, jnp.exp(logit - gmax), 0.0), axis=-1, keepdims=True)
    e_lane = lane - MOE_GROUPS
    in_grp = (e_lane >= 0) & (e_lane < MOE_EXPERTS) & (e_lane // MOE_EXPERTS_PER_GROUP == g_idx)
    el = jnp.where(in_grp, logit, -jnp.inf)
    v1 = jnp.max(el, axis=-1, keepdims=True)
    i1 = first(el == v1)
    el = jnp.where(lane == i1, -jnp.inf, el)
    v2 = jnp.max(el, axis=-1, keepdims=True)
    i2 = first(el == v2)
    e = jnp.exp(v2 - v1)
    w1 = g_p / (1.0 + e)
    w2 = g_p * e / (1.0 + e)
    meta = jnp.where(lane == 0, (i1 - MOE_GROUPS).astype(F32), 0.0)
    meta = jnp.where(lane == 1, (i2 - MOE_GROUPS).astype(F32), meta)
    meta = jnp.where(lane == 2, w1, meta)
    meta_ref[0] = jnp.where(lane == 3, w2, meta)
    cnt_ref[0, 0] = jnp.sum(((lane == i1) | (lane == i2)).astype(F32), axis=0, keepdims=True)


def _route_call(x1, scale, shift, gain, wr, br):
    b, s, d = x1.shape
    ts = min(s, 512)
    sm = scale.shape[1]
    tsm = 1 if sm == 1 else ts
    mod_map = (lambda bi, i: (bi, 0, 0)) if sm == 1 else (lambda bi, i: (bi, i, 0))
    return pl.pallas_call(
        _route_kernel,
        grid=(b, s // ts),
        in_specs=[pl.BlockSpec((1, ts, d), lambda bi, i: (bi, i, 0)),
                  pl.BlockSpec((1, tsm, d), mod_map),
                  pl.BlockSpec((1, tsm, d), mod_map),
                  pl.BlockSpec((1, d), lambda bi, i: (0, 0)),
                  pl.BlockSpec((d, LANE), lambda bi, i: (0, 0)),
                  pl.BlockSpec((1, LANE), lambda bi, i: (0, 0))],
        out_specs=[pl.BlockSpec((1, ts * ROW_CHUNKS, LANE), lambda bi, i: (bi, i, 0)),
                   pl.BlockSpec((1, ts, LANE), lambda bi, i: (bi, i, 0)),
                   pl.BlockSpec((1, 1, 1, LANE), lambda bi, i: (bi, i, 0, 0))],
        out_shape=[jax.ShapeDtypeStruct((b, s * ROW_CHUNKS, LANE), F32), jax.ShapeDtypeStruct((b, s, LANE), F32),
                   jax.ShapeDtypeStruct((b, s // ts, 1, LANE), F32)],
        compiler_params=_cparams("parallel", "parallel"),
        name="moe_route",
    )(x1, scale, shift, gain.reshape(1, d), wr, br)


def _moe_plan(e_ids, w, counts):
    n_pairs = e_ids.size
    tm = TM_MOE
    n_tiles = -(-n_pairs // tm) + MOE_EXPERTS
    n_slots = n_tiles * tm
    n_tok = n_pairs // MOE_TOPK
    pair_ids = jnp.arange(n_pairs, dtype=jnp.int32)
    _, order, w_sorted = lax.sort((e_ids.reshape(-1), pair_ids, w.reshape(-1)), num_keys=1, is_stable=True)
    tiles_per = (counts + tm - 1) // tm
    tile_end = jnp.cumsum(tiles_per)
    tile_start = tile_end - tiles_per
    grp_start = jnp.cumsum(counts) - counts
    n_used = tile_end[-1]
    tiles = jnp.arange(n_tiles, dtype=jnp.int32)
    tile_ids = jnp.minimum(tiles, n_used - 1)
    tile_e = jnp.sum((tile_ids[:, None] >= tile_end[None, :]).astype(jnp.int32), axis=1)
    active = tiles < n_used
    rank0 = (tile_ids - tile_start[tile_e]) * tm
    flag = active.astype(jnp.int32) + (active & (rank0 == 0)).astype(jnp.int32)
    r = jnp.arange(tm, dtype=jnp.int32)[None, :]
    rank = rank0[:, None] + r
    valid = (active[:, None] & (rank < counts[tile_e][:, None])).reshape(-1)
    idx = jnp.clip(grp_start[tile_e][:, None] + rank, 0, n_pairs - 1).reshape(-1)
    pair = order[idx]
    tok = pair // MOE_TOPK
    src = jnp.where(valid, tok, 0) * ROW_CHUNKS
    pad_row = jnp.broadcast_to(n_tok + r, (n_tiles, tm)).reshape(-1)
    dst_row = jnp.where(valid, tok, pad_row) * ROW_CHUNKS
    dst_plane = jnp.where(valid, pair % MOE_TOPK, 0)
    w_slot = jnp.where(valid, w_sorted[idx], 0.0)
    return tile_e, flag, src, dst_row, dst_plane, w_slot.reshape(n_slots, 1)


def _moe_kernel(te_ref, flag_ref, src_ref, row_ref, plane_ref, h_hbm, w_ref, wg_ref, wu_ref, wd_ref, y_hbm, xbuf,
                obuf, xs, wgb, wub, wdb, sem_in, sem_out):
    i = pl.program_id(0)
    n_tiles = pl.num_programs(0)
    flag = flag_ref[i]
    nxt = jnp.minimum(i + 1, n_tiles - 1)
    next_active = (i + 1 < n_tiles) & (flag_ref[nxt] > 0)
    slot = i % 2

    rc = ROW_CHUNKS

    def slab(ref_2d, first_row):
        return ref_2d.at[pl.ds(pl.multiple_of(first_row, rc), rc)]

    def start_gather(tile, sl):
        def body(r, carry):
            pltpu.make_async_copy(slab(h_hbm, src_ref[tile * TM_MOE + r]), slab(xbuf.at[sl], r * rc),
                                  sem_in.at[sl]).start()
            return carry

        lax.fori_loop(0, TM_MOE, body, 0, unroll=8)

    def wait_gather(sl):
        pltpu.make_async_copy(h_hbm.at[pl.ds(0, TM_MOE * rc)], xbuf.at[sl], sem_in.at[sl]).wait()

    def wait_scatter(sl):
        pltpu.make_async_copy(obuf.at[sl], y_hbm.at[0, pl.ds(0, TM_MOE * rc)], sem_out.at[sl]).wait()

    @pl.when(i == 0)
    def _():
        obuf[0] = jnp.zeros(obuf.shape[1:], F32)
        pad_rows = pl.ds(y_hbm.shape[1] - TM_MOE * rc, TM_MOE * rc)
        for plane in range(MOE_TOPK):
            pltpu.make_async_copy(obuf.at[0], y_hbm.at[plane, pad_rows], sem_out.at[0]).start()
        for plane in range(MOE_TOPK):
            pltpu.make_async_copy(obuf.at[0], y_hbm.at[plane, pad_rows], sem_out.at[0]).wait()

    @pl.when(flag > 0)
    def _():
        @pl.when(i == 0)
        def _():
            start_gather(i, slot)

        @pl.when(next_active)
        def _():
            start_gather(i + 1, 1 - slot)

        @pl.when(flag > 1)
        def _():
            wgb[...] = wg_ref[0, 0].astype(BF16)
            wub[...] = wu_ref[0, 0].astype(BF16)
            wdb[...] = wd_ref[0, 0].astype(BF16)

        wait_gather(slot)
        rows = xbuf.at[slot]
        for c in range(rc):
            xs[:, c * LANE:(c + 1) * LANE] = rows[pl.ds(c, TM_MOE, stride=rc), :].astype(BF16)
        x = xs[...]
        a = jnp.dot(x, wgb[...], preferred_element_type=F32)
        u = jnp.dot(x, wub[...], preferred_element_type=F32)
        hid = (_silu(a) * u * w_ref[...]).astype(BF16)
        out = jnp.dot(hid, wdb[...], preferred_element_type=F32)

        @pl.when(i >= 2)
        def _():
            wait_scatter(slot)

        out_rows = obuf.at[slot]
        for c in range(rc):
            out_rows[pl.ds(c, TM_MOE, stride=rc), :] = out[:, c * LANE:(c + 1) * LANE]

        def start_out(r, carry):
            k = i * TM_MOE + r
            pltpu.make_async_copy(slab(out_rows, r * rc), slab(y_hbm.at[plane_ref[k]], row_ref[k]),
                                  sem_out.at[slot]).start()
            return carry

        lax.fori_loop(0, TM_MOE, start_out, 0, unroll=8)

        @pl.when(jnp.logical_not(next_active))
        def _():
            wait_scatter(slot)

            @pl.when(i >= 1)
            def _():
                wait_scatter(1 - slot)


def _moe_call(h_all, plan, li, w_gate, w_up, w_down):
    t = h_all.shape[0] // ROW_CHUNKS
    d = D_MODEL
    tile_e, flag, src, dst_row, dst_plane, w_slot = plan
    n_tiles = tile_e.shape[0]
    tm = TM_MOE
    w_map = lambda i, te, *_: (li, te[i], 0, 0)
    grid_spec = pltpu.PrefetchScalarGridSpec(
        num_scalar_prefetch=5,
        grid=(n_tiles,),
        in_specs=[pl.BlockSpec(memory_space=pl.ANY),
                  pl.BlockSpec((tm, 1), lambda i, *_: (i, 0)),
                  pl.BlockSpec((1, 1, d, MOE_HIDDEN), w_map),
                  pl.BlockSpec((1, 1, d, MOE_HIDDEN), w_map),
                  pl.BlockSpec((1, 1, MOE_HIDDEN, d), w_map)],
        out_specs=pl.BlockSpec(memory_space=pl.ANY),
        scratch_shapes=[pltpu.VMEM((2, tm * ROW_CHUNKS, LANE), F32), pltpu.VMEM((2, tm * ROW_CHUNKS, LANE), F32),
                        pltpu.VMEM((tm, d), BF16),
                        pltpu.VMEM((d, MOE_HIDDEN), BF16), pltpu.VMEM((d, MOE_HIDDEN), BF16),
                        pltpu.VMEM((MOE_HIDDEN, d), BF16),
                        pltpu.SemaphoreType.DMA((2,)), pltpu.SemaphoreType.DMA((2,))])
    return pl.pallas_call(
        _moe_kernel,
        grid_spec=grid_spec,
        out_shape=jax.ShapeDtypeStruct((MOE_TOPK, (t + tm) * ROW_CHUNKS, LANE), F32),
        compiler_params=_cparams("arbitrary"),
        name="moe_experts",
    )(tile_e, flag, src, dst_row, dst_plane, h_all, w_slot, w_gate, w_up, w_down)


def _combine_kernel(x_ref, g_ref, y_ref, o_ref):
    n_rows = x_ref.shape[1]
    for c in range(ROW_CHUNKS):
        cols = slice(c * LANE, (c + 1) * LANE)
        chunk = pl.ds(c, n_rows, stride=ROW_CHUNKS)
        o_ref[0, :, cols] = x_ref[0, :, cols] + g_ref[0, :, cols] * (y_ref[0, chunk, :] + y_ref[1, chunk, :])


def _combine_call(x1, gate, y_all, row0):
    b, s, d = x1.shape
    ts = min(s, 512)
    sm = gate.shape[1]
    tsm = 1 if sm == 1 else ts
    mod_map = (lambda bi, i: (bi, 0, 0)) if sm == 1 else (lambda bi, i: (bi, i, 0))
    blk0 = row0 // ts
    per_b = s // ts
    return pl.pallas_call(
        _combine_kernel,
        grid=(b, per_b),
        in_specs=[pl.BlockSpec((1, ts, d), lambda bi, i: (bi, i, 0)),
                  pl.BlockSpec((1, tsm, d), mod_map),
                  pl.BlockSpec((MOE_TOPK, ts * ROW_CHUNKS, LANE), lambda bi, i: (0, blk0 + bi * per_b + i, 0))],
        out_specs=pl.BlockSpec((1, ts, d), lambda bi, i: (bi, i, 0)),
        out_shape=jax.ShapeDtypeStruct((b, s, d), F32),
        compiler_params=_cparams("parallel", "parallel"),
        name="moe_combine",
    )(x1, gate, y_all)


def _reorder_w_in(w):
    sizes = (SSD_INNER, SSD_CONV_DIM, SSD_HEADS, NSA_WIDTH, 3 * KV_ROW, 3 * NSA_HEADS, RET_HEADS * RET_KEY_DIM,
             RET_HEADS * RET_KEY_DIM, RET_WIDTH, RET_WIDTH)
    z, xbc, dt, nq, nkv, ngate, rq, rk, rv, rg = jnp.split(w, [int(v) for v in np.cumsum(sizes)[:-1]], axis=1)
    padc = lambda a: jnp.pad(a, ((0, 0), (0, LANE - a.shape[1])))
    return jnp.concatenate([z, nq, rv, rg, rq, rk, xbc, nkv, padc(dt), padc(ngate)], axis=1).astype(BF16)


def _kv_rows(a):
    return a.reshape(a.shape[:2] + (2, NSA_KV_HEADS, NSA_HEAD_DIM))


def kernel(x_prompt, x_sample, cache_cmp_kv, cache_sel_kv, cache_win_kv, state_ssm, state_conv, state_ret, page_table, c_prompt, c_sample, w_ada, b_ada, norm_mix, norm_ffn, w_in, w_out, conv_w, conv_b, dt_bias, a_log, ssd_d, ssd_norm, qk_gain, cmp_pe, cmp_w1, cmp_w2, router_group_w, router_group_b, router_expert_w, router_expert_b, expert_w_gate, expert_w_up, expert_w_down):
    bp, sp, d = x_prompt.shape
    bd, sd, _ = x_sample.shape
    n_pages = page_table.shape[1]
    past = n_pages * PAGE_SIZE
    assert d == D_MODEL and sp % LC == 0 and sp % NSA_BLOCK == 0 and sp >= NSA_WINDOW
    assert sd % SUBLANE == 0 and sd <= NSA_BLOCK and sd <= LC and cache_win_kv.shape[2] == NSA_WINDOW
    n_seq = bp + bd
    c_all = jnp.pad(jnp.concatenate([c_prompt, c_sample]), ((0, -n_seq % SUBLANE), (0, 0)))
    mod = _ada_call(c_all, w_ada, b_ada)
    cos_p, sin_p = _rotary_tables(0, sp, sp)
    cos_s, sin_s = _rotary_tables(past, sd, LC)
    cmp_pages = _feature_major_pages(cache_cmp_kv)
    sel_pages = _feature_major_pages(cache_sel_kv)
    xp = x_prompt
    xs = x_sample.reshape(1, bd * sd, d)
    outs_p, outs_s = [], []
    for li in range(DEPTH):
        mods_p = [mod[li, :bp, k * d:(k + 1) * d].reshape(bp, 1, d) for k in range(6)]
        mods_s = [jnp.repeat(mod[li, bp:n_seq, k * d:(k + 1) * d], sd, axis=0).reshape(1, bd * sd, d) for k in range(6)]
        w_in_b = _reorder_w_in(w_in[li])
        w_out_b = w_out[li].astype(BF16)
        cw = _compress_weights(cmp_pe[li], cmp_w1[li], cmp_w2[li], qk_gain[li, 4])
        cw_pages = _cmp_sample_weights(cmp_pe[li], cmp_w1[li], cmp_w2[li], qk_gain[li, 4])
        ssd_w = (conv_w[li], conv_b[li], dt_bias[li], a_log[li], ssd_d[li], ssd_norm[li])
        wr = jnp.pad(jnp.concatenate([router_group_w[li], router_expert_w[li]], axis=1),
                     ((0, 0), (0, LANE - MOE_GROUPS - MOE_EXPERTS)))
        br = jnp.pad(jnp.concatenate([router_group_b[li], router_expert_b[li]]),
                     (0, LANE - MOE_GROUPS - MOE_EXPERTS)).reshape(1, LANE)

        sh_a, sc_a, g_a, sh_f, sc_f, g_f = mods_p
        proj = _inproj_call(xp, sc_a, sh_a, norm_mix[li], w_in_b)
        y_ssd, ssm_p, conv_p = _ssd_call(proj, LC, jnp.zeros((bp, SSD_CONV - 1, SSD_CONV_DIM), F32),
                                         jnp.zeros((bp, SSD_HEADS, SSD_HEAD_DIM, SSD_STATE), F32), *ssd_w)
        o_ret, ret_p = _ret_call(proj, LC, jnp.zeros((bp, RET_HEADS, RET_KEY_DIM, RET_VAL_DIM), F32), cos_p, sin_p)
        qn, kvn_p, kvb = _prep_call(proj, sp, qk_gain[li])
        kcvc = _cmp_prompt_call(kvn_p, cw)
        o_nsa = _nsa_prompt_call(proj, qn, kvb, kcvc)
        x1_p = _outproj_call(y_ssd, o_nsa, o_ret, w_out_b, xp, g_a)
        h2_p, meta_p, cnt_p = _route_call(x1_p, sc_f, sh_f, norm_ffn[li], wr, br)
        gf_p = g_f

        sh_a, sc_a, g_a, sh_f, sc_f, g_f = mods_s
        proj = _inproj_call(xs, sc_a, sh_a, norm_mix[li], w_in_b).reshape(bd, sd, N_PROJ)
        proj = jnp.pad(proj, ((0, 0), (0, LC - sd), (0, 0)))
        y_ssd, ssm_s, conv_s = _ssd_call(proj, sd, state_conv[li], state_ssm[li], *ssd_w)
        o_ret, ret_s = _ret_call(proj, sd, state_ret[li], cos_s, sin_s)
        qn, kvn_s, _ = _prep_call(proj, sd, qk_gain[li])
        kcvc = _cmp_sample_call(cmp_pages, li, page_table, cw_pages)
        win_old = cache_win_kv[li].reshape(bd, NSA_WINDOW, KV_ROW)
        o_nsa = _nsa_sample_call(sel_pages, li, page_table, proj, qn, kvn_s, kcvc, win_old, past)
        o_nsa = o_nsa.reshape(bd, NSA_HEADS, sd, NSA_HEAD_DIM).transpose(0, 2, 1, 3).reshape(1, bd * sd, NSA_WIDTH)
        x1_s = _outproj_call(y_ssd[:, :sd].reshape(1, bd * sd, SSD_INNER), o_nsa,
                             o_ret[:, :sd].reshape(1, bd * sd, RET_WIDTH), w_out_b, xs, g_a)
        h2_s, meta_s, cnt_s = _route_call(x1_s, sc_f, sh_f, norm_ffn[li], wr, br)

        h_all = jnp.concatenate([h2_p.reshape(bp * sp * ROW_CHUNKS, LANE), h2_s.reshape(bd * sd * ROW_CHUNKS, LANE)])
        meta = jnp.concatenate([meta_p.reshape(bp * sp, LANE), meta_s.reshape(bd * sd, LANE)])
        counts = (cnt_p.sum(axis=(0, 1, 2)) + cnt_s.sum(axis=(0, 1, 2)))[MOE_GROUPS:MOE_GROUPS + MOE_EXPERTS]
        plan = _moe_plan(meta[:, :MOE_TOPK].astype(jnp.int32), meta[:, MOE_TOPK:2 * MOE_TOPK],
                         counts.astype(jnp.int32))
        y_all = _moe_call(h_all, plan, li, expert_w_gate, expert_w_up, expert_w_down)
        xp = _combine_call(x1_p, gf_p, y_all, 0)
        xs = _combine_call(x1_s, g_f, y_all, bp * sp)

        win_s = jnp.concatenate([win_old[:, sd:], kvn_s[:, :, 2 * KV_ROW:]], axis=1)
        outs_p.append((_kv_rows(kvn_p[:, :, :KV_ROW]), _kv_rows(kvn_p[:, :, KV_ROW:2 * KV_ROW]),
                       _kv_rows(kvn_p[:, sp - NSA_WINDOW:, 2 * KV_ROW:]), ssm_p, conv_p, ret_p))
        outs_s.append((_kv_rows(kvn_s[:, :, :KV_ROW]), _kv_rows(kvn_s[:, :, KV_ROW:2 * KV_ROW]), _kv_rows(win_s),
                       ssm_s, conv_s, ret_s))
    res = [xp, xs.reshape(bd, sd, d)]
    for k in range(6):
        res.append(jnp.stack([o[k] for o in outs_p]))
        res.append(jnp.stack([o[k] for o in outs_s]))
    return tuple(res)
```

```python
import functools
import math

import numpy as np
import jax
import jax.numpy as jnp
from jax import lax
from jax.experimental import pallas as pl
from jax.experimental.pallas import tpu as pltpu

F32 = jnp.float32
BF16 = jnp.bfloat16
HI = lax.Precision.HIGHEST
EPS = 1e-6
NEG_INF = -1e30

D_MODEL = 2048
DEPTH = 2
PAGE_SIZE = 128
SSD_INNER = D_MODEL // 2
SSD_HEAD_DIM = 64
SSD_HEADS = SSD_INNER // SSD_HEAD_DIM
SSD_GROUPS = 2
SSD_STATE = 128
SSD_CONV = 4
SSD_CONV_DIM = SSD_INNER + 2 * SSD_GROUPS * SSD_STATE
NSA_WIDTH = D_MODEL // 4
NSA_HEAD_DIM = 64
NSA_HEADS = NSA_WIDTH // NSA_HEAD_DIM
NSA_KV_HEADS = 2
NSA_REP = NSA_HEADS // NSA_KV_HEADS
NSA_BLOCK = 64
NSA_TOPN = 16
NSA_WINDOW = 512
NSA_CMP_HIDDEN = 128
RET_WIDTH = D_MODEL // 4
RET_HEADS = 4
RET_VAL_DIM = RET_WIDTH // RET_HEADS
RET_KEY_DIM = RET_VAL_DIM // 2
D_MIX = SSD_INNER + NSA_WIDTH + RET_WIDTH
MOE_GROUPS = 4
MOE_EXPERTS_PER_GROUP = 8
MOE_EXPERTS = MOE_GROUPS * MOE_EXPERTS_PER_GROUP
MOE_TOPK = 2
MOE_HIDDEN = D_MODEL // 4
KV_ROW = 2 * NSA_KV_HEADS * NSA_HEAD_DIM

LANE = 128
SUBLANE = 8
C_Z, C_NQ, C_RV, C_RG, C_RQ, C_RK, C_XBC, C_NKV, C_DT, C_NG = 0, 1024, 1536, 2048, 2560, 2816, 3072, 4608, 5376, 5504
N_PROJ = 5632
LC = 128
TQ = 512
WIN_ROWS = 256
TM_MOE = 256
ROW_CHUNKS = D_MODEL // LANE
VMEM_LIMIT = 56 * 2 ** 20


def _cparams(*sem):
    return pltpu.CompilerParams(dimension_semantics=sem, vmem_limit_bytes=VMEM_LIMIT)


def _silu(x):
    return x * jax.nn.sigmoid(x)


def _nt(a, b, precision=None):
    return lax.dot_general(a, b, (((1,), (1,)), ((), ())), precision=precision, preferred_element_type=F32)


def _tn(a, b, precision=None):
    return lax.dot_general(a, b, (((0,), (0,)), ((), ())), precision=precision, preferred_element_type=F32)


def _iota(shape, dim):
    return lax.broadcasted_iota(jnp.int32, shape, dim)


def _seg_sum(x2, seg):
    n = x2.shape[-1]
    bd = (_iota((n, n), 0) // seg == _iota((n, n), 1) // seg).astype(BF16)
    hi = x2.astype(BF16)
    lo = (x2 - hi.astype(F32)).astype(BF16)
    return jnp.dot(hi, bd, preferred_element_type=F32) + jnp.dot(lo, bd, preferred_element_type=F32)


def _ada_kernel(c_ref, w_ref, b_ref, o_ref):
    o_ref[0] = jnp.dot(_silu(c_ref[...]), w_ref[0], precision=HI, preferred_element_type=F32) + b_ref[0]


def _ada_call(c_all, w_ada, b_ada):
    rows = c_all.shape[0]
    tn = 2048
    return pl.pallas_call(
        _ada_kernel,
        grid=(DEPTH, 6 * D_MODEL // tn),
        in_specs=[pl.BlockSpec((rows, D_MODEL), lambda l, j: (0, 0)),
                  pl.BlockSpec((1, D_MODEL, tn), lambda l, j: (l, 0, j)),
                  pl.BlockSpec((1, 1, tn), lambda l, j: (l, 0, j))],
        out_specs=pl.BlockSpec((1, rows, tn), lambda l, j: (l, 0, j)),
        out_shape=jax.ShapeDtypeStruct((DEPTH, rows, 6 * D_MODEL), F32),
        compiler_params=_cparams("parallel", "parallel"),
        name="ada",
    )(c_all, w_ada, b_ada.reshape(DEPTH, 1, 6 * D_MODEL))


def _mod_norm(x, gain, scale, shift):
    ms = jnp.mean(x * x, axis=-1, keepdims=True)
    return x * lax.rsqrt(ms + EPS) * gain * (1.0 + scale) + shift


def _inproj_kernel(x_ref, sc_ref, sh_ref, g_ref, w_ref, o_ref, h_ref):
    @pl.when(pl.program_id(2) == 0)
    def _():
        h_ref[...] = _mod_norm(x_ref[0], g_ref[...], sc_ref[0], sh_ref[0]).astype(BF16)

    o_ref[0] = jnp.dot(h_ref[...], w_ref[...], preferred_element_type=F32)


def _inproj_call(x, scale, shift, gain, w_bf):
    b, s, d = x.shape
    ts = min(s, 1024)
    tn = 512
    sm = scale.shape[1]
    tsm = 1 if sm == 1 else ts
    mod_map = (lambda bi, i, j: (bi, 0, 0)) if sm == 1 else (lambda bi, i, j: (bi, i, 0))
    return pl.pallas_call(
        _inproj_kernel,
        grid=(b, s // ts, N_PROJ // tn),
        in_specs=[pl.BlockSpec((1, ts, d), lambda bi, i, j: (bi, i, 0)),
                  pl.BlockSpec((1, tsm, d), mod_map),
                  pl.BlockSpec((1, tsm, d), mod_map),
                  pl.BlockSpec((1, d), lambda bi, i, j: (0, 0)),
                  pl.BlockSpec((d, tn), lambda bi, i, j: (0, j))],
        out_specs=pl.BlockSpec((1, ts, tn), lambda bi, i, j: (bi, i, j)),
        out_shape=jax.ShapeDtypeStruct((b, s, N_PROJ), F32),
        scratch_shapes=[pltpu.VMEM((ts, d), BF16)],
        compiler_params=_cparams("parallel", "parallel", "arbitrary"),
        name="in_proj",
    )(x, scale, shift, gain.reshape(1, d), w_bf)


def _ssd_kernel(n_valid, xbc_ref, z_ref, dt_ref, cprev_ref, sprev_ref, cw_ref, cb_ref, dtb_ref, alog_ref, dsk_ref,
                nrm_ref, exp_ref, y_ref, snew_ref, cnew_ref, cbuf, state, ybuf):
    c = pl.program_id(1)
    nc = pl.num_programs(1)
    hd, nst = SSD_HEAD_DIM, SSD_STATE
    hpg = SSD_HEADS // SSD_GROUPS

    @pl.when(c == 0)
    def _():
        cbuf[0:SUBLANE, :] = jnp.zeros((SUBLANE, SSD_CONV_DIM), F32)
        cbuf[5:8, :] = cprev_ref[0]
        state[...] = sprev_ref[0]

    cbuf[8:8 + LC, :] = xbc_ref[0]
    conv = cb_ref[...] + cw_ref[0:1, :] * cbuf[5:5 + LC, :]
    for k in range(1, SSD_CONV):
        conv = conv + cw_ref[k:k + 1, :] * cbuf[5 + k:5 + k + LC, :]
    tail = cbuf[5 + n_valid:8 + n_valid, :]
    cbuf[5:8, :] = tail
    xbc = _silu(conv)
    xs = xbc[:, :SSD_INNER]
    bm = xbc[:, SSD_INNER:SSD_INNER + SSD_GROUPS * nst].astype(BF16)
    cm = xbc[:, SSD_INNER + SSD_GROUPS * nst:].astype(BF16)

    dtr = dt_ref[0] + dtb_ref[...]
    dt = jnp.maximum(dtr, 0.0) + jnp.log1p(jnp.exp(-jnp.abs(dtr)))
    if n_valid < LC:
        dt = jnp.where(_iota((LC, LANE), 0) < n_valid, dt, 0.0)
    a = -jnp.exp(alog_ref[...])
    da = dt * a
    row = _iota((LC, LC), 0)
    col = _iota((LC, LC), 1)
    tril = row >= col
    acs = jnp.dot(tril.astype(F32), da, precision=HI, preferred_element_type=F32)
    eye = (row == col).astype(F32)
    acs_t = _nt(eye, acs, precision=HI)
    expand = exp_ref[...]

    def per_head_to_lanes(v):
        hi = v.astype(BF16)
        lo = (v - hi.astype(F32)).astype(BF16)
        return (jnp.dot(hi, expand, preferred_element_type=F32) + jnp.dot(lo, expand, preferred_element_type=F32))

    e_acs = jnp.exp(acs)
    decay = jnp.exp(acs[LC - 1:LC, :] - acs)
    dt_x = per_head_to_lanes(dt)
    e_acs_x = per_head_to_lanes(e_acs)
    decay_x = per_head_to_lanes(decay)
    xdt = xs * dt_x
    xdec = (xdt * decay_x).astype(BF16)
    xdt_b = xdt.astype(BF16)
    cbs = [_nt(cm[:, g * nst:(g + 1) * nst], bm[:, g * nst:(g + 1) * nst]) for g in range(SSD_GROUPS)]
    for h in range(SSD_HEADS):
        g = h // hpg
        lmat = jnp.exp(jnp.where(tril, acs[:, h:h + 1] - acs_t[h:h + 1, :], -jnp.inf))
        m = (cbs[g] * lmat).astype(BF16)
        y_diag = jnp.dot(m, xdt_b[:, h * hd:(h + 1) * hd], preferred_element_type=F32)
        st = state[h]
        y_off = _nt(cm[:, g * nst:(g + 1) * nst], st.astype(BF16)) * e_acs_x[:, h * hd:(h + 1) * hd]
        state[h] = st * e_acs[LC - 1:LC, h:h + 1] + _tn(xdec[:, h * hd:(h + 1) * hd], bm[:, g * nst:(g + 1) * nst])
        ybuf[:, h * hd:(h + 1) * hd] = y_diag + y_off
    y = (ybuf[...] + dsk_ref[...] * xs) * _silu(z_ref[0])
    gw = SSD_INNER // SSD_GROUPS
    for g in range(SSD_GROUPS):
        seg = y[:, g * gw:(g + 1) * gw]
        ms = jnp.mean(seg * seg, axis=-1, keepdims=True)
        y_ref[0, :, g * gw:(g + 1) * gw] = seg * lax.rsqrt(ms + EPS) * nrm_ref[:, g * gw:(g + 1) * gw]

    @pl.when(c == nc - 1)
    def _():
        snew_ref[0] = state[...]
        cnew_ref[0] = tail


def _ssd_call(proj, n_valid, conv_prev, ssm_prev, conv_w, conv_b, dt_bias, a_log, ssd_d, ssd_norm):
    b, sp, _ = proj.shape
    nc = sp // LC
    pad = lambda v: jnp.pad(v, (0, LANE - SSD_HEADS)).reshape(1, LANE)
    const2 = lambda bi, c: (0, 0)
    return pl.pallas_call(
        functools.partial(_ssd_kernel, n_valid),
        grid=(b, nc),
        in_specs=[pl.BlockSpec((1, LC, SSD_CONV_DIM), lambda bi, c: (bi, c, C_XBC // SSD_CONV_DIM)),
                  pl.BlockSpec((1, LC, SSD_INNER), lambda bi, c: (bi, c, C_Z // SSD_INNER)),
                  pl.BlockSpec((1, LC, LANE), lambda bi, c: (bi, c, C_DT // LANE)),
                  pl.BlockSpec((1, SSD_CONV - 1, SSD_CONV_DIM), lambda bi, c: (bi, 0, 0)),
                  pl.BlockSpec((1, SSD_HEADS, SSD_HEAD_DIM, SSD_STATE), lambda bi, c: (bi, 0, 0, 0)),
                  pl.BlockSpec((SSD_CONV, SSD_CONV_DIM), const2),
                  pl.BlockSpec((1, SSD_CONV_DIM), const2),
                  pl.BlockSpec((1, LANE), const2),
                  pl.BlockSpec((1, LANE), const2),
                  pl.BlockSpec((1, SSD_INNER), const2),
                  pl.BlockSpec((1, SSD_INNER), const2),
                  pl.BlockSpec((LANE, SSD_INNER), const2)],
        out_specs=[pl.BlockSpec((1, LC, SSD_INNER), lambda bi, c: (bi, c, 0)),
                   pl.BlockSpec((1, SSD_HEADS, SSD_HEAD_DIM, SSD_STATE), lambda bi, c: (bi, 0, 0, 0)),
                   pl.BlockSpec((1, SSD_CONV - 1, SSD_CONV_DIM), lambda bi, c: (bi, 0, 0))],
        out_shape=[jax.ShapeDtypeStruct((b, sp, SSD_INNER), F32),
                   jax.ShapeDtypeStruct((b, SSD_HEADS, SSD_HEAD_DIM, SSD_STATE), F32),
                   jax.ShapeDtypeStruct((b, SSD_CONV - 1, SSD_CONV_DIM), F32)],
        scratch_shapes=[pltpu.VMEM((8 + LC, SSD_CONV_DIM), F32),
                        pltpu.VMEM((SSD_HEADS, SSD_HEAD_DIM, SSD_STATE), F32),
                        pltpu.VMEM((LC, SSD_INNER), F32)],
        compiler_params=_cparams("parallel", "arbitrary"),
        name="ssd",
    )(proj, proj, proj, conv_prev, ssm_prev, conv_w, conv_b.reshape(1, -1), pad(dt_bias), pad(a_log),
      jnp.repeat(ssd_d, SSD_HEAD_DIM).reshape(1, -1), ssd_norm.reshape(1, -1),
      (np.arange(LANE)[:, None] == np.arange(SSD_INNER)[None, :] // SSD_HEAD_DIM).astype(BF16))


def _ret_log_g(h):
    return float(np.log1p(-np.exp2(np.float32(-5.0 - h)), dtype=np.float32))


def _ret_kernel(n_valid, q_ref, k_ref, v_ref, g_ref, cos_ref, sin_ref, rprev_ref, o_ref, rnew_ref, state):
    c = pl.program_id(1)
    nc = pl.num_programs(1)
    kd, vd = RET_KEY_DIM, RET_VAL_DIM
    half = kd // 2

    @pl.when(c == 0)
    def _():
        state[...] = rprev_ref[0]

    cos = cos_ref[...]
    sin = sin_ref[...]
    first_half = (_iota((LC, RET_HEADS * kd), 1) % kd) < half

    def rot(x):
        n = x.shape[-1]
        swapped = jnp.where(first_half, pltpu.roll(x, n - half, 1), pltpu.roll(x, half, 1))
        return x * cos + swapped * sin

    q = rot(q_ref[0])
    k = rot(k_ref[0]) * (kd ** -0.5)
    v = v_ref[0]
    gate = g_ref[0]
    ti = _iota((LC, LC), 0)
    tj = _iota((LC, LC), 1)
    diff = (ti - tj).astype(F32)
    ipos = _iota((LC, 1), 0)
    i1 = ipos.astype(F32)
    for h in range(RET_HEADS):
        lg = _ret_log_g(h)
        dmat = jnp.where(diff >= 0, jnp.exp(diff * lg), 0.0)
        q_dec = jnp.exp((i1 + 1.0) * lg)
        k_dec = jnp.where(ipos < n_valid, jnp.exp((n_valid - 1.0 - i1) * lg), 0.0)
        c_dec = math.exp(n_valid * lg)
        qh = q[:, h * kd:(h + 1) * kd].astype(BF16)
        kh = k[:, h * kd:(h + 1) * kd]
        vh = v[:, h * vd:(h + 1) * vd].astype(BF16)
        sc = _nt(qh, kh.astype(BF16)) * dmat
        intra = jnp.dot(sc.astype(BF16), vh, preferred_element_type=F32)
        rs = state[h]
        cross = jnp.dot(qh, rs.astype(BF16), preferred_element_type=F32) * q_dec
        state[h] = rs * c_dec + _tn((kh * k_dec).astype(BF16), vh)
        o = intra + cross
        ms = jnp.mean(o * o, axis=-1, keepdims=True)
        o_ref[0, :, h * vd:(h + 1) * vd] = o * lax.rsqrt(ms + EPS) * _silu(gate[:, h * vd:(h + 1) * vd])

    @pl.when(c == nc - 1)
    def _():
        rnew_ref[0] = state[...]


def _rotary_tables(offset, s_real, s_pad):
    half = RET_KEY_DIM // 2
    freqs = 1.0 / (10000.0 ** jnp.linspace(0.0, 1.0, half, dtype=F32))
    pos = (offset + jnp.arange(s_real)).astype(F32)
    ang = pos[:, None] * freqs[None, :]
    cos, sin = jnp.cos(ang), jnp.sin(ang)
    cos = jnp.tile(jnp.concatenate([cos, cos], axis=-1), (1, RET_HEADS))
    sin = jnp.tile(jnp.concatenate([-sin, sin], axis=-1), (1, RET_HEADS))
    padr = ((0, s_pad - s_real), (0, 0))
    return jnp.pad(cos, padr), jnp.pad(sin, padr)


def _ret_call(proj, n_valid, ret_prev, cos, sin):
    b, sp, _ = proj.shape
    nc = sp // LC
    qw = RET_HEADS * RET_KEY_DIM
    return pl.pallas_call(
        functools.partial(_ret_kernel, n_valid),
        grid=(b, nc),
        in_specs=[pl.BlockSpec((1, LC, qw), lambda bi, c: (bi, c, C_RQ // qw)),
                  pl.BlockSpec((1, LC, qw), lambda bi, c: (bi, c, C_RK // qw)),
                  pl.BlockSpec((1, LC, RET_WIDTH), lambda bi, c: (bi, c, C_RV // RET_WIDTH)),
                  pl.BlockSpec((1, LC, RET_WIDTH), lambda bi, c: (bi, c, C_RG // RET_WIDTH)),
                  pl.BlockSpec((LC, qw), lambda bi, c: (c, 0)),
                  pl.BlockSpec((LC, qw), lambda bi, c: (c, 0)),
                  pl.BlockSpec((1, RET_HEADS, RET_KEY_DIM, RET_VAL_DIM), lambda bi, c: (bi, 0, 0, 0))],
        out_specs=[pl.BlockSpec((1, LC, RET_WIDTH), lambda bi, c: (bi, c, 0)),
                   pl.BlockSpec((1, RET_HEADS, RET_KEY_DIM, RET_VAL_DIM), lambda bi, c: (bi, 0, 0, 0))],
        out_shape=[jax.ShapeDtypeStruct((b, sp, RET_WIDTH), F32),
                   jax.ShapeDtypeStruct((b, RET_HEADS, RET_KEY_DIM, RET_VAL_DIM), F32)],
        scratch_shapes=[pltpu.VMEM((RET_HEADS, RET_KEY_DIM, RET_VAL_DIM), F32)],
        compiler_params=_cparams("parallel", "arbitrary"),
        name="retention",
    )(proj, proj, proj, proj, cos, sin, ret_prev)


def _slope(h):
    return float(2.0 ** -(h + 1))


def _prep_kernel(nq_ref, nkv_ref, gq_ref, gk_ref, isk_ref, qn_ref, kvn_ref, kvb_ref):
    inv = 1.0 / NSA_HEAD_DIM
    nq = nq_ref[0]
    qn_ref[0] = nq * lax.rsqrt(_seg_sum(nq * nq, NSA_HEAD_DIM) * inv + EPS) * gq_ref[...]
    kv = nkv_ref[0]
    normed = kv * lax.rsqrt(_seg_sum(kv * kv, NSA_HEAD_DIM) * inv + EPS) * gk_ref[...]
    kvn = jnp.where(isk_ref[...] > 0.5, normed, kv)
    kvn_ref[0] = kvn
    kvb_ref[0] = kvn[:, KV_ROW:].astype(BF16)


def _prep_call(proj, s_real, qk_gain):
    b = proj.shape[0]
    ts = min(s_real, 512)
    hd = NSA_HEAD_DIM
    gq = (jnp.tile(qk_gain[0], NSA_HEADS) * (hd ** -0.5)).reshape(1, -1)
    ones = jnp.ones((2 * hd,), F32)
    gk = jnp.concatenate([jnp.concatenate([jnp.tile(qk_gain[1 + i], 2), ones]) for i in range(3)]).reshape(1, -1)
    isk = jnp.tile(jnp.concatenate([ones, 0.0 * ones]), 3).reshape(1, -1)
    const2 = lambda bi, i: (0, 0)
    return pl.pallas_call(
        _prep_kernel,
        grid=(b, s_real // ts),
        in_specs=[pl.BlockSpec((1, ts, NSA_WIDTH), lambda bi, i: (bi, i, C_NQ // NSA_WIDTH)),
                  pl.BlockSpec((1, ts, 3 * KV_ROW), lambda bi, i: (bi, i, C_NKV // (3 * KV_ROW))),
                  pl.BlockSpec((1, NSA_WIDTH), const2),
                  pl.BlockSpec((1, 3 * KV_ROW), const2),
                  pl.BlockSpec((1, 3 * KV_ROW), const2)],
        out_specs=[pl.BlockSpec((1, ts, NSA_WIDTH), lambda bi, i: (bi, i, 0)),
                   pl.BlockSpec((1, ts, 3 * KV_ROW), lambda bi, i: (bi, i, 0)),
                   pl.BlockSpec((1, ts, 2 * KV_ROW), lambda bi, i: (bi, i, 0))],
        out_shape=[jax.ShapeDtypeStruct((b, s_real, NSA_WIDTH), F32),
                   jax.ShapeDtypeStruct((b, s_real, 3 * KV_ROW), F32),
                   jax.ShapeDtypeStruct((b, s_real, 2 * KV_ROW), BF16)],
        compiler_params=_cparams("parallel", "parallel"),
        name="nsa_prep",
    )(proj, proj, gq, gk, isk)


def _compress_weights(cmp_pe, cmp_w1, cmp_w2, gain4):
    hd = NSA_HEAD_DIM

    def block_diag4(a0, a1):
        z = jnp.zeros_like(a0)
        rows = [[a0, z, z, z], [z, a0, z, z], [z, z, a1, z], [z, z, z, a1]]
        return jnp.concatenate([jnp.concatenate(r, axis=-1) for r in rows], axis=-2)

    w1 = cmp_w1.astype(BF16)
    w2 = cmp_w2.astype(BF16)
    wbig = block_diag4(w1[0], w1[1])
    w2big = block_diag4(w2[0], w2[1])
    pe_row = jnp.concatenate([cmp_pe[0], cmp_pe[0], cmp_pe[1], cmp_pe[1]], axis=-1)
    g4 = jnp.concatenate([jnp.tile(gain4, 2), jnp.ones((2 * hd,), F32)]).reshape(1, -1)
    return pe_row, wbig, w2big, g4


def _compress_core(get_rows, nblk, pe_ref, wbig_ref, w2_ref, g4_ref):
    acc = jnp.zeros((nblk, 4 * NSA_CMP_HIDDEN), F32)
    for l in range(NSA_BLOCK):
        x = get_rows(l) + pe_ref[l:l + 1, :]
        acc = acc + jnp.dot(x.astype(BF16), wbig_ref[l], preferred_element_type=F32)
    out = jnp.dot(_silu(acc).astype(BF16), w2_ref[...], preferred_element_type=F32)
    ss = _seg_sum(out * out, NSA_HEAD_DIM)
    normed = out * lax.rsqrt(ss * (1.0 / NSA_HEAD_DIM) + EPS) * g4_ref[...]
    return jnp.where(_iota(out.shape, 1) < 2 * NSA_HEAD_DIM, normed, out)


def _cmp_prompt_kernel(nblk, kv_ref, pe_ref, wbig_ref, w2_ref, g4_ref, o_ref):
    o_ref[0] = _compress_core(lambda l: kv_ref[0, :, l, :], nblk, pe_ref, wbig_ref, w2_ref, g4_ref)


def _cmp_prompt_call(kvn, cw):
    b, s, _ = kvn.shape
    nblk = s // NSA_BLOCK
    pe_row, wbig, w2big, g4 = cw
    kv4 = kvn.reshape(b, nblk, NSA_BLOCK, 3 * KV_ROW)
    return pl.pallas_call(
        functools.partial(_cmp_prompt_kernel, nblk),
        grid=(b,),
        in_specs=[pl.BlockSpec((1, nblk, NSA_BLOCK, KV_ROW), lambda bi: (bi, 0, 0, 0)),
                  pl.BlockSpec(pe_row.shape, lambda bi: (0, 0)),
                  pl.BlockSpec(wbig.shape, lambda bi: (0, 0, 0)),
                  pl.BlockSpec(w2big.shape, lambda bi: (0, 0)),
                  pl.BlockSpec(g4.shape, lambda bi: (0, 0))],
        out_specs=pl.BlockSpec((1, nblk, KV_ROW), lambda bi: (bi, 0, 0)),
        out_shape=jax.ShapeDtypeStruct((b, nblk, KV_ROW), F32),
        compiler_params=_cparams("arbitrary"),
        name="nsa_cmp_prompt",
    )(kv4, pe_row, wbig, w2big, g4)


def _masked_softmax(s, mask):
    s = jnp.where(mask, s, NEG_INF)
    p = jnp.exp(s - jnp.max(s, axis=-1, keepdims=True))
    return p / jnp.sum(p, axis=-1, keepdims=True) * mask.astype(F32)


def _select_blocks(imp, blk, n_top, axis=-1):
    sel = jnp.zeros(imp.shape, F32)
    big = imp.shape[axis]
    for _ in range(n_top):
        m = jnp.max(imp, axis=axis, keepdims=True)
        idx = jnp.min(jnp.where(imp == m, blk, big), axis=axis, keepdims=True)
        hit = blk == idx
        sel = jnp.where(hit & (m >= 0.0), 1.0, sel)
        imp = jnp.where(hit, -2.0, imp)
    return sel


M_FLOOR = -1e29


def _nsa_prompt_kernel(nbc, s_len, tk, wk, q_ref, gate_ref, cmp_ref, ksel_ref, kwin_ref, o_ref, bias_ref):
    qi = pl.program_id(1)
    hd, rep, nh = NSA_HEAD_DIM, NSA_REP, NSA_HEADS
    q = q_ref[0]
    qb = q.astype(BF16)
    gates = jax.nn.sigmoid(gate_ref[0])
    t = qi * TQ + _iota((TQ, 1), 0)
    t_row = qi * TQ + _iota((1, TQ), 1)
    kcvc = cmp_ref[0]
    blk = _iota((nbc, TQ), 0)
    dist_c = t_row - ((blk + 1) * NSA_BLOCK - 1)
    mask_c = dist_c >= 0
    mask_cf = mask_c.astype(F32)
    dist_cf = dist_c.astype(F32)
    cur = t_row // NSA_BLOCK
    o_cmp = []
    imps = []
    n_kt = (qi * TQ + TQ + tk - 1) // tk
    for g in range(NSA_KV_HEADS):
        kc = kcvc[:, g * hd:(g + 1) * hd]
        vc = kcvc[:, 2 * hd + g * hd:2 * hd + (g + 1) * hd]
        imp = jnp.zeros((nbc, TQ), F32)
        for r in range(rep):
            h = g * rep + r
            s = jnp.where(mask_c, _nt(kc, q[:, h * hd:(h + 1) * hd], precision=HI) - _slope(h) * dist_cf, NEG_INF)
            p = jnp.exp(s - jnp.max(s, axis=0, keepdims=True))
            p = p / jnp.sum(p, axis=0, keepdims=True) * mask_cf
            imp = imp + p
            o_cmp.append(_tn(p, vc, precision=HI))
        imps.append(jnp.where(blk < cur, imp, -1.0))
    blk_all = jnp.concatenate([blk] * NSA_KV_HEADS, axis=1)
    is_cur = (blk_all == jnp.concatenate([cur] * NSA_KV_HEADS, axis=1)).astype(F32)
    sel_all = jnp.maximum(_select_blocks(jnp.concatenate(imps, axis=1), blk_all, NSA_TOPN - 1, axis=0), is_cur)
    sels = []
    first_kt = []
    bpt = tk // NSA_BLOCK
    for g in range(NSA_KV_HEADS):
        sel = sel_all[:, g * TQ:(g + 1) * TQ]
        sels.append(sel.astype(BF16))
        first = n_kt
        for kt in reversed(range(nbc // bpt)):
            first = jnp.where(jnp.max(sel[kt * bpt:(kt + 1) * bpt, :]) > 0.0, kt, first)
        first_kt.append(first)

    def fill(kt, carry):
        k0 = pl.multiple_of(kt * tk, tk)
        causal = t >= k0 + _iota((TQ, tk), 1)
        expand = (_iota((nbc, tk), 0) == (k0 + _iota((nbc, tk), 1)) // NSA_BLOCK).astype(BF16)
        for g in range(NSA_KV_HEADS):
            picked = _tn(sels[g], expand) > 0.5
            bias_ref[g, :, pl.ds(k0, tk)] = jnp.where(picked & causal, 0.0, NEG_INF).astype(BF16)
        return carry

    lax.fori_loop(jnp.minimum(first_kt[0], first_kt[1]), n_kt, fill, 0)

    rows_g = rep * TQ
    for g in range(NSA_KV_HEADS):
        heads = range(g * rep, (g + 1) * rep)

        def stacked(bias, dist, heads=heads):
            return jnp.concatenate([bias - _slope(h) * dist for h in heads], axis=0)

        qg = jnp.concatenate([qb[:, h * hd:(h + 1) * hd] for h in heads], axis=0)

        def body(kt, carry, g=g, qg=qg, stacked=stacked):
            m, l, acc = carry
            k0 = pl.multiple_of(kt * tk, tk)
            kk = ksel_ref[0, pl.ds(k0, tk), g * hd:(g + 1) * hd]
            vv = ksel_ref[0, pl.ds(k0, tk), 2 * hd + g * hd:2 * hd + (g + 1) * hd]
            dist = (t - (k0 + _iota((TQ, tk), 1))).astype(F32)
            s = _nt(qg, kk) + stacked(bias_ref[g, :, pl.ds(k0, tk)].astype(F32), dist)
            m_new = jnp.maximum(m, jnp.max(s, axis=-1, keepdims=True))
            alpha = jnp.exp(m - m_new)
            p = jnp.exp(s - m_new)
            l = alpha * l + jnp.sum(p, axis=-1, keepdims=True)
            acc = alpha * acc + jnp.dot(p.astype(BF16), vv, preferred_element_type=F32)
            return m_new, l, acc

        init = (jnp.full((rows_g, 1), M_FLOOR, F32), jnp.zeros((rows_g, 1), F32), jnp.zeros((rows_g, hd), F32))
        _, l_s, acc_s = lax.fori_loop(first_kt[g], n_kt, body, init)
        o_sel = acc_s / l_s
        o_win = []
        for w0 in range(0, TQ, WIN_ROWS):
            start_w = pl.multiple_of(jnp.maximum(qi * TQ + w0 + WIN_ROWS - wk, 0), WIN_ROWS)
            kw_all = kwin_ref[0, pl.ds(start_w, wk), :]
            dist_w = t[w0:w0 + WIN_ROWS] - (start_w + _iota((WIN_ROWS, wk), 1))
            bias_w = jnp.where((dist_w >= 0) & (dist_w <= NSA_WINDOW), 0.0, NEG_INF)
            qw = jnp.concatenate([qb[w0:w0 + WIN_ROWS, h * hd:(h + 1) * hd] for h in heads], axis=0)
            s_w = _nt(qw, kw_all[:, g * hd:(g + 1) * hd]) + stacked(bias_w, dist_w.astype(F32))
            p_w = jnp.exp(s_w - jnp.max(s_w, axis=-1, keepdims=True))
            o_win.append(jnp.dot(p_w.astype(BF16), kw_all[:, 2 * hd + g * hd:2 * hd + (g + 1) * hd],
                                 preferred_element_type=F32) / jnp.sum(p_w, axis=-1, keepdims=True))
        for j, h in enumerate(heads):
            win_h = jnp.concatenate([o[j * WIN_ROWS:(j + 1) * WIN_ROWS] for o in o_win], axis=0)
            o_ref[0, :, h * hd:(h + 1) * hd] = (gates[:, h:h + 1] * o_cmp[h]
                                                + gates[:, nh + h:nh + h + 1] * o_sel[j * TQ:(j + 1) * TQ]
                                                + gates[:, 2 * nh + h:2 * nh + h + 1] * win_h)


def _nsa_prompt_call(proj, qn, kvb, kcvc):
    b, s, _ = qn.shape
    nbc = kcvc.shape[1]
    tk = min(512, s)
    wk = min(NSA_WINDOW + WIN_ROWS, s)
    return pl.pallas_call(
        functools.partial(_nsa_prompt_kernel, nbc, s, tk, wk),
        grid=(b, s // TQ),
        in_specs=[pl.BlockSpec((1, TQ, NSA_WIDTH), lambda bi, i: (bi, i, 0)),
                  pl.BlockSpec((1, TQ, LANE), lambda bi, i: (bi, i, C_NG // LANE)),
                  pl.BlockSpec((1, nbc, KV_ROW), lambda bi, i: (bi, 0, 0)),
                  pl.BlockSpec((1, s, KV_ROW), lambda bi, i: (bi, 0, 0)),
                  pl.BlockSpec((1, s, KV_ROW), lambda bi, i: (bi, 0, 1))],
        out_specs=pl.BlockSpec((1, TQ, NSA_WIDTH), lambda bi, i: (bi, i, 0)),
        out_shape=jax.ShapeDtypeStruct((b, s, NSA_WIDTH), F32),
        scratch_shapes=[pltpu.VMEM((NSA_KV_HEADS, TQ, s), BF16)],
        compiler_params=_cparams("parallel", "arbitrary"),
        name="nsa_prompt",
    )(qn, proj, kcvc, kvb, kvb)


def _feature_major_pages(cache):
    return cache.transpose(0, 1, 3, 4, 5, 2).reshape(cache.shape[0], cache.shape[1], KV_ROW, PAGE_SIZE)


def _page_copies(cache_hbm, li, pt_ref, b, step, dst_of, sem, slot, pps):
    return [pltpu.make_async_copy(cache_hbm.at[li, pt_ref[b, step * pps + k]], dst_of(slot, k), sem.at[slot])
            for k in range(pps)]


def _pipelined_pages(cache_hbm, li, pt_ref, dst_of, sem, pps):
    b, s, ns = pl.program_id(0), pl.program_id(1), pl.num_programs(1)
    slot = s % 2
    copies = functools.partial(_page_copies, cache_hbm, li, pt_ref, b, dst_of=dst_of, sem=sem, pps=pps)

    @pl.when(s == 0)
    def _():
        for cp in copies(step=s, slot=slot):
            cp.start()

    @pl.when(s + 1 < ns)
    def _():
        for cp in copies(step=s + 1, slot=1 - slot):
            cp.start()

    for cp in copies(step=s, slot=slot):
        cp.wait()
    return slot


def _cmp_sample_weights(cmp_pe, cmp_w1, cmp_w2, gain4):
    wt = cmp_w1.transpose(0, 2, 1, 3).astype(BF16)
    z = jnp.zeros_like(wt)
    w1 = jnp.concatenate([jnp.concatenate([wt, z], axis=-1), jnp.concatenate([z, wt], axis=-1)], axis=2)
    w1 = w1.reshape(2, NSA_HEAD_DIM // 2, 4 * NSA_BLOCK, 2 * NSA_CMP_HIDDEN)
    w2 = cmp_w2.astype(BF16)
    z2 = jnp.zeros_like(w2)
    w2bd = jnp.concatenate([jnp.concatenate([w2, z2], axis=-1), jnp.concatenate([z2, w2], axis=-1)], axis=1)
    pe_t = jnp.tile(cmp_pe.transpose(0, 2, 1), (1, 1, 2)).reshape(2 * NSA_HEAD_DIM, 2 * NSA_BLOCK)
    return pe_t, w1, w2bd, jnp.tile(gain4, 2).reshape(1, -1)


def _cmp_sample_kernel(li, pps, pt_ref, cache_hbm, pe_ref, w1_ref, w2_ref, g_ref, o_ref, buf, sem):
    slot = _pipelined_pages(cache_hbm, li, pt_ref, lambda sl, k: buf.at[sl, k], sem, pps)
    pages = buf.at[slot]
    hd = NSA_HEAD_DIM
    for c in range(2):
        acc = jnp.zeros((NSA_KV_HEADS * pps, 2 * NSA_CMP_HIDDEN), F32)
        def feature_rows(d):
            x = jnp.concatenate([pages[:, (2 * c + g) * hd + d, :] for g in range(NSA_KV_HEADS)], axis=0)
            return (x + pe_ref[c * hd + d:c * hd + d + 1, :]).astype(BF16)

        for d2 in range(hd // 2):
            x = jnp.concatenate([feature_rows(2 * d2), feature_rows(2 * d2 + 1)], axis=1)
            acc = acc + jnp.dot(x, w1_ref[c, d2], preferred_element_type=F32)
        out = jnp.dot(_silu(acc).astype(BF16), w2_ref[c], preferred_element_type=F32)
        if c == 0:
            out = out * lax.rsqrt(_seg_sum(out * out, hd) * (1.0 / hd) + EPS) * g_ref[...]
        for g in range(NSA_KV_HEADS):
            o_ref[0, :, (2 * c + g) * 2 * hd:(2 * c + g + 1) * 2 * hd] = out[g * pps:(g + 1) * pps]


def _cmp_sample_call(cache_t, li, page_table, cw):
    bd, n_pages = page_table.shape
    pps = math.gcd(n_pages, 64)
    pe_t, w1, w2bd, gain = cw
    hd = NSA_HEAD_DIM
    grid_spec = pltpu.PrefetchScalarGridSpec(
        num_scalar_prefetch=1,
        grid=(bd, n_pages // pps),
        in_specs=[pl.BlockSpec(memory_space=pl.ANY),
                  pl.BlockSpec(pe_t.shape, lambda b, s, pt: (0, 0)),
                  pl.BlockSpec(w1.shape, lambda b, s, pt: (0, 0, 0, 0)),
                  pl.BlockSpec(w2bd.shape, lambda b, s, pt: (0, 0, 0)),
                  pl.BlockSpec(gain.shape, lambda b, s, pt: (0, 0))],
        out_specs=pl.BlockSpec((1, pps, 2 * KV_ROW), lambda b, s, pt: (b, s, 0)),
        scratch_shapes=[pltpu.VMEM((2, pps, KV_ROW, PAGE_SIZE), F32), pltpu.SemaphoreType.DMA((2,))])
    packed = pl.pallas_call(
        functools.partial(_cmp_sample_kernel, li, pps),
        grid_spec=grid_spec,
        out_shape=jax.ShapeDtypeStruct((bd, n_pages, 2 * KV_ROW), F32),
        compiler_params=_cparams("arbitrary", "arbitrary"),
        name="nsa_cmp_sample",
    )(page_table, cache_t, pe_t, w1, w2bd, gain)
    return packed.reshape(bd, n_pages, 4, 2, hd).transpose(0, 1, 3, 2, 4).reshape(bd, 2 * n_pages, KV_ROW)


def _pick_group(x, rows_g0):
    w = x.shape[-1] // 2
    return jnp.where(rows_g0, x[:, :w], x[:, w:])


def _nsa_sample_kernel(li, pps, past, n_tok, pt_ref, cache_hbm, q_ref, gate_ref, cmp_ref, knew_ref, wold_ref, wnew_ref,
                       o_ref, buf, sem, qbd_ref, sel_ref, ocw_ref, m_ref, l_ref, acc_ref):
    s, ns = pl.program_id(1), pl.num_programs(1)
    hd, nh, rep = NSA_HEAD_DIM, NSA_HEADS, NSA_REP
    rows = nh * n_tok
    nbc = past // NSA_BLOCK
    tk = pps * PAGE_SIZE
    slot = _pipelined_pages(cache_hbm, li, pt_ref,
                            lambda sl, k: buf.at[sl, :, pl.ds(k * PAGE_SIZE, PAGE_SIZE)], sem, pps)
    ridx = _iota((rows, 1), 0)
    rows_g0 = ridx < rep * n_tok
    tok = ridx % n_tok
    t = past + tok
    slope = jnp.exp2(-(ridx // n_tok + 1).astype(F32))

    @pl.when(s == 0)
    def _():
        qtok = q_ref[0]
        qrows = jnp.concatenate([qtok[:, h * hd:(h + 1) * hd] for h in range(nh)], axis=0)
        zero = jnp.zeros_like(qrows)
        qbd = jnp.where(rows_g0, jnp.concatenate([qrows, zero], axis=1), jnp.concatenate([zero, qrows], axis=1))
        qbd_ref[...] = qbd
        qbb = qbd.astype(BF16)
        kcvc = cmp_ref[0]
        blk = _iota((rows, nbc), 1)
        dist_c = t - ((blk + 1) * NSA_BLOCK - 1)
        p_c = _masked_softmax(_nt(qbd, kcvc[:, :2 * hd], precision=HI) - slope * dist_c.astype(F32), dist_c >= 0)
        o_c = _pick_group(jnp.dot(p_c, kcvc[:, 2 * hd:], precision=HI, preferred_element_type=F32), rows_g0)
        blk_t = _iota((n_tok, nbc), 1)
        cur_t = (past + _iota((n_tok, 1), 0)) // NSA_BLOCK
        sels = []
        for g in range(NSA_KV_HEADS):
            imp = p_c[g * rep * n_tok:(g * rep + 1) * n_tok]
            for r in range(1, rep):
                imp = imp + p_c[(g * rep + r) * n_tok:(g * rep + r + 1) * n_tok]
            imp = jnp.where(blk_t < cur_t, imp, -1.0)
            sels += [_select_blocks(imp, blk_t, NSA_TOPN - 1)] * rep
        sel_ref[...] = jnp.concatenate(sels, axis=0).astype(BF16)
        knew = knew_ref[0]
        dist_n = tok - _iota((rows, n_tok), 1)
        mask_n = dist_n >= 0
        s_n = jnp.where(mask_n, _nt(qbb, knew[:, :2 * hd].astype(BF16)) - slope * dist_n.astype(F32), NEG_INF)
        m0 = jnp.max(s_n, axis=-1, keepdims=True)
        p_n = jnp.where(mask_n, jnp.exp(s_n - m0), 0.0)
        m_ref[...] = m0
        l_ref[...] = jnp.sum(p_n, axis=-1, keepdims=True)
        acc_ref[...] = jnp.dot(p_n.astype(BF16), knew[:, 2 * hd:].astype(BF16), preferred_element_type=F32)
        kw = jnp.concatenate([wold_ref[0], wnew_ref[0]], axis=0)
        dist_w = tok + NSA_WINDOW - _iota((rows, NSA_WINDOW + n_tok), 1)
        mask_w = (dist_w >= 0) & (dist_w <= NSA_WINDOW)
        p_w = _masked_softmax(_nt(qbb, kw[:, :2 * hd].astype(BF16)) - slope * dist_w.astype(F32), mask_w)
        o_w = _pick_group(jnp.dot(p_w.astype(BF16), kw[:, 2 * hd:].astype(BF16), preferred_element_type=F32), rows_g0)
        ocw_ref[0] = o_c
        ocw_ref[1] = o_w

    pages = buf.at[slot]
    k_t = pages[:2 * hd, :].astype(BF16)
    v_t = pages[2 * hd:, :].astype(BF16)
    k0 = s * tk
    expand = (_iota((nbc, tk), 0) == (k0 + _iota((nbc, tk), 1)) // NSA_BLOCK).astype(BF16)
    mask = jnp.dot(sel_ref[...], expand, preferred_element_type=F32) > 0.5
    dist = t - (k0 + _iota((rows, tk), 1))
    qk = jnp.dot(qbd_ref[...].astype(BF16), k_t, preferred_element_type=F32)
    sc = jnp.where(mask, qk - slope * dist.astype(F32), NEG_INF)
    m_old = m_ref[...]
    m_new = jnp.maximum(m_old, jnp.max(sc, axis=-1, keepdims=True))
    alpha = jnp.exp(m_old - m_new)
    p = jnp.where(mask, jnp.exp(sc - m_new), 0.0)
    m_ref[...] = m_new
    l_ref[...] = alpha * l_ref[...] + jnp.sum(p, axis=-1, keepdims=True)
    acc_ref[...] = alpha * acc_ref[...] + _nt(p.astype(BF16), v_t)

    @pl.when(s == ns - 1)
    def _():
        o_s = _pick_group(acc_ref[...] / l_ref[...], rows_g0)
        gates = jax.nn.sigmoid(gate_ref[0])

        def gate_rows(br):
            return jnp.concatenate([gates[:, br * nh + h:br * nh + h + 1] for h in range(nh)], axis=0)

        o_ref[0] = gate_rows(0) * ocw_ref[0] + gate_rows(1) * o_s + gate_rows(2) * ocw_ref[1]


def _nsa_sample_call(cache_t, li, page_table, proj, qn, kvn, kcvc, win_old, past):
    bd, n_pages = page_table.shape
    n_tok = qn.shape[1]
    pps = math.gcd(n_pages, 32)
    rows = NSA_HEADS * n_tok
    nbc = kcvc.shape[1]
    grid_spec = pltpu.PrefetchScalarGridSpec(
        num_scalar_prefetch=1,
        grid=(bd, n_pages // pps),
        in_specs=[pl.BlockSpec(memory_space=pl.ANY),
                  pl.BlockSpec((1, n_tok, NSA_WIDTH), lambda b, s, pt: (b, 0, 0)),
                  pl.BlockSpec((1, n_tok, LANE), lambda b, s, pt: (b, 0, C_NG // LANE)),
                  pl.BlockSpec((1, nbc, KV_ROW), lambda b, s, pt: (b, 0, 0)),
                  pl.BlockSpec((1, n_tok, KV_ROW), lambda b, s, pt: (b, 0, 1)),
                  pl.BlockSpec((1, NSA_WINDOW, KV_ROW), lambda b, s, pt: (b, 0, 0)),
                  pl.BlockSpec((1, n_tok, KV_ROW), lambda b, s, pt: (b, 0, 2))],
        out_specs=pl.BlockSpec((1, rows, NSA_HEAD_DIM), lambda b, s, pt: (b, 0, 0)),
        scratch_shapes=[pltpu.VMEM((2, KV_ROW, pps * PAGE_SIZE), F32),
                        pltpu.SemaphoreType.DMA((2,)),
                        pltpu.VMEM((rows, 2 * NSA_HEAD_DIM), F32),
                        pltpu.VMEM((rows, nbc), BF16),
                        pltpu.VMEM((2, rows, NSA_HEAD_DIM), F32),
                        pltpu.VMEM((rows, 1), F32),
                        pltpu.VMEM((rows, 1), F32),
                        pltpu.VMEM((rows, 2 * NSA_HEAD_DIM), F32)])
    return pl.pallas_call(
        functools.partial(_nsa_sample_kernel, li, pps, past, n_tok),
        grid_spec=grid_spec,
        out_shape=jax.ShapeDtypeStruct((bd, rows, NSA_HEAD_DIM), F32),
        compiler_params=_cparams("arbitrary", "arbitrary"),
        name="nsa_sample",
    )(page_table, cache_t, qn, proj, kcvc, kvn, win_old, kvn)


def _outproj_kernel(y_ref, n_ref, r_ref, w_ref, x_ref, g_ref, o_ref, mix_ref):
    @pl.when(pl.program_id(2) == 0)
    def _():
        mix_ref[:, :SSD_INNER] = y_ref[0].astype(BF16)
        mix_ref[:, SSD_INNER:SSD_INNER + NSA_WIDTH] = n_ref[0].astype(BF16)
        mix_ref[:, SSD_INNER + NSA_WIDTH:] = r_ref[0].astype(BF16)

    o_ref[0] = x_ref[0] + g_ref[0] * jnp.dot(mix_ref[...], w_ref[...], preferred_element_type=F32)


def _outproj_call(y, o_nsa, o_ret, w_bf, x, gate):
    b, s, d = x.shape
    ts = min(s, 1024)
    tn = 512
    sm = gate.shape[1]
    tsm = 1 if sm == 1 else ts
    mod_map = (lambda bi, i, j: (bi, 0, j)) if sm == 1 else (lambda bi, i, j: (bi, i, j))
    row_map = lambda bi, i, j: (bi, i, 0)
    return pl.pallas_call(
        _outproj_kernel,
        grid=(b, s // ts, d // tn),
        in_specs=[pl.BlockSpec((1, ts, SSD_INNER), row_map),
                  pl.BlockSpec((1, ts, NSA_WIDTH), row_map),
                  pl.BlockSpec((1, ts, RET_WIDTH), row_map),
                  pl.BlockSpec((D_MIX, tn), lambda bi, i, j: (0, j)),
                  pl.BlockSpec((1, ts, tn), lambda bi, i, j: (bi, i, j)),
                  pl.BlockSpec((1, tsm, tn), mod_map)],
        out_specs=pl.BlockSpec((1, ts, tn), lambda bi, i, j: (bi, i, j)),
        out_shape=jax.ShapeDtypeStruct((b, s, d), F32),
        scratch_shapes=[pltpu.VMEM((ts, D_MIX), BF16)],
        compiler_params=_cparams("parallel", "parallel", "arbitrary"),
        name="out_proj",
    )(y, o_nsa, o_ret, w_bf, x, gate)


def _route_kernel(x_ref, sc_ref, sh_ref, g_ref, wr_ref, br_ref, h_ref, meta_ref, cnt_ref):
    h = _mod_norm(x_ref[0], g_ref[...], sc_ref[0], sh_ref[0])
    n_rows = h.shape[0]
    for c in range(ROW_CHUNKS):
        h_ref[0, pl.ds(c, n_rows, stride=ROW_CHUNKS), :] = h[:, c * LANE:(c + 1) * LANE]
    logit = jnp.dot(h, wr_ref[...], precision=HI, preferred_element_type=F32) + br_ref[...]
    lane = _iota(logit.shape, 1)
    first = lambda hit: jnp.min(jnp.where(hit, lane, LANE), axis=-1, keepdims=True)
    is_g = lane < MOE_GROUPS
    gl = jnp.where(is_g, logit, -jnp.inf)
    gmax = jnp.max(gl, axis=-1, keepdims=True)
    g_idx = first(gl == gmax)
    g_p = 1.0 / jnp.sum(jnp.where(is_g, jnp.exp(logit - gmax), 0.0), axis=-1, keepdims=True)
    e_lane = lane - MOE_GROUPS
    in_grp = (e_lane >= 0) & (e_lane < MOE_EXPERTS) & (e_lane // MOE_EXPERTS_PER_GROUP == g_idx)
    el = jnp.where(in_grp, logit, -jnp.inf)
    v1 = jnp.max(el, axis=-1, keepdims=True)
    i1 = first(el == v1)
    el = jnp.where(lane == i1, -jnp.inf, el)
    v2 = jnp.max(el, axis=-1, keepdims=True)
    i2 = first(el == v2)
    e = jnp.exp(v2 - v1)
    w1 = g_p / (1.0 + e)
    w2 = g_p * e / (1.0 + e)
    meta = jnp.where(lane == 0, (i1 - MOE_GROUPS).astype(F32), 0.0)
    meta = jnp.where(lane == 1, (i2 - MOE_GROUPS).astype(F32), meta)
    meta = jnp.where(lane == 2, w1, meta)
    meta_ref[0] = jnp.where(lane == 3, w2, meta)
    cnt_ref[0, 0] = jnp.sum(((lane == i1) | (lane == i2)).astype(F32), axis=0, keepdims=True)


def _route_call(x1, scale, shift, gain, wr, br):
    b, s, d = x1.shape
    ts = min(s, 512)
    sm = scale.shape[1]
    tsm = 1 if sm == 1 else ts
    mod_map = (lambda bi, i: (bi, 0, 0)) if sm == 1 else (lambda bi, i: (bi, i, 0))
    return pl.pallas_call(
        _route_kernel,
        grid=(b, s // ts),
        in_specs=[pl.BlockSpec((1, ts, d), lambda bi, i: (bi, i, 0)),
                  pl.BlockSpec((1, tsm, d), mod_map),
                  pl.BlockSpec((1, tsm, d), mod_map),
                  pl.BlockSpec((1, d), lambda bi, i: (0, 0)),
                  pl.BlockSpec((d, LANE), lambda bi, i: (0, 0)),
                  pl.BlockSpec((1, LANE), lambda bi, i: (0, 0))],
        out_specs=[pl.BlockSpec((1, ts * ROW_CHUNKS, LANE), lambda bi, i: (bi, i, 0)),
                   pl.BlockSpec((1, ts, LANE), lambda bi, i: (bi, i, 0)),
                   pl.BlockSpec((1, 1, 1, LANE), lambda bi, i: (bi, i, 0, 0))],
        out_shape=[jax.ShapeDtypeStruct((b, s * ROW_CHUNKS, LANE), F32), jax.ShapeDtypeStruct((b, s, LANE), F32),
                   jax.ShapeDtypeStruct((b, s // ts, 1, LANE), F32)],
        compiler_params=_cparams("parallel", "parallel"),
        name="moe_route",
    )(x1, scale, shift, gain.reshape(1, d), wr, br)


def _moe_plan(e_ids, w, counts):
    n_pairs = e_ids.size
    tm = TM_MOE
    n_tiles = -(-n_pairs // tm) + MOE_EXPERTS
    n_slots = n_tiles * tm
    n_tok = n_pairs // MOE_TOPK
    pair_ids = jnp.arange(n_pairs, dtype=jnp.int32)
    _, order, w_sorted = lax.sort((e_ids.reshape(-1), pair_ids, w.reshape(-1)), num_keys=1, is_stable=True)
    tiles_per = (counts + tm - 1) // tm
    tile_end = jnp.cumsum(tiles_per)
    tile_start = tile_end - tiles_per
    grp_start = jnp.cumsum(counts) - counts
    n_used = tile_end[-1]
    tiles = jnp.arange(n_tiles, dtype=jnp.int32)
    tile_ids = jnp.minimum(tiles, n_used - 1)
    tile_e = jnp.sum((tile_ids[:, None] >= tile_end[None, :]).astype(jnp.int32), axis=1)
    active = tiles < n_used
    rank0 = (tile_ids - tile_start[tile_e]) * tm
    flag = active.astype(jnp.int32) + (active & (rank0 == 0)).astype(jnp.int32)
    r = jnp.arange(tm, dtype=jnp.int32)[None, :]
    rank = rank0[:, None] + r
    valid = (active[:, None] & (rank < counts[tile_e][:, None])).reshape(-1)
    idx = jnp.clip(grp_start[tile_e][:, None] + rank, 0, n_pairs - 1).reshape(-1)
    pair = order[idx]
    tok = pair // MOE_TOPK
    src = jnp.where(valid, tok, 0) * ROW_CHUNKS
    pad_row = jnp.broadcast_to(n_tok + r, (n_tiles, tm)).reshape(-1)
    dst_row = jnp.where(valid, tok, pad_row) * ROW_CHUNKS
    dst_plane = jnp.where(valid, pair % MOE_TOPK, 0)
    w_slot = jnp.where(valid, w_sorted[idx], 0.0)
    return tile_e, flag, src, dst_row, dst_plane, w_slot.reshape(n_slots, 1)


def _moe_kernel(te_ref, flag_ref, src_ref, row_ref, plane_ref, h_hbm, w_ref, wg_ref, wu_ref, wd_ref, y_hbm, xbuf,
                obuf, xs, wgb, wub, wdb, sem_in, sem_out):
    i = pl.program_id(0)
    n_tiles = pl.num_programs(0)
    flag = flag_ref[i]
    nxt = jnp.minimum(i + 1, n_tiles - 1)
    next_active = (i + 1 < n_tiles) & (flag_ref[nxt] > 0)
    slot = i % 2

    rc = ROW_CHUNKS

    def slab(ref_2d, first_row):
        return ref_2d.at[pl.ds(pl.multiple_of(first_row, rc), rc)]

    def start_gather(tile, sl):
        for r in range(TM_MOE):
            pltpu.make_async_copy(slab(h_hbm, src_ref[tile * TM_MOE + r]), xbuf.at[sl, pl.ds(r * rc, rc)],
                                  sem_in.at[sl]).start()

    def wait_gather(sl):
        pltpu.make_async_copy(h_hbm.at[pl.ds(0, TM_MOE * rc)], xbuf.at[sl], sem_in.at[sl]).wait()

    def wait_scatter(sl):
        pltpu.make_async_copy(obuf.at[sl], y_hbm.at[0, pl.ds(0, TM_MOE * rc)], sem_out.at[sl]).wait()

    @pl.when(i == 0)
    def _():
        obuf[0] = jnp.zeros(obuf.shape[1:], F32)
        pad_rows = pl.ds(y_hbm.shape[1] - TM_MOE * rc, TM_MOE * rc)
        for plane in range(MOE_TOPK):
            pltpu.make_async_copy(obuf.at[0], y_hbm.at[plane, pad_rows], sem_out.at[0]).start()
        for plane in range(MOE_TOPK):
            pltpu.make_async_copy(obuf.at[0], y_hbm.at[plane, pad_rows], sem_out.at[0]).wait()

    @pl.when(flag > 0)
    def _():
        @pl.when(i == 0)
        def _():
            start_gather(i, slot)

        @pl.when(next_active)
        def _():
            start_gather(i + 1, 1 - slot)

        @pl.when(flag > 1)
        def _():
            wgb[...] = wg_ref[0, 0].astype(BF16)
            wub[...] = wu_ref[0, 0].astype(BF16)
            wdb[...] = wd_ref[0, 0].astype(BF16)

        wait_gather(slot)
        rows = xbuf.at[slot]
        for c in range(rc):
            xs[:, c * LANE:(c + 1) * LANE] = rows[pl.ds(c, TM_MOE, stride=rc), :].astype(BF16)
        x = xs[...]
        a = jnp.dot(x, wgb[...], preferred_element_type=F32)
        u = jnp.dot(x, wub[...], preferred_element_type=F32)
        hid = (_silu(a) * u * w_ref[...]).astype(BF16)
        out = jnp.dot(hid, wdb[...], preferred_element_type=F32)

        @pl.when(i >= 2)
        def _():
            wait_scatter(slot)

        out_rows = obuf.at[slot]
        for c in range(rc):
            out_rows[pl.ds(c, TM_MOE, stride=rc), :] = out[:, c * LANE:(c + 1) * LANE]

        for r in range(TM_MOE):
            k = i * TM_MOE + r
            pltpu.make_async_copy(out_rows.at[pl.ds(r * rc, rc)], slab(y_hbm.at[plane_ref[k]], row_ref[k]),
                                  sem_out.at[slot]).start()

        @pl.when(jnp.logical_not(next_active))
        def _():
            wait_scatter(slot)

            @pl.when(i >= 1)
            def _():
                wait_scatter(1 - slot)


def _moe_call(h_all, plan, li, w_gate, w_up, w_down):
    t = h_all.shape[0] // ROW_CHUNKS
    d = D_MODEL
    tile_e, flag, src, dst_row, dst_plane, w_slot = plan
    n_tiles = tile_e.shape[0]
    tm = TM_MOE
    w_map = lambda i, te, *_: (li, te[i], 0, 0)
    grid_spec = pltpu.PrefetchScalarGridSpec(
        num_scalar_prefetch=5,
        grid=(n_tiles,),
        in_specs=[pl.BlockSpec(memory_space=pl.ANY),
                  pl.BlockSpec((tm, 1), lambda i, *_: (i, 0)),
                  pl.BlockSpec((1, 1, d, MOE_HIDDEN), w_map),
                  pl.BlockSpec((1, 1, d, MOE_HIDDEN), w_map),
                  pl.BlockSpec((1, 1, MOE_HIDDEN, d), w_map)],
        out_specs=pl.BlockSpec(memory_space=pl.ANY),
        scratch_shapes=[pltpu.VMEM((2, tm * ROW_CHUNKS, LANE), F32), pltpu.VMEM((2, tm * ROW_CHUNKS, LANE), F32),
                        pltpu.VMEM((tm, d), BF16),
                        pltpu.VMEM((d, MOE_HIDDEN), BF16), pltpu.VMEM((d, MOE_HIDDEN), BF16),
                        pltpu.VMEM((MOE_HIDDEN, d), BF16),
                        pltpu.SemaphoreType.DMA((2,)), pltpu.SemaphoreType.DMA((2,))])
    return pl.pallas_call(
        _moe_kernel,
        grid_spec=grid_spec,
        out_shape=jax.ShapeDtypeStruct((MOE_TOPK, (t + tm) * ROW_CHUNKS, LANE), F32),
        compiler_params=_cparams("arbitrary"),
        name="moe_experts",
    )(tile_e, flag, src, dst_row, dst_plane, h_all, w_slot, w_gate, w_up, w_down)


def _combine_kernel(x_ref, g_ref, y_ref, o_ref):
    n_rows = x_ref.shape[1]
    for c in range(ROW_CHUNKS):
        cols = slice(c * LANE, (c + 1) * LANE)
        chunk = pl.ds(c, n_rows, stride=ROW_CHUNKS)
        o_ref[0, :, cols] = x_ref[0, :, cols] + g_ref[0, :, cols] * (y_ref[0, chunk, :] + y_ref[1, chunk, :])


def _combine_call(x1, gate, y_all, row0):
    b, s, d = x1.shape
    ts = min(s, 512)
    sm = gate.shape[1]
    tsm = 1 if sm == 1 else ts
    mod_map = (lambda bi, i: (bi, 0, 0)) if sm == 1 else (lambda bi, i: (bi, i, 0))
    blk0 = row0 // ts
    per_b = s // ts
    return pl.pallas_call(
        _combine_kernel,
        grid=(b, per_b),
        in_specs=[pl.BlockSpec((1, ts, d), lambda bi, i: (bi, i, 0)),
                  pl.BlockSpec((1, tsm, d), mod_map),
                  pl.BlockSpec((MOE_TOPK, ts * ROW_CHUNKS, LANE), lambda bi, i: (0, blk0 + bi * per_b + i, 0))],
        out_specs=pl.BlockSpec((1, ts, d), lambda bi, i: (bi, i, 0)),
        out_shape=jax.ShapeDtypeStruct((b, s, d), F32),
        compiler_params=_cparams("parallel", "parallel"),
        name="moe_combine",
    )(x1, gate, y_all)


def _reorder_w_in(w):
    sizes = (SSD_INNER, SSD_CONV_DIM, SSD_HEADS, NSA_WIDTH, 3 * KV_ROW, 3 * NSA_HEADS, RET_HEADS * RET_KEY_DIM,
             RET_HEADS * RET_KEY_DIM, RET_WIDTH, RET_WIDTH)
    z, xbc, dt, nq, nkv, ngate, rq, rk, rv, rg = jnp.split(w, [int(v) for v in np.cumsum(sizes)[:-1]], axis=1)
    padc = lambda a: jnp.pad(a, ((0, 0), (0, LANE - a.shape[1])))
    return jnp.concatenate([z, nq, rv, rg, rq, rk, xbc, nkv, padc(dt), padc(ngate)], axis=1).astype(BF16)


def _kv_rows(a):
    return a.reshape(a.shape[:2] + (2, NSA_KV_HEADS, NSA_HEAD_DIM))


def kernel(x_prompt, x_sample, cache_cmp_kv, cache_sel_kv, cache_win_kv, state_ssm, state_conv, state_ret, page_table, c_prompt, c_sample, w_ada, b_ada, norm_mix, norm_ffn, w_in, w_out, conv_w, conv_b, dt_bias, a_log, ssd_d, ssd_norm, qk_gain, cmp_pe, cmp_w1, cmp_w2, router_group_w, router_group_b, router_expert_w, router_expert_b, expert_w_gate, expert_w_up, expert_w_down):
    bp, sp, d = x_prompt.shape
    bd, sd, _ = x_sample.shape
    n_pages = page_table.shape[1]
    past = n_pages * PAGE_SIZE
    assert d == D_MODEL and sp % LC == 0 and sp % NSA_BLOCK == 0 and sp >= NSA_WINDOW
    assert sd % SUBLANE == 0 and sd <= NSA_BLOCK and sd <= LC and cache_win_kv.shape[2] == NSA_WINDOW
    n_seq = bp + bd
    c_all = jnp.pad(jnp.concatenate([c_prompt, c_sample]), ((0, -n_seq % SUBLANE), (0, 0)))
    mod = _ada_call(c_all, w_ada, b_ada)
    cos_p, sin_p = _rotary_tables(0, sp, sp)
    cos_s, sin_s = _rotary_tables(past, sd, LC)
    cmp_pages = _feature_major_pages(cache_cmp_kv)
    sel_pages = _feature_major_pages(cache_sel_kv)
    xp = x_prompt
    xs = x_sample.reshape(1, bd * sd, d)
    outs_p, outs_s = [], []
    for li in range(DEPTH):
        mods_p = [mod[li, :bp, k * d:(k + 1) * d].reshape(bp, 1, d) for k in range(6)]
        mods_s = [jnp.repeat(mod[li, bp:n_seq, k * d:(k + 1) * d], sd, axis=0).reshape(1, bd * sd, d) for k in range(6)]
        w_in_b = _reorder_w_in(w_in[li])
        w_out_b = w_out[li].astype(BF16)
        cw = _compress_weights(cmp_pe[li], cmp_w1[li], cmp_w2[li], qk_gain[li, 4])
        cw_pages = _cmp_sample_weights(cmp_pe[li], cmp_w1[li], cmp_w2[li], qk_gain[li, 4])
        ssd_w = (conv_w[li], conv_b[li], dt_bias[li], a_log[li], ssd_d[li], ssd_norm[li])
        wr = jnp.pad(jnp.concatenate([router_group_w[li], router_expert_w[li]], axis=1),
                     ((0, 0), (0, LANE - MOE_GROUPS - MOE_EXPERTS)))
        br = jnp.pad(jnp.concatenate([router_group_b[li], router_expert_b[li]]),
                     (0, LANE - MOE_GROUPS - MOE_EXPERTS)).reshape(1, LANE)

        sh_a, sc_a, g_a, sh_f, sc_f, g_f = mods_p
        proj = _inproj_call(xp, sc_a, sh_a, norm_mix[li], w_in_b)
        y_ssd, ssm_p, conv_p = _ssd_call(proj, LC, jnp.zeros((bp, SSD_CONV - 1, SSD_CONV_DIM), F32),
                                         jnp.zeros((bp, SSD_HEADS, SSD_HEAD_DIM, SSD_STATE), F32), *ssd_w)
        o_ret, ret_p = _ret_call(proj, LC, jnp.zeros((bp, RET_HEADS, RET_KEY_DIM, RET_VAL_DIM), F32), cos_p, sin_p)
        qn, kvn_p, kvb = _prep_call(proj, sp, qk_gain[li])
        kcvc = _cmp_prompt_call(kvn_p, cw)
        o_nsa = _nsa_prompt_call(proj, qn, kvb, kcvc)
        x1_p = _outproj_call(y_ssd, o_nsa, o_ret, w_out_b, xp, g_a)
        h2_p, meta_p, cnt_p = _route_call(x1_p, sc_f, sh_f, norm_ffn[li], wr, br)
        gf_p = g_f

        sh_a, sc_a, g_a, sh_f, sc_f, g_f = mods_s
        proj = _inproj_call(xs, sc_a, sh_a, norm_mix[li], w_in_b).reshape(bd, sd, N_PROJ)
        proj = jnp.pad(proj, ((0, 0), (0, LC - sd), (0, 0)))
        y_ssd, ssm_s, conv_s = _ssd_call(proj, sd, state_conv[li], state_ssm[li], *ssd_w)
        o_ret, ret_s = _ret_call(proj, sd, state_ret[li], cos_s, sin_s)
        qn, kvn_s, _ = _prep_call(proj, sd, qk_gain[li])
        kcvc = _cmp_sample_call(cmp_pages, li, page_table, cw_pages)
        win_old = cache_win_kv[li].reshape(bd, NSA_WINDOW, KV_ROW)
        o_nsa = _nsa_sample_call(sel_pages, li, page_table, proj, qn, kvn_s, kcvc, win_old, past)
        o_nsa = o_nsa.reshape(bd, NSA_HEADS, sd, NSA_HEAD_DIM).transpose(0, 2, 1, 3).reshape(1, bd * sd, NSA_WIDTH)
        x1_s = _outproj_call(y_ssd[:, :sd].reshape(1, bd * sd, SSD_INNER), o_nsa,
                             o_ret[:, :sd].reshape(1, bd * sd, RET_WIDTH), w_out_b, xs, g_a)
        h2_s, meta_s, cnt_s = _route_call(x1_s, sc_f, sh_f, norm_ffn[li], wr, br)

        h_all = jnp.concatenate([h2_p.reshape(bp * sp * ROW_CHUNKS, LANE), h2_s.reshape(bd * sd * ROW_CHUNKS, LANE)])
        meta = jnp.concatenate([meta_p.reshape(bp * sp, LANE), meta_s.reshape(bd * sd, LANE)])
        counts = (cnt_p.sum(axis=(0, 1, 2)) + cnt_s.sum(axis=(0, 1, 2)))[MOE_GROUPS:MOE_GROUPS + MOE_EXPERTS]
        plan = _moe_plan(meta[:, :MOE_TOPK].astype(jnp.int32), meta[:, MOE_TOPK:2 * MOE_TOPK],
                         counts.astype(jnp.int32))
        y_all = _moe_call(h_all, plan, li, expert_w_gate, expert_w_up, expert_w_down)
        xp = _combine_call(x1_p, gf_p, y_all, 0)
        xs = _combine_call(x1_s, g_f, y_all, bp * sp)

        win_s = jnp.concatenate([win_old[:, sd:], kvn_s[:, :, 2 * KV_ROW:]], axis=1)
        outs_p.append((_kv_rows(kvn_p[:, :, :KV_ROW]), _kv_rows(kvn_p[:, :, KV_ROW:2 * KV_ROW]),
                       _kv_rows(kvn_p[:, sp - NSA_WINDOW:, 2 * KV_ROW:]), ssm_p, conv_p, ret_p))
        outs_s.append((_kv_rows(kvn_s[:, :, :KV_ROW]), _kv_rows(kvn_s[:, :, KV_ROW:2 * KV_ROW]), _kv_rows(win_s),
                       ssm_s, conv_s, ret_s))
    res = [xp, xs.reshape(bd, sd, d)]
    for k in range(6):
        res.append(jnp.stack([o[k] for o in outs_p]))
        res.append(jnp.stack([o[k] for o in outs_s]))
    return tuple(res)
```

```python
import functools
import math

import numpy as np
import jax
import jax.numpy as jnp
from jax import lax
from jax.experimental import pallas as pl
from jax.experimental.pallas import tpu as pltpu

F32 = jnp.float32
BF16 = jnp.bfloat16
HI = lax.Precision.HIGHEST
EPS = 1e-6
NEG_INF = -1e30

D_MODEL = 2048
DEPTH = 2
PAGE_SIZE = 128
SSD_INNER = D_MODEL // 2
SSD_HEAD_DIM = 64
SSD_HEADS = SSD_INNER // SSD_HEAD_DIM
SSD_GROUPS = 2
SSD_STATE = 128
SSD_CONV = 4
SSD_CONV_DIM = SSD_INNER + 2 * SSD_GROUPS * SSD_STATE
NSA_WIDTH = D_MODEL // 4
NSA_HEAD_DIM = 64
NSA_HEADS = NSA_WIDTH // NSA_HEAD_DIM
NSA_KV_HEADS = 2
NSA_REP = NSA_HEADS // NSA_KV_HEADS
NSA_BLOCK = 64
NSA_TOPN = 16
NSA_WINDOW = 512
NSA_CMP_HIDDEN = 128
RET_WIDTH = D_MODEL // 4
RET_HEADS = 4
RET_VAL_DIM = RET_WIDTH // RET_HEADS
RET_KEY_DIM = RET_VAL_DIM // 2
D_MIX = SSD_INNER + NSA_WIDTH + RET_WIDTH
MOE_GROUPS = 4
MOE_EXPERTS_PER_GROUP = 8
MOE_EXPERTS = MOE_GROUPS * MOE_EXPERTS_PER_GROUP
MOE_TOPK = 2
MOE_HIDDEN = D_MODEL // 4
KV_ROW = 2 * NSA_KV_HEADS * NSA_HEAD_DIM

LANE = 128
SUBLANE = 8
C_Z, C_NQ, C_RV, C_RG, C_RQ, C_RK, C_XBC, C_NKV, C_DT, C_NG = 0, 1024, 1536, 2048, 2560, 2816, 3072, 4608, 5376, 5504
N_PROJ = 5632
LC = 128
TQ = 512
WIN_ROWS = 256
TM_MOE = 256
ROW_CHUNKS = D_MODEL // LANE
VMEM_LIMIT = 56 * 2 ** 20


def _cparams(*sem):
    return pltpu.CompilerParams(dimension_semantics=sem, vmem_limit_bytes=VMEM_LIMIT)


def _silu(x):
    return x * jax.nn.sigmoid(x)


def _nt(a, b, precision=None):
    return lax.dot_general(a, b, (((1,), (1,)), ((), ())), precision=precision, preferred_element_type=F32)


def _tn(a, b, precision=None):
    return lax.dot_general(a, b, (((0,), (0,)), ((), ())), precision=precision, preferred_element_type=F32)


def _iota(shape, dim):
    return lax.broadcasted_iota(jnp.int32, shape, dim)


def _seg_sum(x2, seg):
    n = x2.shape[-1]
    bd = (_iota((n, n), 0) // seg == _iota((n, n), 1) // seg).astype(BF16)
    hi = x2.astype(BF16)
    lo = (x2 - hi.astype(F32)).astype(BF16)
    return jnp.dot(hi, bd, preferred_element_type=F32) + jnp.dot(lo, bd, preferred_element_type=F32)


def _ada_kernel(c_ref, w_ref, b_ref, o_ref):
    o_ref[0] = jnp.dot(_silu(c_ref[...]), w_ref[0], precision=HI, preferred_element_type=F32) + b_ref[0]


def _ada_call(c_all, w_ada, b_ada):
    rows = c_all.shape[0]
    tn = 2048
    return pl.pallas_call(
        _ada_kernel,
        grid=(DEPTH, 6 * D_MODEL // tn),
        in_specs=[pl.BlockSpec((rows, D_MODEL), lambda l, j: (0, 0)),
                  pl.BlockSpec((1, D_MODEL, tn), lambda l, j: (l, 0, j)),
                  pl.BlockSpec((1, 1, tn), lambda l, j: (l, 0, j))],
        out_specs=pl.BlockSpec((1, rows, tn), lambda l, j: (l, 0, j)),
        out_shape=jax.ShapeDtypeStruct((DEPTH, rows, 6 * D_MODEL), F32),
        compiler_params=_cparams("parallel", "parallel"),
        name="ada",
    )(c_all, w_ada, b_ada.reshape(DEPTH, 1, 6 * D_MODEL))


def _mod_norm(x, gain, scale, shift):
    ms = jnp.mean(x * x, axis=-1, keepdims=True)
    return x * lax.rsqrt(ms + EPS) * gain * (1.0 + scale) + shift


def _inproj_kernel(x_ref, sc_ref, sh_ref, g_ref, w_ref, o_ref, h_ref):
    @pl.when(pl.program_id(2) == 0)
    def _():
        h_ref[...] = _mod_norm(x_ref[0], g_ref[...], sc_ref[0], sh_ref[0]).astype(BF16)

    o_ref[0] = jnp.dot(h_ref[...], w_ref[...], preferred_element_type=F32)


def _inproj_call(x, scale, shift, gain, w_bf):
    b, s, d = x.shape
    ts = min(s, 1024)
    tn = 512
    sm = scale.shape[1]
    tsm = 1 if sm == 1 else ts
    mod_map = (lambda bi, i, j: (bi, 0, 0)) if sm == 1 else (lambda bi, i, j: (bi, i, 0))
    return pl.pallas_call(
        _inproj_kernel,
        grid=(b, s // ts, N_PROJ // tn),
        in_specs=[pl.BlockSpec((1, ts, d), lambda bi, i, j: (bi, i, 0)),
                  pl.BlockSpec((1, tsm, d), mod_map),
                  pl.BlockSpec((1, tsm, d), mod_map),
                  pl.BlockSpec((1, d), lambda bi, i, j: (0, 0)),
                  pl.BlockSpec((d, tn), lambda bi, i, j: (0, j))],
        out_specs=pl.BlockSpec((1, ts, tn), lambda bi, i, j: (bi, i, j)),
        out_shape=jax.ShapeDtypeStruct((b, s, N_PROJ), F32),
        scratch_shapes=[pltpu.VMEM((ts, d), BF16)],
        compiler_params=_cparams("parallel", "parallel", "arbitrary"),
        name="in_proj",
    )(x, scale, shift, gain.reshape(1, d), w_bf)


def _ssd_kernel(n_valid, xbc_ref, z_ref, dt_ref, cprev_ref, sprev_ref, cw_ref, cb_ref, dtb_ref, alog_ref, dsk_ref,
                nrm_ref, exp_ref, y_ref, snew_ref, cnew_ref, cbuf, state, ybuf):
    c = pl.program_id(1)
    nc = pl.num_programs(1)
    hd, nst = SSD_HEAD_DIM, SSD_STATE
    hpg = SSD_HEADS // SSD_GROUPS

    @pl.when(c == 0)
    def _():
        cbuf[0:SUBLANE, :] = jnp.zeros((SUBLANE, SSD_CONV_DIM), F32)
        cbuf[5:8, :] = cprev_ref[0]
        state[...] = sprev_ref[0]

    cbuf[8:8 + LC, :] = xbc_ref[0]
    conv = cb_ref[...] + cw_ref[0:1, :] * cbuf[5:5 + LC, :]
    for k in range(1, SSD_CONV):
        conv = conv + cw_ref[k:k + 1, :] * cbuf[5 + k:5 + k + LC, :]
    tail = cbuf[5 + n_valid:8 + n_valid, :]
    cbuf[5:8, :] = tail
    xbc = _silu(conv)
    xs = xbc[:, :SSD_INNER]
    bm = xbc[:, SSD_INNER:SSD_INNER + SSD_GROUPS * nst].astype(BF16)
    cm = xbc[:, SSD_INNER + SSD_GROUPS * nst:].astype(BF16)

    dtr = dt_ref[0] + dtb_ref[...]
    dt = jnp.maximum(dtr, 0.0) + jnp.log1p(jnp.exp(-jnp.abs(dtr)))
    if n_valid < LC:
        dt = jnp.where(_iota((LC, LANE), 0) < n_valid, dt, 0.0)
    a = -jnp.exp(alog_ref[...])
    da = dt * a
    row = _iota((LC, LC), 0)
    col = _iota((LC, LC), 1)
    tril = row >= col
    acs = jnp.dot(tril.astype(F32), da, precision=HI, preferred_element_type=F32)
    eye = (row == col).astype(F32)
    acs_t = _nt(eye, acs, precision=HI)
    expand = exp_ref[...]

    def per_head_to_lanes(v):
        hi = v.astype(BF16)
        lo = (v - hi.astype(F32)).astype(BF16)
        return (jnp.dot(hi, expand, preferred_element_type=F32) + jnp.dot(lo, expand, preferred_element_type=F32))

    e_acs = jnp.exp(acs)
    decay = jnp.exp(acs[LC - 1:LC, :] - acs)
    dt_x = per_head_to_lanes(dt)
    e_acs_x = per_head_to_lanes(e_acs)
    decay_x = per_head_to_lanes(decay)
    xdt = xs * dt_x
    xdec = (xdt * decay_x).astype(BF16)
    xdt_b = xdt.astype(BF16)
    cbs = [_nt(cm[:, g * nst:(g + 1) * nst], bm[:, g * nst:(g + 1) * nst]) for g in range(SSD_GROUPS)]
    for h in range(SSD_HEADS):
        g = h // hpg
        lmat = jnp.exp(jnp.where(tril, acs[:, h:h + 1] - acs_t[h:h + 1, :], -jnp.inf))
        m = (cbs[g] * lmat).astype(BF16)
        y_diag = jnp.dot(m, xdt_b[:, h * hd:(h + 1) * hd], preferred_element_type=F32)
        st = state[h]
        y_off = _nt(cm[:, g * nst:(g + 1) * nst], st.astype(BF16)) * e_acs_x[:, h * hd:(h + 1) * hd]
        state[h] = st * e_acs[LC - 1:LC, h:h + 1] + _tn(xdec[:, h * hd:(h + 1) * hd], bm[:, g * nst:(g + 1) * nst])
        ybuf[:, h * hd:(h + 1) * hd] = y_diag + y_off
    y = (ybuf[...] + dsk_ref[...] * xs) * _silu(z_ref[0])
    gw = SSD_INNER // SSD_GROUPS
    for g in range(SSD_GROUPS):
        seg = y[:, g * gw:(g + 1) * gw]
        ms = jnp.mean(seg * seg, axis=-1, keepdims=True)
        y_ref[0, :, g * gw:(g + 1) * gw] = seg * lax.rsqrt(ms + EPS) * nrm_ref[:, g * gw:(g + 1) * gw]

    @pl.when(c == nc - 1)
    def _():
        snew_ref[0] = state[...]
        cnew_ref[0] = tail


def _ssd_call(proj, n_valid, conv_prev, ssm_prev, conv_w, conv_b, dt_bias, a_log, ssd_d, ssd_norm):
    b, sp, _ = proj.shape
    nc = sp // LC
    pad = lambda v: jnp.pad(v, (0, LANE - SSD_HEADS)).reshape(1, LANE)
    const2 = lambda bi, c: (0, 0)
    return pl.pallas_call(
        functools.partial(_ssd_kernel, n_valid),
        grid=(b, nc),
        in_specs=[pl.BlockSpec((1, LC, SSD_CONV_DIM), lambda bi, c: (bi, c, C_XBC // SSD_CONV_DIM)),
                  pl.BlockSpec((1, LC, SSD_INNER), lambda bi, c: (bi, c, C_Z // SSD_INNER)),
                  pl.BlockSpec((1, LC, LANE), lambda bi, c: (bi, c, C_DT // LANE)),
                  pl.BlockSpec((1, SSD_CONV - 1, SSD_CONV_DIM), lambda bi, c: (bi, 0, 0)),
                  pl.BlockSpec((1, SSD_HEADS, SSD_HEAD_DIM, SSD_STATE), lambda bi, c: (bi, 0, 0, 0)),
                  pl.BlockSpec((SSD_CONV, SSD_CONV_DIM), const2),
                  pl.BlockSpec((1, SSD_CONV_DIM), const2),
                  pl.BlockSpec((1, LANE), const2),
                  pl.BlockSpec((1, LANE), const2),
                  pl.BlockSpec((1, SSD_INNER), const2),
                  pl.BlockSpec((1, SSD_INNER), const2),
                  pl.BlockSpec((LANE, SSD_INNER), const2)],
        out_specs=[pl.BlockSpec((1, LC, SSD_INNER), lambda bi, c: (bi, c, 0)),
                   pl.BlockSpec((1, SSD_HEADS, SSD_HEAD_DIM, SSD_STATE), lambda bi, c: (bi, 0, 0, 0)),
                   pl.BlockSpec((1, SSD_CONV - 1, SSD_CONV_DIM), lambda bi, c: (bi, 0, 0))],
        out_shape=[jax.ShapeDtypeStruct((b, sp, SSD_INNER), F32),
                   jax.ShapeDtypeStruct((b, SSD_HEADS, SSD_HEAD_DIM, SSD_STATE), F32),
                   jax.ShapeDtypeStruct((b, SSD_CONV - 1, SSD_CONV_DIM), F32)],
        scratch_shapes=[pltpu.VMEM((8 + LC, SSD_CONV_DIM), F32),
                        pltpu.VMEM((SSD_HEADS, SSD_HEAD_DIM, SSD_STATE), F32),
                        pltpu.VMEM((LC, SSD_INNER), F32)],
        compiler_params=_cparams("parallel", "arbitrary"),
        name="ssd",
    )(proj, proj, proj, conv_prev, ssm_prev, conv_w, conv_b.reshape(1, -1), pad(dt_bias), pad(a_log),
      jnp.repeat(ssd_d, SSD_HEAD_DIM).reshape(1, -1), ssd_norm.reshape(1, -1),
      (np.arange(LANE)[:, None] == np.arange(SSD_INNER)[None, :] // SSD_HEAD_DIM).astype(BF16))


def _ret_log_g(h):
    return float(np.log1p(-np.exp2(np.float32(-5.0 - h)), dtype=np.float32))


def _ret_kernel(n_valid, q_ref, k_ref, v_ref, g_ref, cos_ref, sin_ref, rprev_ref, o_ref, rnew_ref, state):
    c = pl.program_id(1)
    nc = pl.num_programs(1)
    kd, vd = RET_KEY_DIM, RET_VAL_DIM
    half = kd // 2

    @pl.when(c == 0)
    def _():
        state[...] = rprev_ref[0]

    cos = cos_ref[...]
    sin = sin_ref[...]
    first_half = (_iota((LC, RET_HEADS * kd), 1) % kd) < half

    def rot(x):
        n = x.shape[-1]
        swapped = jnp.where(first_half, pltpu.roll(x, n - half, 1), pltpu.roll(x, half, 1))
        return x * cos + swapped * sin

    q = rot(q_ref[0])
    k = rot(k_ref[0]) * (kd ** -0.5)
    v = v_ref[0]
    gate = g_ref[0]
    ti = _iota((LC, LC), 0)
    tj = _iota((LC, LC), 1)
    diff = (ti - tj).astype(F32)
    ipos = _iota((LC, 1), 0)
    i1 = ipos.astype(F32)
    for h in range(RET_HEADS):
        lg = _ret_log_g(h)
        dmat = jnp.where(diff >= 0, jnp.exp(diff * lg), 0.0)
        q_dec = jnp.exp((i1 + 1.0) * lg)
        k_dec = jnp.where(ipos < n_valid, jnp.exp((n_valid - 1.0 - i1) * lg), 0.0)
        c_dec = math.exp(n_valid * lg)
        qh = q[:, h * kd:(h + 1) * kd].astype(BF16)
        kh = k[:, h * kd:(h + 1) * kd]
        vh = v[:, h * vd:(h + 1) * vd].astype(BF16)
        sc = _nt(qh, kh.astype(BF16)) * dmat
        intra = jnp.dot(sc.astype(BF16), vh, preferred_element_type=F32)
        rs = state[h]
        cross = jnp.dot(qh, rs.astype(BF16), preferred_element_type=F32) * q_dec
        state[h] = rs * c_dec + _tn((kh * k_dec).astype(BF16), vh)
        o = intra + cross
        ms = jnp.mean(o * o, axis=-1, keepdims=True)
        o_ref[0, :, h * vd:(h + 1) * vd] = o * lax.rsqrt(ms + EPS) * _silu(gate[:, h * vd:(h + 1) * vd])

    @pl.when(c == nc - 1)
    def _():
        rnew_ref[0] = state[...]


def _rotary_tables(offset, s_real, s_pad):
    half = RET_KEY_DIM // 2
    freqs = 1.0 / (10000.0 ** jnp.linspace(0.0, 1.0, half, dtype=F32))
    pos = (offset + jnp.arange(s_real)).astype(F32)
    ang = pos[:, None] * freqs[None, :]
    cos, sin = jnp.cos(ang), jnp.sin(ang)
    cos = jnp.tile(jnp.concatenate([cos, cos], axis=-1), (1, RET_HEADS))
    sin = jnp.tile(jnp.concatenate([-sin, sin], axis=-1), (1, RET_HEADS))
    padr = ((0, s_pad - s_real), (0, 0))
    return jnp.pad(cos, padr), jnp.pad(sin, padr)


def _ret_call(proj, n_valid, ret_prev, cos, sin):
    b, sp, _ = proj.shape
    nc = sp // LC
    qw = RET_HEADS * RET_KEY_DIM
    return pl.pallas_call(
        functools.partial(_ret_kernel, n_valid),
        grid=(b, nc),
        in_specs=[pl.BlockSpec((1, LC, qw), lambda bi, c: (bi, c, C_RQ // qw)),
                  pl.BlockSpec((1, LC, qw), lambda bi, c: (bi, c, C_RK // qw)),
                  pl.BlockSpec((1, LC, RET_WIDTH), lambda bi, c: (bi, c, C_RV // RET_WIDTH)),
                  pl.BlockSpec((1, LC, RET_WIDTH), lambda bi, c: (bi, c, C_RG // RET_WIDTH)),
                  pl.BlockSpec((LC, qw), lambda bi, c: (c, 0)),
                  pl.BlockSpec((LC, qw), lambda bi, c: (c, 0)),
                  pl.BlockSpec((1, RET_HEADS, RET_KEY_DIM, RET_VAL_DIM), lambda bi, c: (bi, 0, 0, 0))],
        out_specs=[pl.BlockSpec((1, LC, RET_WIDTH), lambda bi, c: (bi, c, 0)),
                   pl.BlockSpec((1, RET_HEADS, RET_KEY_DIM, RET_VAL_DIM), lambda bi, c: (bi, 0, 0, 0))],
        out_shape=[jax.ShapeDtypeStruct((b, sp, RET_WIDTH), F32),
                   jax.ShapeDtypeStruct((b, RET_HEADS, RET_KEY_DIM, RET_VAL_DIM), F32)],
        scratch_shapes=[pltpu.VMEM((RET_HEADS, RET_KEY_DIM, RET_VAL_DIM), F32)],
        compiler_params=_cparams("parallel", "arbitrary"),
        name="retention",
    )(proj, proj, proj, proj, cos, sin, ret_prev)


def _slope(h):
    return float(2.0 ** -(h + 1))


def _prep_kernel(nq_ref, nkv_ref, gq_ref, gk_ref, isk_ref, qn_ref, kvn_ref, kvb_ref):
    inv = 1.0 / NSA_HEAD_DIM
    nq = nq_ref[0]
    qn_ref[0] = nq * lax.rsqrt(_seg_sum(nq * nq, NSA_HEAD_DIM) * inv + EPS) * gq_ref[...]
    kv = nkv_ref[0]
    normed = kv * lax.rsqrt(_seg_sum(kv * kv, NSA_HEAD_DIM) * inv + EPS) * gk_ref[...]
    kvn = jnp.where(isk_ref[...] > 0.5, normed, kv)
    kvn_ref[0] = kvn
    kvb_ref[0] = kvn[:, KV_ROW:].astype(BF16)


def _prep_call(proj, s_real, qk_gain):
    b = proj.shape[0]
    ts = min(s_real, 512)
    hd = NSA_HEAD_DIM
    gq = (jnp.tile(qk_gain[0], NSA_HEADS) * (hd ** -0.5)).reshape(1, -1)
    ones = jnp.ones((2 * hd,), F32)
    gk = jnp.concatenate([jnp.concatenate([jnp.tile(qk_gain[1 + i], 2), ones]) for i in range(3)]).reshape(1, -1)
    isk = jnp.tile(jnp.concatenate([ones, 0.0 * ones]), 3).reshape(1, -1)
    const2 = lambda bi, i: (0, 0)
    return pl.pallas_call(
        _prep_kernel,
        grid=(b, s_real // ts),
        in_specs=[pl.BlockSpec((1, ts, NSA_WIDTH), lambda bi, i: (bi, i, C_NQ // NSA_WIDTH)),
                  pl.BlockSpec((1, ts, 3 * KV_ROW), lambda bi, i: (bi, i, C_NKV // (3 * KV_ROW))),
                  pl.BlockSpec((1, NSA_WIDTH), const2),
                  pl.BlockSpec((1, 3 * KV_ROW), const2),
                  pl.BlockSpec((1, 3 * KV_ROW), const2)],
        out_specs=[pl.BlockSpec((1, ts, NSA_WIDTH), lambda bi, i: (bi, i, 0)),
                   pl.BlockSpec((1, ts, 3 * KV_ROW), lambda bi, i: (bi, i, 0)),
                   pl.BlockSpec((1, ts, 2 * KV_ROW), lambda bi, i: (bi, i, 0))],
        out_shape=[jax.ShapeDtypeStruct((b, s_real, NSA_WIDTH), F32),
                   jax.ShapeDtypeStruct((b, s_real, 3 * KV_ROW), F32),
                   jax.ShapeDtypeStruct((b, s_real, 2 * KV_ROW), BF16)],
        compiler_params=_cparams("parallel", "parallel"),
        name="nsa_prep",
    )(proj, proj, gq, gk, isk)


def _compress_weights(cmp_pe, cmp_w1, cmp_w2, gain4):
    hd = NSA_HEAD_DIM

    def block_diag4(a0, a1):
        z = jnp.zeros_like(a0)
        rows = [[a0, z, z, z], [z, a0, z, z], [z, z, a1, z], [z, z, z, a1]]
        return jnp.concatenate([jnp.concatenate(r, axis=-1) for r in rows], axis=-2)

    w1 = cmp_w1.astype(BF16)
    w2 = cmp_w2.astype(BF16)
    wbig = block_diag4(w1[0], w1[1])
    w2big = block_diag4(w2[0], w2[1])
    pe_row = jnp.concatenate([cmp_pe[0], cmp_pe[0], cmp_pe[1], cmp_pe[1]], axis=-1)
    g4 = jnp.concatenate([jnp.tile(gain4, 2), jnp.ones((2 * hd,), F32)]).reshape(1, -1)
    return pe_row, wbig, w2big, g4


def _compress_core(get_rows, nblk, pe_ref, wbig_ref, w2_ref, g4_ref):
    acc = jnp.zeros((nblk, 4 * NSA_CMP_HIDDEN), F32)
    for l in range(NSA_BLOCK):
        x = get_rows(l) + pe_ref[l:l + 1, :]
        acc = acc + jnp.dot(x.astype(BF16), wbig_ref[l], preferred_element_type=F32)
    out = jnp.dot(_silu(acc).astype(BF16), w2_ref[...], preferred_element_type=F32)
    ss = _seg_sum(out * out, NSA_HEAD_DIM)
    normed = out * lax.rsqrt(ss * (1.0 / NSA_HEAD_DIM) + EPS) * g4_ref[...]
    return jnp.where(_iota(out.shape, 1) < 2 * NSA_HEAD_DIM, normed, out)


def _cmp_prompt_kernel(nblk, kv_ref, pe_ref, wbig_ref, w2_ref, g4_ref, o_ref):
    o_ref[0] = _compress_core(lambda l: kv_ref[0, :, l, :], nblk, pe_ref, wbig_ref, w2_ref, g4_ref)


def _cmp_prompt_call(kvn, cw):
    b, s, _ = kvn.shape
    nblk = s // NSA_BLOCK
    pe_row, wbig, w2big, g4 = cw
    kv4 = kvn.reshape(b, nblk, NSA_BLOCK, 3 * KV_ROW)
    return pl.pallas_call(
        functools.partial(_cmp_prompt_kernel, nblk),
        grid=(b,),
        in_specs=[pl.BlockSpec((1, nblk, NSA_BLOCK, KV_ROW), lambda bi: (bi, 0, 0, 0)),
                  pl.BlockSpec(pe_row.shape, lambda bi: (0, 0)),
                  pl.BlockSpec(wbig.shape, lambda bi: (0, 0, 0)),
                  pl.BlockSpec(w2big.shape, lambda bi: (0, 0)),
                  pl.BlockSpec(g4.shape, lambda bi: (0, 0))],
        out_specs=pl.BlockSpec((1, nblk, KV_ROW), lambda bi: (bi, 0, 0)),
        out_shape=jax.ShapeDtypeStruct((b, nblk, KV_ROW), F32),
        compiler_params=_cparams("arbitrary"),
        name="nsa_cmp_prompt",
    )(kv4, pe_row, wbig, w2big, g4)


def _masked_softmax(s, mask):
    s = jnp.where(mask, s, NEG_INF)
    p = jnp.exp(s - jnp.max(s, axis=-1, keepdims=True))
    return p / jnp.sum(p, axis=-1, keepdims=True) * mask.astype(F32)


def _select_blocks(imp, blk, n_top, axis=-1):
    sel = jnp.zeros(imp.shape, F32)
    big = imp.shape[axis]
    for _ in range(n_top):
        m = jnp.max(imp, axis=axis, keepdims=True)
        idx = jnp.min(jnp.where(imp == m, blk, big), axis=axis, keepdims=True)
        hit = blk == idx
        sel = jnp.where(hit & (m >= 0.0), 1.0, sel)
        imp = jnp.where(hit, -2.0, imp)
    return sel


M_FLOOR = -1e29


def _nsa_prompt_kernel(nbc, s_len, tk, wk, q_ref, gate_ref, cmp_ref, ksel_ref, kwin_ref, o_ref, bias_ref):
    qi = pl.program_id(1)
    hd, rep, nh = NSA_HEAD_DIM, NSA_REP, NSA_HEADS
    q = q_ref[0]
    qb = q.astype(BF16)
    gates = jax.nn.sigmoid(gate_ref[0])
    t = qi * TQ + _iota((TQ, 1), 0)
    t_row = qi * TQ + _iota((1, TQ), 1)
    kcvc = cmp_ref[0]
    blk = _iota((nbc, TQ), 0)
    dist_c = t_row - ((blk + 1) * NSA_BLOCK - 1)
    mask_c = dist_c >= 0
    mask_cf = mask_c.astype(F32)
    dist_cf = dist_c.astype(F32)
    cur = t_row // NSA_BLOCK
    o_cmp = []
    imps = []
    n_kt = (qi * TQ + TQ + tk - 1) // tk
    for g in range(NSA_KV_HEADS):
        kc = kcvc[:, g * hd:(g + 1) * hd]
        vc = kcvc[:, 2 * hd + g * hd:2 * hd + (g + 1) * hd]
        imp = jnp.zeros((nbc, TQ), F32)
        for r in range(rep):
            h = g * rep + r
            s = jnp.where(mask_c, _nt(kc, q[:, h * hd:(h + 1) * hd], precision=HI) - _slope(h) * dist_cf, NEG_INF)
            p = jnp.exp(s - jnp.max(s, axis=0, keepdims=True))
            p = p / jnp.sum(p, axis=0, keepdims=True) * mask_cf
            imp = imp + p
            o_cmp.append(_tn(p, vc, precision=HI))
        imps.append(jnp.where(blk < cur, imp, -1.0))
    blk_all = jnp.concatenate([blk] * NSA_KV_HEADS, axis=1)
    is_cur = (blk_all == jnp.concatenate([cur] * NSA_KV_HEADS, axis=1)).astype(F32)
    sel_all = jnp.maximum(_select_blocks(jnp.concatenate(imps, axis=1), blk_all, NSA_TOPN - 1, axis=0), is_cur)
    sels = []
    first_kt = []
    bpt = tk // NSA_BLOCK
    for g in range(NSA_KV_HEADS):
        sel = sel_all[:, g * TQ:(g + 1) * TQ]
        sels.append(sel.astype(BF16))
        first = n_kt
        for kt in reversed(range(nbc // bpt)):
            first = jnp.where(jnp.max(sel[kt * bpt:(kt + 1) * bpt, :]) > 0.0, kt, first)
        first_kt.append(first)

    def fill(kt, carry):
        k0 = pl.multiple_of(kt * tk, tk)
        causal = t >= k0 + _iota((TQ, tk), 1)
        expand = (_iota((nbc, tk), 0) == (k0 + _iota((nbc, tk), 1)) // NSA_BLOCK).astype(BF16)
        for g in range(NSA_KV_HEADS):
            picked = _tn(sels[g], expand) > 0.5
            bias_ref[g, :, pl.ds(k0, tk)] = jnp.where(picked & causal, 0.0, NEG_INF).astype(BF16)
        return carry

    lax.fori_loop(jnp.minimum(first_kt[0], first_kt[1]), n_kt, fill, 0)

    rows_g = rep * TQ
    for g in range(NSA_KV_HEADS):
        heads = range(g * rep, (g + 1) * rep)

        def stacked(bias, dist, heads=heads):
            return jnp.concatenate([bias - _slope(h) * dist for h in heads], axis=0)

        qg = jnp.concatenate([qb[:, h * hd:(h + 1) * hd] for h in heads], axis=0)

        def body(kt, carry, g=g, qg=qg, stacked=stacked):
            m, l, acc = carry
            k0 = pl.multiple_of(kt * tk, tk)
            kk = ksel_ref[0, pl.ds(k0, tk), g * hd:(g + 1) * hd]
            vv = ksel_ref[0, pl.ds(k0, tk), 2 * hd + g * hd:2 * hd + (g + 1) * hd]
            dist = (t - (k0 + _iota((TQ, tk), 1))).astype(F32)
            s = _nt(qg, kk) + stacked(bias_ref[g, :, pl.ds(k0, tk)].astype(F32), dist)
            m_new = jnp.maximum(m, jnp.max(s, axis=-1, keepdims=True))
            alpha = jnp.exp(m - m_new)
            p = jnp.exp(s - m_new)
            l = alpha * l + jnp.sum(p, axis=-1, keepdims=True)
            acc = alpha * acc + jnp.dot(p.astype(BF16), vv, preferred_element_type=F32)
            return m_new, l, acc

        init = (jnp.full((rows_g, 1), M_FLOOR, F32), jnp.zeros((rows_g, 1), F32), jnp.zeros((rows_g, hd), F32))
        _, l_s, acc_s = lax.fori_loop(first_kt[g], n_kt, body, init)
        o_sel = acc_s / l_s
        o_win = []
        for w0 in range(0, TQ, WIN_ROWS):
            start_w = pl.multiple_of(jnp.maximum(qi * TQ + w0 + WIN_ROWS - wk, 0), WIN_ROWS)
            kw_all = kwin_ref[0, pl.ds(start_w, wk), :]
            dist_w = t[w0:w0 + WIN_ROWS] - (start_w + _iota((WIN_ROWS, wk), 1))
            bias_w = jnp.where((dist_w >= 0) & (dist_w <= NSA_WINDOW), 0.0, NEG_INF)
            qw = jnp.concatenate([qb[w0:w0 + WIN_ROWS, h * hd:(h + 1) * hd] for h in heads], axis=0)
            s_w = _nt(qw, kw_all[:, g * hd:(g + 1) * hd]) + stacked(bias_w, dist_w.astype(F32))
            p_w = jnp.exp(s_w - jnp.max(s_w, axis=-1, keepdims=True))
            o_win.append(jnp.dot(p_w.astype(BF16), kw_all[:, 2 * hd + g * hd:2 * hd + (g + 1) * hd],
                                 preferred_element_type=F32) / jnp.sum(p_w, axis=-1, keepdims=True))
        for j, h in enumerate(heads):
            win_h = jnp.concatenate([o[j * WIN_ROWS:(j + 1) * WIN_ROWS] for o in o_win], axis=0)
            o_ref[0, :, h * hd:(h + 1) * hd] = (gates[:, h:h + 1] * o_cmp[h]
                                                + gates[:, nh + h:nh + h + 1] * o_sel[j * TQ:(j + 1) * TQ]
                                                + gates[:, 2 * nh + h:2 * nh + h + 1] * win_h)


def _nsa_prompt_call(proj, qn, kvb, kcvc):
    b, s, _ = qn.shape
    nbc = kcvc.shape[1]
    tk = min(512, s)
    wk = min(NSA_WINDOW + WIN_ROWS, s)
    return pl.pallas_call(
        functools.partial(_nsa_prompt_kernel, nbc, s, tk, wk),
        grid=(b, s // TQ),
        in_specs=[pl.BlockSpec((1, TQ, NSA_WIDTH), lambda bi, i: (bi, i, 0)),
                  pl.BlockSpec((1, TQ, LANE), lambda bi, i: (bi, i, C_NG // LANE)),
                  pl.BlockSpec((1, nbc, KV_ROW), lambda bi, i: (bi, 0, 0)),
                  pl.BlockSpec((1, s, KV_ROW), lambda bi, i: (bi, 0, 0)),
                  pl.BlockSpec((1, s, KV_ROW), lambda bi, i: (bi, 0, 1))],
        out_specs=pl.BlockSpec((1, TQ, NSA_WIDTH), lambda bi, i: (bi, i, 0)),
        out_shape=jax.ShapeDtypeStruct((b, s, NSA_WIDTH), F32),
        scratch_shapes=[pltpu.VMEM((NSA_KV_HEADS, TQ, s), BF16)],
        compiler_params=_cparams("parallel", "arbitrary"),
        name="nsa_prompt",
    )(qn, proj, kcvc, kvb, kvb)


def _feature_major_pages(cache):
    return cache.transpose(0, 1, 3, 4, 5, 2).reshape(cache.shape[0], cache.shape[1], KV_ROW, PAGE_SIZE)


def _page_copies(cache_hbm, li, pt_ref, b, step, dst_of, sem, slot, pps):
    return [pltpu.make_async_copy(cache_hbm.at[li, pt_ref[b, step * pps + k]], dst_of(slot, k), sem.at[slot])
            for k in range(pps)]


def _pipelined_pages(cache_hbm, li, pt_ref, dst_of, sem, pps):
    b, s, ns = pl.program_id(0), pl.program_id(1), pl.num_programs(1)
    slot = s % 2
    copies = functools.partial(_page_copies, cache_hbm, li, pt_ref, b, dst_of=dst_of, sem=sem, pps=pps)

    @pl.when(s == 0)
    def _():
        for cp in copies(step=s, slot=slot):
            cp.start()

    @pl.when(s + 1 < ns)
    def _():
        for cp in copies(step=s + 1, slot=1 - slot):
            cp.start()

    for cp in copies(step=s, slot=slot):
        cp.wait()
    return slot


def _cmp_sample_weights(cmp_pe, cmp_w1, cmp_w2, gain4):
    wt = cmp_w1.transpose(0, 2, 1, 3).astype(BF16)
    z = jnp.zeros_like(wt)
    w1 = jnp.concatenate([jnp.concatenate([wt, z], axis=-1), jnp.concatenate([z, wt], axis=-1)], axis=2)
    w1 = w1.reshape(2, NSA_HEAD_DIM // 2, 4 * NSA_BLOCK, 2 * NSA_CMP_HIDDEN)
    w2 = cmp_w2.astype(BF16)
    z2 = jnp.zeros_like(w2)
    w2bd = jnp.concatenate([jnp.concatenate([w2, z2], axis=-1), jnp.concatenate([z2, w2], axis=-1)], axis=1)
    pe_t = jnp.tile(cmp_pe.transpose(0, 2, 1), (1, 1, 2)).reshape(2 * NSA_HEAD_DIM, 2 * NSA_BLOCK)
    return pe_t, w1, w2bd, jnp.tile(gain4, 2).reshape(1, -1)


def _cmp_sample_kernel(li, pps, pt_ref, cache_hbm, pe_ref, w1_ref, w2_ref, g_ref, o_ref, buf, sem):
    slot = _pipelined_pages(cache_hbm, li, pt_ref, lambda sl, k: buf.at[sl, k], sem, pps)
    pages = buf.at[slot]
    hd = NSA_HEAD_DIM
    for c in range(2):
        acc = jnp.zeros((NSA_KV_HEADS * pps, 2 * NSA_CMP_HIDDEN), F32)
        def feature_rows(d):
            x = jnp.concatenate([pages[:, (2 * c + g) * hd + d, :] for g in range(NSA_KV_HEADS)], axis=0)
            return (x + pe_ref[c * hd + d:c * hd + d + 1, :]).astype(BF16)

        for d2 in range(hd // 2):
            x = jnp.concatenate([feature_rows(2 * d2), feature_rows(2 * d2 + 1)], axis=1)
            acc = acc + jnp.dot(x, w1_ref[c, d2], preferred_element_type=F32)
        out = jnp.dot(_silu(acc).astype(BF16), w2_ref[c], preferred_element_type=F32)
        if c == 0:
            out = out * lax.rsqrt(_seg_sum(out * out, hd) * (1.0 / hd) + EPS) * g_ref[...]
        for g in range(NSA_KV_HEADS):
            o_ref[0, :, (2 * c + g) * 2 * hd:(2 * c + g + 1) * 2 * hd] = out[g * pps:(g + 1) * pps]


def _cmp_sample_call(cache_t, li, page_table, cw):
    bd, n_pages = page_table.shape
    pps = math.gcd(n_pages, 64)
    pe_t, w1, w2bd, gain = cw
    hd = NSA_HEAD_DIM
    grid_spec = pltpu.PrefetchScalarGridSpec(
        num_scalar_prefetch=1,
        grid=(bd, n_pages // pps),
        in_specs=[pl.BlockSpec(memory_space=pl.ANY),
                  pl.BlockSpec(pe_t.shape, lambda b, s, pt: (0, 0)),
                  pl.BlockSpec(w1.shape, lambda b, s, pt: (0, 0, 0, 0)),
                  pl.BlockSpec(w2bd.shape, lambda b, s, pt: (0, 0, 0)),
                  pl.BlockSpec(gain.shape, lambda b, s, pt: (0, 0))],
        out_specs=pl.BlockSpec((1, pps, 2 * KV_ROW), lambda b, s, pt: (b, s, 0)),
        scratch_shapes=[pltpu.VMEM((2, pps, KV_ROW, PAGE_SIZE), F32), pltpu.SemaphoreType.DMA((2,))])
    packed = pl.pallas_call(
        functools.partial(_cmp_sample_kernel, li, pps),
        grid_spec=grid_spec,
        out_shape=jax.ShapeDtypeStruct((bd, n_pages, 2 * KV_ROW), F32),
        compiler_params=_cparams("arbitrary", "arbitrary"),
        name="nsa_cmp_sample",
    )(page_table, cache_t, pe_t, w1, w2bd, gain)
    return packed.reshape(bd, n_pages, 4, 2, hd).transpose(0, 1, 3, 2, 4).reshape(bd, 2 * n_pages, KV_ROW)


def _pick_group(x, rows_g0):
    w = x.shape[-1] // 2
    return jnp.where(rows_g0, x[:, :w], x[:, w:])


def _nsa_sample_kernel(li, pps, past, n_tok, pt_ref, cache_hbm, q_ref, gate_ref, cmp_ref, knew_ref, wold_ref, wnew_ref,
                       o_ref, buf, sem, qbd_ref, sel_ref, ocw_ref, m_ref, l_ref, acc_ref):
    s, ns = pl.program_id(1), pl.num_programs(1)
    hd, nh, rep = NSA_HEAD_DIM, NSA_HEADS, NSA_REP
    rows = nh * n_tok
    nbc = past // NSA_BLOCK
    tk = pps * PAGE_SIZE
    slot = _pipelined_pages(cache_hbm, li, pt_ref,
                            lambda sl, k: buf.at[sl, :, pl.ds(k * PAGE_SIZE, PAGE_SIZE)], sem, pps)
    ridx = _iota((rows, 1), 0)
    rows_g0 = ridx < rep * n_tok
    tok = ridx % n_tok
    t = past + tok
    slope = jnp.exp2(-(ridx // n_tok + 1).astype(F32))

    @pl.when(s == 0)
    def _():
        qtok = q_ref[0]
        qrows = jnp.concatenate([qtok[:, h * hd:(h + 1) * hd] for h in range(nh)], axis=0)
        zero = jnp.zeros_like(qrows)
        qbd = jnp.where(rows_g0, jnp.concatenate([qrows, zero], axis=1), jnp.concatenate([zero, qrows], axis=1))
        qbd_ref[...] = qbd
        qbb = qbd.astype(BF16)
        kcvc = cmp_ref[0]
        blk = _iota((rows, nbc), 1)
        dist_c = t - ((blk + 1) * NSA_BLOCK - 1)
        p_c = _masked_softmax(_nt(qbd, kcvc[:, :2 * hd], precision=HI) - slope * dist_c.astype(F32), dist_c >= 0)
        o_c = _pick_group(jnp.dot(p_c, kcvc[:, 2 * hd:], precision=HI, preferred_element_type=F32), rows_g0)
        blk_t = _iota((n_tok, nbc), 1)
        cur_t = (past + _iota((n_tok, 1), 0)) // NSA_BLOCK
        sels = []
        for g in range(NSA_KV_HEADS):
            imp = p_c[g * rep * n_tok:(g * rep + 1) * n_tok]
            for r in range(1, rep):
                imp = imp + p_c[(g * rep + r) * n_tok:(g * rep + r + 1) * n_tok]
            imp = jnp.where(blk_t < cur_t, imp, -1.0)
            sels += [_select_blocks(imp, blk_t, NSA_TOPN - 1)] * rep
        sel_ref[...] = jnp.concatenate(sels, axis=0).astype(BF16)
        knew = knew_ref[0]
        dist_n = tok - _iota((rows, n_tok), 1)
        mask_n = dist_n >= 0
        s_n = jnp.where(mask_n, _nt(qbb, knew[:, :2 * hd].astype(BF16)) - slope * dist_n.astype(F32), NEG_INF)
        m0 = jnp.max(s_n, axis=-1, keepdims=True)
        p_n = jnp.where(mask_n, jnp.exp(s_n - m0), 0.0)
        m_ref[...] = m0
        l_ref[...] = jnp.sum(p_n, axis=-1, keepdims=True)
        acc_ref[...] = jnp.dot(p_n.astype(BF16), knew[:, 2 * hd:].astype(BF16), preferred_element_type=F32)
        kw = jnp.concatenate([wold_ref[0], wnew_ref[0]], axis=0)
        dist_w = tok + NSA_WINDOW - _iota((rows, NSA_WINDOW + n_tok), 1)
        mask_w = (dist_w >= 0) & (dist_w <= NSA_WINDOW)
        p_w = _masked_softmax(_nt(qbb, kw[:, :2 * hd].astype(BF16)) - slope * dist_w.astype(F32), mask_w)
        o_w = _pick_group(jnp.dot(p_w.astype(BF16), kw[:, 2 * hd:].astype(BF16), preferred_element_type=F32), rows_g0)
        ocw_ref[0] = o_c
        ocw_ref[1] = o_w

    pages = buf.at[slot]
    k_t = pages[:2 * hd, :].astype(BF16)
    v_t = pages[2 * hd:, :].astype(BF16)
    k0 = s * tk
    expand = (_iota((nbc, tk), 0) == (k0 + _iota((nbc, tk), 1)) // NSA_BLOCK).astype(BF16)
    mask = jnp.dot(sel_ref[...], expand, preferred_element_type=F32) > 0.5
    dist = t - (k0 + _iota((rows, tk), 1))
    qk = jnp.dot(qbd_ref[...].astype(BF16), k_t, preferred_element_type=F32)
    sc = jnp.where(mask, qk - slope * dist.astype(F32), NEG_INF)
    m_old = m_ref[...]
    m_new = jnp.maximum(m_old, jnp.max(sc, axis=-1, keepdims=True))
    alpha = jnp.exp(m_old - m_new)
    p = jnp.where(mask, jnp.exp(sc - m_new), 0.0)
    m_ref[...] = m_new
    l_ref[...] = alpha * l_ref[...] + jnp.sum(p, axis=-1, keepdims=True)
    acc_ref[...] = alpha * acc_ref[...] + _nt(p.astype(BF16), v_t)

    @pl.when(s == ns - 1)
    def _():
        o_s = _pick_group(acc_ref[...] / l_ref[...], rows_g0)
        gates = jax.nn.sigmoid(gate_ref[0])

        def gate_rows(br):
            return jnp.concatenate([gates[:, br * nh + h:br * nh + h + 1] for h in range(nh)], axis=0)

        o_ref[0] = gate_rows(0) * ocw_ref[0] + gate_rows(1) * o_s + gate_rows(2) * ocw_ref[1]


def _nsa_sample_call(cache_t, li, page_table, proj, qn, kvn, kcvc, win_old, past):
    bd, n_pages = page_table.shape
    n_tok = qn.shape[1]
    pps = math.gcd(n_pages, 32)
    rows = NSA_HEADS * n_tok
    nbc = kcvc.shape[1]
    grid_spec = pltpu.PrefetchScalarGridSpec(
        num_scalar_prefetch=1,
        grid=(bd, n_pages // pps),
        in_specs=[pl.BlockSpec(memory_space=pl.ANY),
                  pl.BlockSpec((1, n_tok, NSA_WIDTH), lambda b, s, pt: (b, 0, 0)),
                  pl.BlockSpec((1, n_tok, LANE), lambda b, s, pt: (b, 0, C_NG // LANE)),
                  pl.BlockSpec((1, nbc, KV_ROW), lambda b, s, pt: (b, 0, 0)),
                  pl.BlockSpec((1, n_tok, KV_ROW), lambda b, s, pt: (b, 0, 1)),
                  pl.BlockSpec((1, NSA_WINDOW, KV_ROW), lambda b, s, pt: (b, 0, 0)),
                  pl.BlockSpec((1, n_tok, KV_ROW), lambda b, s, pt: (b, 0, 2))],
        out_specs=pl.BlockSpec((1, rows, NSA_HEAD_DIM), lambda b, s, pt: (b, 0, 0)),
        scratch_shapes=[pltpu.VMEM((2, KV_ROW, pps * PAGE_SIZE), F32),
                        pltpu.SemaphoreType.DMA((2,)),
                        pltpu.VMEM((rows, 2 * NSA_HEAD_DIM), F32),
                        pltpu.VMEM((rows, nbc), BF16),
                        pltpu.VMEM((2, rows, NSA_HEAD_DIM), F32),
                        pltpu.VMEM((rows, 1), F32),
                        pltpu.VMEM((rows, 1), F32),
                        pltpu.VMEM((rows, 2 * NSA_HEAD_DIM), F32)])
    return pl.pallas_call(
        functools.partial(_nsa_sample_kernel, li, pps, past, n_tok),
        grid_spec=grid_spec,
        out_shape=jax.ShapeDtypeStruct((bd, rows, NSA_HEAD_DIM), F32),
        compiler_params=_cparams("arbitrary", "arbitrary"),
        name="nsa_sample",
    )(page_table, cache_t, qn, proj, kcvc, kvn, win_old, kvn)


def _outproj_kernel(y_ref, n_ref, r_ref, w_ref, x_ref, g_ref, o_ref, mix_ref):
    @pl.when(pl.program_id(2) == 0)
    def _():
        mix_ref[:, :SSD_INNER] = y_ref[0].astype(BF16)
        mix_ref[:, SSD_INNER:SSD_INNER + NSA_WIDTH] = n_ref[0].astype(BF16)
        mix_ref[:, SSD_INNER + NSA_WIDTH:] = r_ref[0].astype(BF16)

    o_ref[0] = x_ref[0] + g_ref[0] * jnp.dot(mix_ref[...], w_ref[...], preferred_element_type=F32)


def _outproj_call(y, o_nsa, o_ret, w_bf, x, gate):
    b, s, d = x.shape
    ts = min(s, 1024)
    tn = 512
    sm = gate.shape[1]
    tsm = 1 if sm == 1 else ts
    mod_map = (lambda bi, i, j: (bi, 0, j)) if sm == 1 else (lambda bi, i, j: (bi, i, j))
    row_map = lambda bi, i, j: (bi, i, 0)
    return pl.pallas_call(
        _outproj_kernel,
        grid=(b, s // ts, d // tn),
        in_specs=[pl.BlockSpec((1, ts, SSD_INNER), row_map),
                  pl.BlockSpec((1, ts, NSA_WIDTH), row_map),
                  pl.BlockSpec((1, ts, RET_WIDTH), row_map),
                  pl.BlockSpec((D_MIX, tn), lambda bi, i, j: (0, j)),
                  pl.BlockSpec((1, ts, tn), lambda bi, i, j: (bi, i, j)),
                  pl.BlockSpec((1, tsm, tn), mod_map)],
        out_specs=pl.BlockSpec((1, ts, tn), lambda bi, i, j: (bi, i, j)),
        out_shape=jax.ShapeDtypeStruct((b, s, d), F32),
        scratch_shapes=[pltpu.VMEM((ts, D_MIX), BF16)],
        compiler_params=_cparams("parallel", "parallel", "arbitrary"),
        name="out_proj",
    )(y, o_nsa, o_ret, w_bf, x, gate)


def _route_kernel(x_ref, sc_ref, sh_ref, g_ref, wr_ref, br_ref, h_ref, meta_ref, cnt_ref):
    h = _mod_norm(x_ref[0], g_ref[...], sc_ref[0], sh_ref[0])
    n_rows = h.shape[0]
    for c in range(ROW_CHUNKS):
        h_ref[0, pl.ds(c, n_rows, stride=ROW_CHUNKS), :] = h[:, c * LANE:(c + 1) * LANE]
    logit = jnp.dot(h, wr_ref[...], precision=HI, preferred_element_type=F32) + br_ref[...]
    lane = _iota(logit.shape, 1)
    first = lambda hit: jnp.min(jnp.where(hit, lane, LANE), axis=-1, keepdims=True)
    is_g = lane < MOE_GROUPS
    gl = jnp.where(is_g, logit, -jnp.inf)
    gmax = jnp.max(gl, axis=-1, keepdims=True)
    g_idx = first(gl == gmax)
    g_p = 1.0 / jnp.sum(jnp.where(is_g, jnp.exp(logit - gmax), 0.0), axis=-1, keepdims=True)
    e_lane = lane - MOE_GROUPS
    in_grp = (e_lane >= 0) & (e_lane < MOE_EXPERTS) & (e_lane // MOE_EXPERTS_PER_GROUP == g_idx)
    el = jnp.where(in_grp, logit, -jnp.inf)
    v1 = jnp.max(el, axis=-1, keepdims=True)
    i1 = first(el == v1)
    el = jnp.where(lane == i1, -jnp.inf, el)
    v2 = jnp.max(el, axis=-1, keepdims=True)
    i2 = first(el == v2)
    e = jnp.exp(v2 - v1)
    w1 = g_p / (1.0 + e)
    w2 = g_p * e / (1.0 + e)
    meta = jnp.where(lane == 0, (i1 - MOE_GROUPS).astype(F32), 0.0)
    meta = jnp.where(lane == 1, (i2 - MOE_GROUPS).astype(F32), meta)
    meta = jnp.where(lane == 2, w1, meta)
    meta_ref[0] = jnp.where(lane == 3, w2, meta)
    cnt_ref[0, 0] = jnp.sum(((lane == i1) | (lane == i2)).astype(F32), axis=0, keepdims=True)


def _route_call(x1, scale, shift, gain, wr, br):
    b, s, d = x1.shape
    ts = min(s, 512)
    sm = scale.shape[1]
    tsm = 1 if sm == 1 else ts
    mod_map = (lambda bi, i: (bi, 0, 0)) if sm == 1 else (lambda bi, i: (bi, i, 0))
    return pl.pallas_call(
        _route_kernel,
        grid=(b, s // ts),
        in_specs=[pl.BlockSpec((1, ts, d), lambda bi, i: (bi, i, 0)),
                  pl.BlockSpec((1, tsm, d), mod_map),
                  pl.BlockSpec((1, tsm, d), mod_map),
                  pl.BlockSpec((1, d), lambda bi, i: (0, 0)),
                  pl.BlockSpec((d, LANE), lambda bi, i: (0, 0)),
                  pl.BlockSpec((1, LANE), lambda bi, i: (0, 0))],
        out_specs=[pl.BlockSpec((1, ts * ROW_CHUNKS, LANE), lambda bi, i: (bi, i, 0)),
                   pl.BlockSpec((1, ts, LANE), lambda bi, i: (bi, i, 0)),
                   pl.BlockSpec((1, 1, 1, LANE), lambda bi, i: (bi, i, 0, 0))],
        out_shape=[jax.ShapeDtypeStruct((b, s * ROW_CHUNKS, LANE), F32), jax.ShapeDtypeStruct((b, s, LANE), F32),
                   jax.ShapeDtypeStruct((b, s // ts, 1, LANE), F32)],
        compiler_params=_cparams("parallel", "parallel"),
        name="moe_route",
    )(x1, scale, shift, gain.reshape(1, d), wr, br)


def _moe_plan(e_ids, w, counts):
    n_pairs = e_ids.size
    tm = TM_MOE
    n_tiles = -(-n_pairs // tm) + MOE_EXPERTS
    n_slots = n_tiles * tm
    n_tok = n_pairs // MOE_TOPK
    pair_ids = jnp.arange(n_pairs, dtype=jnp.int32)
    _, order, w_sorted = lax.sort((e_ids.reshape(-1), pair_ids, w.reshape(-1)), num_keys=1, is_stable=True)
    tiles_per = (counts + tm - 1) // tm
    tile_end = jnp.cumsum(tiles_per)
    tile_start = tile_end - tiles_per
    grp_start = jnp.cumsum(counts) - counts
    n_used = tile_end[-1]
    tiles = jnp.arange(n_tiles, dtype=jnp.int32)
    tile_ids = jnp.minimum(tiles, n_used - 1)
    tile_e = jnp.sum((tile_ids[:, None] >= tile_end[None, :]).astype(jnp.int32), axis=1)
    active = tiles < n_used
    rank0 = (tile_ids - tile_start[tile_e]) * tm
    flag = active.astype(jnp.int32) + (active & (rank0 == 0)).astype(jnp.int32)
    r = jnp.arange(tm, dtype=jnp.int32)[None, :]
    rank = rank0[:, None] + r
    valid = (active[:, None] & (rank < counts[tile_e][:, None])).reshape(-1)
    idx = jnp.clip(grp_start[tile_e][:, None] + rank, 0, n_pairs - 1).reshape(-1)
    pair = order[idx]
    tok = pair // MOE_TOPK
    src = jnp.where(valid, tok, 0) * ROW_CHUNKS
    pad_row = jnp.broadcast_to(n_tok + r, (n_tiles, tm)).reshape(-1)
    dst_row = jnp.where(valid, tok, pad_row) * ROW_CHUNKS
    dst_plane = jnp.where(valid, pair % MOE_TOPK, 0)
    w_slot = jnp.where(valid, w_sorted[idx], 0.0)
    return tile_e, flag, src, dst_row, dst_plane, w_slot.reshape(n_slots, 1)


def _moe_kernel(te_ref, flag_ref, src_ref, row_ref, plane_ref, h_hbm, w_ref, wg_ref, wu_ref, wd_ref, y_hbm, xbuf0,
                xbuf1, obuf0, obuf1, xs, wgb, wub, wdb, sem_in, sem_out):
    i = pl.program_id(0)
    n_tiles = pl.num_programs(0)
    flag = flag_ref[i]
    nxt = jnp.minimum(i + 1, n_tiles - 1)
    next_active = (i + 1 < n_tiles) & (flag_ref[nxt] > 0)
    xbufs, obufs = (xbuf0, xbuf1), (obuf0, obuf1)
    rc = ROW_CHUNKS

    def slab(ref_2d, first_row):
        return ref_2d.at[pl.ds(pl.multiple_of(first_row, rc), rc)]

    def start_gather(tile, sl):
        for r in range(TM_MOE):
            pltpu.make_async_copy(slab(h_hbm, src_ref[tile * TM_MOE + r]), xbufs[sl].at[pl.ds(r * rc, rc)],
                                  sem_in.at[sl]).start()

    def wait_gather(sl):
        pltpu.make_async_copy(h_hbm.at[pl.ds(0, TM_MOE * rc)], xbufs[sl], sem_in.at[sl]).wait()

    def wait_scatter(sl):
        pltpu.make_async_copy(obufs[sl], y_hbm.at[0, pl.ds(0, TM_MOE * rc)], sem_out.at[sl]).wait()

    @pl.when(i == 0)
    def _():
        obuf0[...] = jnp.zeros(obuf0.shape, F32)
        pad_rows = pl.ds(y_hbm.shape[1] - TM_MOE * rc, TM_MOE * rc)
        for plane in range(MOE_TOPK):
            pltpu.make_async_copy(obuf0, y_hbm.at[plane, pad_rows], sem_out.at[0]).start()
        for plane in range(MOE_TOPK):
            pltpu.make_async_copy(obuf0, y_hbm.at[plane, pad_rows], sem_out.at[0]).wait()
        start_gather(0, 0)

    def tile_step(sl):
        @pl.when(flag > 1)
        def _():
            wgb[...] = wg_ref[0, 0].astype(BF16)
            wub[...] = wu_ref[0, 0].astype(BF16)
            wdb[...] = wd_ref[0, 0].astype(BF16)

        @pl.when(i >= 2)
        def _():
            wait_scatter(sl)

        wait_gather(sl)
        start_gather(nxt, 1 - sl)
        rows = xbufs[sl]
        for c in range(rc):
            xs[:, c * LANE:(c + 1) * LANE] = rows[pl.ds(c, TM_MOE, stride=rc), :].astype(BF16)
        x = xs[...]
        a = jnp.dot(x, wgb[...], preferred_element_type=F32)
        u = jnp.dot(x, wub[...], preferred_element_type=F32)
        hid = (_silu(a) * u * w_ref[...]).astype(BF16)
        out = jnp.dot(hid, wdb[...], preferred_element_type=F32)
        out_rows = obufs[sl]
        for c in range(rc):
            out_rows[pl.ds(c, TM_MOE, stride=rc), :] = out[:, c * LANE:(c + 1) * LANE]
        for r in range(TM_MOE):
            k = i * TM_MOE + r
            pltpu.make_async_copy(out_rows.at[pl.ds(r * rc, rc)], slab(y_hbm.at[plane_ref[k]], row_ref[k]),
                                  sem_out.at[sl]).start()

        @pl.when(jnp.logical_not(next_active))
        def _():
            wait_gather(1 - sl)
            wait_scatter(sl)

            @pl.when(i >= 1)
            def _():
                wait_scatter(1 - sl)

    for sl in range(2):
        pl.when((flag > 0) & (i % 2 == sl))(functools.partial(tile_step, sl))


def _moe_call(h_all, plan, li, w_gate, w_up, w_down):
    t = h_all.shape[0] // ROW_CHUNKS
    d = D_MODEL
    tile_e, flag, src, dst_row, dst_plane, w_slot = plan
    n_tiles = tile_e.shape[0]
    tm = TM_MOE
    w_map = lambda i, te, *_: (li, te[i], 0, 0)
    grid_spec = pltpu.PrefetchScalarGridSpec(
        num_scalar_prefetch=5,
        grid=(n_tiles,),
        in_specs=[pl.BlockSpec(memory_space=pl.ANY),
                  pl.BlockSpec((tm, 1), lambda i, *_: (i, 0)),
                  pl.BlockSpec((1, 1, d, MOE_HIDDEN), w_map),
                  pl.BlockSpec((1, 1, d, MOE_HIDDEN), w_map),
                  pl.BlockSpec((1, 1, MOE_HIDDEN, d), w_map)],
        out_specs=pl.BlockSpec(memory_space=pl.ANY),
        scratch_shapes=[pltpu.VMEM((tm * ROW_CHUNKS, LANE), F32), pltpu.VMEM((tm * ROW_CHUNKS, LANE), F32),
                        pltpu.VMEM((tm * ROW_CHUNKS, LANE), F32), pltpu.VMEM((tm * ROW_CHUNKS, LANE), F32),
                        pltpu.VMEM((tm, d), BF16),
                        pltpu.VMEM((d, MOE_HIDDEN), BF16), pltpu.VMEM((d, MOE_HIDDEN), BF16),
                        pltpu.VMEM((MOE_HIDDEN, d), BF16),
                        pltpu.SemaphoreType.DMA((2,)), pltpu.SemaphoreType.DMA((2,))])
    return pl.pallas_call(
        _moe_kernel,
        grid_spec=grid_spec,
        out_shape=jax.ShapeDtypeStruct((MOE_TOPK, (t + tm) * ROW_CHUNKS, LANE), F32),
        compiler_params=_cparams("arbitrary"),
        name="moe_experts",
    )(tile_e, flag, src, dst_row, dst_plane, h_all, w_slot, w_gate, w_up, w_down)


def _combine_kernel(x_ref, g_ref, y_ref, o_ref):
    n_rows = x_ref.shape[1]
    for c in range(ROW_CHUNKS):
        cols = slice(c * LANE, (c + 1) * LANE)
        chunk = pl.ds(c, n_rows, stride=ROW_CHUNKS)
        o_ref[0, :, cols] = x_ref[0, :, cols] + g_ref[0, :, cols] * (y_ref[0, chunk, :] + y_ref[1, chunk, :])


def _combine_call(x1, gate, y_all, row0):
    b, s, d = x1.shape
    ts = min(s, 512)
    sm = gate.shape[1]
    tsm = 1 if sm == 1 else ts
    mod_map = (lambda bi, i: (bi, 0, 0)) if sm == 1 else (lambda bi, i: (bi, i, 0))
    blk0 = row0 // ts
    per_b = s // ts
    return pl.pallas_call(
        _combine_kernel,
        grid=(b, per_b),
        in_specs=[pl.BlockSpec((1, ts, d), lambda bi, i: (bi, i, 0)),
                  pl.BlockSpec((1, tsm, d), mod_map),
                  pl.BlockSpec((MOE_TOPK, ts * ROW_CHUNKS, LANE), lambda bi, i: (0, blk0 + bi * per_b + i, 0))],
        out_specs=pl.BlockSpec((1, ts, d), lambda bi, i: (bi, i, 0)),
        out_shape=jax.ShapeDtypeStruct((b, s, d), F32),
        compiler_params=_cparams("parallel", "parallel"),
        name="moe_combine",
    )(x1, gate, y_all)


def _reorder_w_in(w):
    sizes = (SSD_INNER, SSD_CONV_DIM, SSD_HEADS, NSA_WIDTH, 3 * KV_ROW, 3 * NSA_HEADS, RET_HEADS * RET_KEY_DIM,
             RET_HEADS * RET_KEY_DIM, RET_WIDTH, RET_WIDTH)
    z, xbc, dt, nq, nkv, ngate, rq, rk, rv, rg = jnp.split(w, [int(v) for v in np.cumsum(sizes)[:-1]], axis=1)
    padc = lambda a: jnp.pad(a, ((0, 0), (0, LANE - a.shape[1])))
    return jnp.concatenate([z, nq, rv, rg, rq, rk, xbc, nkv, padc(dt), padc(ngate)], axis=1).astype(BF16)


def _kv_rows(a):
    return a.reshape(a.shape[:2] + (2, NSA_KV_HEADS, NSA_HEAD_DIM))


def kernel(x_prompt, x_sample, cache_cmp_kv, cache_sel_kv, cache_win_kv, state_ssm, state_conv, state_ret, page_table, c_prompt, c_sample, w_ada, b_ada, norm_mix, norm_ffn, w_in, w_out, conv_w, conv_b, dt_bias, a_log, ssd_d, ssd_norm, qk_gain, cmp_pe, cmp_w1, cmp_w2, router_group_w, router_group_b, router_expert_w, router_expert_b, expert_w_gate, expert_w_up, expert_w_down):
    bp, sp, d = x_prompt.shape
    bd, sd, _ = x_sample.shape
    n_pages = page_table.shape[1]
    past = n_pages * PAGE_SIZE
    assert d == D_MODEL and sp % LC == 0 and sp % NSA_BLOCK == 0 and sp >= NSA_WINDOW
    assert sd % SUBLANE == 0 and sd <= NSA_BLOCK and sd <= LC and cache_win_kv.shape[2] == NSA_WINDOW
    n_seq = bp + bd
    c_all = jnp.pad(jnp.concatenate([c_prompt, c_sample]), ((0, -n_seq % SUBLANE), (0, 0)))
    mod = _ada_call(c_all, w_ada, b_ada)
    cos_p, sin_p = _rotary_tables(0, sp, sp)
    cos_s, sin_s = _rotary_tables(past, sd, LC)
    cmp_pages = _feature_major_pages(cache_cmp_kv)
    sel_pages = _feature_major_pages(cache_sel_kv)
    xp = x_prompt
    xs = x_sample.reshape(1, bd * sd, d)
    outs_p, outs_s = [], []
    for li in range(DEPTH):
        mods_p = [mod[li, :bp, k * d:(k + 1) * d].reshape(bp, 1, d) for k in range(6)]
        mods_s = [jnp.repeat(mod[li, bp:n_seq, k * d:(k + 1) * d], sd, axis=0).reshape(1, bd * sd, d) for k in range(6)]
        w_in_b = _reorder_w_in(w_in[li])
        w_out_b = w_out[li].astype(BF16)
        cw = _compress_weights(cmp_pe[li], cmp_w1[li], cmp_w2[li], qk_gain[li, 4])
        cw_pages = _cmp_sample_weights(cmp_pe[li], cmp_w1[li], cmp_w2[li], qk_gain[li, 4])
        ssd_w = (conv_w[li], conv_b[li], dt_bias[li], a_log[li], ssd_d[li], ssd_norm[li])
        wr = jnp.pad(jnp.concatenate([router_group_w[li], router_expert_w[li]], axis=1),
                     ((0, 0), (0, LANE - MOE_GROUPS - MOE_EXPERTS)))
        br = jnp.pad(jnp.concatenate([router_group_b[li], router_expert_b[li]]),
                     (0, LANE - MOE_GROUPS - MOE_EXPERTS)).reshape(1, LANE)

        sh_a, sc_a, g_a, sh_f, sc_f, g_f = mods_p
        proj = _inproj_call(xp, sc_a, sh_a, norm_mix[li], w_in_b)
        y_ssd, ssm_p, conv_p = _ssd_call(proj, LC, jnp.zeros((bp, SSD_CONV - 1, SSD_CONV_DIM), F32),
                                         jnp.zeros((bp, SSD_HEADS, SSD_HEAD_DIM, SSD_STATE), F32), *ssd_w)
        o_ret, ret_p = _ret_call(proj, LC, jnp.zeros((bp, RET_HEADS, RET_KEY_DIM, RET_VAL_DIM), F32), cos_p, sin_p)
        qn, kvn_p, kvb = _prep_call(proj, sp, qk_gain[li])
        kcvc = _cmp_prompt_call(kvn_p, cw)
        o_nsa = _nsa_prompt_call(proj, qn, kvb, kcvc)
        x1_p = _outproj_call(y_ssd, o_nsa, o_ret, w_out_b, xp, g_a)
        h2_p, meta_p, cnt_p = _route_call(x1_p, sc_f, sh_f, norm_ffn[li], wr, br)
        gf_p = g_f

        sh_a, sc_a, g_a, sh_f, sc_f, g_f = mods_s
        proj = _inproj_call(xs, sc_a, sh_a, norm_mix[li], w_in_b).reshape(bd, sd, N_PROJ)
        proj = jnp.pad(proj, ((0, 0), (0, LC - sd), (0, 0)))
        y_ssd, ssm_s, conv_s = _ssd_call(proj, sd, state_conv[li], state_ssm[li], *ssd_w)
        o_ret, ret_s = _ret_call(proj, sd, state_ret[li], cos_s, sin_s)
        qn, kvn_s, _ = _prep_call(proj, sd, qk_gain[li])
        kcvc = _cmp_sample_call(cmp_pages, li, page_table, cw_pages)
        win_old = cache_win_kv[li].reshape(bd, NSA_WINDOW, KV_ROW)
        o_nsa = _nsa_sample_call(sel_pages, li, page_table, proj, qn, kvn_s, kcvc, win_old, past)
        o_nsa = o_nsa.reshape(bd, NSA_HEADS, sd, NSA_HEAD_DIM).transpose(0, 2, 1, 3).reshape(1, bd * sd, NSA_WIDTH)
        x1_s = _outproj_call(y_ssd[:, :sd].reshape(1, bd * sd, SSD_INNER), o_nsa,
                             o_ret[:, :sd].reshape(1, bd * sd, RET_WIDTH), w_out_b, xs, g_a)
        h2_s, meta_s, cnt_s = _route_call(x1_s, sc_f, sh_f, norm_ffn[li], wr, br)

        h_all = jnp.concatenate([h2_p.reshape(bp * sp * ROW_CHUNKS, LANE), h2_s.reshape(bd * sd * ROW_CHUNKS, LANE)])
        meta = jnp.concatenate([meta_p.reshape(bp * sp, LANE), meta_s.reshape(bd * sd, LANE)])
        counts = (cnt_p.sum(axis=(0, 1, 2)) + cnt_s.sum(axis=(0, 1, 2)))[MOE_GROUPS:MOE_GROUPS + MOE_EXPERTS]
        plan = _moe_plan(meta[:, :MOE_TOPK].astype(jnp.int32), meta[:, MOE_TOPK:2 * MOE_TOPK],
                         counts.astype(jnp.int32))
        y_all = _moe_call(h_all, plan, li, expert_w_gate, expert_w_up, expert_w_down)
        xp = _combine_call(x1_p, gf_p, y_all, 0)
        xs = _combine_call(x1_s, g_f, y_all, bp * sp)

        win_s = jnp.concatenate([win_old[:, sd:], kvn_s[:, :, 2 * KV_ROW:]], axis=1)
        outs_p.append((_kv_rows(kvn_p[:, :, :KV_ROW]), _kv_rows(kvn_p[:, :, KV_ROW:2 * KV_ROW]),
                       _kv_rows(kvn_p[:, sp - NSA_WINDOW:, 2 * KV_ROW:]), ssm_p, conv_p, ret_p))
        outs_s.append((_kv_rows(kvn_s[:, :, :KV_ROW]), _kv_rows(kvn_s[:, :, KV_ROW:2 * KV_ROW]), _kv_rows(win_s),
                       ssm_s, conv_s, ret_s))
    res = [xp, xs.reshape(bd, sd, d)]
    for k in range(6):
        res.append(jnp.stack([o[k] for o in outs_p]))
        res.append(jnp.stack([o[k] for o in outs_s]))
    return tuple(res)
```

```python
import functools
import math

import numpy as np
import jax
import jax.numpy as jnp
from jax import lax
from jax.experimental import pallas as pl
from jax.experimental.pallas import tpu as pltpu

F32 = jnp.float32
BF16 = jnp.bfloat16
HI = lax.Precision.HIGHEST
EPS = 1e-6
NEG_INF = -1e30

D_MODEL = 2048
DEPTH = 2
PAGE_SIZE = 128
SSD_INNER = D_MODEL // 2
SSD_HEAD_DIM = 64
SSD_HEADS = SSD_INNER // SSD_HEAD_DIM
SSD_GROUPS = 2
SSD_STATE = 128
SSD_CONV = 4
SSD_CONV_DIM = SSD_INNER + 2 * SSD_GROUPS * SSD_STATE
NSA_WIDTH = D_MODEL // 4
NSA_HEAD_DIM = 64
NSA_HEADS = NSA_WIDTH // NSA_HEAD_DIM
NSA_KV_HEADS = 2
NSA_REP = NSA_HEADS // NSA_KV_HEADS
NSA_BLOCK = 64
NSA_TOPN = 16
NSA_WINDOW = 512
NSA_CMP_HIDDEN = 128
RET_WIDTH = D_MODEL // 4
RET_HEADS = 4
RET_VAL_DIM = RET_WIDTH // RET_HEADS
RET_KEY_DIM = RET_VAL_DIM // 2
D_MIX = SSD_INNER + NSA_WIDTH + RET_WIDTH
MOE_GROUPS = 4
MOE_EXPERTS_PER_GROUP = 8
MOE_EXPERTS = MOE_GROUPS * MOE_EXPERTS_PER_GROUP
MOE_TOPK = 2
MOE_HIDDEN = D_MODEL // 4
KV_ROW = 2 * NSA_KV_HEADS * NSA_HEAD_DIM

LANE = 128
SUBLANE = 8
C_Z, C_NQ, C_RV, C_RG, C_RQ, C_RK, C_XBC, C_NKV, C_DT, C_NG = 0, 1024, 1536, 2048, 2560, 2816, 3072, 4608, 5376, 5504
N_PROJ = 5632
LC = 128
TQ = 512
WIN_ROWS = 256
TM_MOE = 256
ROW_CHUNKS = D_MODEL // LANE
VMEM_LIMIT = 56 * 2 ** 20


def _cparams(*sem):
    return pltpu.CompilerParams(dimension_semantics=sem, vmem_limit_bytes=VMEM_LIMIT)


def _silu(x):
    return x * jax.nn.sigmoid(x)


def _nt(a, b, precision=None):
    return lax.dot_general(a, b, (((1,), (1,)), ((), ())), precision=precision, preferred_element_type=F32)


def _tn(a, b, precision=None):
    return lax.dot_general(a, b, (((0,), (0,)), ((), ())), precision=precision, preferred_element_type=F32)


def _iota(shape, dim):
    return lax.broadcasted_iota(jnp.int32, shape, dim)


def _seg_sum(x2, seg):
    n = x2.shape[-1]
    bd = (_iota((n, n), 0) // seg == _iota((n, n), 1) // seg).astype(BF16)
    hi = x2.astype(BF16)
    lo = (x2 - hi.astype(F32)).astype(BF16)
    return jnp.dot(hi, bd, preferred_element_type=F32) + jnp.dot(lo, bd, preferred_element_type=F32)


def _ada_kernel(c_ref, w_ref, b_ref, o_ref):
    o_ref[0] = jnp.dot(_silu(c_ref[...]), w_ref[0], precision=HI, preferred_element_type=F32) + b_ref[0]


def _ada_call(c_all, w_ada, b_ada):
    rows = c_all.shape[0]
    tn = 2048
    return pl.pallas_call(
        _ada_kernel,
        grid=(DEPTH, 6 * D_MODEL // tn),
        in_specs=[pl.BlockSpec((rows, D_MODEL), lambda l, j: (0, 0)),
                  pl.BlockSpec((1, D_MODEL, tn), lambda l, j: (l, 0, j)),
                  pl.BlockSpec((1, 1, tn), lambda l, j: (l, 0, j))],
        out_specs=pl.BlockSpec((1, rows, tn), lambda l, j: (l, 0, j)),
        out_shape=jax.ShapeDtypeStruct((DEPTH, rows, 6 * D_MODEL), F32),
        compiler_params=_cparams("parallel", "parallel"),
        name="ada",
    )(c_all, w_ada, b_ada.reshape(DEPTH, 1, 6 * D_MODEL))


def _mod_norm(x, gain, scale, shift):
    ms = jnp.mean(x * x, axis=-1, keepdims=True)
    return x * lax.rsqrt(ms + EPS) * gain * (1.0 + scale) + shift


def _inproj_kernel(x_ref, sc_ref, sh_ref, g_ref, w_ref, o_ref, h_ref):
    @pl.when(pl.program_id(2) == 0)
    def _():
        h_ref[...] = _mod_norm(x_ref[0], g_ref[...], sc_ref[0], sh_ref[0]).astype(BF16)

    o_ref[0] = jnp.dot(h_ref[...], w_ref[...], preferred_element_type=F32)


def _inproj_call(x, scale, shift, gain, w_bf):
    b, s, d = x.shape
    ts = min(s, 1024)
    tn = 512
    sm = scale.shape[1]
    tsm = 1 if sm == 1 else ts
    mod_map = (lambda bi, i, j: (bi, 0, 0)) if sm == 1 else (lambda bi, i, j: (bi, i, 0))
    return pl.pallas_call(
        _inproj_kernel,
        grid=(b, s // ts, N_PROJ // tn),
        in_specs=[pl.BlockSpec((1, ts, d), lambda bi, i, j: (bi, i, 0)),
                  pl.BlockSpec((1, tsm, d), mod_map),
                  pl.BlockSpec((1, tsm, d), mod_map),
                  pl.BlockSpec((1, d), lambda bi, i, j: (0, 0)),
                  pl.BlockSpec((d, tn), lambda bi, i, j: (0, j))],
        out_specs=pl.BlockSpec((1, ts, tn), lambda bi, i, j: (bi, i, j)),
        out_shape=jax.ShapeDtypeStruct((b, s, N_PROJ), F32),
        scratch_shapes=[pltpu.VMEM((ts, d), BF16)],
        compiler_params=_cparams("parallel", "parallel", "arbitrary"),
        name="in_proj",
    )(x, scale, shift, gain.reshape(1, d), w_bf)


def _ssd_kernel(n_valid, xbc_ref, z_ref, dt_ref, cprev_ref, sprev_ref, cw_ref, cb_ref, dtb_ref, alog_ref, dsk_ref,
                nrm_ref, exp_ref, y_ref, snew_ref, cnew_ref, cbuf, state, ybuf):
    c = pl.program_id(1)
    nc = pl.num_programs(1)
    hd, nst = SSD_HEAD_DIM, SSD_STATE
    hpg = SSD_HEADS // SSD_GROUPS

    @pl.when(c == 0)
    def _():
        cbuf[0:SUBLANE, :] = jnp.zeros((SUBLANE, SSD_CONV_DIM), F32)
        cbuf[5:8, :] = cprev_ref[0]
        state[...] = sprev_ref[0]

    cbuf[8:8 + LC, :] = xbc_ref[0]
    conv = cb_ref[...] + cw_ref[0:1, :] * cbuf[5:5 + LC, :]
    for k in range(1, SSD_CONV):
        conv = conv + cw_ref[k:k + 1, :] * cbuf[5 + k:5 + k + LC, :]
    tail = cbuf[5 + n_valid:8 + n_valid, :]
    cbuf[5:8, :] = tail
    xbc = _silu(conv)
    xs = xbc[:, :SSD_INNER]
    bm = xbc[:, SSD_INNER:SSD_INNER + SSD_GROUPS * nst].astype(BF16)
    cm = xbc[:, SSD_INNER + SSD_GROUPS * nst:].astype(BF16)

    dtr = dt_ref[0] + dtb_ref[...]
    dt = jnp.maximum(dtr, 0.0) + jnp.log1p(jnp.exp(-jnp.abs(dtr)))
    if n_valid < LC:
        dt = jnp.where(_iota((LC, LANE), 0) < n_valid, dt, 0.0)
    a = -jnp.exp(alog_ref[...])
    da = dt * a
    row = _iota((LC, LC), 0)
    col = _iota((LC, LC), 1)
    tril = row >= col
    acs = jnp.dot(tril.astype(F32), da, precision=HI, preferred_element_type=F32)
    eye = (row == col).astype(F32)
    acs_t = _nt(eye, acs, precision=HI)
    expand = exp_ref[...]

    def per_head_to_lanes(v):
        hi = v.astype(BF16)
        lo = (v - hi.astype(F32)).astype(BF16)
        return (jnp.dot(hi, expand, preferred_element_type=F32) + jnp.dot(lo, expand, preferred_element_type=F32))

    e_acs = jnp.exp(acs)
    decay = jnp.exp(acs[LC - 1:LC, :] - acs)
    dt_x = per_head_to_lanes(dt)
    e_acs_x = per_head_to_lanes(e_acs)
    decay_x = per_head_to_lanes(decay)
    xdt = xs * dt_x
    xdec = (xdt * decay_x).astype(BF16)
    xdt_b = xdt.astype(BF16)
    cbs = [_nt(cm[:, g * nst:(g + 1) * nst], bm[:, g * nst:(g + 1) * nst]) for g in range(SSD_GROUPS)]
    for h in range(SSD_HEADS):
        g = h // hpg
        lmat = jnp.exp(jnp.where(tril, acs[:, h:h + 1] - acs_t[h:h + 1, :], -jnp.inf))
        m = (cbs[g] * lmat).astype(BF16)
        y_diag = jnp.dot(m, xdt_b[:, h * hd:(h + 1) * hd], preferred_element_type=F32)
        st = state[h]
        y_off = _nt(cm[:, g * nst:(g + 1) * nst], st.astype(BF16)) * e_acs_x[:, h * hd:(h + 1) * hd]
        state[h] = st * e_acs[LC - 1:LC, h:h + 1] + _tn(xdec[:, h * hd:(h + 1) * hd], bm[:, g * nst:(g + 1) * nst])
        ybuf[:, h * hd:(h + 1) * hd] = y_diag + y_off
    y = (ybuf[...] + dsk_ref[...] * xs) * _silu(z_ref[0])
    gw = SSD_INNER // SSD_GROUPS
    for g in range(SSD_GROUPS):
        seg = y[:, g * gw:(g + 1) * gw]
        ms = jnp.mean(seg * seg, axis=-1, keepdims=True)
        y_ref[0, :, g * gw:(g + 1) * gw] = seg * lax.rsqrt(ms + EPS) * nrm_ref[:, g * gw:(g + 1) * gw]

    @pl.when(c == nc - 1)
    def _():
        snew_ref[0] = state[...]
        cnew_ref[0] = tail


def _ssd_call(proj, n_valid, conv_prev, ssm_prev, conv_w, conv_b, dt_bias, a_log, ssd_d, ssd_norm):
    b, sp, _ = proj.shape
    nc = sp // LC
    pad = lambda v: jnp.pad(v, (0, LANE - SSD_HEADS)).reshape(1, LANE)
    const2 = lambda bi, c: (0, 0)
    return pl.pallas_call(
        functools.partial(_ssd_kernel, n_valid),
        grid=(b, nc),
        in_specs=[pl.BlockSpec((1, LC, SSD_CONV_DIM), lambda bi, c: (bi, c, C_XBC // SSD_CONV_DIM)),
                  pl.BlockSpec((1, LC, SSD_INNER), lambda bi, c: (bi, c, C_Z // SSD_INNER)),
                  pl.BlockSpec((1, LC, LANE), lambda bi, c: (bi, c, C_DT // LANE)),
                  pl.BlockSpec((1, SSD_CONV - 1, SSD_CONV_DIM), lambda bi, c: (bi, 0, 0)),
                  pl.BlockSpec((1, SSD_HEADS, SSD_HEAD_DIM, SSD_STATE), lambda bi, c: (bi, 0, 0, 0)),
                  pl.BlockSpec((SSD_CONV, SSD_CONV_DIM), const2),
                  pl.BlockSpec((1, SSD_CONV_DIM), const2),
                  pl.BlockSpec((1, LANE), const2),
                  pl.BlockSpec((1, LANE), const2),
                  pl.BlockSpec((1, SSD_INNER), const2),
                  pl.BlockSpec((1, SSD_INNER), const2),
                  pl.BlockSpec((LANE, SSD_INNER), const2)],
        out_specs=[pl.BlockSpec((1, LC, SSD_INNER), lambda bi, c: (bi, c, 0)),
                   pl.BlockSpec((1, SSD_HEADS, SSD_HEAD_DIM, SSD_STATE), lambda bi, c: (bi, 0, 0, 0)),
                   pl.BlockSpec((1, SSD_CONV - 1, SSD_CONV_DIM), lambda bi, c: (bi, 0, 0))],
        out_shape=[jax.ShapeDtypeStruct((b, sp, SSD_INNER), F32),
                   jax.ShapeDtypeStruct((b, SSD_HEADS, SSD_HEAD_DIM, SSD_STATE), F32),
                   jax.ShapeDtypeStruct((b, SSD_CONV - 1, SSD_CONV_DIM), F32)],
        scratch_shapes=[pltpu.VMEM((8 + LC, SSD_CONV_DIM), F32),
                        pltpu.VMEM((SSD_HEADS, SSD_HEAD_DIM, SSD_STATE), F32),
                        pltpu.VMEM((LC, SSD_INNER), F32)],
        compiler_params=_cparams("parallel", "arbitrary"),
        name="ssd",
    )(proj, proj, proj, conv_prev, ssm_prev, conv_w, conv_b.reshape(1, -1), pad(dt_bias), pad(a_log),
      jnp.repeat(ssd_d, SSD_HEAD_DIM).reshape(1, -1), ssd_norm.reshape(1, -1),
      (np.arange(LANE)[:, None] == np.arange(SSD_INNER)[None, :] // SSD_HEAD_DIM).astype(BF16))


def _ret_log_g(h):
    return float(np.log1p(-np.exp2(np.float32(-5.0 - h)), dtype=np.float32))


def _ret_kernel(n_valid, q_ref, k_ref, v_ref, g_ref, cos_ref, sin_ref, rprev_ref, o_ref, rnew_ref, state):
    c = pl.program_id(1)
    nc = pl.num_programs(1)
    kd, vd = RET_KEY_DIM, RET_VAL_DIM
    half = kd // 2

    @pl.when(c == 0)
    def _():
        state[...] = rprev_ref[0]

    cos = cos_ref[...]
    sin = sin_ref[...]
    first_half = (_iota((LC, RET_HEADS * kd), 1) % kd) < half

    def rot(x):
        n = x.shape[-1]
        swapped = jnp.where(first_half, pltpu.roll(x, n - half, 1), pltpu.roll(x, half, 1))
        return x * cos + swapped * sin

    q = rot(q_ref[0])
    k = rot(k_ref[0]) * (kd ** -0.5)
    v = v_ref[0]
    gate = g_ref[0]
    ti = _iota((LC, LC), 0)
    tj = _iota((LC, LC), 1)
    diff = (ti - tj).astype(F32)
    ipos = _iota((LC, 1), 0)
    i1 = ipos.astype(F32)
    for h in range(RET_HEADS):
        lg = _ret_log_g(h)
        dmat = jnp.where(diff >= 0, jnp.exp(diff * lg), 0.0)
        q_dec = jnp.exp((i1 + 1.0) * lg)
        k_dec = jnp.where(ipos < n_valid, jnp.exp((n_valid - 1.0 - i1) * lg), 0.0)
        c_dec = math.exp(n_valid * lg)
        qh = q[:, h * kd:(h + 1) * kd].astype(BF16)
        kh = k[:, h * kd:(h + 1) * kd]
        vh = v[:, h * vd:(h + 1) * vd].astype(BF16)
        sc = _nt(qh, kh.astype(BF16)) * dmat
        intra = jnp.dot(sc.astype(BF16), vh, preferred_element_type=F32)
        rs = state[h]
        cross = jnp.dot(qh, rs.astype(BF16), preferred_element_type=F32) * q_dec
        state[h] = rs * c_dec + _tn((kh * k_dec).astype(BF16), vh)
        o = intra + cross
        ms = jnp.mean(o * o, axis=-1, keepdims=True)
        o_ref[0, :, h * vd:(h + 1) * vd] = o * lax.rsqrt(ms + EPS) * _silu(gate[:, h * vd:(h + 1) * vd])

    @pl.when(c == nc - 1)
    def _():
        rnew_ref[0] = state[...]


def _rotary_tables(offset, s_real, s_pad):
    half = RET_KEY_DIM // 2
    freqs = 1.0 / (10000.0 ** jnp.linspace(0.0, 1.0, half, dtype=F32))
    pos = (offset + jnp.arange(s_real)).astype(F32)
    ang = pos[:, None] * freqs[None, :]
    cos, sin = jnp.cos(ang), jnp.sin(ang)
    cos = jnp.tile(jnp.concatenate([cos, cos], axis=-1), (1, RET_HEADS))
    sin = jnp.tile(jnp.concatenate([-sin, sin], axis=-1), (1, RET_HEADS))
    padr = ((0, s_pad - s_real), (0, 0))
    return jnp.pad(cos, padr), jnp.pad(sin, padr)


def _ret_call(proj, n_valid, ret_prev, cos, sin):
    b, sp, _ = proj.shape
    nc = sp // LC
    qw = RET_HEADS * RET_KEY_DIM
    return pl.pallas_call(
        functools.partial(_ret_kernel, n_valid),
        grid=(b, nc),
        in_specs=[pl.BlockSpec((1, LC, qw), lambda bi, c: (bi, c, C_RQ // qw)),
                  pl.BlockSpec((1, LC, qw), lambda bi, c: (bi, c, C_RK // qw)),
                  pl.BlockSpec((1, LC, RET_WIDTH), lambda bi, c: (bi, c, C_RV // RET_WIDTH)),
                  pl.BlockSpec((1, LC, RET_WIDTH), lambda bi, c: (bi, c, C_RG // RET_WIDTH)),
                  pl.BlockSpec((LC, qw), lambda bi, c: (c, 0)),
                  pl.BlockSpec((LC, qw), lambda bi, c: (c, 0)),
                  pl.BlockSpec((1, RET_HEADS, RET_KEY_DIM, RET_VAL_DIM), lambda bi, c: (bi, 0, 0, 0))],
        out_specs=[pl.BlockSpec((1, LC, RET_WIDTH), lambda bi, c: (bi, c, 0)),
                   pl.BlockSpec((1, RET_HEADS, RET_KEY_DIM, RET_VAL_DIM), lambda bi, c: (bi, 0, 0, 0))],
        out_shape=[jax.ShapeDtypeStruct((b, sp, RET_WIDTH), F32),
                   jax.ShapeDtypeStruct((b, RET_HEADS, RET_KEY_DIM, RET_VAL_DIM), F32)],
        scratch_shapes=[pltpu.VMEM((RET_HEADS, RET_KEY_DIM, RET_VAL_DIM), F32)],
        compiler_params=_cparams("parallel", "arbitrary"),
        name="retention",
    )(proj, proj, proj, proj, cos, sin, ret_prev)


def _slope(h):
    return float(2.0 ** -(h + 1))


def _prep_kernel(nq_ref, nkv_ref, gq_ref, gk_ref, isk_ref, qn_ref, kvn_ref, kvb_ref):
    inv = 1.0 / NSA_HEAD_DIM
    nq = nq_ref[0]
    qn_ref[0] = nq * lax.rsqrt(_seg_sum(nq * nq, NSA_HEAD_DIM) * inv + EPS) * gq_ref[...]
    kv = nkv_ref[0]
    normed = kv * lax.rsqrt(_seg_sum(kv * kv, NSA_HEAD_DIM) * inv + EPS) * gk_ref[...]
    kvn = jnp.where(isk_ref[...] > 0.5, normed, kv)
    kvn_ref[0] = kvn
    kvb_ref[0] = kvn[:, KV_ROW:].astype(BF16)


def _prep_call(proj, s_real, qk_gain):
    b = proj.shape[0]
    ts = min(s_real, 512)
    hd = NSA_HEAD_DIM
    gq = (jnp.tile(qk_gain[0], NSA_HEADS) * (hd ** -0.5)).reshape(1, -1)
    ones = jnp.ones((2 * hd,), F32)
    gk = jnp.concatenate([jnp.concatenate([jnp.tile(qk_gain[1 + i], 2), ones]) for i in range(3)]).reshape(1, -1)
    isk = jnp.tile(jnp.concatenate([ones, 0.0 * ones]), 3).reshape(1, -1)
    const2 = lambda bi, i: (0, 0)
    return pl.pallas_call(
        _prep_kernel,
        grid=(b, s_real // ts),
        in_specs=[pl.BlockSpec((1, ts, NSA_WIDTH), lambda bi, i: (bi, i, C_NQ // NSA_WIDTH)),
                  pl.BlockSpec((1, ts, 3 * KV_ROW), lambda bi, i: (bi, i, C_NKV // (3 * KV_ROW))),
                  pl.BlockSpec((1, NSA_WIDTH), const2),
                  pl.BlockSpec((1, 3 * KV_ROW), const2),
                  pl.BlockSpec((1, 3 * KV_ROW), const2)],
        out_specs=[pl.BlockSpec((1, ts, NSA_WIDTH), lambda bi, i: (bi, i, 0)),
                   pl.BlockSpec((1, ts, 3 * KV_ROW), lambda bi, i: (bi, i, 0)),
                   pl.BlockSpec((1, ts, 2 * KV_ROW), lambda bi, i: (bi, i, 0))],
        out_shape=[jax.ShapeDtypeStruct((b, s_real, NSA_WIDTH), F32),
                   jax.ShapeDtypeStruct((b, s_real, 3 * KV_ROW), F32),
                   jax.ShapeDtypeStruct((b, s_real, 2 * KV_ROW), BF16)],
        compiler_params=_cparams("parallel", "parallel"),
        name="nsa_prep",
    )(proj, proj, gq, gk, isk)


def _compress_weights(cmp_pe, cmp_w1, cmp_w2, gain4):
    hd = NSA_HEAD_DIM

    def block_diag4(a0, a1):
        z = jnp.zeros_like(a0)
        rows = [[a0, z, z, z], [z, a0, z, z], [z, z, a1, z], [z, z, z, a1]]
        return jnp.concatenate([jnp.concatenate(r, axis=-1) for r in rows], axis=-2)

    w1 = cmp_w1.astype(BF16)
    w2 = cmp_w2.astype(BF16)
    wbig = block_diag4(w1[0], w1[1])
    w2big = block_diag4(w2[0], w2[1])
    pe_row = jnp.concatenate([cmp_pe[0], cmp_pe[0], cmp_pe[1], cmp_pe[1]], axis=-1)
    g4 = jnp.concatenate([jnp.tile(gain4, 2), jnp.ones((2 * hd,), F32)]).reshape(1, -1)
    return pe_row, wbig, w2big, g4


def _compress_core(get_rows, nblk, pe_ref, wbig_ref, w2_ref, g4_ref):
    acc = jnp.zeros((nblk, 4 * NSA_CMP_HIDDEN), F32)
    for l in range(NSA_BLOCK):
        x = get_rows(l) + pe_ref[l:l + 1, :]
        acc = acc + jnp.dot(x.astype(BF16), wbig_ref[l], preferred_element_type=F32)
    out = jnp.dot(_silu(acc).astype(BF16), w2_ref[...], preferred_element_type=F32)
    ss = _seg_sum(out * out, NSA_HEAD_DIM)
    normed = out * lax.rsqrt(ss * (1.0 / NSA_HEAD_DIM) + EPS) * g4_ref[...]
    return jnp.where(_iota(out.shape, 1) < 2 * NSA_HEAD_DIM, normed, out)


def _cmp_prompt_kernel(nblk, kv_ref, pe_ref, wbig_ref, w2_ref, g4_ref, o_ref):
    o_ref[0] = _compress_core(lambda l: kv_ref[0, :, l, :], nblk, pe_ref, wbig_ref, w2_ref, g4_ref)


def _cmp_prompt_call(kvn, cw):
    b, s, _ = kvn.shape
    nblk = s // NSA_BLOCK
    pe_row, wbig, w2big, g4 = cw
    kv4 = kvn.reshape(b, nblk, NSA_BLOCK, 3 * KV_ROW)
    return pl.pallas_call(
        functools.partial(_cmp_prompt_kernel, nblk),
        grid=(b,),
        in_specs=[pl.BlockSpec((1, nblk, NSA_BLOCK, KV_ROW), lambda bi: (bi, 0, 0, 0)),
                  pl.BlockSpec(pe_row.shape, lambda bi: (0, 0)),
                  pl.BlockSpec(wbig.shape, lambda bi: (0, 0, 0)),
                  pl.BlockSpec(w2big.shape, lambda bi: (0, 0)),
                  pl.BlockSpec(g4.shape, lambda bi: (0, 0))],
        out_specs=pl.BlockSpec((1, nblk, KV_ROW), lambda bi: (bi, 0, 0)),
        out_shape=jax.ShapeDtypeStruct((b, nblk, KV_ROW), F32),
        compiler_params=_cparams("arbitrary"),
        name="nsa_cmp_prompt",
    )(kv4, pe_row, wbig, w2big, g4)


def _masked_softmax(s, mask):
    s = jnp.where(mask, s, NEG_INF)
    p = jnp.exp(s - jnp.max(s, axis=-1, keepdims=True))
    return p / jnp.sum(p, axis=-1, keepdims=True) * mask.astype(F32)


def _select_blocks(imp, blk, n_top, axis=-1):
    sel = jnp.zeros(imp.shape, F32)
    big = imp.shape[axis]
    for _ in range(n_top):
        m = jnp.max(imp, axis=axis, keepdims=True)
        idx = jnp.min(jnp.where(imp == m, blk, big), axis=axis, keepdims=True)
        hit = blk == idx
        sel = jnp.where(hit & (m >= 0.0), 1.0, sel)
        imp = jnp.where(hit, -2.0, imp)
    return sel


M_FLOOR = -1e29


def _nsa_prompt_kernel(nbc, s_len, tk, wk, q_ref, gate_ref, cmp_ref, ksel_ref, kwin_ref, o_ref, bias_ref):
    qi = pl.program_id(1)
    hd, rep, nh = NSA_HEAD_DIM, NSA_REP, NSA_HEADS
    q = q_ref[0]
    qb = q.astype(BF16)
    gates = jax.nn.sigmoid(gate_ref[0])
    t = qi * TQ + _iota((TQ, 1), 0)
    t_row = qi * TQ + _iota((1, TQ), 1)
    kcvc = cmp_ref[0]
    blk = _iota((nbc, TQ), 0)
    dist_c = t_row - ((blk + 1) * NSA_BLOCK - 1)
    mask_c = dist_c >= 0
    mask_cf = mask_c.astype(F32)
    dist_cf = dist_c.astype(F32)
    cur = t_row // NSA_BLOCK
    o_cmp = []
    imps = []
    n_kt = (qi * TQ + TQ + tk - 1) // tk
    for g in range(NSA_KV_HEADS):
        kc = kcvc[:, g * hd:(g + 1) * hd]
        vc = kcvc[:, 2 * hd + g * hd:2 * hd + (g + 1) * hd]
        imp = jnp.zeros((nbc, TQ), F32)
        for r in range(rep):
            h = g * rep + r
            s = jnp.where(mask_c, _nt(kc, q[:, h * hd:(h + 1) * hd], precision=HI) - _slope(h) * dist_cf, NEG_INF)
            p = jnp.exp(s - jnp.max(s, axis=0, keepdims=True))
            p = p / jnp.sum(p, axis=0, keepdims=True) * mask_cf
            imp = imp + p
            o_cmp.append(_tn(p, vc, precision=HI))
        imps.append(jnp.where(blk < cur, imp, -1.0))
    blk_all = jnp.concatenate([blk] * NSA_KV_HEADS, axis=1)
    is_cur = (blk_all == jnp.concatenate([cur] * NSA_KV_HEADS, axis=1)).astype(F32)
    sel_all = jnp.maximum(_select_blocks(jnp.concatenate(imps, axis=1), blk_all, NSA_TOPN - 1, axis=0), is_cur)
    sels = []
    first_kt = []
    bpt = tk // NSA_BLOCK
    for g in range(NSA_KV_HEADS):
        sel = sel_all[:, g * TQ:(g + 1) * TQ]
        sels.append(sel.astype(BF16))
        first = n_kt
        for kt in reversed(range(nbc // bpt)):
            first = jnp.where(jnp.max(sel[kt * bpt:(kt + 1) * bpt, :]) > 0.0, kt, first)
        first_kt.append(first)

    def fill(kt, carry):
        k0 = pl.multiple_of(kt * tk, tk)
        causal = t >= k0 + _iota((TQ, tk), 1)
        expand = (_iota((nbc, tk), 0) == (k0 + _iota((nbc, tk), 1)) // NSA_BLOCK).astype(BF16)
        for g in range(NSA_KV_HEADS):
            picked = _tn(sels[g], expand) > 0.5
            bias_ref[g, :, pl.ds(k0, tk)] = jnp.where(picked & causal, 0.0, NEG_INF).astype(BF16)
        return carry

    lax.fori_loop(jnp.minimum(first_kt[0], first_kt[1]), n_kt, fill, 0)

    rows_g = rep * TQ
    for g in range(NSA_KV_HEADS):
        heads = range(g * rep, (g + 1) * rep)

        def stacked(bias, dist, heads=heads):
            return jnp.concatenate([bias - _slope(h) * dist for h in heads], axis=0)

        qg = jnp.concatenate([qb[:, h * hd:(h + 1) * hd] for h in heads], axis=0)

        def body(kt, carry, g=g, qg=qg, stacked=stacked):
            m, l, acc = carry
            k0 = pl.multiple_of(kt * tk, tk)
            kk = ksel_ref[0, pl.ds(k0, tk), g * hd:(g + 1) * hd]
            vv = ksel_ref[0, pl.ds(k0, tk), 2 * hd + g * hd:2 * hd + (g + 1) * hd]
            dist = (t - (k0 + _iota((TQ, tk), 1))).astype(F32)
            s = _nt(qg, kk) + stacked(bias_ref[g, :, pl.ds(k0, tk)].astype(F32), dist)
            m_new = jnp.maximum(m, jnp.max(s, axis=-1, keepdims=True))
            alpha = jnp.exp(m - m_new)
            p = jnp.exp(s - m_new)
            l = alpha * l + jnp.sum(p, axis=-1, keepdims=True)
            acc = alpha * acc + jnp.dot(p.astype(BF16), vv, preferred_element_type=F32)
            return m_new, l, acc

        init = (jnp.full((rows_g, 1), M_FLOOR, F32), jnp.zeros((rows_g, 1), F32), jnp.zeros((rows_g, hd), F32))
        _, l_s, acc_s = lax.fori_loop(first_kt[g], n_kt, body, init)
        o_sel = acc_s / l_s
        o_win = []
        for w0 in range(0, TQ, WIN_ROWS):
            start_w = pl.multiple_of(jnp.maximum(qi * TQ + w0 + WIN_ROWS - wk, 0), WIN_ROWS)
            kw_all = kwin_ref[0, pl.ds(start_w, wk), :]
            dist_w = t[w0:w0 + WIN_ROWS] - (start_w + _iota((WIN_ROWS, wk), 1))
            bias_w = jnp.where((dist_w >= 0) & (dist_w <= NSA_WINDOW), 0.0, NEG_INF)
            qw = jnp.concatenate([qb[w0:w0 + WIN_ROWS, h * hd:(h + 1) * hd] for h in heads], axis=0)
            s_w = _nt(qw, kw_all[:, g * hd:(g + 1) * hd]) + stacked(bias_w, dist_w.astype(F32))
            p_w = jnp.exp(s_w - jnp.max(s_w, axis=-1, keepdims=True))
            o_win.append(jnp.dot(p_w.astype(BF16), kw_all[:, 2 * hd + g * hd:2 * hd + (g + 1) * hd],
                                 preferred_element_type=F32) / jnp.sum(p_w, axis=-1, keepdims=True))
        for j, h in enumerate(heads):
            win_h = jnp.concatenate([o[j * WIN_ROWS:(j + 1) * WIN_ROWS] for o in o_win], axis=0)
            o_ref[0, :, h * hd:(h + 1) * hd] = (gates[:, h:h + 1] * o_cmp[h]
                                                + gates[:, nh + h:nh + h + 1] * o_sel[j * TQ:(j + 1) * TQ]
                                                + gates[:, 2 * nh + h:2 * nh + h + 1] * win_h)


def _nsa_prompt_call(proj, qn, kvb, kcvc):
    b, s, _ = qn.shape
    nbc = kcvc.shape[1]
    tk = min(512, s)
    wk = min(NSA_WINDOW + WIN_ROWS, s)
    return pl.pallas_call(
        functools.partial(_nsa_prompt_kernel, nbc, s, tk, wk),
        grid=(b, s // TQ),
        in_specs=[pl.BlockSpec((1, TQ, NSA_WIDTH), lambda bi, i: (bi, i, 0)),
                  pl.BlockSpec((1, TQ, LANE), lambda bi, i: (bi, i, C_NG // LANE)),
                  pl.BlockSpec((1, nbc, KV_ROW), lambda bi, i: (bi, 0, 0)),
                  pl.BlockSpec((1, s, KV_ROW), lambda bi, i: (bi, 0, 0)),
                  pl.BlockSpec((1, s, KV_ROW), lambda bi, i: (bi, 0, 1))],
        out_specs=pl.BlockSpec((1, TQ, NSA_WIDTH), lambda bi, i: (bi, i, 0)),
        out_shape=jax.ShapeDtypeStruct((b, s, NSA_WIDTH), F32),
        scratch_shapes=[pltpu.VMEM((NSA_KV_HEADS, TQ, s), BF16)],
        compiler_params=_cparams("parallel", "arbitrary"),
        name="nsa_prompt",
    )(qn, proj, kcvc, kvb, kvb)


def _feature_major_pages(cache):
    return cache.transpose(0, 1, 3, 4, 5, 2).reshape(cache.shape[0], cache.shape[1], KV_ROW, PAGE_SIZE)


def _page_copies(cache_hbm, li, pt_ref, b, step, dst_of, sem, slot, pps):
    return [pltpu.make_async_copy(cache_hbm.at[li, pt_ref[b, step * pps + k]], dst_of(slot, k), sem.at[slot])
            for k in range(pps)]


def _pipelined_pages(cache_hbm, li, pt_ref, dst_of, sem, pps):
    b, s, ns = pl.program_id(0), pl.program_id(1), pl.num_programs(1)
    slot = s % 2
    copies = functools.partial(_page_copies, cache_hbm, li, pt_ref, b, dst_of=dst_of, sem=sem, pps=pps)

    @pl.when(s == 0)
    def _():
        for cp in copies(step=s, slot=slot):
            cp.start()

    @pl.when(s + 1 < ns)
    def _():
        for cp in copies(step=s + 1, slot=1 - slot):
            cp.start()

    for cp in copies(step=s, slot=slot):
        cp.wait()
    return slot


def _cmp_sample_weights(cmp_pe, cmp_w1, cmp_w2, gain4):
    wt = cmp_w1.transpose(0, 2, 1, 3).astype(BF16)
    z = jnp.zeros_like(wt)
    w1 = jnp.concatenate([jnp.concatenate([wt, z], axis=-1), jnp.concatenate([z, wt], axis=-1)], axis=2)
    w1 = w1.reshape(2, NSA_HEAD_DIM // 2, 4 * NSA_BLOCK, 2 * NSA_CMP_HIDDEN)
    w2 = cmp_w2.astype(BF16)
    z2 = jnp.zeros_like(w2)
    w2bd = jnp.concatenate([jnp.concatenate([w2, z2], axis=-1), jnp.concatenate([z2, w2], axis=-1)], axis=1)
    pe_t = jnp.tile(cmp_pe.transpose(0, 2, 1), (1, 1, 2)).reshape(2 * NSA_HEAD_DIM, 2 * NSA_BLOCK)
    return pe_t, w1, w2bd, jnp.tile(gain4, 2).reshape(1, -1)


def _cmp_sample_kernel(li, pps, pt_ref, cache_hbm, pe_ref, w1_ref, w2_ref, g_ref, o_ref, buf, sem):
    slot = _pipelined_pages(cache_hbm, li, pt_ref, lambda sl, k: buf.at[sl, k], sem, pps)
    pages = buf.at[slot]
    hd = NSA_HEAD_DIM
    for c in range(2):
        acc = jnp.zeros((NSA_KV_HEADS * pps, 2 * NSA_CMP_HIDDEN), F32)
        def feature_rows(d):
            x = jnp.concatenate([pages[:, (2 * c + g) * hd + d, :] for g in range(NSA_KV_HEADS)], axis=0)
            return (x + pe_ref[c * hd + d:c * hd + d + 1, :]).astype(BF16)

        for d2 in range(hd // 2):
            x = jnp.concatenate([feature_rows(2 * d2), feature_rows(2 * d2 + 1)], axis=1)
            acc = acc + jnp.dot(x, w1_ref[c, d2], preferred_element_type=F32)
        out = jnp.dot(_silu(acc).astype(BF16), w2_ref[c], preferred_element_type=F32)
        if c == 0:
            out = out * lax.rsqrt(_seg_sum(out * out, hd) * (1.0 / hd) + EPS) * g_ref[...]
        for g in range(NSA_KV_HEADS):
            o_ref[0, :, (2 * c + g) * 2 * hd:(2 * c + g + 1) * 2 * hd] = out[g * pps:(g + 1) * pps]


def _cmp_sample_call(cache_t, li, page_table, cw):
    bd, n_pages = page_table.shape
    pps = math.gcd(n_pages, 64)
    pe_t, w1, w2bd, gain = cw
    hd = NSA_HEAD_DIM
    grid_spec = pltpu.PrefetchScalarGridSpec(
        num_scalar_prefetch=1,
        grid=(bd, n_pages // pps),
        in_specs=[pl.BlockSpec(memory_space=pl.ANY),
                  pl.BlockSpec(pe_t.shape, lambda b, s, pt: (0, 0)),
                  pl.BlockSpec(w1.shape, lambda b, s, pt: (0, 0, 0, 0)),
                  pl.BlockSpec(w2bd.shape, lambda b, s, pt: (0, 0, 0)),
                  pl.BlockSpec(gain.shape, lambda b, s, pt: (0, 0))],
        out_specs=pl.BlockSpec((1, pps, 2 * KV_ROW), lambda b, s, pt: (b, s, 0)),
        scratch_shapes=[pltpu.VMEM((2, pps, KV_ROW, PAGE_SIZE), F32), pltpu.SemaphoreType.DMA((2,))])
    packed = pl.pallas_call(
        functools.partial(_cmp_sample_kernel, li, pps),
        grid_spec=grid_spec,
        out_shape=jax.ShapeDtypeStruct((bd, n_pages, 2 * KV_ROW), F32),
        compiler_params=_cparams("arbitrary", "arbitrary"),
        name="nsa_cmp_sample",
    )(page_table, cache_t, pe_t, w1, w2bd, gain)
    return packed.reshape(bd, n_pages, 4, 2, hd).transpose(0, 1, 3, 2, 4).reshape(bd, 2 * n_pages, KV_ROW)


def _pick_group(x, rows_g0):
    w = x.shape[-1] // 2
    return jnp.where(rows_g0, x[:, :w], x[:, w:])


def _nsa_sample_kernel(li, pps, past, n_tok, pt_ref, cache_hbm, q_ref, gate_ref, cmp_ref, knew_ref, wold_ref, wnew_ref,
                       o_ref, buf, sem, qbd_ref, sel_ref, ocw_ref, m_ref, l_ref, acc_ref):
    s, ns = pl.program_id(1), pl.num_programs(1)
    hd, nh, rep = NSA_HEAD_DIM, NSA_HEADS, NSA_REP
    rows = nh * n_tok
    nbc = past // NSA_BLOCK
    tk = pps * PAGE_SIZE
    slot = _pipelined_pages(cache_hbm, li, pt_ref,
                            lambda sl, k: buf.at[sl, :, pl.ds(k * PAGE_SIZE, PAGE_SIZE)], sem, pps)
    ridx = _iota((rows, 1), 0)
    rows_g0 = ridx < rep * n_tok
    tok = ridx % n_tok
    t = past + tok
    slope = jnp.exp2(-(ridx // n_tok + 1).astype(F32))

    @pl.when(s == 0)
    def _():
        qtok = q_ref[0]
        qrows = jnp.concatenate([qtok[:, h * hd:(h + 1) * hd] for h in range(nh)], axis=0)
        zero = jnp.zeros_like(qrows)
        qbd = jnp.where(rows_g0, jnp.concatenate([qrows, zero], axis=1), jnp.concatenate([zero, qrows], axis=1))
        qbd_ref[...] = qbd
        qbb = qbd.astype(BF16)
        kcvc = cmp_ref[0]
        blk = _iota((rows, nbc), 1)
        dist_c = t - ((blk + 1) * NSA_BLOCK - 1)
        p_c = _masked_softmax(_nt(qbd, kcvc[:, :2 * hd], precision=HI) - slope * dist_c.astype(F32), dist_c >= 0)
        o_c = _pick_group(jnp.dot(p_c, kcvc[:, 2 * hd:], precision=HI, preferred_element_type=F32), rows_g0)
        blk_t = _iota((n_tok, nbc), 1)
        cur_t = (past + _iota((n_tok, 1), 0)) // NSA_BLOCK
        sels = []
        for g in range(NSA_KV_HEADS):
            imp = p_c[g * rep * n_tok:(g * rep + 1) * n_tok]
            for r in range(1, rep):
                imp = imp + p_c[(g * rep + r) * n_tok:(g * rep + r + 1) * n_tok]
            imp = jnp.where(blk_t < cur_t, imp, -1.0)
            sels += [_select_blocks(imp, blk_t, NSA_TOPN - 1)] * rep
        sel_ref[...] = jnp.concatenate(sels, axis=0).astype(BF16)
        knew = knew_ref[0]
        dist_n = tok - _iota((rows, n_tok), 1)
        mask_n = dist_n >= 0
        s_n = jnp.where(mask_n, _nt(qbb, knew[:, :2 * hd].astype(BF16)) - slope * dist_n.astype(F32), NEG_INF)
        m0 = jnp.max(s_n, axis=-1, keepdims=True)
        p_n = jnp.where(mask_n, jnp.exp(s_n - m0), 0.0)
        m_ref[...] = m0
        l_ref[...] = jnp.sum(p_n, axis=-1, keepdims=True)
        acc_ref[...] = jnp.dot(p_n.astype(BF16), knew[:, 2 * hd:].astype(BF16), preferred_element_type=F32)
        kw = jnp.concatenate([wold_ref[0], wnew_ref[0]], axis=0)
        dist_w = tok + NSA_WINDOW - _iota((rows, NSA_WINDOW + n_tok), 1)
        mask_w = (dist_w >= 0) & (dist_w <= NSA_WINDOW)
        p_w = _masked_softmax(_nt(qbb, kw[:, :2 * hd].astype(BF16)) - slope * dist_w.astype(F32), mask_w)
        o_w = _pick_group(jnp.dot(p_w.astype(BF16), kw[:, 2 * hd:].astype(BF16), preferred_element_type=F32), rows_g0)
        ocw_ref[0] = o_c
        ocw_ref[1] = o_w

    pages = buf.at[slot]
    k_t = pages[:2 * hd, :].astype(BF16)
    v_t = pages[2 * hd:, :].astype(BF16)
    k0 = s * tk
    expand = (_iota((nbc, tk), 0) == (k0 + _iota((nbc, tk), 1)) // NSA_BLOCK).astype(BF16)
    mask = jnp.dot(sel_ref[...], expand, preferred_element_type=F32) > 0.5
    dist = t - (k0 + _iota((rows, tk), 1))
    qk = jnp.dot(qbd_ref[...].astype(BF16), k_t, preferred_element_type=F32)
    sc = jnp.where(mask, qk - slope * dist.astype(F32), NEG_INF)
    m_old = m_ref[...]
    m_new = jnp.maximum(m_old, jnp.max(sc, axis=-1, keepdims=True))
    alpha = jnp.exp(m_old - m_new)
    p = jnp.where(mask, jnp.exp(sc - m_new), 0.0)
    m_ref[...] = m_new
    l_ref[...] = alpha * l_ref[...] + jnp.sum(p, axis=-1, keepdims=True)
    acc_ref[...] = alpha * acc_ref[...] + _nt(p.astype(BF16), v_t)

    @pl.when(s == ns - 1)
    def _():
        o_s = _pick_group(acc_ref[...] / l_ref[...], rows_g0)
        gates = jax.nn.sigmoid(gate_ref[0])

        def gate_rows(br):
            return jnp.concatenate([gates[:, br * nh + h:br * nh + h + 1] for h in range(nh)], axis=0)

        o_ref[0] = gate_rows(0) * ocw_ref[0] + gate_rows(1) * o_s + gate_rows(2) * ocw_ref[1]


def _nsa_sample_call(cache_t, li, page_table, proj, qn, kvn, kcvc, win_old, past):
    bd, n_pages = page_table.shape
    n_tok = qn.shape[1]
    pps = math.gcd(n_pages, 32)
    rows = NSA_HEADS * n_tok
    nbc = kcvc.shape[1]
    grid_spec = pltpu.PrefetchScalarGridSpec(
        num_scalar_prefetch=1,
        grid=(bd, n_pages // pps),
        in_specs=[pl.BlockSpec(memory_space=pl.ANY),
                  pl.BlockSpec((1, n_tok, NSA_WIDTH), lambda b, s, pt: (b, 0, 0)),
                  pl.BlockSpec((1, n_tok, LANE), lambda b, s, pt: (b, 0, C_NG // LANE)),
                  pl.BlockSpec((1, nbc, KV_ROW), lambda b, s, pt: (b, 0, 0)),
                  pl.BlockSpec((1, n_tok, KV_ROW), lambda b, s, pt: (b, 0, 1)),
                  pl.BlockSpec((1, NSA_WINDOW, KV_ROW), lambda b, s, pt: (b, 0, 0)),
                  pl.BlockSpec((1, n_tok, KV_ROW), lambda b, s, pt: (b, 0, 2))],
        out_specs=pl.BlockSpec((1, rows, NSA_HEAD_DIM), lambda b, s, pt: (b, 0, 0)),
        scratch_shapes=[pltpu.VMEM((2, KV_ROW, pps * PAGE_SIZE), F32),
                        pltpu.SemaphoreType.DMA((2,)),
                        pltpu.VMEM((rows, 2 * NSA_HEAD_DIM), F32),
                        pltpu.VMEM((rows, nbc), BF16),
                        pltpu.VMEM((2, rows, NSA_HEAD_DIM), F32),
                        pltpu.VMEM((rows, 1), F32),
                        pltpu.VMEM((rows, 1), F32),
                        pltpu.VMEM((rows, 2 * NSA_HEAD_DIM), F32)])
    return pl.pallas_call(
        functools.partial(_nsa_sample_kernel, li, pps, past, n_tok),
        grid_spec=grid_spec,
        out_shape=jax.ShapeDtypeStruct((bd, rows, NSA_HEAD_DIM), F32),
        compiler_params=_cparams("arbitrary", "arbitrary"),
        name="nsa_sample",
    )(page_table, cache_t, qn, proj, kcvc, kvn, win_old, kvn)


def _outproj_kernel(y_ref, n_ref, r_ref, w_ref, x_ref, g_ref, o_ref, mix_ref):
    @pl.when(pl.program_id(2) == 0)
    def _():
        mix_ref[:, :SSD_INNER] = y_ref[0].astype(BF16)
        mix_ref[:, SSD_INNER:SSD_INNER + NSA_WIDTH] = n_ref[0].astype(BF16)
        mix_ref[:, SSD_INNER + NSA_WIDTH:] = r_ref[0].astype(BF16)

    o_ref[0] = x_ref[0] + g_ref[0] * jnp.dot(mix_ref[...], w_ref[...], preferred_element_type=F32)


def _outproj_call(y, o_nsa, o_ret, w_bf, x, gate):
    b, s, d = x.shape
    ts = min(s, 1024)
    tn = 512
    sm = gate.shape[1]
    tsm = 1 if sm == 1 else ts
    mod_map = (lambda bi, i, j: (bi, 0, j)) if sm == 1 else (lambda bi, i, j: (bi, i, j))
    row_map = lambda bi, i, j: (bi, i, 0)
    return pl.pallas_call(
        _outproj_kernel,
        grid=(b, s // ts, d // tn),
        in_specs=[pl.BlockSpec((1, ts, SSD_INNER), row_map),
                  pl.BlockSpec((1, ts, NSA_WIDTH), row_map),
                  pl.BlockSpec((1, ts, RET_WIDTH), row_map),
                  pl.BlockSpec((D_MIX, tn), lambda bi, i, j: (0, j)),
                  pl.BlockSpec((1, ts, tn), lambda bi, i, j: (bi, i, j)),
                  pl.BlockSpec((1, tsm, tn), mod_map)],
        out_specs=pl.BlockSpec((1, ts, tn), lambda bi, i, j: (bi, i, j)),
        out_shape=jax.ShapeDtypeStruct((b, s, d), F32),
        scratch_shapes=[pltpu.VMEM((ts, D_MIX), BF16)],
        compiler_params=_cparams("parallel", "parallel", "arbitrary"),
        name="out_proj",
    )(y, o_nsa, o_ret, w_bf, x, gate)


def _route_kernel(x_ref, sc_ref, sh_ref, g_ref, wr_ref, br_ref, h_ref, meta_ref, cnt_ref):
    h = _mod_norm(x_ref[0], g_ref[...], sc_ref[0], sh_ref[0])
    n_rows = h.shape[0]
    for c in range(ROW_CHUNKS):
        h_ref[0, pl.ds(c, n_rows, stride=ROW_CHUNKS), :] = h[:, c * LANE:(c + 1) * LANE]
    logit = jnp.dot(h, wr_ref[...], precision=HI, preferred_element_type=F32) + br_ref[...]
    lane = _iota(logit.shape, 1)
    first = lambda hit: jnp.min(jnp.where(hit, lane, LANE), axis=-1, keepdims=True)
    is_g = lane < MOE_GROUPS
    gl = jnp.where(is_g, logit, -jnp.inf)
    gmax = jnp.max(gl, axis=-1, keepdims=True)
    g_idx = first(gl == gmax)
    g_p = 1.0 / jnp.sum(jnp.where(is_g, jnp.exp(logit - gmax), 0.0), axis=-1, keepdims=True)
    e_lane = lane - MOE_GROUPS
    in_grp = (e_lane >= 0) & (e_lane < MOE_EXPERTS) & (e_lane // MOE_EXPERTS_PER_GROUP == g_idx)
    el = jnp.where(in_grp, logit, -jnp.inf)
    v1 = jnp.max(el, axis=-1, keepdims=True)
    i1 = first(el == v1)
    el = jnp.where(lane == i1, -jnp.inf, el)
    v2 = jnp.max(el, axis=-1, keepdims=True)
    i2 = first(el == v2)
    e = jnp.exp(v2 - v1)
    w1 = g_p / (1.0 + e)
    w2 = g_p * e / (1.0 + e)
    meta = jnp.where(lane == 0, (i1 - MOE_GROUPS).astype(F32), 0.0)
    meta = jnp.where(lane == 1, (i2 - MOE_GROUPS).astype(F32), meta)
    meta = jnp.where(lane == 2, w1, meta)
    meta_ref[0] = jnp.where(lane == 3, w2, meta)
    cnt_ref[0, 0] = jnp.sum(((lane == i1) | (lane == i2)).astype(F32), axis=0, keepdims=True)


def _route_call(x1, scale, shift, gain, wr, br):
    b, s, d = x1.shape
    ts = min(s, 512)
    sm = scale.shape[1]
    tsm = 1 if sm == 1 else ts
    mod_map = (lambda bi, i: (bi, 0, 0)) if sm == 1 else (lambda bi, i: (bi, i, 0))
    return pl.pallas_call(
        _route_kernel,
        grid=(b, s // ts),
        in_specs=[pl.BlockSpec((1, ts, d), lambda bi, i: (bi, i, 0)),
                  pl.BlockSpec((1, tsm, d), mod_map),
                  pl.BlockSpec((1, tsm, d), mod_map),
                  pl.BlockSpec((1, d), lambda bi, i: (0, 0)),
                  pl.BlockSpec((d, LANE), lambda bi, i: (0, 0)),
                  pl.BlockSpec((1, LANE), lambda bi, i: (0, 0))],
        out_specs=[pl.BlockSpec((1, ts * ROW_CHUNKS, LANE), lambda bi, i: (bi, i, 0)),
                   pl.BlockSpec((1, ts, LANE), lambda bi, i: (bi, i, 0)),
                   pl.BlockSpec((1, 1, 1, LANE), lambda bi, i: (bi, i, 0, 0))],
        out_shape=[jax.ShapeDtypeStruct((b, s * ROW_CHUNKS, LANE), F32), jax.ShapeDtypeStruct((b, s, LANE), F32),
                   jax.ShapeDtypeStruct((b, s // ts, 1, LANE), F32)],
        compiler_params=_cparams("parallel", "parallel"),
        name="moe_route",
    )(x1, scale, shift, gain.reshape(1, d), wr, br)


def _moe_plan(e_ids, w, counts):
    n_pairs = e_ids.size
    tm = TM_MOE
    n_tiles = -(-n_pairs // tm) + MOE_EXPERTS
    n_slots = n_tiles * tm
    n_tok = n_pairs // MOE_TOPK
    pair_ids = jnp.arange(n_pairs, dtype=jnp.int32)
    _, order, w_sorted = lax.sort((e_ids.reshape(-1), pair_ids, w.reshape(-1)), num_keys=1, is_stable=True)
    tiles_per = (counts + tm - 1) // tm
    tile_end = jnp.cumsum(tiles_per)
    tile_start = tile_end - tiles_per
    grp_start = jnp.cumsum(counts) - counts
    n_used = tile_end[-1]
    tiles = jnp.arange(n_tiles, dtype=jnp.int32)
    tile_ids = jnp.minimum(tiles, n_used - 1)
    tile_e = jnp.sum((tile_ids[:, None] >= tile_end[None, :]).astype(jnp.int32), axis=1)
    active = tiles < n_used
    rank0 = (tile_ids - tile_start[tile_e]) * tm
    flag = active.astype(jnp.int32) + (active & (rank0 == 0)).astype(jnp.int32)
    r = jnp.arange(tm, dtype=jnp.int32)[None, :]
    rank = rank0[:, None] + r
    valid = (active[:, None] & (rank < counts[tile_e][:, None])).reshape(-1)
    idx = jnp.clip(grp_start[tile_e][:, None] + rank, 0, n_pairs - 1).reshape(-1)
    pair = order[idx]
    tok = pair // MOE_TOPK
    src = jnp.where(valid, tok, 0) * ROW_CHUNKS
    pad_row = jnp.broadcast_to(n_tok + r, (n_tiles, tm)).reshape(-1)
    dst_row = jnp.where(valid, tok, pad_row) * ROW_CHUNKS
    dst_plane = jnp.where(valid, pair % MOE_TOPK, 0)
    w_slot = jnp.where(valid, w_sorted[idx], 0.0)
    return tile_e, flag, src, dst_row, dst_plane, w_slot.reshape(n_slots, 1)


def _moe_kernel(te_ref, flag_ref, src_ref, row_ref, plane_ref, h_hbm, w_ref, wg_ref, wu_ref, wd_ref, y_hbm, xbuf,
                obuf, xs, wgb, wub, wdb, sem_in, sem_out):
    i = pl.program_id(0)
    n_tiles = pl.num_programs(0)
    flag = flag_ref[i]
    nxt = jnp.minimum(i + 1, n_tiles - 1)
    next_active = (i + 1 < n_tiles) & (flag_ref[nxt] > 0)
    slot = i % 2

    rc = ROW_CHUNKS

    def slab(ref_2d, first_row):
        return ref_2d.at[pl.ds(pl.multiple_of(first_row, rc), rc)]

    def start_gather(tile, sl):
        for r in range(TM_MOE):
            pltpu.make_async_copy(slab(h_hbm, src_ref[tile * TM_MOE + r]), xbuf.at[sl, pl.ds(r * rc, rc)],
                                  sem_in.at[sl]).start()

    def wait_gather(sl):
        pltpu.make_async_copy(h_hbm.at[pl.ds(0, TM_MOE * rc)], xbuf.at[sl], sem_in.at[sl]).wait()

    def wait_scatter(sl):
        pltpu.make_async_copy(obuf.at[sl], y_hbm.at[0, pl.ds(0, TM_MOE * rc)], sem_out.at[sl]).wait()

    @pl.when(i == 0)
    def _():
        obuf[0] = jnp.zeros(obuf.shape[1:], F32)
        pad_rows = pl.ds(y_hbm.shape[1] - TM_MOE * rc, TM_MOE * rc)
        for plane in range(MOE_TOPK):
            pltpu.make_async_copy(obuf.at[0], y_hbm.at[plane, pad_rows], sem_out.at[0]).start()
        for plane in range(MOE_TOPK):
            pltpu.make_async_copy(obuf.at[0], y_hbm.at[plane, pad_rows], sem_out.at[0]).wait()

    @pl.when(flag > 0)
    def _():
        @pl.when(i == 0)
        def _():
            start_gather(i, slot)

        @pl.when(next_active)
        def _():
            start_gather(i + 1, 1 - slot)

        @pl.when(flag > 1)
        def _():
            wgb[...] = wg_ref[0, 0].astype(BF16)
            wub[...] = wu_ref[0, 0].astype(BF16)
            wdb[...] = wd_ref[0, 0].astype(BF16)

        wait_gather(slot)
        rows = xbuf.at[slot]
        for c in range(rc):
            xs[:, c * LANE:(c + 1) * LANE] = rows[pl.ds(c, TM_MOE, stride=rc), :].astype(BF16)
        x = xs[...]
        a = jnp.dot(x, wgb[...], preferred_element_type=F32)
        u = jnp.dot(x, wub[...], preferred_element_type=F32)
        hid = (_silu(a) * u * w_ref[...]).astype(BF16)
        out = jnp.dot(hid, wdb[...], preferred_element_type=F32)

        @pl.when(i >= 2)
        def _():
            wait_scatter(slot)

        out_rows = obuf.at[slot]
        for c in range(rc):
            out_rows[pl.ds(c, TM_MOE, stride=rc), :] = out[:, c * LANE:(c + 1) * LANE]

        for r in range(TM_MOE):
            k = i * TM_MOE + r
            pltpu.make_async_copy(out_rows.at[pl.ds(r * rc, rc)], slab(y_hbm.at[plane_ref[k]], row_ref[k]),
                                  sem_out.at[slot]).start()

        @pl.when(jnp.logical_not(next_active))
        def _():
            wait_scatter(slot)

            @pl.when(i >= 1)
            def _():
                wait_scatter(1 - slot)


def _moe_call(h_all, plan, li, w_gate, w_up, w_down):
    t = h_all.shape[0] // ROW_CHUNKS
    d = D_MODEL
    tile_e, flag, src, dst_row, dst_plane, w_slot = plan
    n_tiles = tile_e.shape[0]
    tm = TM_MOE
    w_map = lambda i, te, *_: (li, te[i], 0, 0)
    grid_spec = pltpu.PrefetchScalarGridSpec(
        num_scalar_prefetch=5,
        grid=(n_tiles,),
        in_specs=[pl.BlockSpec(memory_space=pl.ANY),
                  pl.BlockSpec((tm, 1), lambda i, *_: (i, 0)),
                  pl.BlockSpec((1, 1, d, MOE_HIDDEN), w_map),
                  pl.BlockSpec((1, 1, d, MOE_HIDDEN), w_map),
                  pl.BlockSpec((1, 1, MOE_HIDDEN, d), w_map)],
        out_specs=pl.BlockSpec(memory_space=pl.ANY),
        scratch_shapes=[pltpu.VMEM((2, tm * ROW_CHUNKS, LANE), F32), pltpu.VMEM((2, tm * ROW_CHUNKS, LANE), F32),
                        pltpu.VMEM((tm, d), BF16),
                        pltpu.VMEM((d, MOE_HIDDEN), BF16), pltpu.VMEM((d, MOE_HIDDEN), BF16),
                        pltpu.VMEM((MOE_HIDDEN, d), BF16),
                        pltpu.SemaphoreType.DMA((2,)), pltpu.SemaphoreType.DMA((2,))])
    return pl.pallas_call(
        _moe_kernel,
        grid_spec=grid_spec,
        out_shape=jax.ShapeDtypeStruct((MOE_TOPK, (t + tm) * ROW_CHUNKS, LANE), F32),
        compiler_params=_cparams("arbitrary"),
        name="moe_experts",
    )(tile_e, flag, src, dst_row, dst_plane, h_all, w_slot, w_gate, w_up, w_down)


def _combine_kernel(x_ref, g_ref, y_ref, o_ref):
    n_rows = x_ref.shape[1]
    for c in range(ROW_CHUNKS):
        cols = slice(c * LANE, (c + 1) * LANE)
        chunk = pl.ds(c, n_rows, stride=ROW_CHUNKS)
        o_ref[0, :, cols] = x_ref[0, :, cols] + g_ref[0, :, cols] * (y_ref[0, chunk, :] + y_ref[1, chunk, :])


def _combine_call(x1, gate, y_all, row0):
    b, s, d = x1.shape
    ts = min(s, 512)
    sm = gate.shape[1]
    tsm = 1 if sm == 1 else ts
    mod_map = (lambda bi, i: (bi, 0, 0)) if sm == 1 else (lambda bi, i: (bi, i, 0))
    blk0 = row0 // ts
    per_b = s // ts
    return pl.pallas_call(
        _combine_kernel,
        grid=(b, per_b),
        in_specs=[pl.BlockSpec((1, ts, d), lambda bi, i: (bi, i, 0)),
                  pl.BlockSpec((1, tsm, d), mod_map),
                  pl.BlockSpec((MOE_TOPK, ts * ROW_CHUNKS, LANE), lambda bi, i: (0, blk0 + bi * per_b + i, 0))],
        out_specs=pl.BlockSpec((1, ts, d), lambda bi, i: (bi, i, 0)),
        out_shape=jax.ShapeDtypeStruct((b, s, d), F32),
        compiler_params=_cparams("parallel", "parallel"),
        name="moe_combine",
    )(x1, gate, y_all)


def _reorder_w_in(w):
    sizes = (SSD_INNER, SSD_CONV_DIM, SSD_HEADS, NSA_WIDTH, 3 * KV_ROW, 3 * NSA_HEADS, RET_HEADS * RET_KEY_DIM,
             RET_HEADS * RET_KEY_DIM, RET_WIDTH, RET_WIDTH)
    z, xbc, dt, nq, nkv, ngate, rq, rk, rv, rg = jnp.split(w, [int(v) for v in np.cumsum(sizes)[:-1]], axis=1)
    padc = lambda a: jnp.pad(a, ((0, 0), (0, LANE - a.shape[1])))
    return jnp.concatenate([z, nq, rv, rg, rq, rk, xbc, nkv, padc(dt), padc(ngate)], axis=1).astype(BF16)


def _kv_rows(a):
    return a.reshape(a.shape[:2] + (2, NSA_KV_HEADS, NSA_HEAD_DIM))


def kernel(x_prompt, x_sample, cache_cmp_kv, cache_sel_kv, cache_win_kv, state_ssm, state_conv, state_ret, page_table, c_prompt, c_sample, w_ada, b_ada, norm_mix, norm_ffn, w_in, w_out, conv_w, conv_b, dt_bias, a_log, ssd_d, ssd_norm, qk_gain, cmp_pe, cmp_w1, cmp_w2, router_group_w, router_group_b, router_expert_w, router_expert_b, expert_w_gate, expert_w_up, expert_w_down):
    bp, sp, d = x_prompt.shape
    bd, sd, _ = x_sample.shape
    n_pages = page_table.shape[1]
    past = n_pages * PAGE_SIZE
    assert d == D_MODEL and sp % LC == 0 and sp % NSA_BLOCK == 0 and sp >= NSA_WINDOW
    assert sd % SUBLANE == 0 and sd <= NSA_BLOCK and sd <= LC and cache_win_kv.shape[2] == NSA_WINDOW
    n_seq = bp + bd
    c_all = jnp.pad(jnp.concatenate([c_prompt, c_sample]), ((0, -n_seq % SUBLANE), (0, 0)))
    mod = _ada_call(c_all, w_ada, b_ada)
    cos_p, sin_p = _rotary_tables(0, sp, sp)
    cos_s, sin_s = _rotary_tables(past, sd, LC)
    cmp_pages = _feature_major_pages(cache_cmp_kv)
    sel_pages = _feature_major_pages(cache_sel_kv)
    xp = x_prompt
    xs = x_sample.reshape(1, bd * sd, d)
    outs_p, outs_s = [], []
    for li in range(DEPTH):
        mods_p = [mod[li, :bp, k * d:(k + 1) * d].reshape(bp, 1, d) for k in range(6)]
        mods_s = [jnp.repeat(mod[li, bp:n_seq, k * d:(k + 1) * d], sd, axis=0).reshape(1, bd * sd, d) for k in range(6)]
        w_in_b = _reorder_w_in(w_in[li])
        w_out_b = w_out[li].astype(BF16)
        cw = _compress_weights(cmp_pe[li], cmp_w1[li], cmp_w2[li], qk_gain[li, 4])
        cw_pages = _cmp_sample_weights(cmp_pe[li], cmp_w1[li], cmp_w2[li], qk_gain[li, 4])
        ssd_w = (conv_w[li], conv_b[li], dt_bias[li], a_log[li], ssd_d[li], ssd_norm[li])
        wr = jnp.pad(jnp.concatenate([router_group_w[li], router_expert_w[li]], axis=1),
                     ((0, 0), (0, LANE - MOE_GROUPS - MOE_EXPERTS)))
        br = jnp.pad(jnp.concatenate([router_group_b[li], router_expert_b[li]]),
                     (0, LANE - MOE_GROUPS - MOE_EXPERTS)).reshape(1, LANE)

        sh_a, sc_a, g_a, sh_f, sc_f, g_f = mods_p
        proj = _inproj_call(xp, sc_a, sh_a, norm_mix[li], w_in_b)
        y_ssd, ssm_p, conv_p = _ssd_call(proj, LC, jnp.zeros((bp, SSD_CONV - 1, SSD_CONV_DIM), F32),
                                         jnp.zeros((bp, SSD_HEADS, SSD_HEAD_DIM, SSD_STATE), F32), *ssd_w)
        o_ret, ret_p = _ret_call(proj, LC, jnp.zeros((bp, RET_HEADS, RET_KEY_DIM, RET_VAL_DIM), F32), cos_p, sin_p)
        qn, kvn_p, kvb = _prep_call(proj, sp, qk_gain[li])
        kcvc = _cmp_prompt_call(kvn_p, cw)
        o_nsa = _nsa_prompt_call(proj, qn, kvb, kcvc)
        x1_p = _outproj_call(y_ssd, o_nsa, o_ret, w_out_b, xp, g_a)
        h2_p, meta_p, cnt_p = _route_call(x1_p, sc_f, sh_f, norm_ffn[li], wr, br)
        gf_p = g_f

        sh_a, sc_a, g_a, sh_f, sc_f, g_f = mods_s
        proj = _inproj_call(xs, sc_a, sh_a, norm_mix[li], w_in_b).reshape(bd, sd, N_PROJ)
        proj = jnp.pad(proj, ((0, 0), (0, LC - sd), (0, 0)))
        y_ssd, ssm_s, conv_s = _ssd_call(proj, sd, state_conv[li], state_ssm[li], *ssd_w)
        o_ret, ret_s = _ret_call(proj, sd, state_ret[li], cos_s, sin_s)
        qn, kvn_s, _ = _prep_call(proj, sd, qk_gain[li])
        kcvc = _cmp_sample_call(cmp_pages, li, page_table, cw_pages)
        win_old = cache_win_kv[li].reshape(bd, NSA_WINDOW, KV_ROW)
        o_nsa = _nsa_sample_call(sel_pages, li, page_table, proj, qn, kvn_s, kcvc, win_old, past)
        o_nsa = o_nsa.reshape(bd, NSA_HEADS, sd, NSA_HEAD_DIM).transpose(0, 2, 1, 3).reshape(1, bd * sd, NSA_WIDTH)
        x1_s = _outproj_call(y_ssd[:, :sd].reshape(1, bd * sd, SSD_INNER), o_nsa,
                             o_ret[:, :sd].reshape(1, bd * sd, RET_WIDTH), w_out_b, xs, g_a)
        h2_s, meta_s, cnt_s = _route_call(x1_s, sc_f, sh_f, norm_ffn[li], wr, br)

        h_all = jnp.concatenate([h2_p.reshape(bp * sp * ROW_CHUNKS, LANE), h2_s.reshape(bd * sd * ROW_CHUNKS, LANE)])
        meta = jnp.concatenate([meta_p.reshape(bp * sp, LANE), meta_s.reshape(bd * sd, LANE)])
        counts = (cnt_p.sum(axis=(0, 1, 2)) + cnt_s.sum(axis=(0, 1, 2)))[MOE_GROUPS:MOE_GROUPS + MOE_EXPERTS]
        plan = _moe_plan(meta[:, :MOE_TOPK].astype(jnp.int32), meta[:, MOE_TOPK:2 * MOE_TOPK],
                         counts.astype(jnp.int32))
        y_all = _moe_call(h_all, plan, li, expert_w_gate, expert_w_up, expert_w_down)
        xp = _combine_call(x1_p, gf_p, y_all, 0)
        xs = _combine_call(x1_s, g_f, y_all, bp * sp)

        win_s = jnp.concatenate([win_old[:, sd:], kvn_s[:, :, 2 * KV_ROW:]], axis=1)
        outs_p.append((_kv_rows(kvn_p[:, :, :KV_ROW]), _kv_rows(kvn_p[:, :, KV_ROW:2 * KV_ROW]),
                       _kv_rows(kvn_p[:, sp - NSA_WINDOW:, 2 * KV_ROW:]), ssm_p, conv_p, ret_p))
        outs_s.append((_kv_rows(kvn_s[:, :, :KV_ROW]), _kv_rows(kvn_s[:, :, KV_ROW:2 * KV_ROW]), _kv_rows(win_s),
                       ssm_s, conv_s, ret_s))
    res = [xp, xs.reshape(bd, sd, d)]
    for k in range(6):
        res.append(jnp.stack([o[k] for o in outs_p]))
        res.append(jnp.stack([o[k] for o in outs_s]))
    return tuple(res)
```

```python
import functools
import math

import numpy as np
import jax
import jax.numpy as jnp
from jax import lax
from jax.experimental import pallas as pl
from jax.experimental.pallas import tpu as pltpu

F32 = jnp.float32
BF16 = jnp.bfloat16
HI = lax.Precision.HIGHEST
EPS = 1e-6
NEG_INF = -1e30

D_MODEL = 2048
DEPTH = 2
PAGE_SIZE = 128
SSD_INNER = D_MODEL // 2
SSD_HEAD_DIM = 64
SSD_HEADS = SSD_INNER // SSD_HEAD_DIM
SSD_GROUPS = 2
SSD_STATE = 128
SSD_CONV = 4
SSD_CONV_DIM = SSD_INNER + 2 * SSD_GROUPS * SSD_STATE
NSA_WIDTH = D_MODEL // 4
NSA_HEAD_DIM = 64
NSA_HEADS = NSA_WIDTH // NSA_HEAD_DIM
NSA_KV_HEADS = 2
NSA_REP = NSA_HEADS // NSA_KV_HEADS
NSA_BLOCK = 64
NSA_TOPN = 16
NSA_WINDOW = 512
NSA_CMP_HIDDEN = 128
RET_WIDTH = D_MODEL // 4
RET_HEADS = 4
RET_VAL_DIM = RET_WIDTH // RET_HEADS
RET_KEY_DIM = RET_VAL_DIM // 2
D_MIX = SSD_INNER + NSA_WIDTH + RET_WIDTH
MOE_GROUPS = 4
MOE_EXPERTS_PER_GROUP = 8
MOE_EXPERTS = MOE_GROUPS * MOE_EXPERTS_PER_GROUP
MOE_TOPK = 2
MOE_HIDDEN = D_MODEL // 4
KV_ROW = 2 * NSA_KV_HEADS * NSA_HEAD_DIM

LANE = 128
SUBLANE = 8
C_Z, C_NQ, C_RV, C_RG, C_RQ, C_RK, C_XBC, C_NKV, C_DT, C_NG = 0, 1024, 1536, 2048, 2560, 2816, 3072, 4608, 5376, 5504
N_PROJ = 5632
LC = 128
TQ = 512
WIN_ROWS = 128
TM_MOE = 256
ROW_CHUNKS = D_MODEL // LANE
VMEM_LIMIT = 56 * 2 ** 20


def _cparams(*sem):
    return pltpu.CompilerParams(dimension_semantics=sem, vmem_limit_bytes=VMEM_LIMIT)


def _silu(x):
    return x * jax.nn.sigmoid(x)


def _nt(a, b, precision=None):
    return lax.dot_general(a, b, (((1,), (1,)), ((), ())), precision=precision, preferred_element_type=F32)


def _tn(a, b, precision=None):
    return lax.dot_general(a, b, (((0,), (0,)), ((), ())), precision=precision, preferred_element_type=F32)


def _iota(shape, dim):
    return lax.broadcasted_iota(jnp.int32, shape, dim)


def _seg_sum(x2, seg):
    n = x2.shape[-1]
    bd = (_iota((n, n), 0) // seg == _iota((n, n), 1) // seg).astype(BF16)
    hi = x2.astype(BF16)
    lo = (x2 - hi.astype(F32)).astype(BF16)
    return jnp.dot(hi, bd, preferred_element_type=F32) + jnp.dot(lo, bd, preferred_element_type=F32)


def _ada_kernel(c_ref, w_ref, b_ref, o_ref):
    o_ref[0] = jnp.dot(_silu(c_ref[...]), w_ref[0], precision=HI, preferred_element_type=F32) + b_ref[0]


def _ada_call(c_all, w_ada, b_ada):
    rows = c_all.shape[0]
    tn = 2048
    return pl.pallas_call(
        _ada_kernel,
        grid=(DEPTH, 6 * D_MODEL // tn),
        in_specs=[pl.BlockSpec((rows, D_MODEL), lambda l, j: (0, 0)),
                  pl.BlockSpec((1, D_MODEL, tn), lambda l, j: (l, 0, j)),
                  pl.BlockSpec((1, 1, tn), lambda l, j: (l, 0, j))],
        out_specs=pl.BlockSpec((1, rows, tn), lambda l, j: (l, 0, j)),
        out_shape=jax.ShapeDtypeStruct((DEPTH, rows, 6 * D_MODEL), F32),
        compiler_params=_cparams("parallel", "parallel"),
        name="ada",
    )(c_all, w_ada, b_ada.reshape(DEPTH, 1, 6 * D_MODEL))


def _mod_norm(x, gain, scale, shift):
    ms = jnp.mean(x * x, axis=-1, keepdims=True)
    return x * lax.rsqrt(ms + EPS) * gain * (1.0 + scale) + shift


def _inproj_kernel(x_ref, sc_ref, sh_ref, g_ref, w_ref, o_ref, h_ref):
    @pl.when(pl.program_id(2) == 0)
    def _():
        h_ref[...] = _mod_norm(x_ref[0], g_ref[...], sc_ref[0], sh_ref[0]).astype(BF16)

    o_ref[0] = jnp.dot(h_ref[...], w_ref[...], preferred_element_type=F32)


def _inproj_call(x, scale, shift, gain, w_bf):
    b, s, d = x.shape
    ts = min(s, 1024)
    tn = 512
    sm = scale.shape[1]
    tsm = 1 if sm == 1 else ts
    mod_map = (lambda bi, i, j: (bi, 0, 0)) if sm == 1 else (lambda bi, i, j: (bi, i, 0))
    return pl.pallas_call(
        _inproj_kernel,
        grid=(b, s // ts, N_PROJ // tn),
        in_specs=[pl.BlockSpec((1, ts, d), lambda bi, i, j: (bi, i, 0)),
                  pl.BlockSpec((1, tsm, d), mod_map),
                  pl.BlockSpec((1, tsm, d), mod_map),
                  pl.BlockSpec((1, d), lambda bi, i, j: (0, 0)),
                  pl.BlockSpec((d, tn), lambda bi, i, j: (0, j))],
        out_specs=pl.BlockSpec((1, ts, tn), lambda bi, i, j: (bi, i, j)),
        out_shape=jax.ShapeDtypeStruct((b, s, N_PROJ), F32),
        scratch_shapes=[pltpu.VMEM((ts, d), BF16)],
        compiler_params=_cparams("parallel", "parallel", "arbitrary"),
        name="in_proj",
    )(x, scale, shift, gain.reshape(1, d), w_bf)


def _ssd_kernel(n_valid, xbc_ref, z_ref, dt_ref, cprev_ref, sprev_ref, cw_ref, cb_ref, dtb_ref, alog_ref, dsk_ref,
                nrm_ref, exp_ref, y_ref, snew_ref, cnew_ref, cbuf, state, ybuf):
    c = pl.program_id(1)
    nc = pl.num_programs(1)
    hd, nst = SSD_HEAD_DIM, SSD_STATE
    hpg = SSD_HEADS // SSD_GROUPS

    @pl.when(c == 0)
    def _():
        cbuf[0:SUBLANE, :] = jnp.zeros((SUBLANE, SSD_CONV_DIM), F32)
        cbuf[5:8, :] = cprev_ref[0]
        state[...] = sprev_ref[0]

    cbuf[8:8 + LC, :] = xbc_ref[0]
    conv = cb_ref[...] + cw_ref[0:1, :] * cbuf[5:5 + LC, :]
    for k in range(1, SSD_CONV):
        conv = conv + cw_ref[k:k + 1, :] * cbuf[5 + k:5 + k + LC, :]
    tail = cbuf[5 + n_valid:8 + n_valid, :]
    cbuf[5:8, :] = tail
    xbc = _silu(conv)
    xs = xbc[:, :SSD_INNER]
    bm = xbc[:, SSD_INNER:SSD_INNER + SSD_GROUPS * nst].astype(BF16)
    cm = xbc[:, SSD_INNER + SSD_GROUPS * nst:].astype(BF16)

    dtr = dt_ref[0] + dtb_ref[...]
    dt = jnp.maximum(dtr, 0.0) + jnp.log1p(jnp.exp(-jnp.abs(dtr)))
    if n_valid < LC:
        dt = jnp.where(_iota((LC, LANE), 0) < n_valid, dt, 0.0)
    a = -jnp.exp(alog_ref[...])
    da = dt * a
    row = _iota((LC, LC), 0)
    col = _iota((LC, LC), 1)
    tril = row >= col
    acs = jnp.dot(tril.astype(F32), da, precision=HI, preferred_element_type=F32)
    eye = (row == col).astype(F32)
    acs_t = _nt(eye, acs, precision=HI)
    expand = exp_ref[...]

    def per_head_to_lanes(v):
        hi = v.astype(BF16)
        lo = (v - hi.astype(F32)).astype(BF16)
        return (jnp.dot(hi, expand, preferred_element_type=F32) + jnp.dot(lo, expand, preferred_element_type=F32))

    e_acs = jnp.exp(acs)
    decay = jnp.exp(acs[LC - 1:LC, :] - acs)
    dt_x = per_head_to_lanes(dt)
    e_acs_x = per_head_to_lanes(e_acs)
    decay_x = per_head_to_lanes(decay)
    xdt = xs * dt_x
    xdec = (xdt * decay_x).astype(BF16)
    xdt_b = xdt.astype(BF16)
    cbs = [_nt(cm[:, g * nst:(g + 1) * nst], bm[:, g * nst:(g + 1) * nst]) for g in range(SSD_GROUPS)]
    for h in range(SSD_HEADS):
        g = h // hpg
        lmat = jnp.exp(jnp.where(tril, acs[:, h:h + 1] - acs_t[h:h + 1, :], -jnp.inf))
        m = (cbs[g] * lmat).astype(BF16)
        y_diag = jnp.dot(m, xdt_b[:, h * hd:(h + 1) * hd], preferred_element_type=F32)
        st = state[h]
        y_off = _nt(cm[:, g * nst:(g + 1) * nst], st.astype(BF16)) * e_acs_x[:, h * hd:(h + 1) * hd]
        state[h] = st * e_acs[LC - 1:LC, h:h + 1] + _tn(xdec[:, h * hd:(h + 1) * hd], bm[:, g * nst:(g + 1) * nst])
        ybuf[:, h * hd:(h + 1) * hd] = y_diag + y_off
    y = (ybuf[...] + dsk_ref[...] * xs) * _silu(z_ref[0])
    gw = SSD_INNER // SSD_GROUPS
    for g in range(SSD_GROUPS):
        seg = y[:, g * gw:(g + 1) * gw]
        ms = jnp.mean(seg * seg, axis=-1, keepdims=True)
        y_ref[0, :, g * gw:(g + 1) * gw] = seg * lax.rsqrt(ms + EPS) * nrm_ref[:, g * gw:(g + 1) * gw]

    @pl.when(c == nc - 1)
    def _():
        snew_ref[0] = state[...]
        cnew_ref[0] = tail


def _ssd_call(proj, n_valid, conv_prev, ssm_prev, conv_w, conv_b, dt_bias, a_log, ssd_d, ssd_norm):
    b, sp, _ = proj.shape
    nc = sp // LC
    pad = lambda v: jnp.pad(v, (0, LANE - SSD_HEADS)).reshape(1, LANE)
    const2 = lambda bi, c: (0, 0)
    return pl.pallas_call(
        functools.partial(_ssd_kernel, n_valid),
        grid=(b, nc),
        in_specs=[pl.BlockSpec((1, LC, SSD_CONV_DIM), lambda bi, c: (bi, c, C_XBC // SSD_CONV_DIM)),
                  pl.BlockSpec((1, LC, SSD_INNER), lambda bi, c: (bi, c, C_Z // SSD_INNER)),
                  pl.BlockSpec((1, LC, LANE), lambda bi, c: (bi, c, C_DT // LANE)),
                  pl.BlockSpec((1, SSD_CONV - 1, SSD_CONV_DIM), lambda bi, c: (bi, 0, 0)),
                  pl.BlockSpec((1, SSD_HEADS, SSD_HEAD_DIM, SSD_STATE), lambda bi, c: (bi, 0, 0, 0)),
                  pl.BlockSpec((SSD_CONV, SSD_CONV_DIM), const2),
                  pl.BlockSpec((1, SSD_CONV_DIM), const2),
                  pl.BlockSpec((1, LANE), const2),
                  pl.BlockSpec((1, LANE), const2),
                  pl.BlockSpec((1, SSD_INNER), const2),
                  pl.BlockSpec((1, SSD_INNER), const2),
                  pl.BlockSpec((LANE, SSD_INNER), const2)],
        out_specs=[pl.BlockSpec((1, LC, SSD_INNER), lambda bi, c: (bi, c, 0)),
                   pl.BlockSpec((1, SSD_HEADS, SSD_HEAD_DIM, SSD_STATE), lambda bi, c: (bi, 0, 0, 0)),
                   pl.BlockSpec((1, SSD_CONV - 1, SSD_CONV_DIM), lambda bi, c: (bi, 0, 0))],
        out_shape=[jax.ShapeDtypeStruct((b, sp, SSD_INNER), F32),
                   jax.ShapeDtypeStruct((b, SSD_HEADS, SSD_HEAD_DIM, SSD_STATE), F32),
                   jax.ShapeDtypeStruct((b, SSD_CONV - 1, SSD_CONV_DIM), F32)],
        scratch_shapes=[pltpu.VMEM((8 + LC, SSD_CONV_DIM), F32),
                        pltpu.VMEM((SSD_HEADS, SSD_HEAD_DIM, SSD_STATE), F32),
                        pltpu.VMEM((LC, SSD_INNER), F32)],
        compiler_params=_cparams("parallel", "arbitrary"),
        name="ssd",
    )(proj, proj, proj, conv_prev, ssm_prev, conv_w, conv_b.reshape(1, -1), pad(dt_bias), pad(a_log),
      jnp.repeat(ssd_d, SSD_HEAD_DIM).reshape(1, -1), ssd_norm.reshape(1, -1),
      (np.arange(LANE)[:, None] == np.arange(SSD_INNER)[None, :] // SSD_HEAD_DIM).astype(BF16))


def _ret_log_g(h):
    return float(np.log1p(-np.exp2(np.float32(-5.0 - h)), dtype=np.float32))


def _ret_kernel(n_valid, q_ref, k_ref, v_ref, g_ref, cos_ref, sin_ref, rprev_ref, o_ref, rnew_ref, state):
    c = pl.program_id(1)
    nc = pl.num_programs(1)
    kd, vd = RET_KEY_DIM, RET_VAL_DIM
    half = kd // 2

    @pl.when(c == 0)
    def _():
        state[...] = rprev_ref[0]

    cos = cos_ref[...]
    sin = sin_ref[...]
    first_half = (_iota((LC, RET_HEADS * kd), 1) % kd) < half

    def rot(x):
        n = x.shape[-1]
        swapped = jnp.where(first_half, pltpu.roll(x, n - half, 1), pltpu.roll(x, half, 1))
        return x * cos + swapped * sin

    q = rot(q_ref[0])
    k = rot(k_ref[0]) * (kd ** -0.5)
    v = v_ref[0]
    gate = g_ref[0]
    ti = _iota((LC, LC), 0)
    tj = _iota((LC, LC), 1)
    diff = (ti - tj).astype(F32)
    ipos = _iota((LC, 1), 0)
    i1 = ipos.astype(F32)
    for h in range(RET_HEADS):
        lg = _ret_log_g(h)
        dmat = jnp.where(diff >= 0, jnp.exp(diff * lg), 0.0)
        q_dec = jnp.exp((i1 + 1.0) * lg)
        k_dec = jnp.where(ipos < n_valid, jnp.exp((n_valid - 1.0 - i1) * lg), 0.0)
        c_dec = math.exp(n_valid * lg)
        qh = q[:, h * kd:(h + 1) * kd].astype(BF16)
        kh = k[:, h * kd:(h + 1) * kd]
        vh = v[:, h * vd:(h + 1) * vd].astype(BF16)
        sc = _nt(qh, kh.astype(BF16)) * dmat
        intra = jnp.dot(sc.astype(BF16), vh, preferred_element_type=F32)
        rs = state[h]
        cross = jnp.dot(qh, rs.astype(BF16), preferred_element_type=F32) * q_dec
        state[h] = rs * c_dec + _tn((kh * k_dec).astype(BF16), vh)
        o = intra + cross
        ms = jnp.mean(o * o, axis=-1, keepdims=True)
        o_ref[0, :, h * vd:(h + 1) * vd] = o * lax.rsqrt(ms + EPS) * _silu(gate[:, h * vd:(h + 1) * vd])

    @pl.when(c == nc - 1)
    def _():
        rnew_ref[0] = state[...]


def _rotary_tables(offset, s_real, s_pad):
    half = RET_KEY_DIM // 2
    freqs = 1.0 / (10000.0 ** jnp.linspace(0.0, 1.0, half, dtype=F32))
    pos = (offset + jnp.arange(s_real)).astype(F32)
    ang = pos[:, None] * freqs[None, :]
    cos, sin = jnp.cos(ang), jnp.sin(ang)
    cos = jnp.tile(jnp.concatenate([cos, cos], axis=-1), (1, RET_HEADS))
    sin = jnp.tile(jnp.concatenate([-sin, sin], axis=-1), (1, RET_HEADS))
    padr = ((0, s_pad - s_real), (0, 0))
    return jnp.pad(cos, padr), jnp.pad(sin, padr)


def _ret_call(proj, n_valid, ret_prev, cos, sin):
    b, sp, _ = proj.shape
    nc = sp // LC
    qw = RET_HEADS * RET_KEY_DIM
    return pl.pallas_call(
        functools.partial(_ret_kernel, n_valid),
        grid=(b, nc),
        in_specs=[pl.BlockSpec((1, LC, qw), lambda bi, c: (bi, c, C_RQ // qw)),
                  pl.BlockSpec((1, LC, qw), lambda bi, c: (bi, c, C_RK // qw)),
                  pl.BlockSpec((1, LC, RET_WIDTH), lambda bi, c: (bi, c, C_RV // RET_WIDTH)),
                  pl.BlockSpec((1, LC, RET_WIDTH), lambda bi, c: (bi, c, C_RG // RET_WIDTH)),
                  pl.BlockSpec((LC, qw), lambda bi, c: (c, 0)),
                  pl.BlockSpec((LC, qw), lambda bi, c: (c, 0)),
                  pl.BlockSpec((1, RET_HEADS, RET_KEY_DIM, RET_VAL_DIM), lambda bi, c: (bi, 0, 0, 0))],
        out_specs=[pl.BlockSpec((1, LC, RET_WIDTH), lambda bi, c: (bi, c, 0)),
                   pl.BlockSpec((1, RET_HEADS, RET_KEY_DIM, RET_VAL_DIM), lambda bi, c: (bi, 0, 0, 0))],
        out_shape=[jax.ShapeDtypeStruct((b, sp, RET_WIDTH), F32),
                   jax.ShapeDtypeStruct((b, RET_HEADS, RET_KEY_DIM, RET_VAL_DIM), F32)],
        scratch_shapes=[pltpu.VMEM((RET_HEADS, RET_KEY_DIM, RET_VAL_DIM), F32)],
        compiler_params=_cparams("parallel", "arbitrary"),
        name="retention",
    )(proj, proj, proj, proj, cos, sin, ret_prev)


def _slope(h):
    return float(2.0 ** -(h + 1))


def _prep_kernel(nq_ref, nkv_ref, gq_ref, gk_ref, isk_ref, qn_ref, kvn_ref, kvb_ref):
    inv = 1.0 / NSA_HEAD_DIM
    nq = nq_ref[0]
    qn_ref[0] = nq * lax.rsqrt(_seg_sum(nq * nq, NSA_HEAD_DIM) * inv + EPS) * gq_ref[...]
    kv = nkv_ref[0]
    normed = kv * lax.rsqrt(_seg_sum(kv * kv, NSA_HEAD_DIM) * inv + EPS) * gk_ref[...]
    kvn = jnp.where(isk_ref[...] > 0.5, normed, kv)
    kvn_ref[0] = kvn
    kvb_ref[0] = kvn[:, KV_ROW:].astype(BF16)


def _prep_call(proj, s_real, qk_gain):
    b = proj.shape[0]
    ts = min(s_real, 512)
    hd = NSA_HEAD_DIM
    gq = (jnp.tile(qk_gain[0], NSA_HEADS) * (hd ** -0.5)).reshape(1, -1)
    ones = jnp.ones((2 * hd,), F32)
    gk = jnp.concatenate([jnp.concatenate([jnp.tile(qk_gain[1 + i], 2), ones]) for i in range(3)]).reshape(1, -1)
    isk = jnp.tile(jnp.concatenate([ones, 0.0 * ones]), 3).reshape(1, -1)
    const2 = lambda bi, i: (0, 0)
    return pl.pallas_call(
        _prep_kernel,
        grid=(b, s_real // ts),
        in_specs=[pl.BlockSpec((1, ts, NSA_WIDTH), lambda bi, i: (bi, i, C_NQ // NSA_WIDTH)),
                  pl.BlockSpec((1, ts, 3 * KV_ROW), lambda bi, i: (bi, i, C_NKV // (3 * KV_ROW))),
                  pl.BlockSpec((1, NSA_WIDTH), const2),
                  pl.BlockSpec((1, 3 * KV_ROW), const2),
                  pl.BlockSpec((1, 3 * KV_ROW), const2)],
        out_specs=[pl.BlockSpec((1, ts, NSA_WIDTH), lambda bi, i: (bi, i, 0)),
                   pl.BlockSpec((1, ts, 3 * KV_ROW), lambda bi, i: (bi, i, 0)),
                   pl.BlockSpec((1, ts, 2 * KV_ROW), lambda bi, i: (bi, i, 0))],
        out_shape=[jax.ShapeDtypeStruct((b, s_real, NSA_WIDTH), F32),
                   jax.ShapeDtypeStruct((b, s_real, 3 * KV_ROW), F32),
                   jax.ShapeDtypeStruct((b, s_real, 2 * KV_ROW), BF16)],
        compiler_params=_cparams("parallel", "parallel"),
        name="nsa_prep",
    )(proj, proj, gq, gk, isk)


def _compress_weights(cmp_pe, cmp_w1, cmp_w2, gain4):
    hd = NSA_HEAD_DIM

    def block_diag4(a0, a1):
        z = jnp.zeros_like(a0)
        rows = [[a0, z, z, z], [z, a0, z, z], [z, z, a1, z], [z, z, z, a1]]
        return jnp.concatenate([jnp.concatenate(r, axis=-1) for r in rows], axis=-2)

    w1 = cmp_w1.astype(BF16)
    w2 = cmp_w2.astype(BF16)
    wbig = block_diag4(w1[0], w1[1])
    w2big = block_diag4(w2[0], w2[1])
    pe_row = jnp.concatenate([cmp_pe[0], cmp_pe[0], cmp_pe[1], cmp_pe[1]], axis=-1)
    g4 = jnp.concatenate([jnp.tile(gain4, 2), jnp.ones((2 * hd,), F32)]).reshape(1, -1)
    return pe_row, wbig, w2big, g4


def _compress_core(get_rows, nblk, pe_ref, wbig_ref, w2_ref, g4_ref):
    acc = jnp.zeros((nblk, 4 * NSA_CMP_HIDDEN), F32)
    for l in range(NSA_BLOCK):
        x = get_rows(l) + pe_ref[l:l + 1, :]
        acc = acc + jnp.dot(x.astype(BF16), wbig_ref[l], preferred_element_type=F32)
    out = jnp.dot(_silu(acc).astype(BF16), w2_ref[...], preferred_element_type=F32)
    ss = _seg_sum(out * out, NSA_HEAD_DIM)
    normed = out * lax.rsqrt(ss * (1.0 / NSA_HEAD_DIM) + EPS) * g4_ref[...]
    return jnp.where(_iota(out.shape, 1) < 2 * NSA_HEAD_DIM, normed, out)


def _cmp_prompt_kernel(nblk, kv_ref, pe_ref, wbig_ref, w2_ref, g4_ref, o_ref):
    o_ref[0] = _compress_core(lambda l: kv_ref[0, :, l, :], nblk, pe_ref, wbig_ref, w2_ref, g4_ref)


def _cmp_prompt_call(kvn, cw):
    b, s, _ = kvn.shape
    nblk = s // NSA_BLOCK
    pe_row, wbig, w2big, g4 = cw
    kv4 = kvn.reshape(b, nblk, NSA_BLOCK, 3 * KV_ROW)
    return pl.pallas_call(
        functools.partial(_cmp_prompt_kernel, nblk),
        grid=(b,),
        in_specs=[pl.BlockSpec((1, nblk, NSA_BLOCK, KV_ROW), lambda bi: (bi, 0, 0, 0)),
                  pl.BlockSpec(pe_row.shape, lambda bi: (0, 0)),
                  pl.BlockSpec(wbig.shape, lambda bi: (0, 0, 0)),
                  pl.BlockSpec(w2big.shape, lambda bi: (0, 0)),
                  pl.BlockSpec(g4.shape, lambda bi: (0, 0))],
        out_specs=pl.BlockSpec((1, nblk, KV_ROW), lambda bi: (bi, 0, 0)),
        out_shape=jax.ShapeDtypeStruct((b, nblk, KV_ROW), F32),
        compiler_params=_cparams("arbitrary"),
        name="nsa_cmp_prompt",
    )(kv4, pe_row, wbig, w2big, g4)


def _masked_softmax(s, mask):
    s = jnp.where(mask, s, NEG_INF)
    p = jnp.exp(s - jnp.max(s, axis=-1, keepdims=True))
    return p / jnp.sum(p, axis=-1, keepdims=True) * mask.astype(F32)


def _select_blocks(imp, blk, n_top, axis=-1):
    sel = jnp.zeros(imp.shape, F32)
    big = imp.shape[axis]
    for _ in range(n_top):
        m = jnp.max(imp, axis=axis, keepdims=True)
        idx = jnp.min(jnp.where(imp == m, blk, big), axis=axis, keepdims=True)
        hit = blk == idx
        sel = jnp.where(hit & (m >= 0.0), 1.0, sel)
        imp = jnp.where(hit, -2.0, imp)
    return sel


M_FLOOR = -1e29


def _nsa_prompt_kernel(nbc, s_len, tk, wk, q_ref, gate_ref, cmp_ref, ksel_ref, kwin_ref, o_ref, bias_ref):
    qi = pl.program_id(1)
    hd, rep, nh = NSA_HEAD_DIM, NSA_REP, NSA_HEADS
    q = q_ref[0]
    qb = q.astype(BF16)
    gates = jax.nn.sigmoid(gate_ref[0])
    t = qi * TQ + _iota((TQ, 1), 0)
    t_row = qi * TQ + _iota((1, TQ), 1)
    kcvc = cmp_ref[0]
    blk = _iota((nbc, TQ), 0)
    dist_c = t_row - ((blk + 1) * NSA_BLOCK - 1)
    mask_c = dist_c >= 0
    mask_cf = mask_c.astype(F32)
    dist_cf = dist_c.astype(F32)
    cur = t_row // NSA_BLOCK
    o_cmp = []
    imps = []
    n_kt = (qi * TQ + TQ + tk - 1) // tk
    for g in range(NSA_KV_HEADS):
        kc = kcvc[:, g * hd:(g + 1) * hd]
        vc = kcvc[:, 2 * hd + g * hd:2 * hd + (g + 1) * hd]
        imp = jnp.zeros((nbc, TQ), F32)
        for r in range(rep):
            h = g * rep + r
            s = jnp.where(mask_c, _nt(kc, q[:, h * hd:(h + 1) * hd], precision=HI) - _slope(h) * dist_cf, NEG_INF)
            p = jnp.exp(s - jnp.max(s, axis=0, keepdims=True))
            p = p / jnp.sum(p, axis=0, keepdims=True) * mask_cf
            imp = imp + p
            o_cmp.append(_tn(p, vc, precision=HI))
        imps.append(jnp.where(blk < cur, imp, -1.0))
    blk_all = jnp.concatenate([blk] * NSA_KV_HEADS, axis=1)
    is_cur = (blk_all == jnp.concatenate([cur] * NSA_KV_HEADS, axis=1)).astype(F32)
    sel_all = jnp.maximum(_select_blocks(jnp.concatenate(imps, axis=1), blk_all, NSA_TOPN - 1, axis=0), is_cur)
    sels = []
    first_kt = []
    bpt = tk // NSA_BLOCK
    for g in range(NSA_KV_HEADS):
        sel = sel_all[:, g * TQ:(g + 1) * TQ]
        sels.append(sel.astype(BF16))
        first = n_kt
        for kt in reversed(range(nbc // bpt)):
            first = jnp.where(jnp.max(sel[kt * bpt:(kt + 1) * bpt, :]) > 0.0, kt, first)
        first_kt.append(first)

    def fill(kt, carry):
        k0 = pl.multiple_of(kt * tk, tk)
        causal = t >= k0 + _iota((TQ, tk), 1)
        expand = (_iota((nbc, tk), 0) == (k0 + _iota((nbc, tk), 1)) // NSA_BLOCK).astype(BF16)
        for g in range(NSA_KV_HEADS):
            picked = _tn(sels[g], expand) > 0.5
            bias_ref[g, :, pl.ds(k0, tk)] = jnp.where(picked & causal, 0.0, NEG_INF).astype(BF16)
        return carry

    lax.fori_loop(jnp.minimum(first_kt[0], first_kt[1]), n_kt, fill, 0)

    rows_g = rep * TQ
    for g in range(NSA_KV_HEADS):
        heads = range(g * rep, (g + 1) * rep)

        def stacked(bias, dist, heads=heads):
            return jnp.concatenate([bias - _slope(h) * dist for h in heads], axis=0)

        qg = jnp.concatenate([qb[:, h * hd:(h + 1) * hd] for h in heads], axis=0)

        def body(kt, carry, g=g, qg=qg, stacked=stacked):
            m, l, acc = carry
            k0 = pl.multiple_of(kt * tk, tk)
            kk = ksel_ref[0, pl.ds(k0, tk), g * hd:(g + 1) * hd]
            vv = ksel_ref[0, pl.ds(k0, tk), 2 * hd + g * hd:2 * hd + (g + 1) * hd]
            dist = (t - (k0 + _iota((TQ, tk), 1))).astype(F32)
            s = _nt(qg, kk) + stacked(bias_ref[g, :, pl.ds(k0, tk)].astype(F32), dist)
            m_new = jnp.maximum(m, jnp.max(s, axis=-1, keepdims=True))
            alpha = jnp.exp(m - m_new)
            p = jnp.exp(s - m_new)
            l = alpha * l + jnp.sum(p, axis=-1, keepdims=True)
            acc = alpha * acc + jnp.dot(p.astype(BF16), vv, preferred_element_type=F32)
            return m_new, l, acc

        init = (jnp.full((rows_g, 1), M_FLOOR, F32), jnp.zeros((rows_g, 1), F32), jnp.zeros((rows_g, hd), F32))
        _, l_s, acc_s = lax.fori_loop(first_kt[g], n_kt, body, init)
        o_sel = acc_s / l_s
        o_win = []
        for w0 in range(0, TQ, WIN_ROWS):
            start_w = pl.multiple_of(jnp.maximum(qi * TQ + w0 + WIN_ROWS - wk, 0), WIN_ROWS)
            kw_all = kwin_ref[0, pl.ds(start_w, wk), :]
            dist_w = t[w0:w0 + WIN_ROWS] - (start_w + _iota((WIN_ROWS, wk), 1))
            bias_w = jnp.where((dist_w >= 0) & (dist_w <= NSA_WINDOW), 0.0, NEG_INF)
            qw = jnp.concatenate([qb[w0:w0 + WIN_ROWS, h * hd:(h + 1) * hd] for h in heads], axis=0)
            s_w = _nt(qw, kw_all[:, g * hd:(g + 1) * hd]) + stacked(bias_w, dist_w.astype(F32))
            p_w = jnp.exp(s_w - jnp.max(s_w, axis=-1, keepdims=True))
            o_win.append(jnp.dot(p_w.astype(BF16), kw_all[:, 2 * hd + g * hd:2 * hd + (g + 1) * hd],
                                 preferred_element_type=F32) / jnp.sum(p_w, axis=-1, keepdims=True))
        for j, h in enumerate(heads):
            win_h = jnp.concatenate([o[j * WIN_ROWS:(j + 1) * WIN_ROWS] for o in o_win], axis=0)
            o_ref[0, :, h * hd:(h + 1) * hd] = (gates[:, h:h + 1] * o_cmp[h]
                                                + gates[:, nh + h:nh + h + 1] * o_sel[j * TQ:(j + 1) * TQ]
                                                + gates[:, 2 * nh + h:2 * nh + h + 1] * win_h)


def _nsa_prompt_call(proj, qn, kvb, kcvc):
    b, s, _ = qn.shape
    nbc = kcvc.shape[1]
    tk = min(512, s)
    wk = min(NSA_WINDOW + WIN_ROWS, s)
    return pl.pallas_call(
        functools.partial(_nsa_prompt_kernel, nbc, s, tk, wk),
        grid=(b, s // TQ),
        in_specs=[pl.BlockSpec((1, TQ, NSA_WIDTH), lambda bi, i: (bi, i, 0)),
                  pl.BlockSpec((1, TQ, LANE), lambda bi, i: (bi, i, C_NG // LANE)),
                  pl.BlockSpec((1, nbc, KV_ROW), lambda bi, i: (bi, 0, 0)),
                  pl.BlockSpec((1, s, KV_ROW), lambda bi, i: (bi, 0, 0)),
                  pl.BlockSpec((1, s, KV_ROW), lambda bi, i: (bi, 0, 1))],
        out_specs=pl.BlockSpec((1, TQ, NSA_WIDTH), lambda bi, i: (bi, i, 0)),
        out_shape=jax.ShapeDtypeStruct((b, s, NSA_WIDTH), F32),
        scratch_shapes=[pltpu.VMEM((NSA_KV_HEADS, TQ, s), BF16)],
        compiler_params=_cparams("parallel", "arbitrary"),
        name="nsa_prompt",
    )(qn, proj, kcvc, kvb, kvb)


def _feature_major_pages(cache):
    return cache.transpose(0, 1, 3, 4, 5, 2).reshape(cache.shape[0], cache.shape[1], KV_ROW, PAGE_SIZE)


def _page_copies(cache_hbm, li, pt_ref, b, step, dst_of, sem, slot, pps):
    return [pltpu.make_async_copy(cache_hbm.at[li, pt_ref[b, step * pps + k]], dst_of(slot, k), sem.at[slot])
            for k in range(pps)]


def _pipelined_pages(cache_hbm, li, pt_ref, dst_of, sem, pps):
    b, s, ns = pl.program_id(0), pl.program_id(1), pl.num_programs(1)
    slot = s % 2
    copies = functools.partial(_page_copies, cache_hbm, li, pt_ref, b, dst_of=dst_of, sem=sem, pps=pps)

    @pl.when(s == 0)
    def _():
        for cp in copies(step=s, slot=slot):
            cp.start()

    @pl.when(s + 1 < ns)
    def _():
        for cp in copies(step=s + 1, slot=1 - slot):
            cp.start()

    for cp in copies(step=s, slot=slot):
        cp.wait()
    return slot


def _cmp_sample_weights(cmp_pe, cmp_w1, cmp_w2, gain4):
    wt = cmp_w1.transpose(0, 2, 1, 3).astype(BF16)
    z = jnp.zeros_like(wt)
    w1 = jnp.concatenate([jnp.concatenate([wt, z], axis=-1), jnp.concatenate([z, wt], axis=-1)], axis=2)
    w1 = w1.reshape(2, NSA_HEAD_DIM // 2, 4 * NSA_BLOCK, 2 * NSA_CMP_HIDDEN)
    w2 = cmp_w2.astype(BF16)
    z2 = jnp.zeros_like(w2)
    w2bd = jnp.concatenate([jnp.concatenate([w2, z2], axis=-1), jnp.concatenate([z2, w2], axis=-1)], axis=1)
    pe_t = jnp.tile(cmp_pe.transpose(0, 2, 1), (1, 1, 2)).reshape(2 * NSA_HEAD_DIM, 2 * NSA_BLOCK)
    return pe_t, w1, w2bd, jnp.tile(gain4, 2).reshape(1, -1)


def _cmp_sample_kernel(li, pps, pt_ref, cache_hbm, pe_ref, w1_ref, w2_ref, g_ref, o_ref, buf, sem):
    slot = _pipelined_pages(cache_hbm, li, pt_ref, lambda sl, k: buf.at[sl, k], sem, pps)
    pages = buf.at[slot]
    hd = NSA_HEAD_DIM
    for c in range(2):
        acc = jnp.zeros((NSA_KV_HEADS * pps, 2 * NSA_CMP_HIDDEN), F32)
        def feature_rows(d):
            x = jnp.concatenate([pages[:, (2 * c + g) * hd + d, :] for g in range(NSA_KV_HEADS)], axis=0)
            return (x + pe_ref[c * hd + d:c * hd + d + 1, :]).astype(BF16)

        for d2 in range(hd // 2):
            x = jnp.concatenate([feature_rows(2 * d2), feature_rows(2 * d2 + 1)], axis=1)
            acc = acc + jnp.dot(x, w1_ref[c, d2], preferred_element_type=F32)
        out = jnp.dot(_silu(acc).astype(BF16), w2_ref[c], preferred_element_type=F32)
        if c == 0:
            out = out * lax.rsqrt(_seg_sum(out * out, hd) * (1.0 / hd) + EPS) * g_ref[...]
        for g in range(NSA_KV_HEADS):
            o_ref[0, :, (2 * c + g) * 2 * hd:(2 * c + g + 1) * 2 * hd] = out[g * pps:(g + 1) * pps]


def _cmp_sample_call(cache_t, li, page_table, cw):
    bd, n_pages = page_table.shape
    pps = math.gcd(n_pages, 64)
    pe_t, w1, w2bd, gain = cw
    hd = NSA_HEAD_DIM
    grid_spec = pltpu.PrefetchScalarGridSpec(
        num_scalar_prefetch=1,
        grid=(bd, n_pages // pps),
        in_specs=[pl.BlockSpec(memory_space=pl.ANY),
                  pl.BlockSpec(pe_t.shape, lambda b, s, pt: (0, 0)),
                  pl.BlockSpec(w1.shape, lambda b, s, pt: (0, 0, 0, 0)),
                  pl.BlockSpec(w2bd.shape, lambda b, s, pt: (0, 0, 0)),
                  pl.BlockSpec(gain.shape, lambda b, s, pt: (0, 0))],
        out_specs=pl.BlockSpec((1, pps, 2 * KV_ROW), lambda b, s, pt: (b, s, 0)),
        scratch_shapes=[pltpu.VMEM((2, pps, KV_ROW, PAGE_SIZE), F32), pltpu.SemaphoreType.DMA((2,))])
    packed = pl.pallas_call(
        functools.partial(_cmp_sample_kernel, li, pps),
        grid_spec=grid_spec,
        out_shape=jax.ShapeDtypeStruct((bd, n_pages, 2 * KV_ROW), F32),
        compiler_params=_cparams("arbitrary", "arbitrary"),
        name="nsa_cmp_sample",
    )(page_table, cache_t, pe_t, w1, w2bd, gain)
    return packed.reshape(bd, n_pages, 4, 2, hd).transpose(0, 1, 3, 2, 4).reshape(bd, 2 * n_pages, KV_ROW)


def _pick_group(x, rows_g0):
    w = x.shape[-1] // 2
    return jnp.where(rows_g0, x[:, :w], x[:, w:])


def _nsa_sample_kernel(li, pps, past, n_tok, pt_ref, cache_hbm, q_ref, gate_ref, cmp_ref, knew_ref, wold_ref, wnew_ref,
                       o_ref, buf, sem, qbd_ref, sel_ref, ocw_ref, m_ref, l_ref, acc_ref):
    s, ns = pl.program_id(1), pl.num_programs(1)
    hd, nh, rep = NSA_HEAD_DIM, NSA_HEADS, NSA_REP
    rows = nh * n_tok
    nbc = past // NSA_BLOCK
    tk = pps * PAGE_SIZE
    slot = _pipelined_pages(cache_hbm, li, pt_ref,
                            lambda sl, k: buf.at[sl, :, pl.ds(k * PAGE_SIZE, PAGE_SIZE)], sem, pps)
    ridx = _iota((rows, 1), 0)
    rows_g0 = ridx < rep * n_tok
    tok = ridx % n_tok
    t = past + tok
    slope = jnp.exp2(-(ridx // n_tok + 1).astype(F32))

    @pl.when(s == 0)
    def _():
        qtok = q_ref[0]
        qrows = jnp.concatenate([qtok[:, h * hd:(h + 1) * hd] for h in range(nh)], axis=0)
        zero = jnp.zeros_like(qrows)
        qbd = jnp.where(rows_g0, jnp.concatenate([qrows, zero], axis=1), jnp.concatenate([zero, qrows], axis=1))
        qbd_ref[...] = qbd
        qbb = qbd.astype(BF16)
        kcvc = cmp_ref[0]
        blk = _iota((rows, nbc), 1)
        dist_c = t - ((blk + 1) * NSA_BLOCK - 1)
        p_c = _masked_softmax(_nt(qbd, kcvc[:, :2 * hd], precision=HI) - slope * dist_c.astype(F32), dist_c >= 0)
        o_c = _pick_group(jnp.dot(p_c, kcvc[:, 2 * hd:], precision=HI, preferred_element_type=F32), rows_g0)
        blk_t = _iota((n_tok, nbc), 1)
        cur_t = (past + _iota((n_tok, 1), 0)) // NSA_BLOCK
        sels = []
        for g in range(NSA_KV_HEADS):
            imp = p_c[g * rep * n_tok:(g * rep + 1) * n_tok]
            for r in range(1, rep):
                imp = imp + p_c[(g * rep + r) * n_tok:(g * rep + r + 1) * n_tok]
            imp = jnp.where(blk_t < cur_t, imp, -1.0)
            sels += [_select_blocks(imp, blk_t, NSA_TOPN - 1)] * rep
        sel_ref[...] = jnp.concatenate(sels, axis=0).astype(BF16)
        knew = knew_ref[0]
        dist_n = tok - _iota((rows, n_tok), 1)
        mask_n = dist_n >= 0
        s_n = jnp.where(mask_n, _nt(qbb, knew[:, :2 * hd].astype(BF16)) - slope * dist_n.astype(F32), NEG_INF)
        m0 = jnp.max(s_n, axis=-1, keepdims=True)
        p_n = jnp.where(mask_n, jnp.exp(s_n - m0), 0.0)
        m_ref[...] = m0
        l_ref[...] = jnp.sum(p_n, axis=-1, keepdims=True)
        acc_ref[...] = jnp.dot(p_n.astype(BF16), knew[:, 2 * hd:].astype(BF16), preferred_element_type=F32)
        kw = jnp.concatenate([wold_ref[0], wnew_ref[0]], axis=0)
        dist_w = tok + NSA_WINDOW - _iota((rows, NSA_WINDOW + n_tok), 1)
        mask_w = (dist_w >= 0) & (dist_w <= NSA_WINDOW)
        p_w = _masked_softmax(_nt(qbb, kw[:, :2 * hd].astype(BF16)) - slope * dist_w.astype(F32), mask_w)
        o_w = _pick_group(jnp.dot(p_w.astype(BF16), kw[:, 2 * hd:].astype(BF16), preferred_element_type=F32), rows_g0)
        ocw_ref[0] = o_c
        ocw_ref[1] = o_w

    pages = buf.at[slot]
    k_t = pages[:2 * hd, :].astype(BF16)
    v_t = pages[2 * hd:, :].astype(BF16)
    k0 = s * tk
    expand = (_iota((nbc, tk), 0) == (k0 + _iota((nbc, tk), 1)) // NSA_BLOCK).astype(BF16)
    mask = jnp.dot(sel_ref[...], expand, preferred_element_type=F32) > 0.5
    dist = t - (k0 + _iota((rows, tk), 1))
    qk = jnp.dot(qbd_ref[...].astype(BF16), k_t, preferred_element_type=F32)
    sc = jnp.where(mask, qk - slope * dist.astype(F32), NEG_INF)
    m_old = m_ref[...]
    m_new = jnp.maximum(m_old, jnp.max(sc, axis=-1, keepdims=True))
    alpha = jnp.exp(m_old - m_new)
    p = jnp.where(mask, jnp.exp(sc - m_new), 0.0)
    m_ref[...] = m_new
    l_ref[...] = alpha * l_ref[...] + jnp.sum(p, axis=-1, keepdims=True)
    acc_ref[...] = alpha * acc_ref[...] + _nt(p.astype(BF16), v_t)

    @pl.when(s == ns - 1)
    def _():
        o_s = _pick_group(acc_ref[...] / l_ref[...], rows_g0)
        gates = jax.nn.sigmoid(gate_ref[0])

        def gate_rows(br):
            return jnp.concatenate([gates[:, br * nh + h:br * nh + h + 1] for h in range(nh)], axis=0)

        o_ref[0] = gate_rows(0) * ocw_ref[0] + gate_rows(1) * o_s + gate_rows(2) * ocw_ref[1]


def _nsa_sample_call(cache_t, li, page_table, proj, qn, kvn, kcvc, win_old, past):
    bd, n_pages = page_table.shape
    n_tok = qn.shape[1]
    pps = math.gcd(n_pages, 32)
    rows = NSA_HEADS * n_tok
    nbc = kcvc.shape[1]
    grid_spec = pltpu.PrefetchScalarGridSpec(
        num_scalar_prefetch=1,
        grid=(bd, n_pages // pps),
        in_specs=[pl.BlockSpec(memory_space=pl.ANY),
                  pl.BlockSpec((1, n_tok, NSA_WIDTH), lambda b, s, pt: (b, 0, 0)),
                  pl.BlockSpec((1, n_tok, LANE), lambda b, s, pt: (b, 0, C_NG // LANE)),
                  pl.BlockSpec((1, nbc, KV_ROW), lambda b, s, pt: (b, 0, 0)),
                  pl.BlockSpec((1, n_tok, KV_ROW), lambda b, s, pt: (b, 0, 1)),
                  pl.BlockSpec((1, NSA_WINDOW, KV_ROW), lambda b, s, pt: (b, 0, 0)),
                  pl.BlockSpec((1, n_tok, KV_ROW), lambda b, s, pt: (b, 0, 2))],
        out_specs=pl.BlockSpec((1, rows, NSA_HEAD_DIM), lambda b, s, pt: (b, 0, 0)),
        scratch_shapes=[pltpu.VMEM((2, KV_ROW, pps * PAGE_SIZE), F32),
                        pltpu.SemaphoreType.DMA((2,)),
                        pltpu.VMEM((rows, 2 * NSA_HEAD_DIM), F32),
                        pltpu.VMEM((rows, nbc), BF16),
                        pltpu.VMEM((2, rows, NSA_HEAD_DIM), F32),
                        pltpu.VMEM((rows, 1), F32),
                        pltpu.VMEM((rows, 1), F32),
                        pltpu.VMEM((rows, 2 * NSA_HEAD_DIM), F32)])
    return pl.pallas_call(
        functools.partial(_nsa_sample_kernel, li, pps, past, n_tok),
        grid_spec=grid_spec,
        out_shape=jax.ShapeDtypeStruct((bd, rows, NSA_HEAD_DIM), F32),
        compiler_params=_cparams("arbitrary", "arbitrary"),
        name="nsa_sample",
    )(page_table, cache_t, qn, proj, kcvc, kvn, win_old, kvn)


def _outproj_kernel(y_ref, n_ref, r_ref, w_ref, x_ref, g_ref, o_ref, mix_ref):
    @pl.when(pl.program_id(2) == 0)
    def _():
        mix_ref[:, :SSD_INNER] = y_ref[0].astype(BF16)
        mix_ref[:, SSD_INNER:SSD_INNER + NSA_WIDTH] = n_ref[0].astype(BF16)
        mix_ref[:, SSD_INNER + NSA_WIDTH:] = r_ref[0].astype(BF16)

    o_ref[0] = x_ref[0] + g_ref[0] * jnp.dot(mix_ref[...], w_ref[...], preferred_element_type=F32)


def _outproj_call(y, o_nsa, o_ret, w_bf, x, gate):
    b, s, d = x.shape
    ts = min(s, 1024)
    tn = 512
    sm = gate.shape[1]
    tsm = 1 if sm == 1 else ts
    mod_map = (lambda bi, i, j: (bi, 0, j)) if sm == 1 else (lambda bi, i, j: (bi, i, j))
    row_map = lambda bi, i, j: (bi, i, 0)
    return pl.pallas_call(
        _outproj_kernel,
        grid=(b, s // ts, d // tn),
        in_specs=[pl.BlockSpec((1, ts, SSD_INNER), row_map),
                  pl.BlockSpec((1, ts, NSA_WIDTH), row_map),
                  pl.BlockSpec((1, ts, RET_WIDTH), row_map),
                  pl.BlockSpec((D_MIX, tn), lambda bi, i, j: (0, j)),
                  pl.BlockSpec((1, ts, tn), lambda bi, i, j: (bi, i, j)),
                  pl.BlockSpec((1, tsm, tn), mod_map)],
        out_specs=pl.BlockSpec((1, ts, tn), lambda bi, i, j: (bi, i, j)),
        out_shape=jax.ShapeDtypeStruct((b, s, d), F32),
        scratch_shapes=[pltpu.VMEM((ts, D_MIX), BF16)],
        compiler_params=_cparams("parallel", "parallel", "arbitrary"),
        name="out_proj",
    )(y, o_nsa, o_ret, w_bf, x, gate)


def _route_kernel(x_ref, sc_ref, sh_ref, g_ref, wr_ref, br_ref, h_ref, meta_ref, cnt_ref):
    h = _mod_norm(x_ref[0], g_ref[...], sc_ref[0], sh_ref[0])
    n_rows = h.shape[0]
    for c in range(ROW_CHUNKS):
        h_ref[0, pl.ds(c, n_rows, stride=ROW_CHUNKS), :] = h[:, c * LANE:(c + 1) * LANE]
    logit = jnp.dot(h, wr_ref[...], precision=HI, preferred_element_type=F32) + br_ref[...]
    lane = _iota(logit.shape, 1)
    first = lambda hit: jnp.min(jnp.where(hit, lane, LANE), axis=-1, keepdims=True)
    is_g = lane < MOE_GROUPS
    gl = jnp.where(is_g, logit, -jnp.inf)
    gmax = jnp.max(gl, axis=-1, keepdims=True)
    g_idx = first(gl == gmax)
    g_p = 1.0 / jnp.sum(jnp.where(is_g, jnp.exp(logit - gmax), 0.0), axis=-1, keepdims=True)
    e_lane = lane - MOE_GROUPS
    in_grp = (e_lane >= 0) & (e_lane < MOE_EXPERTS) & (e_lane // MOE_EXPERTS_PER_GROUP == g_idx)
    el = jnp.where(in_grp, logit, -jnp.inf)
    v1 = jnp.max(el, axis=-1, keepdims=True)
    i1 = first(el == v1)
    el = jnp.where(lane == i1, -jnp.inf, el)
    v2 = jnp.max(el, axis=-1, keepdims=True)
    i2 = first(el == v2)
    e = jnp.exp(v2 - v1)
    w1 = g_p / (1.0 + e)
    w2 = g_p * e / (1.0 + e)
    meta = jnp.where(lane == 0, (i1 - MOE_GROUPS).astype(F32), 0.0)
    meta = jnp.where(lane == 1, (i2 - MOE_GROUPS).astype(F32), meta)
    meta = jnp.where(lane == 2, w1, meta)
    meta_ref[0] = jnp.where(lane == 3, w2, meta)
    cnt_ref[0, 0] = jnp.sum(((lane == i1) | (lane == i2)).astype(F32), axis=0, keepdims=True)


def _route_call(x1, scale, shift, gain, wr, br):
    b, s, d = x1.shape
    ts = min(s, 512)
    sm = scale.shape[1]
    tsm = 1 if sm == 1 else ts
    mod_map = (lambda bi, i: (bi, 0, 0)) if sm == 1 else (lambda bi, i: (bi, i, 0))
    return pl.pallas_call(
        _route_kernel,
        grid=(b, s // ts),
        in_specs=[pl.BlockSpec((1, ts, d), lambda bi, i: (bi, i, 0)),
                  pl.BlockSpec((1, tsm, d), mod_map),
                  pl.BlockSpec((1, tsm, d), mod_map),
                  pl.BlockSpec((1, d), lambda bi, i: (0, 0)),
                  pl.BlockSpec((d, LANE), lambda bi, i: (0, 0)),
                  pl.BlockSpec((1, LANE), lambda bi, i: (0, 0))],
        out_specs=[pl.BlockSpec((1, ts * ROW_CHUNKS, LANE), lambda bi, i: (bi, i, 0)),
                   pl.BlockSpec((1, ts, LANE), lambda bi, i: (bi, i, 0)),
                   pl.BlockSpec((1, 1, 1, LANE), lambda bi, i: (bi, i, 0, 0))],
        out_shape=[jax.ShapeDtypeStruct((b, s * ROW_CHUNKS, LANE), F32), jax.ShapeDtypeStruct((b, s, LANE), F32),
                   jax.ShapeDtypeStruct((b, s // ts, 1, LANE), F32)],
        compiler_params=_cparams("parallel", "parallel"),
        name="moe_route",
    )(x1, scale, shift, gain.reshape(1, d), wr, br)


def _moe_plan(e_ids, w, counts):
    n_pairs = e_ids.size
    tm = TM_MOE
    n_tiles = -(-n_pairs // tm) + MOE_EXPERTS
    n_slots = n_tiles * tm
    n_tok = n_pairs // MOE_TOPK
    pair_ids = jnp.arange(n_pairs, dtype=jnp.int32)
    _, order, w_sorted = lax.sort((e_ids.reshape(-1), pair_ids, w.reshape(-1)), num_keys=1, is_stable=True)
    tiles_per = (counts + tm - 1) // tm
    tile_end = jnp.cumsum(tiles_per)
    tile_start = tile_end - tiles_per
    grp_start = jnp.cumsum(counts) - counts
    n_used = tile_end[-1]
    tiles = jnp.arange(n_tiles, dtype=jnp.int32)
    tile_ids = jnp.minimum(tiles, n_used - 1)
    tile_e = jnp.sum((tile_ids[:, None] >= tile_end[None, :]).astype(jnp.int32), axis=1)
    active = tiles < n_used
    rank0 = (tile_ids - tile_start[tile_e]) * tm
    flag = active.astype(jnp.int32) + (active & (rank0 == 0)).astype(jnp.int32)
    r = jnp.arange(tm, dtype=jnp.int32)[None, :]
    rank = rank0[:, None] + r
    valid = (active[:, None] & (rank < counts[tile_e][:, None])).reshape(-1)
    idx = jnp.clip(grp_start[tile_e][:, None] + rank, 0, n_pairs - 1).reshape(-1)
    pair = order[idx]
    tok = pair // MOE_TOPK
    src = jnp.where(valid, tok, 0) * ROW_CHUNKS
    pad_row = jnp.broadcast_to(n_tok + r, (n_tiles, tm)).reshape(-1)
    dst_row = jnp.where(valid, tok, pad_row) * ROW_CHUNKS
    dst_plane = jnp.where(valid, pair % MOE_TOPK, 0)
    w_slot = jnp.where(valid, w_sorted[idx], 0.0)
    return tile_e, flag, src, dst_row, dst_plane, w_slot.reshape(n_slots, 1)


def _moe_kernel(te_ref, flag_ref, src_ref, row_ref, plane_ref, h_hbm, w_ref, wg_ref, wu_ref, wd_ref, y_hbm, xbuf,
                obuf, xs, wgb, wub, wdb, sem_in, sem_out):
    i = pl.program_id(0)
    n_tiles = pl.num_programs(0)
    flag = flag_ref[i]
    nxt = jnp.minimum(i + 1, n_tiles - 1)
    next_active = (i + 1 < n_tiles) & (flag_ref[nxt] > 0)
    slot = i % 2

    rc = ROW_CHUNKS

    def slab(ref_2d, first_row):
        return ref_2d.at[pl.ds(pl.multiple_of(first_row, rc), rc)]

    def start_gather(tile, sl):
        for r in range(TM_MOE):
            pltpu.make_async_copy(slab(h_hbm, src_ref[tile * TM_MOE + r]), xbuf.at[sl, pl.ds(r * rc, rc)],
                                  sem_in.at[sl]).start()

    def wait_gather(sl):
        pltpu.make_async_copy(h_hbm.at[pl.ds(0, TM_MOE * rc)], xbuf.at[sl], sem_in.at[sl]).wait()

    def wait_scatter(sl):
        pltpu.make_async_copy(obuf.at[sl], y_hbm.at[0, pl.ds(0, TM_MOE * rc)], sem_out.at[sl]).wait()

    @pl.when(i == 0)
    def _():
        obuf[0] = jnp.zeros(obuf.shape[1:], F32)
        pad_rows = pl.ds(y_hbm.shape[1] - TM_MOE * rc, TM_MOE * rc)
        for plane in range(MOE_TOPK):
            pltpu.make_async_copy(obuf.at[0], y_hbm.at[plane, pad_rows], sem_out.at[0]).start()
        for plane in range(MOE_TOPK):
            pltpu.make_async_copy(obuf.at[0], y_hbm.at[plane, pad_rows], sem_out.at[0]).wait()

    @pl.when(flag > 0)
    def _():
        @pl.when(i == 0)
        def _():
            start_gather(i, slot)

        @pl.when(next_active)
        def _():
            start_gather(i + 1, 1 - slot)

        @pl.when(flag > 1)
        def _():
            wgb[...] = wg_ref[0, 0].astype(BF16)
            wub[...] = wu_ref[0, 0].astype(BF16)
            wdb[...] = wd_ref[0, 0].astype(BF16)

        wait_gather(slot)
        rows = xbuf.at[slot]
        for c in range(rc):
            xs[:, c * LANE:(c + 1) * LANE] = rows[pl.ds(c, TM_MOE, stride=rc), :].astype(BF16)
        x = xs[...]
        a = jnp.dot(x, wgb[...], preferred_element_type=F32)
        u = jnp.dot(x, wub[...], preferred_element_type=F32)
        hid = (_silu(a) * u * w_ref[...]).astype(BF16)
        out = jnp.dot(hid, wdb[...], preferred_element_type=F32)

        @pl.when(i >= 2)
        def _():
            wait_scatter(slot)

        out_rows = obuf.at[slot]
        for c in range(rc):
            out_rows[pl.ds(c, TM_MOE, stride=rc), :] = out[:, c * LANE:(c + 1) * LANE]

        for r in range(TM_MOE):
            k = i * TM_MOE + r
            pltpu.make_async_copy(out_rows.at[pl.ds(r * rc, rc)], slab(y_hbm.at[plane_ref[k]], row_ref[k]),
                                  sem_out.at[slot]).start()

        @pl.when(jnp.logical_not(next_active))
        def _():
            wait_scatter(slot)

            @pl.when(i >= 1)
            def _():
                wait_scatter(1 - slot)


def _moe_call(h_all, plan, li, w_gate, w_up, w_down):
    t = h_all.shape[0] // ROW_CHUNKS
    d = D_MODEL
    tile_e, flag, src, dst_row, dst_plane, w_slot = plan
    n_tiles = tile_e.shape[0]
    tm = TM_MOE
    w_map = lambda i, te, *_: (li, te[i], 0, 0)
    grid_spec = pltpu.PrefetchScalarGridSpec(
        num_scalar_prefetch=5,
        grid=(n_tiles,),
        in_specs=[pl.BlockSpec(memory_space=pl.ANY),
                  pl.BlockSpec((tm, 1), lambda i, *_: (i, 0)),
                  pl.BlockSpec((1, 1, d, MOE_HIDDEN), w_map),
                  pl.BlockSpec((1, 1, d, MOE_HIDDEN), w_map),
                  pl.BlockSpec((1, 1, MOE_HIDDEN, d), w_map)],
        out_specs=pl.BlockSpec(memory_space=pl.ANY),
        scratch_shapes=[pltpu.VMEM((2, tm * ROW_CHUNKS, LANE), F32), pltpu.VMEM((2, tm * ROW_CHUNKS, LANE), F32),
                        pltpu.VMEM((tm, d), BF16),
                        pltpu.VMEM((d, MOE_HIDDEN), BF16), pltpu.VMEM((d, MOE_HIDDEN), BF16),
                        pltpu.VMEM((MOE_HIDDEN, d), BF16),
                        pltpu.SemaphoreType.DMA((2,)), pltpu.SemaphoreType.DMA((2,))])
    return pl.pallas_call(
        _moe_kernel,
        grid_spec=grid_spec,
        out_shape=jax.ShapeDtypeStruct((MOE_TOPK, (t + tm) * ROW_CHUNKS, LANE), F32),
        compiler_params=_cparams("arbitrary"),
        name="moe_experts",
    )(tile_e, flag, src, dst_row, dst_plane, h_all, w_slot, w_gate, w_up, w_down)


def _combine_kernel(x_ref, g_ref, y_ref, o_ref):
    n_rows = x_ref.shape[1]
    for c in range(ROW_CHUNKS):
        cols = slice(c * LANE, (c + 1) * LANE)
        chunk = pl.ds(c, n_rows, stride=ROW_CHUNKS)
        o_ref[0, :, cols] = x_ref[0, :, cols] + g_ref[0, :, cols] * (y_ref[0, chunk, :] + y_ref[1, chunk, :])


def _combine_call(x1, gate, y_all, row0):
    b, s, d = x1.shape
    ts = min(s, 512)
    sm = gate.shape[1]
    tsm = 1 if sm == 1 else ts
    mod_map = (lambda bi, i: (bi, 0, 0)) if sm == 1 else (lambda bi, i: (bi, i, 0))
    blk0 = row0 // ts
    per_b = s // ts
    return pl.pallas_call(
        _combine_kernel,
        grid=(b, per_b),
        in_specs=[pl.BlockSpec((1, ts, d), lambda bi, i: (bi, i, 0)),
                  pl.BlockSpec((1, tsm, d), mod_map),
                  pl.BlockSpec((MOE_TOPK, ts * ROW_CHUNKS, LANE), lambda bi, i: (0, blk0 + bi * per_b + i, 0))],
        out_specs=pl.BlockSpec((1, ts, d), lambda bi, i: (bi, i, 0)),
        out_shape=jax.ShapeDtypeStruct((b, s, d), F32),
        compiler_params=_cparams("parallel", "parallel"),
        name="moe_combine",
    )(x1, gate, y_all)


def _reorder_w_in(w):
    sizes = (SSD_INNER, SSD_CONV_DIM, SSD_HEADS, NSA_WIDTH, 3 * KV_ROW, 3 * NSA_HEADS, RET_HEADS * RET_KEY_DIM,
             RET_HEADS * RET_KEY_DIM, RET_WIDTH, RET_WIDTH)
    z, xbc, dt, nq, nkv, ngate, rq, rk, rv, rg = jnp.split(w, [int(v) for v in np.cumsum(sizes)[:-1]], axis=1)
    padc = lambda a: jnp.pad(a, ((0, 0), (0, LANE - a.shape[1])))
    return jnp.concatenate([z, nq, rv, rg, rq, rk, xbc, nkv, padc(dt), padc(ngate)], axis=1).astype(BF16)


def _kv_rows(a):
    return a.reshape(a.shape[:2] + (2, NSA_KV_HEADS, NSA_HEAD_DIM))


def kernel(x_prompt, x_sample, cache_cmp_kv, cache_sel_kv, cache_win_kv, state_ssm, state_conv, state_ret, page_table, c_prompt, c_sample, w_ada, b_ada, norm_mix, norm_ffn, w_in, w_out, conv_w, conv_b, dt_bias, a_log, ssd_d, ssd_norm, qk_gain, cmp_pe, cmp_w1, cmp_w2, router_group_w, router_group_b, router_expert_w, router_expert_b, expert_w_gate, expert_w_up, expert_w_down):
    bp, sp, d = x_prompt.shape
    bd, sd, _ = x_sample.shape
    n_pages = page_table.shape[1]
    past = n_pages * PAGE_SIZE
    assert d == D_MODEL and sp % LC == 0 and sp % NSA_BLOCK == 0 and sp >= NSA_WINDOW
    assert sd % SUBLANE == 0 and sd <= NSA_BLOCK and sd <= LC and cache_win_kv.shape[2] == NSA_WINDOW
    n_seq = bp + bd
    c_all = jnp.pad(jnp.concatenate([c_prompt, c_sample]), ((0, -n_seq % SUBLANE), (0, 0)))
    mod = _ada_call(c_all, w_ada, b_ada)
    cos_p, sin_p = _rotary_tables(0, sp, sp)
    cos_s, sin_s = _rotary_tables(past, sd, LC)
    cmp_pages = _feature_major_pages(cache_cmp_kv)
    sel_pages = _feature_major_pages(cache_sel_kv)
    xp = x_prompt
    xs = x_sample.reshape(1, bd * sd, d)
    outs_p, outs_s = [], []
    for li in range(DEPTH):
        mods_p = [mod[li, :bp, k * d:(k + 1) * d].reshape(bp, 1, d) for k in range(6)]
        mods_s = [jnp.repeat(mod[li, bp:n_seq, k * d:(k + 1) * d], sd, axis=0).reshape(1, bd * sd, d) for k in range(6)]
        w_in_b = _reorder_w_in(w_in[li])
        w_out_b = w_out[li].astype(BF16)
        cw = _compress_weights(cmp_pe[li], cmp_w1[li], cmp_w2[li], qk_gain[li, 4])
        cw_pages = _cmp_sample_weights(cmp_pe[li], cmp_w1[li], cmp_w2[li], qk_gain[li, 4])
        ssd_w = (conv_w[li], conv_b[li], dt_bias[li], a_log[li], ssd_d[li], ssd_norm[li])
        wr = jnp.pad(jnp.concatenate([router_group_w[li], router_expert_w[li]], axis=1),
                     ((0, 0), (0, LANE - MOE_GROUPS - MOE_EXPERTS)))
        br = jnp.pad(jnp.concatenate([router_group_b[li], router_expert_b[li]]),
                     (0, LANE - MOE_GROUPS - MOE_EXPERTS)).reshape(1, LANE)

        sh_a, sc_a, g_a, sh_f, sc_f, g_f = mods_p
        proj = _inproj_call(xp, sc_a, sh_a, norm_mix[li], w_in_b)
        y_ssd, ssm_p, conv_p = _ssd_call(proj, LC, jnp.zeros((bp, SSD_CONV - 1, SSD_CONV_DIM), F32),
                                         jnp.zeros((bp, SSD_HEADS, SSD_HEAD_DIM, SSD_STATE), F32), *ssd_w)
        o_ret, ret_p = _ret_call(proj, LC, jnp.zeros((bp, RET_HEADS, RET_KEY_DIM, RET_VAL_DIM), F32), cos_p, sin_p)
        qn, kvn_p, kvb = _prep_call(proj, sp, qk_gain[li])
        kcvc = _cmp_prompt_call(kvn_p, cw)
        o_nsa = _nsa_prompt_call(proj, qn, kvb, kcvc)
        x1_p = _outproj_call(y_ssd, o_nsa, o_ret, w_out_b, xp, g_a)
        h2_p, meta_p, cnt_p = _route_call(x1_p, sc_f, sh_f, norm_ffn[li], wr, br)
        gf_p = g_f

        sh_a, sc_a, g_a, sh_f, sc_f, g_f = mods_s
        proj = _inproj_call(xs, sc_a, sh_a, norm_mix[li], w_in_b).reshape(bd, sd, N_PROJ)
        proj = jnp.pad(proj, ((0, 0), (0, LC - sd), (0, 0)))
        y_ssd, ssm_s, conv_s = _ssd_call(proj, sd, state_conv[li], state_ssm[li], *ssd_w)
        o_ret, ret_s = _ret_call(proj, sd, state_ret[li], cos_s, sin_s)
        qn, kvn_s, _ = _prep_call(proj, sd, qk_gain[li])
        kcvc = _cmp_sample_call(cmp_pages, li, page_table, cw_pages)
        win_old = cache_win_kv[li].reshape(bd, NSA_WINDOW, KV_ROW)
        o_nsa = _nsa_sample_call(sel_pages, li, page_table, proj, qn, kvn_s, kcvc, win_old, past)
        o_nsa = o_nsa.reshape(bd, NSA_HEADS, sd, NSA_HEAD_DIM).transpose(0, 2, 1, 3).reshape(1, bd * sd, NSA_WIDTH)
        x1_s = _outproj_call(y_ssd[:, :sd].reshape(1, bd * sd, SSD_INNER), o_nsa,
                             o_ret[:, :sd].reshape(1, bd * sd, RET_WIDTH), w_out_b, xs, g_a)
        h2_s, meta_s, cnt_s = _route_call(x1_s, sc_f, sh_f, norm_ffn[li], wr, br)

        h_all = jnp.concatenate([h2_p.reshape(bp * sp * ROW_CHUNKS, LANE), h2_s.reshape(bd * sd * ROW_CHUNKS, LANE)])
        meta = jnp.concatenate([meta_p.reshape(bp * sp, LANE), meta_s.reshape(bd * sd, LANE)])
        counts = (cnt_p.sum(axis=(0, 1, 2)) + cnt_s.sum(axis=(0, 1, 2)))[MOE_GROUPS:MOE_GROUPS + MOE_EXPERTS]
        plan = _moe_plan(meta[:, :MOE_TOPK].astype(jnp.int32), meta[:, MOE_TOPK:2 * MOE_TOPK],
                         counts.astype(jnp.int32))
        y_all = _moe_call(h_all, plan, li, expert_w_gate, expert_w_up, expert_w_down)
        xp = _combine_call(x1_p, gf_p, y_all, 0)
        xs = _combine_call(x1_s, g_f, y_all, bp * sp)

        win_s = jnp.concatenate([win_old[:, sd:], kvn_s[:, :, 2 * KV_ROW:]], axis=1)
        outs_p.append((_kv_rows(kvn_p[:, :, :KV_ROW]), _kv_rows(kvn_p[:, :, KV_ROW:2 * KV_ROW]),
                       _kv_rows(kvn_p[:, sp - NSA_WINDOW:, 2 * KV_ROW:]), ssm_p, conv_p, ret_p))
        outs_s.append((_kv_rows(kvn_s[:, :, :KV_ROW]), _kv_rows(kvn_s[:, :, KV_ROW:2 * KV_ROW]), _kv_rows(win_s),
                       ssm_s, conv_s, ret_s))
    res = [xp, xs.reshape(bd, sd, d)]
    for k in range(6):
        res.append(jnp.stack([o[k] for o in outs_p]))
        res.append(jnp.stack([o[k] for o in outs_s]))
    return tuple(res)
```
